```python
import math
import jax
import jax.numpy as jnp
from jax import lax
import numpy as np

D_MODEL = 1024
BATCH = 1
SEQ = 16384
DEPTH = 1

N_MEM = 256
NORM_EPS = 1e-6
NEG_INF = -1e30

RWKV_HEADS = 8
RWKV_HEAD_DIM = 64
RWKV_WIDTH = RWKV_HEADS * RWKV_HEAD_DIM
DECAY_LORA = 64
ICLR_LORA = 64
GATE_LORA = 128
RWKV_COLS = 3 * RWKV_WIDTH + DECAY_LORA + ICLR_LORA + GATE_LORA
RWKV_SPLITS = [RWKV_WIDTH, 2 * RWKV_WIDTH, 3 * RWKV_WIDTH, 3 * RWKV_WIDTH + DECAY_LORA, 3 * RWKV_WIDTH + DECAY_LORA + ICLR_LORA]
LN_X_EPS = 64e-5
KK_EPS = 1e-12

MOBA_HEADS = 8
MOBA_HEAD_DIM = 64
MOBA_WIDTH = MOBA_HEADS * MOBA_HEAD_DIM
MOBA_BLOCK = 256
MOBA_TOP_BLOCKS = 3
Q_CHUNK = 64

REL_BUCKETS = 32
REL_MAX_DISTANCE = 4096

N_BRANCHES = 2
IN_COLS = RWKV_COLS + 3 * MOBA_WIDTH + N_BRANCHES * D_MODEL

XATTN_HEADS = 4
XATTN_HEAD_DIM = 128
XATTN_WIDTH = XATTN_HEADS * XATTN_HEAD_DIM

N_GROUPS = 4
EXPERTS_PER_GROUP = 8
N_EXPERTS = N_GROUPS * EXPERTS_PER_GROUP
TOP_K_EXPERTS = 2
D_EXPERT = 512
MOE_BLOCK = 128

kernel_name = 'hybrid_rwkv7_moba_hmoe_layer'


def rmsnorm(x, g):
    xf = x.astype(jnp.float32)
    y = xf * lax.rsqrt(jnp.mean(xf * xf, axis=-1, keepdims=True) + NORM_EPS)
    return (y * g.astype(jnp.float32)).astype(x.dtype)


def t5_bucket(dist):
    n = jnp.maximum(dist, 0)
    max_exact = REL_BUCKETS // 2
    nf = jnp.maximum(n, max_exact).astype(jnp.float32)
    large = max_exact + (jnp.log(nf / max_exact) / math.log(REL_MAX_DISTANCE / max_exact)
                         * (REL_BUCKETS - max_exact)).astype(jnp.int32)
    large = jnp.minimum(large, REL_BUCKETS - 1)
    return jnp.where(n < max_exact, n, large)


def wkv7_scan(r, w, k, v, a, b):
    B, T, H, N = r.shape

    def step(S, inp):
        r_t, w_t, k_t, v_t, a_t, b_t = inp
        sa = jnp.einsum('bhvk,bhk->bhv', S, a_t)
        S = S * w_t[:, :, None, :] + sa[..., None] * b_t[:, :, None, :] + v_t[..., None] * k_t[:, :, None, :]
        return S, jnp.einsum('bhvk,bhk->bhv', S, r_t)

    xs = tuple(jnp.moveaxis(t.astype(jnp.float32), 1, 0) for t in (r, w, k, v, a, b))
    S0 = jnp.zeros((B, H, N, N), jnp.float32)
    _, y = lax.scan(step, S0, xs)
    return jnp.moveaxis(y, 0, 1)


def rwkv7_mixer(u, mu, w0, decay_up, iclr_a0, iclr_up, gate_up, k_k, k_a, r_k, ln_w, ln_b):
    B, T, _ = u.shape
    H, N = RWKV_HEADS, RWKV_HEAD_DIM
    u_prev = jnp.pad(u[:, :-1], ((0, 0), (1, 0), (0, 0)))
    u = u + mu * (u_prev - u)
    r, k, v, wd, ad, gd = jnp.split(u, RWKV_SPLITS, axis=-1)
    w_log = -jax.nn.softplus(-(w0 + jnp.tanh(wd) @ decay_up)) - 0.5
    decay = jnp.exp(-jnp.exp(w_log.astype(jnp.float32)))
    a = jax.nn.sigmoid(iclr_a0 + ad @ iclr_up)
    g = jax.nn.sigmoid(gd) @ gate_up
    kk = (k * k_k).reshape(B, T, H, N).astype(jnp.float32)
    kk = kk * lax.rsqrt(jnp.sum(kk * kk, axis=-1, keepdims=True) + KK_EPS)
    k = k * (1.0 + (a - 1.0) * k_a)
    r_h = r.reshape(B, T, H, N)
    k_h = k.reshape(B, T, H, N)
    v_h = v.reshape(B, T, H, N)
    a_h = a.reshape(B, T, H, N).astype(jnp.float32)
    y = wkv7_scan(r_h, decay.reshape(B, T, H, N), k_h, v_h, -kk, kk * a_h)
    mean = jnp.mean(y, axis=-1, keepdims=True)
    var = jnp.mean(jnp.square(y - mean), axis=-1, keepdims=True)
    y = ((y - mean) * lax.rsqrt(var + LN_X_EPS)).reshape(B, T, RWKV_WIDTH)
    y = y * ln_w.astype(jnp.float32) + ln_b.astype(jnp.float32)
    bonus = jnp.sum((r_h * k_h * r_k).astype(jnp.float32), axis=-1, keepdims=True) * v_h.astype(jnp.float32)
    y = y + bonus.reshape(B, T, RWKV_WIDTH)
    return (y * g.astype(jnp.float32)).astype(u.dtype)


def moba_attention(q, k, v, rel_bias):
    B, T, H, Dh = q.shape
    nb = -(-T // MOBA_BLOCK)
    pad = nb * MOBA_BLOCK - T
    q = q.transpose(0, 2, 1, 3)
    kb = jnp.pad(k.transpose(0, 2, 1, 3), ((0, 0), (0, 0), (0, pad), (0, 0))).reshape(B, H, nb, MOBA_BLOCK, Dh)
    vb = jnp.pad(v.transpose(0, 2, 1, 3), ((0, 0), (0, 0), (0, pad), (0, 0))).reshape(B, H, nb, MOBA_BLOCK, Dh)
    k_mean = jnp.mean(kb.astype(jnp.float32), axis=3)
    n_sel = min(MOBA_TOP_BLOCKS, nb)
    scale = Dh ** -0.5
    bias_hb = rel_bias.T.astype(jnp.float32)
    b_idx = jnp.arange(B)[:, None, None, None]
    h_idx = jnp.arange(H)[None, :, None, None]
    blk_ids = jnp.arange(nb)
    offs = jnp.arange(MOBA_BLOCK)

    def one_chunk(c):
        start = c * Q_CHUNK
        qc = lax.dynamic_slice_in_dim(q, start, Q_CHUNK, axis=2).astype(jnp.float32)
        q_pos = start + jnp.arange(Q_CHUNK)
        q_blk = start // MOBA_BLOCK
        blk_score = jnp.einsum('bhqd,bhnd->bhqn', qc, k_mean)
        blk_score = jnp.where(blk_ids < q_blk, blk_score, NEG_INF)
        _, sel = lax.top_k(blk_score, n_sel)
        sel_ok = sel < q_blk
        k_sel = kb[b_idx, h_idx, sel].astype(jnp.float32)
        v_sel = vb[b_idx, h_idx, sel].astype(jnp.float32)
        k_pos_sel = sel[..., None] * MOBA_BLOCK + offs
        bias_sel = bias_hb[h_idx[..., None], t5_bucket(q_pos[:, None, None] - k_pos_sel)]
        logit_sel = jnp.einsum('bhqd,bhqskd->bhqsk', qc, k_sel) * scale + bias_sel
        logit_sel = jnp.where(sel_ok[..., None], logit_sel, NEG_INF)
        k_own = lax.dynamic_index_in_dim(kb, q_blk, axis=2, keepdims=False).astype(jnp.float32)
        v_own = lax.dynamic_index_in_dim(vb, q_blk, axis=2, keepdims=False).astype(jnp.float32)
        rel_own = q_pos[:, None] - (q_blk * MOBA_BLOCK + offs)[None, :]
        logit_own = jnp.einsum('bhqd,bhkd->bhqk', qc, k_own) * scale + bias_hb[:, t5_bucket(rel_own)]
        logit_own = jnp.where(rel_own >= 0, logit_own, NEG_INF)
        logits = jnp.concatenate([logit_sel.reshape(B, H, Q_CHUNK, n_sel * MOBA_BLOCK), logit_own], axis=-1)
        p = jax.nn.softmax(logits, axis=-1)
        p_sel = p[..., :n_sel * MOBA_BLOCK].reshape(B, H, Q_CHUNK, n_sel, MOBA_BLOCK)
        p_own = p[..., n_sel * MOBA_BLOCK:]
        out = (jnp.einsum('bhqsk,bhqskd->bhqd', p_sel, v_sel)
               + jnp.einsum('bhqk,bhkd->bhqd', p_own, v_own))
        return out.astype(q.dtype)

    out = lax.map(one_chunk, jnp.arange(T // Q_CHUNK))
    return out.transpose(1, 0, 3, 2, 4).reshape(B, T, H * Dh)


def memory_cross_attention(h, mem_n, w_q, w_kv, w_o):
    B, T, _ = h.shape
    M = mem_n.shape[1]
    q = (h @ w_q).reshape(B, T, XATTN_HEADS, XATTN_HEAD_DIM)
    k, v = jnp.split(mem_n @ w_kv, 2, axis=-1)
    k = k.reshape(B, M, XATTN_HEADS, XATTN_HEAD_DIM)
    v = v.reshape(B, M, XATTN_HEADS, XATTN_HEAD_DIM)
    logits = jnp.einsum('bthd,bmhd->bhtm', q, k).astype(jnp.float32) * (XATTN_HEAD_DIM ** -0.5)
    p = jax.nn.softmax(logits, axis=-1).astype(v.dtype)
    o = jnp.einsum('bhtm,bmhd->bthd', p, v).reshape(B, T, XATTN_WIDTH)
    return o @ w_o


def hierarchical_moe(h, w_router_group, w_router_expert, w_exp_gate, w_exp_up, w_exp_down):
    B, T, D = h.shape
    n_tok = B * T
    xf = h.reshape(n_tok, D)
    g_prob = jax.nn.softmax((xf @ w_router_group).astype(jnp.float32), axis=-1)
    g_top_p, g_sel = lax.top_k(g_prob, 1)
    e_logits = (xf @ w_router_expert).astype(jnp.float32).reshape(n_tok, N_GROUPS, EXPERTS_PER_GROUP)
    e_in_group = e_logits[jnp.arange(n_tok), g_sel[:, 0]]
    top_v, top_i = lax.top_k(e_in_group, TOP_K_EXPERTS)
    gate = jax.nn.softmax(top_v, axis=-1) * g_top_p
    expert = g_sel * EXPERTS_PER_GROUP + top_i
    n_assign = n_tok * TOP_K_EXPERTS
    e_flat = expert.reshape(n_assign)
    tok_flat = jnp.repeat(jnp.arange(n_tok, dtype=jnp.int32), TOP_K_EXPERTS)
    w_flat = gate.reshape(n_assign)
    order = jnp.argsort(e_flat)
    e_s, tok_s, w_s = e_flat[order], tok_flat[order], w_flat[order]
    counts = jnp.bincount(e_flat, length=N_EXPERTS)
    start = jnp.cumsum(counts) - counts
    padded = (counts + MOE_BLOCK - 1) // MOE_BLOCK * MOE_BLOCK
    p_end = jnp.cumsum(padded)
    p_start = p_end - padded
    dest = p_start[e_s] + jnp.arange(n_assign) - start[e_s]
    n_blocks = (n_assign + N_EXPERTS * (MOE_BLOCK - 1) + MOE_BLOCK - 1) // MOE_BLOCK
    cap = n_blocks * MOE_BLOCK
    buf_tok = jnp.zeros((cap,), jnp.int32).at[dest].set(tok_s)
    buf_w = jnp.zeros((cap,), jnp.float32).at[dest].set(w_s)
    block_expert = jnp.minimum(jnp.searchsorted(p_end, jnp.arange(n_blocks) * MOE_BLOCK, side='right'), N_EXPERTS - 1)

    def expert_block(args):
        toks, e = args
        xb = xf[toks]
        hid = jax.nn.silu(xb @ w_exp_gate[e]) * (xb @ w_exp_up[e])
        return hid @ w_exp_down[e]

    y_blocks = lax.map(expert_block, (buf_tok.reshape(n_blocks, MOE_BLOCK), block_expert))
    y = y_blocks.reshape(cap, D) * buf_w[:, None].astype(h.dtype)
    return jax.ops.segment_sum(y, buf_tok, num_segments=n_tok).reshape(B, T, D)


def setup_inputs(seed: int = 0) -> dict:
    key = jax.random.key(seed)
    ks = jax.random.split(key, 32)
    f32 = jnp.float32

    def nrm(k, shape, scale):
        return jax.random.normal(k, shape, f32) * scale

    def gain(k, shape, base=1.0):
        return base + 0.02 * jax.random.normal(k, shape, f32)

    L, D = DEPTH, D_MODEL
    return {
        'x': nrm(ks[0], (BATCH, SEQ, D), 1.0),
        'mem': nrm(ks[1], (BATCH, N_MEM, D), 1.0),
        'rel_bias': nrm(ks[2], (REL_BUCKETS, MOBA_HEADS), 0.5),
        'mem_norm': gain(ks[3], (D,)),
        'norm_mix': gain(ks[4], (L, D)),
        'w_in': nrm(ks[5], (L, D, IN_COLS), D ** -0.5),
        'tshift_mu': jax.random.uniform(ks[6], (L, RWKV_COLS), f32),
        'decay_w0': jax.random.uniform(ks[7], (L, RWKV_WIDTH), f32, minval=-6.0, maxval=1.0),
        'decay_up': nrm(ks[8], (L, DECAY_LORA, RWKV_WIDTH), 0.5 * DECAY_LORA ** -0.5),
        'iclr_a0': nrm(ks[9], (L, RWKV_WIDTH), 0.1),
        'iclr_up': nrm(ks[10], (L, ICLR_LORA, RWKV_WIDTH), ICLR_LORA ** -0.5),
        'gate_up': nrm(ks[11], (L, GATE_LORA, RWKV_WIDTH), GATE_LORA ** -0.5),
        'k_k': gain(ks[12], (L, RWKV_WIDTH), 0.85),
        'k_a': gain(ks[13], (L, RWKV_WIDTH)),
        'r_k': nrm(ks[14], (L, RWKV_HEADS, RWKV_HEAD_DIM), 0.1),
        'ln_x_w': gain(ks[15], (L, RWKV_WIDTH)),
        'ln_x_b': nrm(ks[16], (L, RWKV_WIDTH), 0.02),
        'w_o_rwkv': nrm(ks[17], (L, RWKV_WIDTH, D), RWKV_WIDTH ** -0.5),
        'w_o_moba': nrm(ks[18], (L, MOBA_WIDTH, D), MOBA_WIDTH ** -0.5),
        'w_out': nrm(ks[19], (L, D, D), D ** -0.5),
        'norm_xattn': gain(ks[20], (L, D)),
        'w_q_x': nrm(ks[21], (L, D, XATTN_WIDTH), D ** -0.5),
        'w_kv_x': nrm(ks[22], (L, D, 2 * XATTN_WIDTH), D ** -0.5),
        'w_o_x': nrm(ks[23], (L, XATTN_WIDTH, D), XATTN_WIDTH ** -0.5),
        'norm_ffn': gain(ks[24], (L, D)),
        'w_router_group': nrm(ks[25], (L, D, N_GROUPS), D ** -0.5),
        'w_router_expert': nrm(ks[26], (L, D, N_EXPERTS), D ** -0.5),
        'w_exp_gate': nrm(ks[27], (L, N_EXPERTS, D, D_EXPERT), D ** -0.5),
        'w_exp_up': nrm(ks[28], (L, N_EXPERTS, D, D_EXPERT), D ** -0.5),
        'w_exp_down': nrm(ks[29], (L, N_EXPERTS, D_EXPERT, D), D_EXPERT ** -0.5),
        'norm_final': gain(ks[30], (D,)),
    }


def reference(x, mem, rel_bias, mem_norm, norm_mix, w_in, tshift_mu, decay_w0, decay_up, iclr_a0,
              iclr_up, gate_up, k_k, k_a, r_k, ln_x_w, ln_x_b, w_o_rwkv, w_o_moba, w_out,
              norm_xattn, w_q_x, w_kv_x, w_o_x, norm_ffn, w_router_group, w_router_expert,
              w_exp_gate, w_exp_up, w_exp_down, norm_final):
    B, T, _ = x.shape
    mem_n = rmsnorm(mem, mem_norm)
    for l in range(DEPTH):
        h = rmsnorm(x, norm_mix[l])
        u = h @ w_in[l]
        u_rwkv, u_moba, u_gate = jnp.split(u, [RWKV_COLS, RWKV_COLS + 3 * MOBA_WIDTH], axis=-1)
        o_a = rwkv7_mixer(u_rwkv, tshift_mu[l], decay_w0[l], decay_up[l], iclr_a0[l], iclr_up[l],
                          gate_up[l], k_k[l], k_a[l], r_k[l], ln_x_w[l], ln_x_b[l])
        q, k, v = jnp.split(u_moba, 3, axis=-1)
        o_b = moba_attention(q.reshape(B, T, MOBA_HEADS, MOBA_HEAD_DIM),
                             k.reshape(B, T, MOBA_HEADS, MOBA_HEAD_DIM),
                             v.reshape(B, T, MOBA_HEADS, MOBA_HEAD_DIM), rel_bias)
        gate_a, gate_b = jnp.split(u_gate, N_BRANCHES, axis=-1)
        merged = jax.nn.sigmoid(gate_a) * (o_a @ w_o_rwkv[l]) + jax.nn.sigmoid(gate_b) * (o_b @ w_o_moba[l])
        x = x + merged @ w_out[l]
        x = x + memory_cross_attention(rmsnorm(x, norm_xattn[l]), mem_n, w_q_x[l], w_kv_x[l], w_o_x[l])
        x = x + hierarchical_moe(rmsnorm(x, norm_ffn[l]), w_router_group[l], w_router_expert[l],
                                 w_exp_gate[l], w_exp_up[l], w_exp_down[l])
    return rmsnorm(x, norm_final)
```

```python
import functools
import math

import jax
import jax.numpy as jnp
import numpy as np
from jax import lax
from jax.experimental import pallas as pl
from jax.experimental.pallas import tpu as pltpu

F32 = jnp.float32
BF16 = jnp.bfloat16

D_MODEL = 1024
N_MEM = 256
NORM_EPS = 1e-6
NEG_INF = -1e30

HEADS = 8
HEAD_DIM = 64
WIDTH = HEADS * HEAD_DIM
DECAY_LORA = 64
ICLR_LORA = 64
GATE_LORA = 128
LORA_COLS = DECAY_LORA + ICLR_LORA + GATE_LORA
RWKV_COLS = 3 * WIDTH + LORA_COLS
QKV_COLS = 3 * WIDTH
GATE_COLS = 2 * D_MODEL
IN_COLS = RWKV_COLS + QKV_COLS + GATE_COLS
LN_X_EPS = 64e-5
KK_EPS = 1e-12

MOBA_BLOCK = 256
MOBA_TOP = 3
REL_BUCKETS = 32
REL_MAX_DISTANCE = 4096
N_BIAS_TILES = 14

XATTN_HEADS = 4
XATTN_HEAD_DIM = 128
XATTN_WIDTH = XATTN_HEADS * XATTN_HEAD_DIM

N_GROUPS = 4
EXPERTS_PER_GROUP = 8
N_EXPERTS = N_GROUPS * EXPERTS_PER_GROUP
D_EXPERT = 512
MOE_BLOCK = 128

CHUNK = 64
LANES = 128
VMEM_LIMIT = 48 * 1024 * 1024

HI = lax.Precision.HIGHEST


def _cparams(sem):
    return pltpu.CompilerParams(dimension_semantics=sem, vmem_limit_bytes=VMEM_LIMIT)


def _bdot(a, b):
    return jnp.dot(a.astype(BF16), b.astype(BF16), preferred_element_type=F32)


def _bdot_nt(a, b):
    return lax.dot_general(a.astype(BF16), b.astype(BF16), (((1,), (1,)), ((), ())),
                           preferred_element_type=F32)


def _bdot_tn(a, b):
    return lax.dot_general(a.astype(BF16), b.astype(BF16), (((0,), (0,)), ((), ())),
                           preferred_element_type=F32)


def _rms(x, g):
    return x * lax.rsqrt(jnp.mean(x * x, axis=-1, keepdims=True) + NORM_EPS) * g


def _sigmoid(x):
    return 1.0 / (1.0 + jnp.exp(-x))


def _inproj_kernel(x_ref, g_ref, w_ref, ur_ref, qkv_ref, gate_ref):
    h = _rms(x_ref[...], g_ref[...]).astype(BF16)
    step = 256
    for c0 in range(0, RWKV_COLS, step):
        ur_ref[:, c0:c0 + step] = jnp.dot(h, w_ref[:, c0:c0 + step], preferred_element_type=F32)
    for c0 in range(0, QKV_COLS, step):
        o = jnp.dot(h, w_ref[:, RWKV_COLS + c0:RWKV_COLS + c0 + step], preferred_element_type=F32)
        if c0 < WIDTH:
            o = o * (HEAD_DIM ** -0.5)
        qkv_ref[:, c0:c0 + step] = o.astype(BF16)
    base = RWKV_COLS + QKV_COLS
    for c0 in range(0, GATE_COLS, step):
        o = jnp.dot(h, w_ref[:, base + c0:base + c0 + step], preferred_element_type=F32)
        gate_ref[:, c0:c0 + step] = o.astype(BF16)


def _inproj(x2d, g, w_bf):
    T = x2d.shape[0]
    tm = 256
    return pl.pallas_call(
        _inproj_kernel,
        grid=(T // tm,),
        in_specs=[
            pl.BlockSpec((tm, D_MODEL), lambda i: (i, 0)),
            pl.BlockSpec((1, D_MODEL), lambda i: (0, 0)),
            pl.BlockSpec((D_MODEL, IN_COLS), lambda i: (0, 0)),
        ],
        out_specs=[
            pl.BlockSpec((tm, RWKV_COLS), lambda i: (i, 0)),
            pl.BlockSpec((tm, QKV_COLS), lambda i: (i, 0)),
            pl.BlockSpec((tm, GATE_COLS), lambda i: (i, 0)),
        ],
        out_shape=[
            jax.ShapeDtypeStruct((T, RWKV_COLS), F32),
            jax.ShapeDtypeStruct((T, QKV_COLS), BF16),
            jax.ShapeDtypeStruct((T, GATE_COLS), BF16),
        ],
        compiler_params=_cparams(("parallel",)),
        name="inproj",
    )(x2d, g, w_bf)


def _head_ones():
    r = lax.broadcasted_iota(jnp.int32, (LANES, LANES), 0) // HEAD_DIM
    c = lax.broadcasted_iota(jnp.int32, (LANES, LANES), 1) // HEAD_DIM
    return jnp.where(r == c, 1.0, 0.0).astype(BF16)


def _head_sum(x, ones):
    parts = [_bdot(x[:, c:c + LANES], ones) for c in range(0, WIDTH, LANES)]
    return jnp.concatenate(parts, axis=-1)


def _rwkv_prep_kernel(u_ref, up_ref, mu_ref, w0_ref, a0_ref, kk_ref, ka_ref, rk_ref, wl_ref,
                      r_ref, lw_ref, k_ref, v_ref, a_ref, b_ref, g_ref, bo_ref):
    i = pl.program_id(0)
    u = u_ref[...]
    tm = u.shape[0]
    prev_last = up_ref[7:8, :] * jnp.where(i > 0, 1.0, 0.0)
    rolled = pltpu.roll(u, 1, 0)
    row = lax.broadcasted_iota(jnp.int32, u.shape, 0)
    u_prev = jnp.where(row == 0, prev_last, rolled)
    u = u + mu_ref[...] * (u_prev - u)
    r = u[:, 0:WIDTH]
    k = u[:, WIDTH:2 * WIDTH]
    v = u[:, 2 * WIDTH:3 * WIDTH]
    lo = u[:, 3 * WIDTH:3 * WIDTH + LORA_COLS]
    lane = lax.broadcasted_iota(jnp.int32, lo.shape, 1)
    act = jnp.where(lane < DECAY_LORA, jnp.tanh(lo),
                    jnp.where(lane < DECAY_LORA + ICLR_LORA, lo, _sigmoid(lo)))
    up = jnp.dot(act, wl_ref[...], preferred_element_type=F32, precision=HI)
    z = -(w0_ref[...] + up[:, 0:WIDTH])
    softplus = jnp.maximum(z, 0.0) + jnp.log(1.0 + jnp.exp(-jnp.abs(z)))
    w_log = -softplus - 0.5
    lw = -jnp.exp(w_log)
    iclr = _sigmoid(a0_ref[...] + up[:, WIDTH:2 * WIDTH])
    g = up[:, 2 * WIDTH:3 * WIDTH]
    ones = _head_ones()
    kk = k * kk_ref[...]
    kk = kk * lax.rsqrt(_head_sum(kk * kk, ones) + KK_EPS)
    k2 = k * (1.0 + (iclr - 1.0) * ka_ref[...])
    bonus = _head_sum(r * k2 * rk_ref[...], ones) * v
    r_ref[...] = r
    lw_ref[...] = lw
    k_ref[...] = k2
    v_ref[...] = v
    a_ref[...] = -kk
    b_ref[...] = kk * iclr
    g_ref[...] = g
    bo_ref[...] = bonus


def _rwkv_prep(ur, mu, w0, a0, k_k, k_a, r_k, w_lora):
    T = ur.shape[0]
    tm = 256
    row = lambda w: pl.BlockSpec((1, w), lambda i: (0, 0))
    out = pl.BlockSpec((tm, WIDTH), lambda i: (i, 0))
    return pl.pallas_call(
        _rwkv_prep_kernel,
        grid=(T // tm,),
        in_specs=[
            pl.BlockSpec((tm, RWKV_COLS), lambda i: (i, 0)),
            pl.BlockSpec((8, RWKV_COLS), lambda i: (jnp.maximum(i * (tm // 8) - 1, 0), 0)),
            row(RWKV_COLS), row(WIDTH), row(WIDTH), row(WIDTH), row(WIDTH), row(WIDTH),
            pl.BlockSpec((LORA_COLS, 3 * WIDTH), lambda i: (0, 0)),
        ],
        out_specs=[out] * 8,
        out_shape=[jax.ShapeDtypeStruct((T, WIDTH), F32)] * 8,
        compiler_params=_cparams(("parallel",)),
        name="rwkv_prep",
    )(ur, ur, mu, w0, a0, k_k, k_a, r_k, w_lora)


def _rwkv_core_kernel(r_ref, lw_ref, k_ref, v_ref, a_ref, b_ref, g_ref, bo_ref, lnw_ref, lnb_ref,
                      o_ref, s_ref):
    c = pl.program_id(0)

    @pl.when(c == 0)
    def _():
        s_ref[...] = jnp.zeros_like(s_ref)

    C = CHUNK
    ri = lax.broadcasted_iota(jnp.int32, (C, C), 0)
    ci = lax.broadcasted_iota(jnp.int32, (C, C), 1)
    tril_incl = ri >= ci
    tril_strict = ri > ci
    lw = lw_ref[...]
    cum = jnp.dot(jnp.where(tril_incl, 1.0, 0.0), lw, preferred_element_type=F32, precision=HI)
    lam = jnp.exp(cum)
    inv_lam = jnp.exp(-cum)
    r_t = r_ref[...] * lam
    a_t = a_ref[...] * jnp.exp(cum - lw)
    b_t = b_ref[...] * inv_lam
    k_t = k_ref[...] * inv_lam
    tot = cum[C - 1:C, :]
    rest = jnp.exp(tot - cum)
    b_h = b_ref[...] * rest
    k_h = k_ref[...] * rest
    lam_c = jnp.exp(tot)
    v_all = v_ref[...]
    eye = jnp.where(ri == ci, 1.0, 0.0)
    outs = []
    for h in range(HEADS):
        sl = slice(h * HEAD_DIM, (h + 1) * HEAD_DIM)
        at, rt, bt, kt = a_t[:, sl], r_t[:, sl], b_t[:, sl], k_t[:, sl]
        bh, kh, vv = b_h[:, sl], k_h[:, sl], v_all[:, sl]
        ar = jnp.concatenate([at, rt], axis=0)
        gb = _bdot_nt(ar, bt)
        gk = _bdot_nt(ar, kt)
        a_ab = jnp.where(tril_strict, gb[0:C], 0.0)
        a_ak = jnp.where(tril_strict, gk[0:C], 0.0)
        a_rb = jnp.where(tril_incl, gb[C:2 * C], 0.0)
        a_rk = jnp.where(tril_incl, gk[C:2 * C], 0.0)
        npow = a_ab
        tinv = eye + a_ab
        for _ in range(5):
            npow = _bdot(npow, npow)
            tinv = tinv + _bdot(npow, tinv)
        w_m = _bdot(tinv, at)
        u0 = _bdot(tinv, _bdot(a_ak, vv))
        q_m = rt + _bdot(a_rb, w_m)
        y0 = _bdot(a_rb, u0) + _bdot(a_rk, vv)
        m_k = _bdot_tn(w_m, bh)
        n0 = _bdot_tn(u0, bh) + _bdot_tn(vv, kh)
        s_old = s_ref[h]
        y = _bdot_nt(q_m, s_old) + y0
        s_ref[h] = s_old * lam_c[:, sl] + _bdot(s_old, m_k) + n0
        mean = jnp.mean(y, axis=-1, keepdims=True)
        var = jnp.mean(jnp.square(y - mean), axis=-1, keepdims=True)
        outs.append((y - mean) * lax.rsqrt(var + LN_X_EPS))
    yn = jnp.concatenate(outs, axis=-1)
    yn = yn * lnw_ref[...] + lnb_ref[...] + bo_ref[...]
    o_ref[...] = (yn * g_ref[...]).astype(BF16)


def _rwkv_core(r, lw, k, v, a, b, g, bonus, ln_w, ln_b):
    T = r.shape[0]
    blk = pl.BlockSpec((CHUNK, WIDTH), lambda c: (c, 0))
    row = pl.BlockSpec((1, WIDTH), lambda c: (0, 0))
    return pl.pallas_call(
        _rwkv_core_kernel,
        grid=(T // CHUNK,),
        in_specs=[blk] * 8 + [row, row],
        out_specs=blk,
        out_shape=jax.ShapeDtypeStruct((T, WIDTH), BF16),
        scratch_shapes=[pltpu.VMEM((HEADS, HEAD_DIM, HEAD_DIM), F32)],
        compiler_params=_cparams(("arbitrary",)),
        name="rwkv_core",
    )(r, lw, k, v, a, b, g, bonus, ln_w, ln_b)


def _kmean_kernel(k_ref, o_ref):
    o_ref[0] = jnp.mean(k_ref[...].astype(F32), axis=0, keepdims=True)


def _kmean(qkv):
    T = qkv.shape[0]
    nb = T // MOBA_BLOCK
    return pl.pallas_call(
        _kmean_kernel,
        grid=(nb,),
        in_specs=[pl.BlockSpec((MOBA_BLOCK, WIDTH), lambda j: (j, 1))],
        out_specs=pl.BlockSpec((1, 1, WIDTH), lambda j: (j, 0, 0)),
        out_shape=jax.ShapeDtypeStruct((nb, 1, WIDTH), F32),
        compiler_params=_cparams(("parallel",)),
        name="kmean",
    )(qkv)


def _t5_bucket(dist):
    n = jnp.maximum(dist, 0)
    max_exact = REL_BUCKETS // 2
    nf = jnp.maximum(n, max_exact).astype(jnp.float32)
    large = max_exact + (jnp.log(nf / max_exact) / math.log(REL_MAX_DISTANCE / max_exact)
                         * (REL_BUCKETS - max_exact)).astype(jnp.int32)
    large = jnp.minimum(large, REL_BUCKETS - 1)
    return jnp.where(n < max_exact, n, large)


def _bucket_tiles():
    i = jnp.arange(MOBA_BLOCK)[:, None]
    j = jnp.arange(MOBA_BLOCK)[None, :]
    d = jnp.arange(N_BIAS_TILES)[:, None, None]
    dist = d * MOBA_BLOCK + i - j
    bucket = _t5_bucket(dist)
    bucket = jnp.where(d == N_BIAS_TILES - 1, REL_BUCKETS - 1, bucket)
    return jnp.where(dist < 0, -1, bucket).astype(jnp.int32)


def _bias_tiles_kernel(idx_ref, rb_ref, o_ref):
    h = pl.program_id(0)
    idx = idx_ref[0]
    acc = jnp.where(idx < 0, NEG_INF, 0.0)
    for bkt in range(REL_BUCKETS):
        acc = jnp.where(idx == bkt, rb_ref[bkt, h], acc)
    o_ref[0, 0] = acc


def _bias_tiles(rel_bias):
    idx = _bucket_tiles()
    return pl.pallas_call(
        _bias_tiles_kernel,
        grid=(HEADS, N_BIAS_TILES),
        in_specs=[
            pl.BlockSpec((1, MOBA_BLOCK, MOBA_BLOCK), lambda h, d: (d, 0, 0)),
            pl.BlockSpec(memory_space=pltpu.SMEM),
        ],
        out_specs=pl.BlockSpec((1, 1, MOBA_BLOCK, MOBA_BLOCK), lambda h, d: (h, d, 0, 0)),
        out_shape=jax.ShapeDtypeStruct((HEADS, N_BIAS_TILES, MOBA_BLOCK, MOBA_BLOCK), F32),
        compiler_params=_cparams(("parallel", "parallel")),
        name="bias_tiles",
    )(idx, rel_bias)


def _moba_kernel(q_ref, k_ref, v_ref, r_ref, bias_ref, o_ref):
    qb = pl.program_id(1)
    B = MOBA_BLOCK
    qblk = q_ref[...]
    lane = lax.broadcasted_iota(jnp.int32, (B, LANES), 1)
    big = jnp.int32(1 << 20)
    results = []
    for h2 in range(2):
        off = HEAD_DIM * (1 - h2)
        inr = (lane >= off) & (lane < off + HEAD_DIM)
        jl = lane - off
        valid = inr & (jl < qb)
        sc = jnp.dot(qblk, r_ref[0, h2], preferred_element_type=F32)
        s = jnp.where(valid, sc, NEG_INF)
        s = jnp.where(inr, s, -jnp.inf)
        sel = jnp.zeros((B, LANES), jnp.bool_)
        for _ in range(MOBA_TOP):
            m = jnp.max(s, axis=-1, keepdims=True)
            idx = jnp.min(jnp.where(s == m, lane, big), axis=-1, keepdims=True)
            pick = lane == idx
            sel = jnp.logical_or(sel, pick)
            s = jnp.where(pick, -jnp.inf, s)
        sel = jnp.logical_and(sel, valid)
        q_aug = jnp.where(inr, jnp.where(sel, 0.0, NEG_INF).astype(BF16), qblk)

        def tile(j, d, own=False, q_aug=q_aug, inr=inr, off=off, h2=h2):
            start = pl.multiple_of(j * B, B)
            kblk = k_ref[pl.ds(start, B), :]
            vblk = v_ref[pl.ds(start, B), :]
            if own:
                hot = jnp.zeros_like(kblk)
            else:
                hot = jnp.where(lane == off + j, 1.0, 0.0).astype(BF16)
            k_aug = jnp.where(inr, hot, kblk)
            v_aug = jnp.where(inr, jnp.ones_like(vblk), vblk)
            s_t = lax.dot_general(q_aug, k_aug, (((1,), (1,)), ((), ())),
                                  preferred_element_type=F32)
            return s_t + bias_ref[h2, d], v_aug

        s0, v0 = tile(qb, 0, own=True)
        m0 = jnp.max(s0, axis=-1, keepdims=True)
        p0 = jnp.exp(s0 - m0)
        acc0 = jnp.dot(p0.astype(BF16), v0, preferred_element_type=F32)

        def body(j, carry):
            m_prev, acc = carry
            d = jnp.minimum(qb - j, N_BIAS_TILES - 1)
            s_t, v_aug = tile(j, d)
            m_new = jnp.maximum(m_prev, jnp.max(s_t, axis=-1, keepdims=True))
            alpha = jnp.exp(m_prev - m_new)
            p = jnp.exp(s_t - m_new)
            acc = acc * alpha + jnp.dot(p.astype(BF16), v_aug, preferred_element_type=F32)
            return m_new, acc

        _, acc = lax.fori_loop(0, qb, body, (m0, acc0))
        results.append(acc / pltpu.roll(acc, HEAD_DIM, 1))
    o_ref[...] = jnp.where(lane < HEAD_DIM, results[0], results[1]).astype(BF16)


def _moba(qkv, r_mats, bias_tiles):
    T = qkv.shape[0]
    nb = T // MOBA_BLOCK
    npair = HEADS // 2
    kcol = WIDTH // LANES
    return pl.pallas_call(
        _moba_kernel,
        grid=(npair, nb),
        in_specs=[
            pl.BlockSpec((MOBA_BLOCK, LANES), lambda p, qb: (qb, p)),
            pl.BlockSpec((T, LANES), lambda p, qb: (0, kcol + p)),
            pl.BlockSpec((T, LANES), lambda p, qb: (0, 2 * kcol + p)),
            pl.BlockSpec((1, 2, LANES, LANES), lambda p, qb: (p, 0, 0, 0)),
            pl.BlockSpec((2, N_BIAS_TILES, MOBA_BLOCK, MOBA_BLOCK), lambda p, qb: (p, 0, 0, 0)),
        ],
        out_specs=pl.BlockSpec((MOBA_BLOCK, LANES), lambda p, qb: (qb, p)),
        out_shape=jax.ShapeDtypeStruct((T, WIDTH), BF16),
        compiler_params=_cparams(("arbitrary", "arbitrary")),
        name="moba",
    )(qkv, qkv, qkv, r_mats, bias_tiles)


def _score_mats(kmean):
    nb = kmean.shape[0]
    km = kmean.reshape(nb, HEADS, HEAD_DIM).transpose(1, 2, 0)
    km = jnp.pad(km, ((0, 0), (0, 0), (0, HEAD_DIM - nb)))
    z = jnp.zeros((HEADS // 2, HEAD_DIM, HEAD_DIM), F32)
    even = jnp.concatenate([jnp.concatenate([z, km[0::2]], axis=2),
                            jnp.concatenate([z, z], axis=2)], axis=1)
    odd = jnp.concatenate([jnp.concatenate([z, z], axis=2),
                           jnp.concatenate([km[1::2], z], axis=2)], axis=1)
    return jnp.stack([even, odd], axis=1).astype(BF16)


def _memkv_kernel(m_ref, g_ref, w_ref, o_ref):
    h = _rms(m_ref[...], g_ref[...]).astype(BF16)
    o_ref[...] = jnp.dot(h, w_ref[...], preferred_element_type=F32).astype(BF16)


def _memkv(mem2d, g, w_bf):
    return pl.pallas_call(
        _memkv_kernel,
        out_shape=jax.ShapeDtypeStruct((N_MEM, 2 * XATTN_WIDTH), BF16),
        compiler_params=pltpu.CompilerParams(vmem_limit_bytes=VMEM_LIMIT),
        name="memkv",
    )(mem2d, g, w_bf)


def _merge_kernel(x_ref, oa_ref, ob_ref, gt_ref, woa_ref, wob_ref, wout_ref, nx_ref, wq_ref,
                  kv_ref, wo_ref, nf_ref, wr_ref, x2_ref, h3_ref, rt_ref):
    pa = jnp.dot(oa_ref[...], woa_ref[...], preferred_element_type=F32)
    pb = jnp.dot(ob_ref[...], wob_ref[...], preferred_element_type=F32)
    ga = gt_ref[:, 0:D_MODEL].astype(F32)
    gb = gt_ref[:, D_MODEL:2 * D_MODEL].astype(F32)
    merged = _sigmoid(ga) * pa + _sigmoid(gb) * pb
    x1 = x_ref[...] + jnp.dot(merged.astype(BF16), wout_ref[...], preferred_element_type=F32)
    h2 = _rms(x1, nx_ref[...]).astype(BF16)
    q = jnp.dot(h2, wq_ref[...], preferred_element_type=F32).astype(BF16)
    heads = []
    for h in range(XATTN_HEADS):
        sl = slice(h * XATTN_HEAD_DIM, (h + 1) * XATTN_HEAD_DIM)
        km = kv_ref[:, sl]
        vm = kv_ref[:, XATTN_WIDTH + h * XATTN_HEAD_DIM:XATTN_WIDTH + (h + 1) * XATTN_HEAD_DIM]
        lg = lax.dot_general(q[:, sl], km, (((1,), (1,)), ((), ())),
                             preferred_element_type=F32) * (XATTN_HEAD_DIM ** -0.5)
        e = jnp.exp(lg - jnp.max(lg, axis=-1, keepdims=True))
        p = e / jnp.sum(e, axis=-1, keepdims=True)
        heads.append(jnp.dot(p.astype(BF16), vm, preferred_element_type=F32))
    o = jnp.concatenate(heads, axis=-1).astype(BF16)
    x2 = x1 + jnp.dot(o, wo_ref[...], preferred_element_type=F32)
    x2_ref[...] = x2
    h3 = _rms(x2, nf_ref[...])
    h3_ref[...] = h3
    lg = jnp.dot(h3, wr_ref[...], preferred_element_type=F32, precision=HI)
    lane = lax.broadcasted_iota(jnp.int32, lg.shape, 1)
    big = jnp.int32(1 << 20)
    is_g = lane < N_GROUPS
    gl = jnp.where(is_g, lg, -jnp.inf)
    gmax = jnp.max(gl, axis=-1, keepdims=True)
    gsel = jnp.min(jnp.where(gl == gmax, lane, big), axis=-1, keepdims=True)
    p_top = 1.0 / jnp.sum(jnp.where(is_g, jnp.exp(gl - gmax), 0.0), axis=-1, keepdims=True)
    e_id = lane - N_GROUPS
    in_grp = (e_id >= gsel * EXPERTS_PER_GROUP) & (e_id < (gsel + 1) * EXPERTS_PER_GROUP)
    el = jnp.where(in_grp, lg, -jnp.inf)
    v1 = jnp.max(el, axis=-1, keepdims=True)
    i1 = jnp.min(jnp.where(el == v1, lane, big), axis=-1, keepdims=True)
    el2 = jnp.where(lane == i1, -jnp.inf, el)
    v2 = jnp.max(el2, axis=-1, keepdims=True)
    i2 = jnp.min(jnp.where(el2 == v2, lane, big), axis=-1, keepdims=True)
    e21 = jnp.exp(v2 - v1)
    g1 = p_top / (1.0 + e21)
    g2 = p_top * e21 / (1.0 + e21)
    rt = jnp.where(lane == 0, (i1 - N_GROUPS).astype(F32),
                   jnp.where(lane == 1, (i2 - N_GROUPS).astype(F32),
                             jnp.where(lane == 2, g1, jnp.where(lane == 3, g2, 0.0))))
    rt_ref[...] = rt


def _merge(x2d, o_a, o_b, gates, w_oa, w_ob, w_out, n_x, w_q, kv, w_o, n_f, w_r):
    T = x2d.shape[0]
    tm = 256
    full = lambda a: pl.BlockSpec(a.shape, lambda i: (0,) * a.ndim)
    tile = lambda w: pl.BlockSpec((tm, w), lambda i: (i, 0))
    return pl.pallas_call(
        _merge_kernel,
        grid=(T // tm,),
        in_specs=[tile(D_MODEL), tile(WIDTH), tile(WIDTH), tile(GATE_COLS), full(w_oa), full(w_ob),
                  full(w_out), full(n_x), full(w_q), full(kv), full(w_o), full(n_f), full(w_r)],
        out_specs=[tile(D_MODEL), tile(D_MODEL), tile(LANES)],
        out_shape=[jax.ShapeDtypeStruct((T, D_MODEL), F32),
                   jax.ShapeDtypeStruct((T, D_MODEL), F32),
                   jax.ShapeDtypeStruct((T, LANES), F32)],
        compiler_params=_cparams(("parallel",)),
        name="merge",
    )(x2d, o_a, o_b, gates, w_oa, w_ob, w_out, n_x, w_q, kv, w_o, n_f, w_r)


def _moe_kernel(be_ref, nu_ref, tok_ref, h_hbm, wg_ref, wu_ref, wd_ref, o_ref, xb_ref, sem):
    i = pl.program_id(0)

    @pl.when(i < nu_ref[0])
    def _():
        def row_copy(r):
            t = tok_ref[i * MOE_BLOCK + r]
            return pltpu.make_async_copy(h_hbm.at[pl.ds(t, 1)], xb_ref.at[pl.ds(r, 1)], sem)

        def start(r, c):
            row_copy(r).start()
            return c

        lax.fori_loop(0, MOE_BLOCK, start, 0)

        def wait(r, c):
            row_copy(r).wait()
            return c

        lax.fori_loop(0, MOE_BLOCK, wait, 0)
        xb = xb_ref[...].astype(BF16)
        gate = jnp.dot(xb, wg_ref[0], preferred_element_type=F32)
        up = jnp.dot(xb, wu_ref[0], preferred_element_type=F32)
        hid = gate * _sigmoid(gate) * up
        o_ref[...] = jnp.dot(hid.astype(BF16), wd_ref[0], preferred_element_type=F32)

    @pl.when(i >= nu_ref[0])
    def _():
        o_ref[...] = jnp.zeros_like(o_ref)


def _moe(block_expert, n_used, buf_tok, h3, wg, wu, wd):
    cap = buf_tok.shape[0]
    n_blocks = cap // MOE_BLOCK
    grid_spec = pltpu.PrefetchScalarGridSpec(
        num_scalar_prefetch=3,
        grid=(n_blocks,),
        in_specs=[
            pl.BlockSpec(memory_space=pl.ANY),
            pl.BlockSpec((1, D_MODEL, D_EXPERT), lambda i, be, nu, tk: (be[i], 0, 0)),
            pl.BlockSpec((1, D_MODEL, D_EXPERT), lambda i, be, nu, tk: (be[i], 0, 0)),
            pl.BlockSpec((1, D_EXPERT, D_MODEL), lambda i, be, nu, tk: (be[i], 0, 0)),
        ],
        out_specs=pl.BlockSpec((MOE_BLOCK, D_MODEL), lambda i, be, nu, tk: (i, 0)),
        scratch_shapes=[pltpu.VMEM((MOE_BLOCK, D_MODEL), F32), pltpu.SemaphoreType.DMA],
    )
    return pl.pallas_call(
        _moe_kernel,
        grid_spec=grid_spec,
        out_shape=jax.ShapeDtypeStruct((cap, D_MODEL), F32),
        compiler_params=_cparams(("arbitrary",)),
        name="moe",
    )(block_expert, n_used, buf_tok, h3, wg, wu, wd)


def _moe_plan(expert, n_tok):
    n_assign = n_tok * 2
    e_flat = expert.reshape(n_assign)
    onehot = (e_flat[:, None] == jnp.arange(N_EXPERTS, dtype=jnp.int32)[None, :]).astype(jnp.int32)
    counts = jnp.sum(onehot, axis=0)
    rank = jnp.sum((jnp.cumsum(onehot, axis=0) - onehot) * onehot, axis=1)
    padded = (counts + MOE_BLOCK - 1) // MOE_BLOCK * MOE_BLOCK
    p_end = jnp.cumsum(padded)
    p_start = p_end - padded
    pos = (p_start[e_flat] + rank).astype(jnp.int32)
    n_blocks = (n_assign + N_EXPERTS * (MOE_BLOCK - 1) + MOE_BLOCK - 1) // MOE_BLOCK
    cap = n_blocks * MOE_BLOCK
    tok_flat = jnp.repeat(jnp.arange(n_tok, dtype=jnp.int32), 2)
    buf_tok = jnp.zeros((cap,), jnp.int32).at[pos].set(tok_flat)
    block_expert = jnp.minimum(
        jnp.searchsorted(p_end, jnp.arange(n_blocks, dtype=jnp.int32) * MOE_BLOCK, side='right'),
        N_EXPERTS - 1).astype(jnp.int32)
    n_used = (p_end[-1] // MOE_BLOCK).astype(jnp.int32).reshape(1)
    return pos, buf_tok, block_expert, n_used


def _final_kernel(pos_ref, x_ref, rt_ref, y_hbm, g_ref, o_ref, y1_ref, y2_ref, sem):
    i = pl.program_id(0)
    tm = x_ref.shape[0]

    def copies(r):
        p1 = pos_ref[2 * (i * tm + r)]
        p2 = pos_ref[2 * (i * tm + r) + 1]
        return (pltpu.make_async_copy(y_hbm.at[pl.ds(p1, 1)], y1_ref.at[pl.ds(r, 1)], sem),
                pltpu.make_async_copy(y_hbm.at[pl.ds(p2, 1)], y2_ref.at[pl.ds(r, 1)], sem))

    def start(r, c):
        c1, c2 = copies(r)
        c1.start()
        c2.start()
        return c

    lax.fori_loop(0, tm, start, 0)

    def wait(r, c):
        c1, c2 = copies(r)
        c1.wait()
        c2.wait()
        return c

    lax.fori_loop(0, tm, wait, 0)
    rt = rt_ref[...]
    g1 = rt[:, 2:3]
    g2 = rt[:, 3:4]
    x3 = x_ref[...] + y1_ref[...] * g1 + y2_ref[...] * g2
    o_ref[...] = _rms(x3, g_ref[...])


def _final(pos, x2, rt, y_sorted, g):
    T = x2.shape[0]
    tm = 256
    grid_spec = pltpu.PrefetchScalarGridSpec(
        num_scalar_prefetch=1,
        grid=(T // tm,),
        in_specs=[
            pl.BlockSpec((tm, D_MODEL), lambda i, ps: (i, 0)),
            pl.BlockSpec((tm, LANES), lambda i, ps: (i, 0)),
            pl.BlockSpec(memory_space=pl.ANY),
            pl.BlockSpec((1, D_MODEL), lambda i, ps: (0, 0)),
        ],
        out_specs=pl.BlockSpec((tm, D_MODEL), lambda i, ps: (i, 0)),
        scratch_shapes=[pltpu.VMEM((tm, D_MODEL), F32), pltpu.VMEM((tm, D_MODEL), F32),
                        pltpu.SemaphoreType.DMA],
    )
    return pl.pallas_call(
        _final_kernel,
        grid_spec=grid_spec,
        out_shape=jax.ShapeDtypeStruct((T, D_MODEL), F32),
        compiler_params=_cparams(("arbitrary",)),
        name="final",
    )(pos, x2, rt, y_sorted, g)


def _lora_weight(decay_up, iclr_up, gate_up):
    w = jnp.zeros((LORA_COLS, 3 * WIDTH), F32)
    w = w.at[0:DECAY_LORA, 0:WIDTH].set(decay_up)
    w = w.at[DECAY_LORA:DECAY_LORA + ICLR_LORA, WIDTH:2 * WIDTH].set(iclr_up)
    w = w.at[DECAY_LORA + ICLR_LORA:, 2 * WIDTH:].set(gate_up)
    return w


def _router_weight(w_group, w_expert):
    w = jnp.zeros((D_MODEL, LANES), F32)
    w = w.at[:, 0:N_GROUPS].set(w_group)
    return w.at[:, N_GROUPS:N_GROUPS + N_EXPERTS].set(w_expert)


def kernel(x, mem, rel_bias, mem_norm, norm_mix, w_in, tshift_mu, decay_w0, decay_up, iclr_a0,
           iclr_up, gate_up, k_k, k_a, r_k, ln_x_w, ln_x_b, w_o_rwkv, w_o_moba, w_out,
           norm_xattn, w_q_x, w_kv_x, w_o_x, norm_ffn, w_router_group, w_router_expert,
           w_exp_gate, w_exp_up, w_exp_down, norm_final):
    B, T, D = x.shape
    assert B == 1 and D == D_MODEL and T % MOBA_BLOCK == 0 and T // MOBA_BLOCK <= HEAD_DIM
    assert norm_mix.shape[0] == 1
    row = lambda a: a.reshape(1, -1)
    x2d = x.reshape(T, D)
    kv = _memkv(mem.reshape(N_MEM, D), row(mem_norm), w_kv_x[0].astype(BF16))

    ur, qkv, gates = _inproj(x2d, row(norm_mix[0]), w_in[0].astype(BF16))
    prep = _rwkv_prep(ur, row(tshift_mu[0]), row(decay_w0[0]), row(iclr_a0[0]), row(k_k[0]),
                      row(k_a[0]), row(r_k[0]), _lora_weight(decay_up[0], iclr_up[0], gate_up[0]))
    o_a = _rwkv_core(*prep, row(ln_x_w[0]), row(ln_x_b[0]))

    kmean = _kmean(qkv).reshape(T // MOBA_BLOCK, WIDTH)
    o_b = _moba(qkv, _score_mats(kmean), _bias_tiles(rel_bias))

    x2, h3, rt = _merge(x2d, o_a, o_b, gates, w_o_rwkv[0].astype(BF16), w_o_moba[0].astype(BF16),
                        w_out[0].astype(BF16), row(norm_xattn[0]), w_q_x[0].astype(BF16), kv,
                        w_o_x[0].astype(BF16), row(norm_ffn[0]),
                        _router_weight(w_router_group[0], w_router_expert[0]))

    expert = rt[:, 0:2].astype(jnp.int32)
    pos, buf_tok, block_expert, n_used = _moe_plan(expert, T)
    y_sorted = _moe(block_expert, n_used, buf_tok, h3, w_exp_gate[0].astype(BF16),
                    w_exp_up[0].astype(BF16), w_exp_down[0].astype(BF16))
    out = _final(pos, x2, rt, y_sorted, row(norm_final))
    return out.reshape(B, T, D)
```

```python
import functools
import math

import jax
import jax.numpy as jnp
import numpy as np
from jax import lax
from jax.experimental import pallas as pl
from jax.experimental.pallas import tpu as pltpu

F32 = jnp.float32
BF16 = jnp.bfloat16

D_MODEL = 1024
N_MEM = 256
NORM_EPS = 1e-6
NEG_INF = -1e30

HEADS = 8
HEAD_DIM = 64
WIDTH = HEADS * HEAD_DIM
DECAY_LORA = 64
ICLR_LORA = 64
GATE_LORA = 128
LORA_COLS = DECAY_LORA + ICLR_LORA + GATE_LORA
RWKV_COLS = 3 * WIDTH + LORA_COLS
QKV_COLS = 3 * WIDTH
GATE_COLS = 2 * D_MODEL
IN_COLS = RWKV_COLS + QKV_COLS + GATE_COLS
LN_X_EPS = 64e-5
KK_EPS = 1e-12

MOBA_BLOCK = 256
MOBA_TOP = 3
REL_BUCKETS = 32
REL_MAX_DISTANCE = 4096
N_BIAS_TILES = 14

XATTN_HEADS = 4
XATTN_HEAD_DIM = 128
XATTN_WIDTH = XATTN_HEADS * XATTN_HEAD_DIM

N_GROUPS = 4
EXPERTS_PER_GROUP = 8
N_EXPERTS = N_GROUPS * EXPERTS_PER_GROUP
D_EXPERT = 512
MOE_BLOCK = 128

CHUNK = 64
LANES = 128
VMEM_LIMIT = 48 * 1024 * 1024

HI = lax.Precision.HIGHEST
LOG2E = math.log2(math.e)


def _cparams(sem):
    return pltpu.CompilerParams(dimension_semantics=sem, vmem_limit_bytes=VMEM_LIMIT)


def _bdot(a, b):
    return jnp.dot(a.astype(BF16), b.astype(BF16), preferred_element_type=F32)


def _bdot_nt(a, b):
    return lax.dot_general(a.astype(BF16), b.astype(BF16), (((1,), (1,)), ((), ())),
                           preferred_element_type=F32)


def _bdot_tn(a, b):
    return lax.dot_general(a.astype(BF16), b.astype(BF16), (((0,), (0,)), ((), ())),
                           preferred_element_type=F32)


def _rms(x, g):
    return x * lax.rsqrt(jnp.mean(x * x, axis=-1, keepdims=True) + NORM_EPS) * g


def _sigmoid(x):
    return 1.0 / (1.0 + jnp.exp(-x))


def _inproj_kernel(x_ref, g_ref, w_ref, ur_ref, qkv_ref, gate_ref):
    h = _rms(x_ref[...], g_ref[...]).astype(BF16)
    step = 256
    for c0 in range(0, RWKV_COLS, step):
        ur_ref[:, c0:c0 + step] = jnp.dot(h, w_ref[:, c0:c0 + step], preferred_element_type=F32)
    for c0 in range(0, QKV_COLS, step):
        o = jnp.dot(h, w_ref[:, RWKV_COLS + c0:RWKV_COLS + c0 + step], preferred_element_type=F32)
        if c0 < WIDTH:
            o = o * (HEAD_DIM ** -0.5 * LOG2E)
        qkv_ref[:, c0:c0 + step] = o.astype(BF16)
    base = RWKV_COLS + QKV_COLS
    for c0 in range(0, GATE_COLS, step):
        o = jnp.dot(h, w_ref[:, base + c0:base + c0 + step], preferred_element_type=F32)
        gate_ref[:, c0:c0 + step] = o.astype(BF16)


def _inproj(x2d, g, w_bf):
    T = x2d.shape[0]
    tm = 256
    return pl.pallas_call(
        _inproj_kernel,
        grid=(T // tm,),
        in_specs=[
            pl.BlockSpec((tm, D_MODEL), lambda i: (i, 0)),
            pl.BlockSpec((1, D_MODEL), lambda i: (0, 0)),
            pl.BlockSpec((D_MODEL, IN_COLS), lambda i: (0, 0)),
        ],
        out_specs=[
            pl.BlockSpec((tm, RWKV_COLS), lambda i: (i, 0)),
            pl.BlockSpec((tm, QKV_COLS), lambda i: (i, 0)),
            pl.BlockSpec((tm, GATE_COLS), lambda i: (i, 0)),
        ],
        out_shape=[
            jax.ShapeDtypeStruct((T, RWKV_COLS), F32),
            jax.ShapeDtypeStruct((T, QKV_COLS), BF16),
            jax.ShapeDtypeStruct((T, GATE_COLS), BF16),
        ],
        compiler_params=_cparams(("parallel",)),
        name="inproj",
    )(x2d, g, w_bf)


def _head_ones():
    r = lax.broadcasted_iota(jnp.int32, (LANES, LANES), 0) // HEAD_DIM
    c = lax.broadcasted_iota(jnp.int32, (LANES, LANES), 1) // HEAD_DIM
    return jnp.where(r == c, 1.0, 0.0).astype(BF16)


def _head_sum(x, ones):
    parts = [_bdot(x[:, c:c + LANES], ones) for c in range(0, WIDTH, LANES)]
    return jnp.concatenate(parts, axis=-1)


def _rwkv_prep_kernel(u_ref, up_ref, mu_ref, w0_ref, a0_ref, kk_ref, ka_ref, rk_ref, wl_ref,
                      r_ref, lw_ref, k_ref, v_ref, a_ref, b_ref, g_ref, bo_ref):
    i = pl.program_id(0)
    u = u_ref[...]
    tm = u.shape[0]
    prev_last = up_ref[7:8, :] * jnp.where(i > 0, 1.0, 0.0)
    rolled = pltpu.roll(u, 1, 0)
    row = lax.broadcasted_iota(jnp.int32, u.shape, 0)
    u_prev = jnp.where(row == 0, prev_last, rolled)
    u = u + mu_ref[...] * (u_prev - u)
    r = u[:, 0:WIDTH]
    k = u[:, WIDTH:2 * WIDTH]
    v = u[:, 2 * WIDTH:3 * WIDTH]
    lo = u[:, 3 * WIDTH:3 * WIDTH + LORA_COLS]
    lane = lax.broadcasted_iota(jnp.int32, lo.shape, 1)
    act = jnp.where(lane < DECAY_LORA, jnp.tanh(lo),
                    jnp.where(lane < DECAY_LORA + ICLR_LORA, lo, _sigmoid(lo)))
    up = jnp.dot(act, wl_ref[...], preferred_element_type=F32, precision=HI)
    z = -(w0_ref[...] + up[:, 0:WIDTH])
    softplus = jnp.maximum(z, 0.0) + jnp.log(1.0 + jnp.exp(-jnp.abs(z)))
    w_log = -softplus - 0.5
    lw = -jnp.exp(w_log)
    iclr = _sigmoid(a0_ref[...] + up[:, WIDTH:2 * WIDTH])
    g = up[:, 2 * WIDTH:3 * WIDTH]
    ones = _head_ones()
    kk = k * kk_ref[...]
    kk = kk * lax.rsqrt(_head_sum(kk * kk, ones) + KK_EPS)
    k2 = k * (1.0 + (iclr - 1.0) * ka_ref[...])
    bonus = _head_sum(r * k2 * rk_ref[...], ones) * v
    r_ref[...] = r
    lw_ref[...] = lw
    k_ref[...] = k2
    v_ref[...] = v
    a_ref[...] = -kk
    b_ref[...] = kk * iclr
    g_ref[...] = g
    bo_ref[...] = bonus


def _rwkv_prep(ur, mu, w0, a0, k_k, k_a, r_k, w_lora):
    T = ur.shape[0]
    tm = 256
    row = lambda w: pl.BlockSpec((1, w), lambda i: (0, 0))
    out = pl.BlockSpec((tm, WIDTH), lambda i: (i, 0))
    return pl.pallas_call(
        _rwkv_prep_kernel,
        grid=(T // tm,),
        in_specs=[
            pl.BlockSpec((tm, RWKV_COLS), lambda i: (i, 0)),
            pl.BlockSpec((8, RWKV_COLS), lambda i: (jnp.maximum(i * (tm // 8) - 1, 0), 0)),
            row(RWKV_COLS), row(WIDTH), row(WIDTH), row(WIDTH), row(WIDTH), row(WIDTH),
            pl.BlockSpec((LORA_COLS, 3 * WIDTH), lambda i: (0, 0)),
        ],
        out_specs=[out] * 8,
        out_shape=[jax.ShapeDtypeStruct((T, WIDTH), F32)] * 8,
        compiler_params=_cparams(("parallel",)),
        name="rwkv_prep",
    )(ur, ur, mu, w0, a0, k_k, k_a, r_k, w_lora)


def _rwkv_core_kernel(r_ref, lw_ref, k_ref, v_ref, a_ref, b_ref, g_ref, bo_ref, lnw_ref, lnb_ref,
                      o_ref, s_ref):
    c = pl.program_id(0)

    @pl.when(c == 0)
    def _():
        s_ref[...] = jnp.zeros_like(s_ref)

    C = CHUNK
    ri = lax.broadcasted_iota(jnp.int32, (C, C), 0)
    ci = lax.broadcasted_iota(jnp.int32, (C, C), 1)
    tril_incl = ri >= ci
    tril_strict = ri > ci
    lw = lw_ref[...]
    cum = jnp.dot(jnp.where(tril_incl, 1.0, 0.0), lw, preferred_element_type=F32, precision=HI)
    lam = jnp.exp(cum)
    inv_lam = jnp.exp(-cum)
    r_t = r_ref[...] * lam
    a_t = a_ref[...] * jnp.exp(cum - lw)
    b_t = b_ref[...] * inv_lam
    k_t = k_ref[...] * inv_lam
    tot = cum[C - 1:C, :]
    rest = jnp.exp(tot - cum)
    b_h = b_ref[...] * rest
    k_h = k_ref[...] * rest
    lam_c = jnp.exp(tot)
    v_all = v_ref[...]
    eye = jnp.where(ri == ci, 1.0, 0.0)
    H = range(HEADS)
    sls = [slice(h * HEAD_DIM, (h + 1) * HEAD_DIM) for h in H]
    bf = lambda x: x.astype(BF16)
    at = [bf(a_t[:, sl]) for sl in sls]
    rt = [r_t[:, sl] for sl in sls]
    bt = [bf(b_t[:, sl]) for sl in sls]
    kt = [bf(k_t[:, sl]) for sl in sls]
    bh = [bf(b_h[:, sl]) for sl in sls]
    kh = [bf(k_h[:, sl]) for sl in sls]
    vv = [bf(v_all[:, sl]) for sl in sls]
    s_old = [s_ref[h] for h in H]
    ar = [jnp.concatenate([at[h], bf(rt[h])], axis=0) for h in H]
    gb = [_bdot_nt(ar[h], bt[h]) for h in H]
    gk = [_bdot_nt(ar[h], kt[h]) for h in H]
    a_ab = [jnp.where(tril_strict, gb[h][0:C], 0.0) for h in H]
    a_ak = [bf(jnp.where(tril_strict, gk[h][0:C], 0.0)) for h in H]
    a_rb = [bf(jnp.where(tril_incl, gb[h][C:2 * C], 0.0)) for h in H]
    a_rk = [bf(jnp.where(tril_incl, gk[h][C:2 * C], 0.0)) for h in H]
    akv = [bf(_bdot(a_ak[h], vv[h])) for h in H]
    npow = [bf(a_ab[h]) for h in H]
    tinv = [eye + a_ab[h] for h in H]
    for _ in range(5):
        npow = [bf(_bdot(npow[h], npow[h])) for h in H]
        tinv = [tinv[h] + _bdot(npow[h], tinv[h]) for h in H]
    tinv = [bf(t) for t in tinv]
    w_m = [bf(_bdot(tinv[h], at[h])) for h in H]
    u0 = [bf(_bdot(tinv[h], akv[h])) for h in H]
    q_m = [rt[h] + _bdot(a_rb[h], w_m[h]) for h in H]
    y0 = [_bdot(a_rb[h], u0[h]) + _bdot(a_rk[h], vv[h]) for h in H]
    m_k = [_bdot_tn(w_m[h], bh[h]) for h in H]
    n0 = [_bdot_tn(u0[h], bh[h]) + _bdot_tn(vv[h], kh[h]) for h in H]
    ys = [_bdot_nt(q_m[h], s_old[h]) + y0[h] for h in H]
    s_new = [s_old[h] * lam_c[:, sls[h]] + _bdot(s_old[h], m_k[h]) + n0[h] for h in H]
    for h in H:
        s_ref[h] = s_new[h]
    outs = []
    for y in ys:
        mean = jnp.mean(y, axis=-1, keepdims=True)
        var = jnp.mean(jnp.square(y - mean), axis=-1, keepdims=True)
        outs.append((y - mean) * lax.rsqrt(var + LN_X_EPS))
    yn = jnp.concatenate(outs, axis=-1)
    yn = yn * lnw_ref[...] + lnb_ref[...] + bo_ref[...]
    o_ref[...] = (yn * g_ref[...]).astype(BF16)


def _rwkv_core(r, lw, k, v, a, b, g, bonus, ln_w, ln_b):
    T = r.shape[0]
    blk = pl.BlockSpec((CHUNK, WIDTH), lambda c: (c, 0))
    row = pl.BlockSpec((1, WIDTH), lambda c: (0, 0))
    return pl.pallas_call(
        _rwkv_core_kernel,
        grid=(T // CHUNK,),
        in_specs=[blk] * 8 + [row, row],
        out_specs=blk,
        out_shape=jax.ShapeDtypeStruct((T, WIDTH), BF16),
        scratch_shapes=[pltpu.VMEM((HEADS, HEAD_DIM, HEAD_DIM), F32)],
        compiler_params=_cparams(("arbitrary",)),
        name="rwkv_core",
    )(r, lw, k, v, a, b, g, bonus, ln_w, ln_b)


def _kmean_kernel(k_ref, o_ref):
    o_ref[0] = jnp.mean(k_ref[...].astype(F32), axis=0, keepdims=True)


def _kmean(qkv):
    T = qkv.shape[0]
    nb = T // MOBA_BLOCK
    return pl.pallas_call(
        _kmean_kernel,
        grid=(nb,),
        in_specs=[pl.BlockSpec((MOBA_BLOCK, WIDTH), lambda j: (j, 1))],
        out_specs=pl.BlockSpec((1, 1, WIDTH), lambda j: (j, 0, 0)),
        out_shape=jax.ShapeDtypeStruct((nb, 1, WIDTH), F32),
        compiler_params=_cparams(("parallel",)),
        name="kmean",
    )(qkv)


def _t5_bucket(dist):
    n = jnp.maximum(dist, 0)
    max_exact = REL_BUCKETS // 2
    nf = jnp.maximum(n, max_exact).astype(jnp.float32)
    large = max_exact + (jnp.log(nf / max_exact) / math.log(REL_MAX_DISTANCE / max_exact)
                         * (REL_BUCKETS - max_exact)).astype(jnp.int32)
    large = jnp.minimum(large, REL_BUCKETS - 1)
    return jnp.where(n < max_exact, n, large)


def _bucket_tiles():
    i = jnp.arange(MOBA_BLOCK)[:, None]
    j = jnp.arange(MOBA_BLOCK)[None, :]
    d = jnp.arange(N_BIAS_TILES)[:, None, None]
    dist = d * MOBA_BLOCK + i - j
    bucket = _t5_bucket(dist)
    bucket = jnp.where(d == N_BIAS_TILES - 1, REL_BUCKETS - 1, bucket)
    return jnp.where(dist < 0, -1, bucket).astype(jnp.int32)


def _bias_tiles_kernel(idx_ref, rb_ref, o_ref):
    h = pl.program_id(0)
    idx = idx_ref[0]
    acc = jnp.where(idx < 0, NEG_INF, 0.0)
    for bkt in range(REL_BUCKETS):
        acc = jnp.where(idx == bkt, rb_ref[bkt, h] * LOG2E, acc)
    o_ref[0, 0] = acc


def _bias_tiles(rel_bias):
    idx = _bucket_tiles()
    return pl.pallas_call(
        _bias_tiles_kernel,
        grid=(HEADS, N_BIAS_TILES),
        in_specs=[
            pl.BlockSpec((1, MOBA_BLOCK, MOBA_BLOCK), lambda h, d: (d, 0, 0)),
            pl.BlockSpec(memory_space=pltpu.SMEM),
        ],
        out_specs=pl.BlockSpec((1, 1, MOBA_BLOCK, MOBA_BLOCK), lambda h, d: (h, d, 0, 0)),
        out_shape=jax.ShapeDtypeStruct((HEADS, N_BIAS_TILES, MOBA_BLOCK, MOBA_BLOCK), F32),
        compiler_params=_cparams(("parallel", "parallel")),
        name="bias_tiles",
    )(idx, rel_bias)


def _moba_kernel(q_ref, k_ref, v_ref, r_ref, bias_ref, o_ref):
    qb = pl.program_id(1)
    B = MOBA_BLOCK
    qblk = q_ref[...]
    lane = lax.broadcasted_iota(jnp.int32, (B, LANES), 1)
    lane2 = lax.broadcasted_iota(jnp.int32, (2 * B, LANES), 1)
    row2 = lax.broadcasted_iota(jnp.int32, (2 * B, LANES), 0)
    big = jnp.int32(1 << 20)
    nt = (((1,), (1,)), ((), ()))
    q_augs = []
    for h2 in range(2):
        off = HEAD_DIM * (1 - h2)
        inr = (lane >= off) & (lane < off + HEAD_DIM)
        valid = inr & (lane - off < qb)
        sc = jnp.dot(qblk, r_ref[0, h2], preferred_element_type=F32)
        s = jnp.where(valid, sc, NEG_INF)
        s = jnp.where(inr, s, -jnp.inf)
        sel = jnp.zeros((B, LANES), jnp.bool_)
        for _ in range(MOBA_TOP):
            m = jnp.max(s, axis=-1, keepdims=True)
            idx = jnp.min(jnp.where(s == m, lane, big), axis=-1, keepdims=True)
            pick = lane == idx
            sel = jnp.logical_or(sel, pick)
            s = jnp.where(pick, -jnp.inf, s)
        sel = jnp.logical_and(sel, valid)
        q_augs.append(jnp.where(inr, jnp.where(sel, 0.0, NEG_INF).astype(BF16), qblk))

    own = pl.multiple_of(qb * B, B)
    kblk = k_ref[pl.ds(own, B), :]
    vblk = v_ref[pl.ds(own, B), :]
    carry = []
    for h2 in range(2):
        off = HEAD_DIM * (1 - h2)
        inr = (lane >= off) & (lane < off + HEAD_DIM)
        k_aug = jnp.where(inr, jnp.zeros_like(kblk), kblk)
        v_aug = jnp.where(inr, jnp.ones_like(vblk), vblk)
        s0 = lax.dot_general(q_augs[h2], k_aug, nt, preferred_element_type=F32) + bias_ref[h2, 0]
        m0 = jnp.max(s0, axis=-1, keepdims=True)
        p0 = jnp.exp2(s0 - m0)
        carry += [m0, jnp.dot(p0.astype(BF16), v_aug, preferred_element_type=F32)]

    def body(jj, carry):
        start = pl.multiple_of(jj * (2 * B), 2 * B)
        kblk = k_ref[pl.ds(start, 2 * B), :]
        vblk = v_ref[pl.ds(start, 2 * B), :]
        d_a = jnp.minimum(qb - 2 * jj, N_BIAS_TILES - 1)
        d_b = jnp.minimum(qb - 2 * jj - 1, N_BIAS_TILES - 1)
        blk = 2 * jj + jnp.where(row2 >= B, 1, 0)
        out = []
        for h2 in range(2):
            m_prev, acc = carry[2 * h2], carry[2 * h2 + 1]
            off = HEAD_DIM * (1 - h2)
            inr = (lane2 >= off) & (lane2 < off + HEAD_DIM)
            hot = jnp.where(lane2 == off + blk, 1.0, 0.0).astype(BF16)
            k_aug = jnp.where(inr, hot, kblk)
            v_aug = jnp.where(inr, jnp.ones_like(vblk), vblk)
            s_t = lax.dot_general(q_augs[h2], k_aug, nt, preferred_element_type=F32)
            s_a = s_t[:, 0:B] + bias_ref[h2, d_a]
            s_b = s_t[:, B:2 * B] + bias_ref[h2, d_b]
            m_new = jnp.maximum(m_prev, jnp.max(jnp.maximum(s_a, s_b), axis=-1, keepdims=True))
            alpha = jnp.exp2(m_prev - m_new)
            p_a = jnp.exp2(s_a - m_new).astype(BF16)
            p_b = jnp.exp2(s_b - m_new).astype(BF16)
            acc = (acc * alpha + jnp.dot(p_a, v_aug[0:B], preferred_element_type=F32)
                   + jnp.dot(p_b, v_aug[B:2 * B], preferred_element_type=F32))
            out += [m_new, acc]
        return tuple(out)

    carry = lax.fori_loop(0, (qb + 1) // 2, body, tuple(carry))
    res = [carry[1] / pltpu.roll(carry[1], HEAD_DIM, 1), carry[3] / pltpu.roll(carry[3], HEAD_DIM, 1)]
    o_ref[...] = jnp.where(lane < HEAD_DIM, res[0], res[1]).astype(BF16)


def _moba(qkv, r_mats, bias_tiles):
    T = qkv.shape[0]
    nb = T // MOBA_BLOCK
    npair = HEADS // 2
    kcol = WIDTH // LANES
    return pl.pallas_call(
        _moba_kernel,
        grid=(npair, nb),
        in_specs=[
            pl.BlockSpec((MOBA_BLOCK, LANES), lambda p, qb: (qb, p)),
            pl.BlockSpec((T, LANES), lambda p, qb: (0, kcol + p)),
            pl.BlockSpec((T, LANES), lambda p, qb: (0, 2 * kcol + p)),
            pl.BlockSpec((1, 2, LANES, LANES), lambda p, qb: (p, 0, 0, 0)),
            pl.BlockSpec((2, N_BIAS_TILES, MOBA_BLOCK, MOBA_BLOCK), lambda p, qb: (p, 0, 0, 0)),
        ],
        out_specs=pl.BlockSpec((MOBA_BLOCK, LANES), lambda p, qb: (qb, p)),
        out_shape=jax.ShapeDtypeStruct((T, WIDTH), BF16),
        compiler_params=_cparams(("arbitrary", "arbitrary")),
        name="moba",
    )(qkv, qkv, qkv, r_mats, bias_tiles)


def _score_mats(kmean):
    nb = kmean.shape[0]
    km = kmean.reshape(nb, HEADS, HEAD_DIM).transpose(1, 2, 0)
    km = jnp.pad(km, ((0, 0), (0, 0), (0, HEAD_DIM - nb)))
    z = jnp.zeros((HEADS // 2, HEAD_DIM, HEAD_DIM), F32)
    even = jnp.concatenate([jnp.concatenate([z, km[0::2]], axis=2),
                            jnp.concatenate([z, z], axis=2)], axis=1)
    odd = jnp.concatenate([jnp.concatenate([z, z], axis=2),
                           jnp.concatenate([km[1::2], z], axis=2)], axis=1)
    return jnp.stack([even, odd], axis=1).astype(BF16)


def _memkv_kernel(m_ref, g_ref, w_ref, o_ref):
    h = _rms(m_ref[...], g_ref[...]).astype(BF16)
    o_ref[...] = jnp.dot(h, w_ref[...], preferred_element_type=F32).astype(BF16)


def _memkv(mem2d, g, w_bf):
    return pl.pallas_call(
        _memkv_kernel,
        out_shape=jax.ShapeDtypeStruct((N_MEM, 2 * XATTN_WIDTH), BF16),
        compiler_params=pltpu.CompilerParams(vmem_limit_bytes=VMEM_LIMIT),
        name="memkv",
    )(mem2d, g, w_bf)


def _merge_kernel(x_ref, oa_ref, ob_ref, gt_ref, woa_ref, wob_ref, wout_ref, nx_ref, wq_ref,
                  kv_ref, wo_ref, nf_ref, wr_ref, x2_ref, h3_ref, rt_ref):
    pa = jnp.dot(oa_ref[...], woa_ref[...], preferred_element_type=F32)
    pb = jnp.dot(ob_ref[...], wob_ref[...], preferred_element_type=F32)
    ga = gt_ref[:, 0:D_MODEL].astype(F32)
    gb = gt_ref[:, D_MODEL:2 * D_MODEL].astype(F32)
    merged = _sigmoid(ga) * pa + _sigmoid(gb) * pb
    x1 = x_ref[...] + jnp.dot(merged.astype(BF16), wout_ref[...], preferred_element_type=F32)
    h2 = _rms(x1, nx_ref[...]).astype(BF16)
    q = jnp.dot(h2, wq_ref[...], preferred_element_type=F32).astype(BF16)
    heads = []
    for h in range(XATTN_HEADS):
        sl = slice(h * XATTN_HEAD_DIM, (h + 1) * XATTN_HEAD_DIM)
        km = kv_ref[:, sl]
        vm = kv_ref[:, XATTN_WIDTH + h * XATTN_HEAD_DIM:XATTN_WIDTH + (h + 1) * XATTN_HEAD_DIM]
        lg = lax.dot_general(q[:, sl], km, (((1,), (1,)), ((), ())),
                             preferred_element_type=F32) * (XATTN_HEAD_DIM ** -0.5)
        e = jnp.exp(lg - jnp.max(lg, axis=-1, keepdims=True))
        p = e / jnp.sum(e, axis=-1, keepdims=True)
        heads.append(jnp.dot(p.astype(BF16), vm, preferred_element_type=F32))
    o = jnp.concatenate(heads, axis=-1).astype(BF16)
    x2 = x1 + jnp.dot(o, wo_ref[...], preferred_element_type=F32)
    x2_ref[...] = x2
    h3 = _rms(x2, nf_ref[...])
    h3_ref[...] = h3
    lg = jnp.dot(h3, wr_ref[...], preferred_element_type=F32, precision=HI)
    lane = lax.broadcasted_iota(jnp.int32, lg.shape, 1)
    big = jnp.int32(1 << 20)
    is_g = lane < N_GROUPS
    gl = jnp.where(is_g, lg, -jnp.inf)
    gmax = jnp.max(gl, axis=-1, keepdims=True)
    gsel = jnp.min(jnp.where(gl == gmax, lane, big), axis=-1, keepdims=True)
    p_top = 1.0 / jnp.sum(jnp.where(is_g, jnp.exp(gl - gmax), 0.0), axis=-1, keepdims=True)
    e_id = lane - N_GROUPS
    in_grp = (e_id >= gsel * EXPERTS_PER_GROUP) & (e_id < (gsel + 1) * EXPERTS_PER_GROUP)
    el = jnp.where(in_grp, lg, -jnp.inf)
    v1 = jnp.max(el, axis=-1, keepdims=True)
    i1 = jnp.min(jnp.where(el == v1, lane, big), axis=-1, keepdims=True)
    el2 = jnp.where(lane == i1, -jnp.inf, el)
    v2 = jnp.max(el2, axis=-1, keepdims=True)
    i2 = jnp.min(jnp.where(el2 == v2, lane, big), axis=-1, keepdims=True)
    e21 = jnp.exp(v2 - v1)
    g1 = p_top / (1.0 + e21)
    g2 = p_top * e21 / (1.0 + e21)
    rt = jnp.where(lane == 0, (i1 - N_GROUPS).astype(F32),
                   jnp.where(lane == 1, (i2 - N_GROUPS).astype(F32),
                             jnp.where(lane == 2, g1, jnp.where(lane == 3, g2, 0.0))))
    rt_ref[...] = rt


def _merge(x2d, o_a, o_b, gates, w_oa, w_ob, w_out, n_x, w_q, kv, w_o, n_f, w_r):
    T = x2d.shape[0]
    tm = 256
    full = lambda a: pl.BlockSpec(a.shape, lambda i: (0,) * a.ndim)
    tile = lambda w: pl.BlockSpec((tm, w), lambda i: (i, 0))
    return pl.pallas_call(
        _merge_kernel,
        grid=(T // tm,),
        in_specs=[tile(D_MODEL), tile(WIDTH), tile(WIDTH), tile(GATE_COLS), full(w_oa), full(w_ob),
                  full(w_out), full(n_x), full(w_q), full(kv), full(w_o), full(n_f), full(w_r)],
        out_specs=[tile(D_MODEL), tile(D_MODEL), tile(LANES)],
        out_shape=[jax.ShapeDtypeStruct((T, D_MODEL), F32),
                   jax.ShapeDtypeStruct((T, D_MODEL), F32),
                   jax.ShapeDtypeStruct((T, LANES), F32)],
        compiler_params=_cparams(("parallel",)),
        name="merge",
    )(x2d, o_a, o_b, gates, w_oa, w_ob, w_out, n_x, w_q, kv, w_o, n_f, w_r)


def _moe_kernel(be_ref, nu_ref, tok_ref, h_hbm, wg_ref, wu_ref, wd_ref, o_ref, xb_ref, sem):
    i = pl.program_id(0)

    @pl.when(i < nu_ref[0])
    def _():
        def row_copy(r):
            t = tok_ref[i * MOE_BLOCK + r]
            return pltpu.make_async_copy(h_hbm.at[pl.ds(t, 1)], xb_ref.at[pl.ds(r, 1)], sem)

        def start(r, c):
            row_copy(r).start()
            return c

        lax.fori_loop(0, MOE_BLOCK, start, 0)

        def wait(r, c):
            row_copy(r).wait()
            return c

        lax.fori_loop(0, MOE_BLOCK, wait, 0)
        xb = xb_ref[...].astype(BF16)
        gate = jnp.dot(xb, wg_ref[0], preferred_element_type=F32)
        up = jnp.dot(xb, wu_ref[0], preferred_element_type=F32)
        hid = gate * _sigmoid(gate) * up
        o_ref[...] = jnp.dot(hid.astype(BF16), wd_ref[0], preferred_element_type=F32)

    @pl.when(i >= nu_ref[0])
    def _():
        o_ref[...] = jnp.zeros_like(o_ref)


def _moe(block_expert, n_used, buf_tok, h3, wg, wu, wd):
    cap = buf_tok.shape[0]
    n_blocks = cap // MOE_BLOCK
    grid_spec = pltpu.PrefetchScalarGridSpec(
        num_scalar_prefetch=3,
        grid=(n_blocks,),
        in_specs=[
            pl.BlockSpec(memory_space=pl.ANY),
            pl.BlockSpec((1, D_MODEL, D_EXPERT), lambda i, be, nu, tk: (be[i], 0, 0)),
            pl.BlockSpec((1, D_MODEL, D_EXPERT), lambda i, be, nu, tk: (be[i], 0, 0)),
            pl.BlockSpec((1, D_EXPERT, D_MODEL), lambda i, be, nu, tk: (be[i], 0, 0)),
        ],
        out_specs=pl.BlockSpec((MOE_BLOCK, D_MODEL), lambda i, be, nu, tk: (i, 0)),
        scratch_shapes=[pltpu.VMEM((MOE_BLOCK, D_MODEL), F32), pltpu.SemaphoreType.DMA],
    )
    return pl.pallas_call(
        _moe_kernel,
        grid_spec=grid_spec,
        out_shape=jax.ShapeDtypeStruct((cap, D_MODEL), F32),
        compiler_params=_cparams(("arbitrary",)),
        name="moe",
    )(block_expert, n_used, buf_tok, h3, wg, wu, wd)


def _moe_plan(expert, n_tok):
    n_assign = n_tok * 2
    e_flat = expert.reshape(n_assign)
    onehot = (e_flat[:, None] == jnp.arange(N_EXPERTS, dtype=jnp.int32)[None, :]).astype(jnp.int32)
    counts = jnp.sum(onehot, axis=0)
    rank = jnp.sum((jnp.cumsum(onehot, axis=0) - onehot) * onehot, axis=1)
    padded = (counts + MOE_BLOCK - 1) // MOE_BLOCK * MOE_BLOCK
    p_end = jnp.cumsum(padded)
    p_start = p_end - padded
    pos = (p_start[e_flat] + rank).astype(jnp.int32)
    n_blocks = (n_assign + N_EXPERTS * (MOE_BLOCK - 1) + MOE_BLOCK - 1) // MOE_BLOCK
    cap = n_blocks * MOE_BLOCK
    tok_flat = jnp.repeat(jnp.arange(n_tok, dtype=jnp.int32), 2)
    buf_tok = jnp.zeros((cap,), jnp.int32).at[pos].set(tok_flat)
    block_expert = jnp.minimum(
        jnp.searchsorted(p_end, jnp.arange(n_blocks, dtype=jnp.int32) * MOE_BLOCK, side='right'),
        N_EXPERTS - 1).astype(jnp.int32)
    n_used = (p_end[-1] // MOE_BLOCK).astype(jnp.int32).reshape(1)
    return pos, buf_tok, block_expert, n_used


def _final_kernel(pos_ref, x_ref, rt_ref, y_hbm, g_ref, o_ref, y1_ref, y2_ref, sem):
    i = pl.program_id(0)
    tm = x_ref.shape[0]

    def copies(r):
        p1 = pos_ref[2 * (i * tm + r)]
        p2 = pos_ref[2 * (i * tm + r) + 1]
        return (pltpu.make_async_copy(y_hbm.at[pl.ds(p1, 1)], y1_ref.at[pl.ds(r, 1)], sem),
                pltpu.make_async_copy(y_hbm.at[pl.ds(p2, 1)], y2_ref.at[pl.ds(r, 1)], sem))

    def start(r, c):
        c1, c2 = copies(r)
        c1.start()
        c2.start()
        return c

    lax.fori_loop(0, tm, start, 0)

    def wait(r, c):
        c1, c2 = copies(r)
        c1.wait()
        c2.wait()
        return c

    lax.fori_loop(0, tm, wait, 0)
    rt = rt_ref[...]
    g1 = rt[:, 2:3]
    g2 = rt[:, 3:4]
    x3 = x_ref[...] + y1_ref[...] * g1 + y2_ref[...] * g2
    o_ref[...] = _rms(x3, g_ref[...])


def _final(pos, x2, rt, y_sorted, g):
    T = x2.shape[0]
    tm = 256
    grid_spec = pltpu.PrefetchScalarGridSpec(
        num_scalar_prefetch=1,
        grid=(T // tm,),
        in_specs=[
            pl.BlockSpec((tm, D_MODEL), lambda i, ps: (i, 0)),
            pl.BlockSpec((tm, LANES), lambda i, ps: (i, 0)),
            pl.BlockSpec(memory_space=pl.ANY),
            pl.BlockSpec((1, D_MODEL), lambda i, ps: (0, 0)),
        ],
        out_specs=pl.BlockSpec((tm, D_MODEL), lambda i, ps: (i, 0)),
        scratch_shapes=[pltpu.VMEM((tm, D_MODEL), F32), pltpu.VMEM((tm, D_MODEL), F32),
                        pltpu.SemaphoreType.DMA],
    )
    return pl.pallas_call(
        _final_kernel,
        grid_spec=grid_spec,
        out_shape=jax.ShapeDtypeStruct((T, D_MODEL), F32),
        compiler_params=_cparams(("arbitrary",)),
        name="final",
    )(pos, x2, rt, y_sorted, g)


def _lora_weight(decay_up, iclr_up, gate_up):
    w = jnp.zeros((LORA_COLS, 3 * WIDTH), F32)
    w = w.at[0:DECAY_LORA, 0:WIDTH].set(decay_up)
    w = w.at[DECAY_LORA:DECAY_LORA + ICLR_LORA, WIDTH:2 * WIDTH].set(iclr_up)
    w = w.at[DECAY_LORA + ICLR_LORA:, 2 * WIDTH:].set(gate_up)
    return w


def _router_weight(w_group, w_expert):
    w = jnp.zeros((D_MODEL, LANES), F32)
    w = w.at[:, 0:N_GROUPS].set(w_group)
    return w.at[:, N_GROUPS:N_GROUPS + N_EXPERTS].set(w_expert)


def kernel(x, mem, rel_bias, mem_norm, norm_mix, w_in, tshift_mu, decay_w0, decay_up, iclr_a0,
           iclr_up, gate_up, k_k, k_a, r_k, ln_x_w, ln_x_b, w_o_rwkv, w_o_moba, w_out,
           norm_xattn, w_q_x, w_kv_x, w_o_x, norm_ffn, w_router_group, w_router_expert,
           w_exp_gate, w_exp_up, w_exp_down, norm_final):
    B, T, D = x.shape
    assert B == 1 and D == D_MODEL and T % MOBA_BLOCK == 0 and T // MOBA_BLOCK <= HEAD_DIM
    assert norm_mix.shape[0] == 1
    row = lambda a: a.reshape(1, -1)
    x2d = x.reshape(T, D)
    kv = _memkv(mem.reshape(N_MEM, D), row(mem_norm), w_kv_x[0].astype(BF16))

    ur, qkv, gates = _inproj(x2d, row(norm_mix[0]), w_in[0].astype(BF16))
    prep = _rwkv_prep(ur, row(tshift_mu[0]), row(decay_w0[0]), row(iclr_a0[0]), row(k_k[0]),
                      row(k_a[0]), row(r_k[0]), _lora_weight(decay_up[0], iclr_up[0], gate_up[0]))
    o_a = _rwkv_core(*prep, row(ln_x_w[0]), row(ln_x_b[0]))

    kmean = _kmean(qkv).reshape(T // MOBA_BLOCK, WIDTH)
    o_b = _moba(qkv, _score_mats(kmean), _bias_tiles(rel_bias))

    x2, h3, rt = _merge(x2d, o_a, o_b, gates, w_o_rwkv[0].astype(BF16), w_o_moba[0].astype(BF16),
                        w_out[0].astype(BF16), row(norm_xattn[0]), w_q_x[0].astype(BF16), kv,
                        w_o_x[0].astype(BF16), row(norm_ffn[0]),
                        _router_weight(w_router_group[0], w_router_expert[0]))

    expert = rt[:, 0:2].astype(jnp.int32)
    pos, buf_tok, block_expert, n_used = _moe_plan(expert, T)
    y_sorted = _moe(block_expert, n_used, buf_tok, h3, w_exp_gate[0].astype(BF16),
                    w_exp_up[0].astype(BF16), w_exp_down[0].astype(BF16))
    out = _final(pos, x2, rt, y_sorted, row(norm_final))
    return out.reshape(B, T, D)
```

```python
import functools
import math

import jax
import jax.numpy as jnp
import numpy as np
from jax import lax
from jax.experimental import pallas as pl
from jax.experimental.pallas import tpu as pltpu

F32 = jnp.float32
BF16 = jnp.bfloat16

D_MODEL = 1024
N_MEM = 256
NORM_EPS = 1e-6
NEG_INF = -1e30

HEADS = 8
HEAD_DIM = 64
WIDTH = HEADS * HEAD_DIM
DECAY_LORA = 64
ICLR_LORA = 64
GATE_LORA = 128
LORA_COLS = DECAY_LORA + ICLR_LORA + GATE_LORA
RWKV_COLS = 3 * WIDTH + LORA_COLS
QKV_COLS = 3 * WIDTH
GATE_COLS = 2 * D_MODEL
IN_COLS = RWKV_COLS + QKV_COLS + GATE_COLS
LN_X_EPS = 64e-5
KK_EPS = 1e-12

MOBA_BLOCK = 256
MOBA_TOP = 3
REL_BUCKETS = 32
REL_MAX_DISTANCE = 4096
N_BIAS_TILES = 14

XATTN_HEADS = 4
XATTN_HEAD_DIM = 128
XATTN_WIDTH = XATTN_HEADS * XATTN_HEAD_DIM

N_GROUPS = 4
EXPERTS_PER_GROUP = 8
N_EXPERTS = N_GROUPS * EXPERTS_PER_GROUP
D_EXPERT = 512
MOE_BLOCK = 128

CHUNK = 64
LANES = 128
VMEM_LIMIT = 48 * 1024 * 1024

LOG2E = math.log2(math.e)


def _cparams(sem):
    return pltpu.CompilerParams(dimension_semantics=sem, vmem_limit_bytes=VMEM_LIMIT)


def _bdot(a, b):
    return jnp.dot(a.astype(BF16), b.astype(BF16), preferred_element_type=F32)


def _bdot_nt(a, b):
    return lax.dot_general(a.astype(BF16), b.astype(BF16), (((1,), (1,)), ((), ())),
                           preferred_element_type=F32)


def _bdot_tn(a, b):
    return lax.dot_general(a.astype(BF16), b.astype(BF16), (((0,), (0,)), ((), ())),
                           preferred_element_type=F32)


def _rms(x, g):
    return x * lax.rsqrt(jnp.mean(x * x, axis=-1, keepdims=True) + NORM_EPS) * g


def _sigmoid(x):
    return 1.0 / (1.0 + jnp.exp(-x))


def _inproj_kernel(x_ref, g_ref, w_ref, ur_ref, qkv_ref, gate_ref):
    h = _rms(x_ref[...], g_ref[...]).astype(BF16)
    step = 256
    for c0 in range(0, RWKV_COLS, step):
        ur_ref[:, c0:c0 + step] = jnp.dot(h, w_ref[:, c0:c0 + step], preferred_element_type=F32)
    for c0 in range(0, QKV_COLS, step):
        o = jnp.dot(h, w_ref[:, RWKV_COLS + c0:RWKV_COLS + c0 + step], preferred_element_type=F32)
        if c0 < WIDTH:
            o = o * (HEAD_DIM ** -0.5 * LOG2E)
        qkv_ref[:, c0:c0 + step] = o.astype(BF16)
    base = RWKV_COLS + QKV_COLS
    for c0 in range(0, GATE_COLS, step):
        o = jnp.dot(h, w_ref[:, base + c0:base + c0 + step], preferred_element_type=F32)
        gate_ref[:, c0:c0 + step] = o.astype(BF16)


def _inproj(x2d, g, w_bf):
    T = x2d.shape[0]
    tm = 256
    return pl.pallas_call(
        _inproj_kernel,
        grid=(T // tm,),
        in_specs=[
            pl.BlockSpec((tm, D_MODEL), lambda i: (i, 0)),
            pl.BlockSpec((1, D_MODEL), lambda i: (0, 0)),
            pl.BlockSpec((D_MODEL, IN_COLS), lambda i: (0, 0)),
        ],
        out_specs=[
            pl.BlockSpec((tm, RWKV_COLS), lambda i: (i, 0)),
            pl.BlockSpec((tm, QKV_COLS), lambda i: (i, 0)),
            pl.BlockSpec((tm, GATE_COLS), lambda i: (i, 0)),
        ],
        out_shape=[
            jax.ShapeDtypeStruct((T, RWKV_COLS), F32),
            jax.ShapeDtypeStruct((T, QKV_COLS), BF16),
            jax.ShapeDtypeStruct((T, GATE_COLS), BF16),
        ],
        compiler_params=_cparams(("parallel",)),
        name="inproj",
    )(x2d, g, w_bf)


def _head_ones():
    r = lax.broadcasted_iota(jnp.int32, (LANES, LANES), 0) // HEAD_DIM
    c = lax.broadcasted_iota(jnp.int32, (LANES, LANES), 1) // HEAD_DIM
    return jnp.where(r == c, 1.0, 0.0).astype(BF16)


def _head_sum(x, ones):
    parts = [_bdot(x[:, c:c + LANES], ones) for c in range(0, WIDTH, LANES)]
    return jnp.concatenate(parts, axis=-1)


def _rwkv_prep_kernel(u_ref, up_ref, mu_ref, w0_ref, a0_ref, kk_ref, ka_ref, rk_ref, wl_ref,
                      r_ref, lw_ref, k_ref, v_ref, a_ref, b_ref, g_ref, bo_ref):
    i = pl.program_id(0)
    u = u_ref[...]
    tm = u.shape[0]
    prev_last = up_ref[7:8, :] * jnp.where(i > 0, 1.0, 0.0)
    rolled = pltpu.roll(u, 1, 0)
    row = lax.broadcasted_iota(jnp.int32, u.shape, 0)
    u_prev = jnp.where(row == 0, prev_last, rolled)
    u = u + mu_ref[...] * (u_prev - u)
    r = u[:, 0:WIDTH]
    k = u[:, WIDTH:2 * WIDTH]
    v = u[:, 2 * WIDTH:3 * WIDTH]
    lo = u[:, 3 * WIDTH:3 * WIDTH + LORA_COLS]
    lane = lax.broadcasted_iota(jnp.int32, lo.shape, 1)
    act = jnp.where(lane < DECAY_LORA, jnp.tanh(lo),
                    jnp.where(lane < DECAY_LORA + ICLR_LORA, lo, _sigmoid(lo)))
    up = _bdot(act, wl_ref[...])
    z = -(w0_ref[...] + up[:, 0:WIDTH])
    softplus = jnp.maximum(z, 0.0) + jnp.log(1.0 + jnp.exp(-jnp.abs(z)))
    w_log = -softplus - 0.5
    lw = -jnp.exp(w_log)
    iclr = _sigmoid(a0_ref[...] + up[:, WIDTH:2 * WIDTH])
    g = up[:, 2 * WIDTH:3 * WIDTH]
    ones = _head_ones()
    kk = k * kk_ref[...]
    kk = kk * lax.rsqrt(_head_sum(kk * kk, ones) + KK_EPS)
    k2 = k * (1.0 + (iclr - 1.0) * ka_ref[...])
    bonus = _head_sum(r * k2 * rk_ref[...], ones) * v
    r_ref[...] = r
    lw_ref[...] = lw
    k_ref[...] = k2
    v_ref[...] = v
    a_ref[...] = -kk
    b_ref[...] = kk * iclr
    g_ref[...] = g
    bo_ref[...] = bonus


def _rwkv_prep(ur, mu, w0, a0, k_k, k_a, r_k, w_lora):
    T = ur.shape[0]
    tm = 256
    row = lambda w: pl.BlockSpec((1, w), lambda i: (0, 0))
    out = pl.BlockSpec((tm, WIDTH), lambda i: (i, 0))
    return pl.pallas_call(
        _rwkv_prep_kernel,
        grid=(T // tm,),
        in_specs=[
            pl.BlockSpec((tm, RWKV_COLS), lambda i: (i, 0)),
            pl.BlockSpec((8, RWKV_COLS), lambda i: (jnp.maximum(i * (tm // 8) - 1, 0), 0)),
            row(RWKV_COLS), row(WIDTH), row(WIDTH), row(WIDTH), row(WIDTH), row(WIDTH),
            pl.BlockSpec((LORA_COLS, 3 * WIDTH), lambda i: (0, 0)),
        ],
        out_specs=[out] * 8,
        out_shape=[jax.ShapeDtypeStruct((T, WIDTH), F32)] * 8,
        compiler_params=_cparams(("parallel",)),
        name="rwkv_prep",
    )(ur, ur, mu, w0, a0, k_k, k_a, r_k, w_lora)


def _rwkv_core_kernel(r_ref, lw_ref, k_ref, v_ref, a_ref, b_ref, g_ref, bo_ref, lnw_ref, lnb_ref,
                      o_ref, s_ref):
    c = pl.program_id(0)

    @pl.when(c == 0)
    def _():
        s_ref[...] = jnp.zeros_like(s_ref)

    C = CHUNK
    ri = lax.broadcasted_iota(jnp.int32, (C, C), 0)
    ci = lax.broadcasted_iota(jnp.int32, (C, C), 1)
    tril_incl = ri >= ci
    tril_strict = ri > ci
    lw = lw_ref[...]
    tri = jnp.where(tril_incl, 1.0, 0.0).astype(BF16)
    lw_hi = lw.astype(BF16)
    lw_r1 = lw - lw_hi.astype(F32)
    lw_mid = lw_r1.astype(BF16)
    lw_lo = (lw_r1 - lw_mid.astype(F32)).astype(BF16)
    cum = (jnp.dot(tri, lw_hi, preferred_element_type=F32)
           + jnp.dot(tri, lw_mid, preferred_element_type=F32)
           + jnp.dot(tri, lw_lo, preferred_element_type=F32))
    lam = jnp.exp(cum)
    inv_lam = jnp.exp(-cum)
    r_t = r_ref[...] * lam
    a_t = a_ref[...] * jnp.exp(cum - lw)
    b_t = b_ref[...] * inv_lam
    k_t = k_ref[...] * inv_lam
    tot = cum[C - 1:C, :]
    rest = jnp.exp(tot - cum)
    b_h = b_ref[...] * rest
    k_h = k_ref[...] * rest
    lam_c = jnp.exp(tot)
    v_all = v_ref[...]
    eye = jnp.where(ri == ci, 1.0, 0.0)
    H = range(HEADS)
    sls = [slice(h * HEAD_DIM, (h + 1) * HEAD_DIM) for h in H]
    bf = lambda x: x.astype(BF16)
    at = [bf(a_t[:, sl]) for sl in sls]
    rt = [r_t[:, sl] for sl in sls]
    bt = [bf(b_t[:, sl]) for sl in sls]
    kt = [bf(k_t[:, sl]) for sl in sls]
    bh = [bf(b_h[:, sl]) for sl in sls]
    kh = [bf(k_h[:, sl]) for sl in sls]
    vv = [bf(v_all[:, sl]) for sl in sls]
    s_old = [s_ref[h] for h in H]
    ar = [jnp.concatenate([at[h], bf(rt[h])], axis=0) for h in H]
    gb = [_bdot_nt(ar[h], bt[h]) for h in H]
    gk = [_bdot_nt(ar[h], kt[h]) for h in H]
    a_ab = [jnp.where(tril_strict, gb[h][0:C], 0.0) for h in H]
    a_ak = [bf(jnp.where(tril_strict, gk[h][0:C], 0.0)) for h in H]
    a_rb = [bf(jnp.where(tril_incl, gb[h][C:2 * C], 0.0)) for h in H]
    a_rk = [bf(jnp.where(tril_incl, gk[h][C:2 * C], 0.0)) for h in H]
    akv = [bf(_bdot(a_ak[h], vv[h])) for h in H]
    npow = [bf(a_ab[h]) for h in H]
    tinv = [eye + a_ab[h] for h in H]
    for _ in range(5):
        npow = [bf(_bdot(npow[h], npow[h])) for h in H]
        tinv = [tinv[h] + _bdot(npow[h], tinv[h]) for h in H]
    tinv = [bf(t) for t in tinv]
    w_m = [bf(_bdot(tinv[h], at[h])) for h in H]
    u0 = [bf(_bdot(tinv[h], akv[h])) for h in H]
    q_m = [rt[h] + _bdot(a_rb[h], w_m[h]) for h in H]
    y0 = [_bdot(a_rb[h], u0[h]) + _bdot(a_rk[h], vv[h]) for h in H]
    m_k = [_bdot_tn(w_m[h], bh[h]) for h in H]
    n0 = [_bdot_tn(u0[h], bh[h]) + _bdot_tn(vv[h], kh[h]) for h in H]
    ys = [_bdot_nt(q_m[h], s_old[h]) + y0[h] for h in H]
    s_new = [s_old[h] * lam_c[:, sls[h]] + _bdot(s_old[h], m_k[h]) + n0[h] for h in H]
    for h in H:
        s_ref[h] = s_new[h]
    outs = []
    for y in ys:
        mean = jnp.mean(y, axis=-1, keepdims=True)
        var = jnp.mean(jnp.square(y - mean), axis=-1, keepdims=True)
        outs.append((y - mean) * lax.rsqrt(var + LN_X_EPS))
    yn = jnp.concatenate(outs, axis=-1)
    yn = yn * lnw_ref[...] + lnb_ref[...] + bo_ref[...]
    o_ref[...] = (yn * g_ref[...]).astype(BF16)


def _rwkv_core(r, lw, k, v, a, b, g, bonus, ln_w, ln_b):
    T = r.shape[0]
    blk = pl.BlockSpec((CHUNK, WIDTH), lambda c: (c, 0))
    row = pl.BlockSpec((1, WIDTH), lambda c: (0, 0))
    return pl.pallas_call(
        _rwkv_core_kernel,
        grid=(T // CHUNK,),
        in_specs=[blk] * 8 + [row, row],
        out_specs=blk,
        out_shape=jax.ShapeDtypeStruct((T, WIDTH), BF16),
        scratch_shapes=[pltpu.VMEM((HEADS, HEAD_DIM, HEAD_DIM), F32)],
        compiler_params=_cparams(("arbitrary",)),
        name="rwkv_core",
    )(r, lw, k, v, a, b, g, bonus, ln_w, ln_b)


def _kmean_kernel(k_ref, o_ref):
    o_ref[0] = jnp.mean(k_ref[...].astype(F32), axis=0, keepdims=True)


def _kmean(qkv):
    T = qkv.shape[0]
    nb = T // MOBA_BLOCK
    return pl.pallas_call(
        _kmean_kernel,
        grid=(nb,),
        in_specs=[pl.BlockSpec((MOBA_BLOCK, WIDTH), lambda j: (j, 1))],
        out_specs=pl.BlockSpec((1, 1, WIDTH), lambda j: (j, 0, 0)),
        out_shape=jax.ShapeDtypeStruct((nb, 1, WIDTH), F32),
        compiler_params=_cparams(("parallel",)),
        name="kmean",
    )(qkv)


def _t5_bucket(dist):
    n = jnp.maximum(dist, 0)
    max_exact = REL_BUCKETS // 2
    nf = jnp.maximum(n, max_exact).astype(jnp.float32)
    large = max_exact + (jnp.log(nf / max_exact) / math.log(REL_MAX_DISTANCE / max_exact)
                         * (REL_BUCKETS - max_exact)).astype(jnp.int32)
    large = jnp.minimum(large, REL_BUCKETS - 1)
    return jnp.where(n < max_exact, n, large)


def _bucket_tiles():
    i = jnp.arange(MOBA_BLOCK)[None, :]
    j = jnp.arange(MOBA_BLOCK)[:, None]
    d = jnp.arange(N_BIAS_TILES + 1)[:, None, None]
    dist = d * MOBA_BLOCK + i - j
    bucket = _t5_bucket(dist)
    bucket = jnp.where(d == N_BIAS_TILES - 1, REL_BUCKETS - 1, bucket)
    return jnp.where((dist < 0) | (d == N_BIAS_TILES), -1, bucket).astype(jnp.int32)


def _bias_tiles_kernel(idx_ref, rb_ref, o_ref):
    h = pl.program_id(0)
    idx = idx_ref[0]
    acc = jnp.where(idx < 0, NEG_INF, 0.0)
    for bkt in range(REL_BUCKETS):
        acc = jnp.where(idx == bkt, rb_ref[bkt, h] * LOG2E, acc)
    o_ref[0, 0] = acc


def _bias_tiles(rel_bias):
    idx = _bucket_tiles()
    return pl.pallas_call(
        _bias_tiles_kernel,
        grid=(HEADS, N_BIAS_TILES + 1),
        in_specs=[
            pl.BlockSpec((1, MOBA_BLOCK, MOBA_BLOCK), lambda h, d: (d, 0, 0)),
            pl.BlockSpec(memory_space=pltpu.SMEM),
        ],
        out_specs=pl.BlockSpec((1, 1, MOBA_BLOCK, MOBA_BLOCK), lambda h, d: (h, d, 0, 0)),
        out_shape=jax.ShapeDtypeStruct((HEADS, N_BIAS_TILES + 1, MOBA_BLOCK, MOBA_BLOCK), F32),
        compiler_params=_cparams(("parallel", "parallel")),
        name="bias_tiles",
    )(idx, rel_bias)


def _moba_kernel(q_ref, ka_ref, vt_ref, r_ref, bias_ref, o_ref, s_ref, p_ref, acc_ref):
    qb = pl.program_id(1)
    B = MOBA_BLOCK
    q_tr = q_ref[...].astype(F32).T
    q_tr_bf = q_tr.astype(BF16)
    row = lax.broadcasted_iota(jnp.int32, (LANES, B), 0)
    big = jnp.int32(1 << 20)
    q_t, q_own_t = [], []
    for h2 in range(2):
        off = HEAD_DIM * (1 - h2)
        inr = (row >= off) & (row < off + HEAD_DIM)
        valid = inr & (row - off < qb)
        sc = jnp.dot(r_ref[0, h2], q_tr_bf, preferred_element_type=F32)
        s = jnp.where(valid, sc, NEG_INF)
        s = jnp.where(inr, s, -jnp.inf)
        sel = jnp.zeros((LANES, B), jnp.bool_)
        for _ in range(MOBA_TOP):
            m = jnp.max(s, axis=0, keepdims=True)
            idx = jnp.min(jnp.where(s == m, row, big), axis=0, keepdims=True)
            pick = row == idx
            sel = jnp.logical_or(sel, pick)
            s = jnp.where(pick, -jnp.inf, s)
        sel = jnp.logical_and(sel, valid)
        q_t.append(jnp.where(inr, jnp.where(sel, 0.0, NEG_INF), q_tr).astype(BF16))
        q_own_t.append(jnp.where(inr, 0.0, q_tr).astype(BF16))

    carry = []
    for h2 in range(2):
        k_own = ka_ref[h2, pl.ds(pl.multiple_of(qb * B, B), B), :]
        s0 = jnp.dot(k_own, q_own_t[h2], preferred_element_type=F32) + bias_ref[h2, 0]
        m0 = jnp.max(s0, axis=0, keepdims=True)
        p0 = jnp.exp2(s0 - m0).astype(BF16)
        carry += [m0, jnp.dot(vt_ref[h2, qb], p0, preferred_element_type=F32)]

    n_tiles = (qb + 1) // 2
    last = jnp.maximum(n_tiles - 1, 0)
    for h2 in range(2):
        acc_ref[h2] = carry[2 * h2 + 1]
        s_ref[1, h2] = jnp.zeros((2 * B, B), F32)
        p_ref[1, h2] = jnp.zeros((2 * B, B), BF16)

    def trip(t, w, stats):
        r = 1 - w
        ok = (t >= 1) & (t <= n_tiles)
        d_a = jnp.where(ok, jnp.clip(qb - 2 * (t - 1), 0, N_BIAS_TILES - 1), N_BIAS_TILES)
        d_b = jnp.where(ok, jnp.clip(qb - 2 * (t - 1) - 1, 0, N_BIAS_TILES - 1), N_BIAS_TILES)
        start = pl.multiple_of(jnp.clip(t, 0, last) * (2 * B), 2 * B)
        t_pv = jnp.clip(t - 2, 0, last)
        out = []
        for h2 in range(2):
            m_prev, alpha_p = stats[2 * h2], stats[2 * h2 + 1]
            acc_ref[h2] = (acc_ref[h2] * alpha_p
                           + jnp.dot(vt_ref[h2, 2 * t_pv], p_ref[r, h2, 0:B], preferred_element_type=F32)
                           + jnp.dot(vt_ref[h2, 2 * t_pv + 1], p_ref[r, h2, B:2 * B],
                                     preferred_element_type=F32))
            s_a = s_ref[r, h2, 0:B] + bias_ref[h2, d_a]
            s_b = s_ref[r, h2, B:2 * B] + bias_ref[h2, d_b]
            m_new = jnp.maximum(m_prev, jnp.max(jnp.maximum(s_a, s_b), axis=0, keepdims=True))
            out += [m_new, jnp.exp2(m_prev - m_new)]
            p_ref[w, h2, 0:B] = jnp.exp2(s_a - m_new).astype(BF16)
            p_ref[w, h2, B:2 * B] = jnp.exp2(s_b - m_new).astype(BF16)
            s_ref[w, h2] = jnp.dot(ka_ref[h2, pl.ds(start, 2 * B), :], q_t[h2],
                                   preferred_element_type=F32)
        return out

    def body(u, stats):
        stats = trip(2 * u, 0, list(stats))
        return tuple(trip(2 * u + 1, 1, stats))

    stats = []
    for h2 in range(2):
        stats += [carry[2 * h2], jnp.ones_like(carry[2 * h2])]
    lax.fori_loop(0, (n_tiles + 3) // 2, body, tuple(stats))
    acc0, acc1 = acc_ref[0], acc_ref[1]
    row = lax.broadcasted_iota(jnp.int32, acc0.shape, 0)
    o_t = jnp.where(row < HEAD_DIM, acc0 / acc0[HEAD_DIM:HEAD_DIM + 1, :], acc1 / acc1[0:1, :])
    o_ref[...] = o_t.T.astype(BF16)


def _moba(qkv, k_aug, vt_aug, r_mats, bias_tiles):
    T = qkv.shape[0]
    nb = T // MOBA_BLOCK
    npair = HEADS // 2
    once = pl.Buffered(1)
    return pl.pallas_call(
        _moba_kernel,
        grid=(npair, nb),
        in_specs=[
            pl.BlockSpec((MOBA_BLOCK, LANES), lambda p, qb: (qb, p)),
            pl.BlockSpec((2, T, LANES), lambda p, qb: (p, 0, 0), pipeline_mode=once),
            pl.BlockSpec((2, nb, LANES, MOBA_BLOCK), lambda p, qb: (p, 0, 0, 0), pipeline_mode=once),
            pl.BlockSpec((1, 2, LANES, LANES), lambda p, qb: (p, 0, 0, 0)),
            pl.BlockSpec((2, N_BIAS_TILES + 1, MOBA_BLOCK, MOBA_BLOCK), lambda p, qb: (p, 0, 0, 0),
                         pipeline_mode=once),
        ],
        out_specs=pl.BlockSpec((MOBA_BLOCK, LANES), lambda p, qb: (qb, p)),
        out_shape=jax.ShapeDtypeStruct((T, WIDTH), BF16),
        scratch_shapes=[pltpu.VMEM((2, 2, 2 * MOBA_BLOCK, MOBA_BLOCK), F32),
                        pltpu.VMEM((2, 2, 2 * MOBA_BLOCK, MOBA_BLOCK), BF16),
                        pltpu.VMEM((2, LANES, MOBA_BLOCK), F32)],
        compiler_params=_cparams(("arbitrary", "arbitrary")),
        name="moba",
    )(qkv, k_aug, vt_aug, r_mats, bias_tiles)


def _moba_prep_kernel(k_ref, v_ref, ka_ref, vt_ref):
    j = pl.program_id(1)
    kblk = k_ref[...]
    lane = lax.broadcasted_iota(jnp.int32, kblk.shape, 1)
    v_t = v_ref[...].astype(F32).T
    row = lax.broadcasted_iota(jnp.int32, v_t.shape, 0)
    for h2 in range(2):
        off = HEAD_DIM * (1 - h2)
        inr = (lane >= off) & (lane < off + HEAD_DIM)
        hot = jnp.where(lane == off + j, 1.0, 0.0).astype(BF16)
        ka_ref[h2] = jnp.where(inr, hot, kblk)
        ones_rows = (row >= off) & (row < off + HEAD_DIM)
        vt_ref[h2, 0] = jnp.where(ones_rows, 1.0, v_t).astype(BF16)


def _moba_prep(qkv):
    T = qkv.shape[0]
    nb = T // MOBA_BLOCK
    npair = HEADS // 2
    kcol = WIDTH // LANES
    return pl.pallas_call(
        _moba_prep_kernel,
        grid=(npair, nb),
        in_specs=[
            pl.BlockSpec((MOBA_BLOCK, LANES), lambda p, j: (j, kcol + p)),
            pl.BlockSpec((MOBA_BLOCK, LANES), lambda p, j: (j, 2 * kcol + p)),
        ],
        out_specs=[
            pl.BlockSpec((2, MOBA_BLOCK, LANES), lambda p, j: (p, j, 0)),
            pl.BlockSpec((2, 1, LANES, MOBA_BLOCK), lambda p, j: (p, j, 0, 0)),
        ],
        out_shape=[
            jax.ShapeDtypeStruct((HEADS, T, LANES), BF16),
            jax.ShapeDtypeStruct((HEADS, nb, LANES, MOBA_BLOCK), BF16),
        ],
        compiler_params=_cparams(("parallel", "parallel")),
        name="moba_prep",
    )(qkv, qkv)


def _score_mats(kmean):
    nb = kmean.shape[0]
    km = kmean.reshape(nb, HEADS, HEAD_DIM).transpose(1, 2, 0)
    km = jnp.pad(km, ((0, 0), (0, 0), (0, HEAD_DIM - nb)))
    z = jnp.zeros((HEADS // 2, HEAD_DIM, HEAD_DIM), F32)
    even = jnp.concatenate([jnp.concatenate([z, km[0::2]], axis=2),
                            jnp.concatenate([z, z], axis=2)], axis=1)
    odd = jnp.concatenate([jnp.concatenate([z, z], axis=2),
                           jnp.concatenate([km[1::2], z], axis=2)], axis=1)
    return jnp.swapaxes(jnp.stack([even, odd], axis=1), -1, -2).astype(BF16)


def _memkv_kernel(m_ref, g_ref, w_ref, o_ref):
    h = _rms(m_ref[...], g_ref[...]).astype(BF16)
    o_ref[...] = jnp.dot(h, w_ref[...], preferred_element_type=F32).astype(BF16)


def _memkv(mem2d, g, w_bf):
    return pl.pallas_call(
        _memkv_kernel,
        out_shape=jax.ShapeDtypeStruct((N_MEM, 2 * XATTN_WIDTH), BF16),
        compiler_params=pltpu.CompilerParams(vmem_limit_bytes=VMEM_LIMIT),
        name="memkv",
    )(mem2d, g, w_bf)


def _merge_kernel(x_ref, oa_ref, ob_ref, gt_ref, woa_ref, wob_ref, wout_ref, nx_ref, wq_ref,
                  kv_ref, wo_ref, nf_ref, wr_ref, x2_ref, h3_ref, rt_ref):
    pa = jnp.dot(oa_ref[...], woa_ref[...], preferred_element_type=F32)
    pb = jnp.dot(ob_ref[...], wob_ref[...], preferred_element_type=F32)
    ga = gt_ref[:, 0:D_MODEL].astype(F32)
    gb = gt_ref[:, D_MODEL:2 * D_MODEL].astype(F32)
    merged = _sigmoid(ga) * pa + _sigmoid(gb) * pb
    x1 = x_ref[...] + jnp.dot(merged.astype(BF16), wout_ref[...], preferred_element_type=F32)
    h2 = _rms(x1, nx_ref[...]).astype(BF16)
    q = jnp.dot(h2, wq_ref[...], preferred_element_type=F32).astype(BF16)
    heads = []
    for h in range(XATTN_HEADS):
        sl = slice(h * XATTN_HEAD_DIM, (h + 1) * XATTN_HEAD_DIM)
        km = kv_ref[:, sl]
        vm = kv_ref[:, XATTN_WIDTH + h * XATTN_HEAD_DIM:XATTN_WIDTH + (h + 1) * XATTN_HEAD_DIM]
        lg = lax.dot_general(q[:, sl], km, (((1,), (1,)), ((), ())),
                             preferred_element_type=F32) * (XATTN_HEAD_DIM ** -0.5)
        e = jnp.exp(lg - jnp.max(lg, axis=-1, keepdims=True))
        p = e / jnp.sum(e, axis=-1, keepdims=True)
        heads.append(jnp.dot(p.astype(BF16), vm, preferred_element_type=F32))
    o = jnp.concatenate(heads, axis=-1).astype(BF16)
    x2 = x1 + jnp.dot(o, wo_ref[...], preferred_element_type=F32)
    x2_ref[...] = x2
    h3 = _rms(x2, nf_ref[...])
    h3_ref[...] = h3
    lg = _bdot(h3, wr_ref[...])
    lane = lax.broadcasted_iota(jnp.int32, lg.shape, 1)
    big = jnp.int32(1 << 20)
    is_g = lane < N_GROUPS
    gl = jnp.where(is_g, lg, -jnp.inf)
    gmax = jnp.max(gl, axis=-1, keepdims=True)
    gsel = jnp.min(jnp.where(gl == gmax, lane, big), axis=-1, keepdims=True)
    p_top = 1.0 / jnp.sum(jnp.where(is_g, jnp.exp(gl - gmax), 0.0), axis=-1, keepdims=True)
    e_id = lane - N_GROUPS
    in_grp = (e_id >= gsel * EXPERTS_PER_GROUP) & (e_id < (gsel + 1) * EXPERTS_PER_GROUP)
    el = jnp.where(in_grp, lg, -jnp.inf)
    v1 = jnp.max(el, axis=-1, keepdims=True)
    i1 = jnp.min(jnp.where(el == v1, lane, big), axis=-1, keepdims=True)
    el2 = jnp.where(lane == i1, -jnp.inf, el)
    v2 = jnp.max(el2, axis=-1, keepdims=True)
    i2 = jnp.min(jnp.where(el2 == v2, lane, big), axis=-1, keepdims=True)
    e21 = jnp.exp(v2 - v1)
    g1 = p_top / (1.0 + e21)
    g2 = p_top * e21 / (1.0 + e21)
    rt = jnp.where(lane == 0, (i1 - N_GROUPS).astype(F32),
                   jnp.where(lane == 1, (i2 - N_GROUPS).astype(F32),
                             jnp.where(lane == 2, g1, jnp.where(lane == 3, g2, 0.0))))
    rt_ref[...] = rt


def _merge(x2d, o_a, o_b, gates, w_oa, w_ob, w_out, n_x, w_q, kv, w_o, n_f, w_r):
    T = x2d.shape[0]
    tm = 256
    full = lambda a: pl.BlockSpec(a.shape, lambda i: (0,) * a.ndim)
    tile = lambda w: pl.BlockSpec((tm, w), lambda i: (i, 0))
    return pl.pallas_call(
        _merge_kernel,
        grid=(T // tm,),
        in_specs=[tile(D_MODEL), tile(WIDTH), tile(WIDTH), tile(GATE_COLS), full(w_oa), full(w_ob),
                  full(w_out), full(n_x), full(w_q), full(kv), full(w_o), full(n_f), full(w_r)],
        out_specs=[tile(D_MODEL), tile(D_MODEL), tile(LANES)],
        out_shape=[jax.ShapeDtypeStruct((T, D_MODEL), F32),
                   jax.ShapeDtypeStruct((T, D_MODEL), F32),
                   jax.ShapeDtypeStruct((T, LANES), F32)],
        compiler_params=_cparams(("parallel",)),
        name="merge",
    )(x2d, o_a, o_b, gates, w_oa, w_ob, w_out, n_x, w_q, kv, w_o, n_f, w_r)


def _moe_kernel(be_ref, nu_ref, tok_ref, h_hbm, wg_ref, wu_ref, wd_ref, o_ref, xb_ref, sem):
    i = pl.program_id(0)
    n_used = nu_ref[0]
    slot = i % 2

    def row_copy(blk, r, s):
        t = tok_ref[blk * MOE_BLOCK + r]
        return pltpu.make_async_copy(h_hbm.at[pl.ds(t, 1)], xb_ref.at[s, pl.ds(r, 1)], sem.at[s])

    def gather(blk, s):
        def start(r, c):
            row_copy(blk, r, s).start()
            return c

        lax.fori_loop(0, MOE_BLOCK, start, 0, unroll=8)

    @pl.when((i == 0) & (n_used > 0))
    def _():
        gather(0, 0)

    @pl.when(i + 1 < n_used)
    def _():
        gather(i + 1, 1 - slot)

    @pl.when(i < n_used)
    def _():
        def wait(r, c):
            row_copy(i, r, slot).wait()
            return c

        lax.fori_loop(0, MOE_BLOCK, wait, 0, unroll=8)
        xb = xb_ref[slot].astype(BF16)
        gate = jnp.dot(xb, wg_ref[0], preferred_element_type=F32)
        up = jnp.dot(xb, wu_ref[0], preferred_element_type=F32)
        hid = gate * _sigmoid(gate) * up
        o_ref[...] = jnp.dot(hid.astype(BF16), wd_ref[0], preferred_element_type=F32)

    @pl.when(i >= nu_ref[0])
    def _():
        o_ref[...] = jnp.zeros_like(o_ref)


def _moe(block_expert, n_used, buf_tok, h3, wg, wu, wd):
    cap = buf_tok.shape[0]
    n_blocks = cap // MOE_BLOCK
    grid_spec = pltpu.PrefetchScalarGridSpec(
        num_scalar_prefetch=3,
        grid=(n_blocks,),
        in_specs=[
            pl.BlockSpec(memory_space=pl.ANY),
            pl.BlockSpec((1, D_MODEL, D_EXPERT), lambda i, be, nu, tk: (be[i], 0, 0)),
            pl.BlockSpec((1, D_MODEL, D_EXPERT), lambda i, be, nu, tk: (be[i], 0, 0)),
            pl.BlockSpec((1, D_EXPERT, D_MODEL), lambda i, be, nu, tk: (be[i], 0, 0)),
        ],
        out_specs=pl.BlockSpec((MOE_BLOCK, D_MODEL), lambda i, be, nu, tk: (i, 0)),
        scratch_shapes=[pltpu.VMEM((2, MOE_BLOCK, D_MODEL), F32), pltpu.SemaphoreType.DMA((2,))],
    )
    return pl.pallas_call(
        _moe_kernel,
        grid_spec=grid_spec,
        out_shape=jax.ShapeDtypeStruct((cap, D_MODEL), F32),
        compiler_params=_cparams(("arbitrary",)),
        name="moe",
    )(block_expert, n_used, buf_tok, h3, wg, wu, wd)


def _moe_plan(expert, n_tok):
    n_assign = n_tok * 2
    e_flat = expert.reshape(n_assign)
    onehot = (e_flat[:, None] == jnp.arange(N_EXPERTS, dtype=jnp.int32)[None, :]).astype(jnp.int32)
    counts = jnp.sum(onehot, axis=0)
    rank = jnp.sum((jnp.cumsum(onehot, axis=0) - onehot) * onehot, axis=1)
    padded = (counts + MOE_BLOCK - 1) // MOE_BLOCK * MOE_BLOCK
    p_end = jnp.cumsum(padded)
    p_start = p_end - padded
    pos = (p_start[e_flat] + rank).astype(jnp.int32)
    n_blocks = (n_assign + N_EXPERTS * (MOE_BLOCK - 1) + MOE_BLOCK - 1) // MOE_BLOCK
    cap = n_blocks * MOE_BLOCK
    tok_flat = jnp.repeat(jnp.arange(n_tok, dtype=jnp.int32), 2)
    buf_tok = jnp.zeros((cap,), jnp.int32).at[pos].set(tok_flat)
    block_expert = jnp.minimum(
        jnp.searchsorted(p_end, jnp.arange(n_blocks, dtype=jnp.int32) * MOE_BLOCK, side='right'),
        N_EXPERTS - 1).astype(jnp.int32)
    n_used = (p_end[-1] // MOE_BLOCK).astype(jnp.int32).reshape(1)
    return pos, buf_tok, block_expert, n_used


def _final_kernel(pos_ref, x_ref, rt_ref, y_hbm, g_ref, o_ref, y1_ref, y2_ref, sem):
    i = pl.program_id(0)
    n = pl.num_programs(0)
    tm = x_ref.shape[0]
    slot = i % 2

    def copies(tile, r, s):
        p1 = pos_ref[2 * (tile * tm + r)]
        p2 = pos_ref[2 * (tile * tm + r) + 1]
        return (pltpu.make_async_copy(y_hbm.at[pl.ds(p1, 1)], y1_ref.at[s, pl.ds(r, 1)], sem.at[s]),
                pltpu.make_async_copy(y_hbm.at[pl.ds(p2, 1)], y2_ref.at[s, pl.ds(r, 1)], sem.at[s]))

    def gather(tile, s):
        def start(r, c):
            c1, c2 = copies(tile, r, s)
            c1.start()
            c2.start()
            return c

        lax.fori_loop(0, tm, start, 0, unroll=8)

    @pl.when(i == 0)
    def _():
        gather(0, 0)

    @pl.when(i + 1 < n)
    def _():
        gather(i + 1, 1 - slot)

    def wait(r, c):
        c1, c2 = copies(i, r, slot)
        c1.wait()
        c2.wait()
        return c

    lax.fori_loop(0, tm, wait, 0, unroll=8)
    rt = rt_ref[...]
    g1 = rt[:, 2:3]
    g2 = rt[:, 3:4]
    x3 = x_ref[...] + y1_ref[slot] * g1 + y2_ref[slot] * g2
    o_ref[...] = _rms(x3, g_ref[...])


def _final(pos, x2, rt, y_sorted, g):
    T = x2.shape[0]
    tm = 256
    grid_spec = pltpu.PrefetchScalarGridSpec(
        num_scalar_prefetch=1,
        grid=(T // tm,),
        in_specs=[
            pl.BlockSpec((tm, D_MODEL), lambda i, ps: (i, 0)),
            pl.BlockSpec((tm, LANES), lambda i, ps: (i, 0)),
            pl.BlockSpec(memory_space=pl.ANY),
            pl.BlockSpec((1, D_MODEL), lambda i, ps: (0, 0)),
        ],
        out_specs=pl.BlockSpec((tm, D_MODEL), lambda i, ps: (i, 0)),
        scratch_shapes=[pltpu.VMEM((2, tm, D_MODEL), F32), pltpu.VMEM((2, tm, D_MODEL), F32),
                        pltpu.SemaphoreType.DMA((2,))],
    )
    return pl.pallas_call(
        _final_kernel,
        grid_spec=grid_spec,
        out_shape=jax.ShapeDtypeStruct((T, D_MODEL), F32),
        compiler_params=_cparams(("arbitrary",)),
        name="final",
    )(pos, x2, rt, y_sorted, g)


def _lora_weight(decay_up, iclr_up, gate_up):
    w = jnp.zeros((LORA_COLS, 3 * WIDTH), F32)
    w = w.at[0:DECAY_LORA, 0:WIDTH].set(decay_up)
    w = w.at[DECAY_LORA:DECAY_LORA + ICLR_LORA, WIDTH:2 * WIDTH].set(iclr_up)
    w = w.at[DECAY_LORA + ICLR_LORA:, 2 * WIDTH:].set(gate_up)
    return w


def _router_weight(w_group, w_expert):
    w = jnp.zeros((D_MODEL, LANES), F32)
    w = w.at[:, 0:N_GROUPS].set(w_group)
    return w.at[:, N_GROUPS:N_GROUPS + N_EXPERTS].set(w_expert)


def kernel(x, mem, rel_bias, mem_norm, norm_mix, w_in, tshift_mu, decay_w0, decay_up, iclr_a0,
           iclr_up, gate_up, k_k, k_a, r_k, ln_x_w, ln_x_b, w_o_rwkv, w_o_moba, w_out,
           norm_xattn, w_q_x, w_kv_x, w_o_x, norm_ffn, w_router_group, w_router_expert,
           w_exp_gate, w_exp_up, w_exp_down, norm_final):
    B, T, D = x.shape
    assert B == 1 and D == D_MODEL and T % MOBA_BLOCK == 0 and T // MOBA_BLOCK <= HEAD_DIM
    assert norm_mix.shape[0] == 1
    row = lambda a: a.reshape(1, -1)
    x2d = x.reshape(T, D)
    kv = _memkv(mem.reshape(N_MEM, D), row(mem_norm), w_kv_x[0].astype(BF16))

    ur, qkv, gates = _inproj(x2d, row(norm_mix[0]), w_in[0].astype(BF16))
    prep = _rwkv_prep(ur, row(tshift_mu[0]), row(decay_w0[0]), row(iclr_a0[0]), row(k_k[0]),
                      row(k_a[0]), row(r_k[0]),
                      _lora_weight(decay_up[0], iclr_up[0], gate_up[0]).astype(BF16))
    o_a = _rwkv_core(*prep, row(ln_x_w[0]), row(ln_x_b[0]))

    kmean = _kmean(qkv).reshape(T // MOBA_BLOCK, WIDTH)
    k_aug, vt_aug = _moba_prep(qkv)
    o_b = _moba(qkv, k_aug, vt_aug, _score_mats(kmean), _bias_tiles(rel_bias))

    x2, h3, rt = _merge(x2d, o_a, o_b, gates, w_o_rwkv[0].astype(BF16), w_o_moba[0].astype(BF16),
                        w_out[0].astype(BF16), row(norm_xattn[0]), w_q_x[0].astype(BF16), kv,
                        w_o_x[0].astype(BF16), row(norm_ffn[0]),
                        _router_weight(w_router_group[0], w_router_expert[0]).astype(BF16))

    expert = rt[:, 0:2].astype(jnp.int32)
    pos, buf_tok, block_expert, n_used = _moe_plan(expert, T)
    y_sorted = _moe(block_expert, n_used, buf_tok, h3, w_exp_gate[0].astype(BF16),
                    w_exp_up[0].astype(BF16), w_exp_down[0].astype(BF16))
    out = _final(pos, x2, rt, y_sorted, row(norm_final))
    return out.reshape(B, T, D)
```

```python
import functools
import math

import jax
import jax.numpy as jnp
import numpy as np
from jax import lax
from jax.experimental import pallas as pl
from jax.experimental.pallas import tpu as pltpu

F32 = jnp.float32
BF16 = jnp.bfloat16

D_MODEL = 1024
N_MEM = 256
NORM_EPS = 1e-6
NEG_INF = -1e30

HEADS = 8
HEAD_DIM = 64
WIDTH = HEADS * HEAD_DIM
DECAY_LORA = 64
ICLR_LORA = 64
GATE_LORA = 128
LORA_COLS = DECAY_LORA + ICLR_LORA + GATE_LORA
RWKV_COLS = 3 * WIDTH + LORA_COLS
QKV_COLS = 3 * WIDTH
GATE_COLS = 2 * D_MODEL
IN_COLS = RWKV_COLS + QKV_COLS + GATE_COLS
LN_X_EPS = 64e-5
KK_EPS = 1e-12

MOBA_BLOCK = 256
MOBA_TOP = 3
REL_BUCKETS = 32
REL_MAX_DISTANCE = 4096
N_BIAS_TILES = 14

XATTN_HEADS = 4
XATTN_HEAD_DIM = 128
XATTN_WIDTH = XATTN_HEADS * XATTN_HEAD_DIM

N_GROUPS = 4
EXPERTS_PER_GROUP = 8
N_EXPERTS = N_GROUPS * EXPERTS_PER_GROUP
D_EXPERT = 512
MOE_BLOCK = 128

PREP_BLOCKS = 4
CHUNK = 64
CHUNKS_PER_STEP = 4
LANES = 128
VMEM_LIMIT = 48 * 1024 * 1024

LOG2E = math.log2(math.e)


def _cparams(sem):
    return pltpu.CompilerParams(dimension_semantics=sem, vmem_limit_bytes=VMEM_LIMIT)


def _bdot(a, b):
    return jnp.dot(a.astype(BF16), b.astype(BF16), preferred_element_type=F32)


def _bdot_nt(a, b):
    return lax.dot_general(a.astype(BF16), b.astype(BF16), (((1,), (1,)), ((), ())),
                           preferred_element_type=F32)


def _bdot_tn(a, b):
    return lax.dot_general(a.astype(BF16), b.astype(BF16), (((0,), (0,)), ((), ())),
                           preferred_element_type=F32)


def _rms(x, g):
    return x * lax.rsqrt(jnp.mean(x * x, axis=-1, keepdims=True) + NORM_EPS) * g


def _sigmoid(x):
    return 1.0 / (1.0 + jnp.exp(-x))


def _inproj_kernel(x_ref, g_ref, w_ref, ur_ref, qkv_ref, gate_ref):
    h = _rms(x_ref[...], g_ref[...]).astype(BF16)
    step = 256
    for c0 in range(0, RWKV_COLS, step):
        ur_ref[:, c0:c0 + step] = jnp.dot(h, w_ref[:, c0:c0 + step], preferred_element_type=F32)
    for c0 in range(0, QKV_COLS, step):
        o = jnp.dot(h, w_ref[:, RWKV_COLS + c0:RWKV_COLS + c0 + step], preferred_element_type=F32)
        if c0 < WIDTH:
            o = o * (HEAD_DIM ** -0.5 * LOG2E)
        qkv_ref[:, c0:c0 + step] = o.astype(BF16)
    base = RWKV_COLS + QKV_COLS
    for c0 in range(0, GATE_COLS, step):
        o = jnp.dot(h, w_ref[:, base + c0:base + c0 + step], preferred_element_type=F32)
        gate_ref[:, c0:c0 + step] = o.astype(BF16)


def _inproj(x2d, g, w_bf):
    T = x2d.shape[0]
    tm = 256
    return pl.pallas_call(
        _inproj_kernel,
        grid=(T // tm,),
        in_specs=[
            pl.BlockSpec((tm, D_MODEL), lambda i: (i, 0)),
            pl.BlockSpec((1, D_MODEL), lambda i: (0, 0)),
            pl.BlockSpec((D_MODEL, IN_COLS), lambda i: (0, 0)),
        ],
        out_specs=[
            pl.BlockSpec((tm, RWKV_COLS), lambda i: (i, 0)),
            pl.BlockSpec((tm, QKV_COLS), lambda i: (i, 0)),
            pl.BlockSpec((tm, GATE_COLS), lambda i: (i, 0)),
        ],
        out_shape=[
            jax.ShapeDtypeStruct((T, RWKV_COLS), F32),
            jax.ShapeDtypeStruct((T, QKV_COLS), BF16),
            jax.ShapeDtypeStruct((T, GATE_COLS), BF16),
        ],
        compiler_params=_cparams(("parallel",)),
        name="inproj",
    )(x2d, g, w_bf)


def _head_ones():
    r = lax.broadcasted_iota(jnp.int32, (LANES, LANES), 0) // HEAD_DIM
    c = lax.broadcasted_iota(jnp.int32, (LANES, LANES), 1) // HEAD_DIM
    return jnp.where(r == c, 1.0, 0.0).astype(BF16)


def _head_sum(x, ones):
    parts = [_bdot(x[:, c:c + LANES], ones) for c in range(0, WIDTH, LANES)]
    return jnp.concatenate(parts, axis=-1)


def _rwkv_prep_kernel(u_ref, up_ref, mu_ref, w0_ref, a0_ref, kk_ref, ka_ref, rk_ref, wl_ref,
                      r_ref, lw_ref, k_ref, v_ref, a_ref, b_ref, g_ref, bo_ref):
    i = pl.program_id(0)
    u = u_ref[...]
    tm = u.shape[0]
    prev_last = up_ref[7:8, :] * jnp.where(i > 0, 1.0, 0.0)
    rolled = pltpu.roll(u, 1, 0)
    row = lax.broadcasted_iota(jnp.int32, u.shape, 0)
    u_prev = jnp.where(row == 0, prev_last, rolled)
    u = u + mu_ref[...] * (u_prev - u)
    r = u[:, 0:WIDTH]
    k = u[:, WIDTH:2 * WIDTH]
    v = u[:, 2 * WIDTH:3 * WIDTH]
    lo = u[:, 3 * WIDTH:3 * WIDTH + LORA_COLS]
    lane = lax.broadcasted_iota(jnp.int32, lo.shape, 1)
    act = jnp.where(lane < DECAY_LORA, jnp.tanh(lo),
                    jnp.where(lane < DECAY_LORA + ICLR_LORA, lo, _sigmoid(lo)))
    up = _bdot(act, wl_ref[...])
    z = -(w0_ref[...] + up[:, 0:WIDTH])
    softplus = jnp.maximum(z, 0.0) + jnp.log(1.0 + jnp.exp(-jnp.abs(z)))
    w_log = -softplus - 0.5
    lw = -jnp.exp(w_log)
    iclr = _sigmoid(a0_ref[...] + up[:, WIDTH:2 * WIDTH])
    g = up[:, 2 * WIDTH:3 * WIDTH]
    ones = _head_ones()
    kk = k * kk_ref[...]
    kk = kk * lax.rsqrt(_head_sum(kk * kk, ones) + KK_EPS)
    k2 = k * (1.0 + (iclr - 1.0) * ka_ref[...])
    bonus = _head_sum(r * k2 * rk_ref[...], ones) * v
    r_ref[...] = r
    lw_ref[...] = lw
    k_ref[...] = k2
    v_ref[...] = v
    a_ref[...] = -kk
    b_ref[...] = kk * iclr
    g_ref[...] = g
    bo_ref[...] = bonus


def _rwkv_prep(ur, mu, w0, a0, k_k, k_a, r_k, w_lora):
    T = ur.shape[0]
    tm = 256
    row = lambda w: pl.BlockSpec((1, w), lambda i: (0, 0))
    out = pl.BlockSpec((tm, WIDTH), lambda i: (i, 0))
    return pl.pallas_call(
        _rwkv_prep_kernel,
        grid=(T // tm,),
        in_specs=[
            pl.BlockSpec((tm, RWKV_COLS), lambda i: (i, 0)),
            pl.BlockSpec((8, RWKV_COLS), lambda i: (jnp.maximum(i * (tm // 8) - 1, 0), 0)),
            row(RWKV_COLS), row(WIDTH), row(WIDTH), row(WIDTH), row(WIDTH), row(WIDTH),
            pl.BlockSpec((LORA_COLS, 3 * WIDTH), lambda i: (0, 0)),
        ],
        out_specs=[out] * 8,
        out_shape=[jax.ShapeDtypeStruct((T, WIDTH), F32)] * 8,
        compiler_params=_cparams(("parallel",)),
        name="rwkv_prep",
    )(ur, ur, mu, w0, a0, k_k, k_a, r_k, w_lora)


def _rwkv_core_kernel(r_ref, lw_ref, k_ref, v_ref, a_ref, b_ref, g_ref, bo_ref, lnw_ref, lnb_ref,
                      o_ref, s_ref):
    c = pl.program_id(0)

    @pl.when(c == 0)
    def _():
        s_ref[...] = jnp.zeros_like(s_ref)

    C = CHUNK
    G = CHUNKS_PER_STEP
    ri = lax.broadcasted_iota(jnp.int32, (C, C), 0)
    ci = lax.broadcasted_iota(jnp.int32, (C, C), 1)
    rg = lax.broadcasted_iota(jnp.int32, (G * C, G * C), 0)
    cg = lax.broadcasted_iota(jnp.int32, (G * C, G * C), 1)
    lw = lw_ref[...]
    tri = jnp.where((rg >= cg) & (rg // C == cg // C), 1.0, 0.0).astype(BF16)
    lw_hi = lw.astype(BF16)
    lw_r1 = lw - lw_hi.astype(F32)
    lw_mid = lw_r1.astype(BF16)
    lw_lo = (lw_r1 - lw_mid.astype(F32)).astype(BF16)
    cum = (jnp.dot(tri, lw_hi, preferred_element_type=F32)
           + jnp.dot(tri, lw_mid, preferred_element_type=F32)
           + jnp.dot(tri, lw_lo, preferred_element_type=F32))
    lam = jnp.exp(cum)
    inv_lam = jnp.exp(-cum)
    r_t = r_ref[...] * lam
    a_t = a_ref[...] * jnp.exp(cum - lw)
    b_t = b_ref[...] * inv_lam
    k_t = k_ref[...] * inv_lam
    tots = [cum[g * C + C - 1:g * C + C, :] for g in range(G)]
    rowg = lax.broadcasted_iota(jnp.int32, cum.shape, 0) // C
    tot = tots[G - 1]
    for g in range(G - 2, -1, -1):
        tot = jnp.where(rowg == g, tots[g], tot)
    rest = jnp.exp(tot - cum)
    b_h = b_ref[...] * rest
    k_h = k_ref[...] * rest
    lam_c = [jnp.exp(t) for t in tots]
    v_all = v_ref[...]
    eye = jnp.where(ri == ci, 1.0, 0.0)
    H = range(G * HEADS)
    sls = [slice(h * HEAD_DIM, (h + 1) * HEAD_DIM) for h in range(HEADS)]
    bf = lambda x: x.astype(BF16)
    part = lambda x, i: x[(i // HEADS) * C:(i // HEADS + 1) * C, sls[i % HEADS]]
    at = [bf(part(a_t, i)) for i in H]
    rt = [part(r_t, i) for i in H]
    bt = [bf(part(b_t, i)) for i in H]
    kt = [bf(part(k_t, i)) for i in H]
    bh = [bf(part(b_h, i)) for i in H]
    kh = [bf(part(k_h, i)) for i in H]
    vv = [bf(part(v_all, i)) for i in H]
    ci2 = lax.broadcasted_iota(jnp.int32, (C, 2 * C), 1)
    ri2 = lax.broadcasted_iota(jnp.int32, (C, 2 * C), 0)
    cm2 = jnp.where(ci2 >= C, ci2 - C, ci2)
    left = ci2 < C
    ar = [jnp.concatenate([at[h], bf(rt[h])], axis=0) for h in H]
    bk = [jnp.concatenate([bt[h], kt[h]], axis=0) for h in H]
    g = [_bdot_nt(ar[h], bk[h]) for h in H]
    top = [jnp.where(ri2 > cm2, g[h][0:C], 0.0) for h in H]
    bot = [bf(jnp.where(ri2 >= cm2, g[h][C:2 * C], 0.0)) for h in H]
    a_ab = [top[h][:, 0:C] for h in H]
    akv = [_bdot(top[h][:, C:2 * C], vv[h]) for h in H]
    z = [jnp.concatenate([a_ab[h], eye], axis=1) for h in H]
    for _ in range(6):
        z = [_bdot(z[h][:, 0:C], z[h]) + jnp.where(left, 0.0, z[h]) for h in H]
    tinv = [bf(z[h][:, C:2 * C]) for h in H]
    wu = [_bdot(tinv[h], jnp.concatenate([at[h], bf(akv[h])], axis=1)) for h in H]
    w_m = [bf(wu[h][:, 0:C]) for h in H]
    uv = [jnp.concatenate([bf(wu[h][:, C:2 * C]), vv[h]], axis=0) for h in H]
    q_m = [rt[h] + _bdot(bot[h][:, 0:C], w_m[h]) for h in H]
    y0 = [_bdot(bot[h], uv[h]) for h in H]
    m_k = [_bdot_tn(w_m[h], bh[h]) for h in H]
    n0 = [_bdot_tn(uv[h], jnp.concatenate([bh[h], kh[h]], axis=0)) for h in H]
    state = [s_ref[h] for h in range(HEADS)]
    blocks = []
    for g in range(G):
        outs = []
        for h in range(HEADS):
            i = g * HEADS + h
            y = _bdot_nt(q_m[i], state[h]) + y0[i]
            state[h] = state[h] * lam_c[g][:, sls[h]] + _bdot(state[h], m_k[i]) + n0[i]
            mean = jnp.mean(y, axis=-1, keepdims=True)
            var = jnp.mean(jnp.square(y - mean), axis=-1, keepdims=True)
            outs.append((y - mean) * lax.rsqrt(var + LN_X_EPS))
        blocks.append(jnp.concatenate(outs, axis=-1))
    for h in range(HEADS):
        s_ref[h] = state[h]
    yn = jnp.concatenate(blocks, axis=0)
    yn = yn * lnw_ref[...] + lnb_ref[...] + bo_ref[...]
    o_ref[...] = (yn * g_ref[...]).astype(BF16)


def _rwkv_core(r, lw, k, v, a, b, g, bonus, ln_w, ln_b):
    T = r.shape[0]
    rows = CHUNK * CHUNKS_PER_STEP
    blk = pl.BlockSpec((rows, WIDTH), lambda c: (c, 0))
    row = pl.BlockSpec((1, WIDTH), lambda c: (0, 0))
    return pl.pallas_call(
        _rwkv_core_kernel,
        grid=(T // rows,),
        in_specs=[blk] * 8 + [row, row],
        out_specs=blk,
        out_shape=jax.ShapeDtypeStruct((T, WIDTH), BF16),
        scratch_shapes=[pltpu.VMEM((HEADS, HEAD_DIM, HEAD_DIM), F32)],
        compiler_params=_cparams(("arbitrary",)),
        name="rwkv_core",
    )(r, lw, k, v, a, b, g, bonus, ln_w, ln_b)


def _kmean_kernel(k_ref, o_ref):
    o_ref[0] = jnp.mean(k_ref[...].astype(F32), axis=0, keepdims=True)


def _kmean(qkv):
    T = qkv.shape[0]
    nb = T // MOBA_BLOCK
    return pl.pallas_call(
        _kmean_kernel,
        grid=(nb,),
        in_specs=[pl.BlockSpec((MOBA_BLOCK, WIDTH), lambda j: (j, 1))],
        out_specs=pl.BlockSpec((1, 1, WIDTH), lambda j: (j, 0, 0)),
        out_shape=jax.ShapeDtypeStruct((nb, 1, WIDTH), F32),
        compiler_params=_cparams(("parallel",)),
        name="kmean",
    )(qkv)


def _t5_bucket(dist):
    n = jnp.maximum(dist, 0)
    max_exact = REL_BUCKETS // 2
    nf = jnp.maximum(n, max_exact).astype(jnp.float32)
    large = max_exact + (jnp.log(nf / max_exact) / math.log(REL_MAX_DISTANCE / max_exact)
                         * (REL_BUCKETS - max_exact)).astype(jnp.int32)
    large = jnp.minimum(large, REL_BUCKETS - 1)
    return jnp.where(n < max_exact, n, large)


def _bucket_tiles():
    i = jnp.arange(MOBA_BLOCK)[None, :]
    j = jnp.arange(MOBA_BLOCK)[:, None]
    d = jnp.arange(N_BIAS_TILES + 1)[:, None, None]
    dist = d * MOBA_BLOCK + i - j
    bucket = _t5_bucket(dist)
    bucket = jnp.where(d == N_BIAS_TILES - 1, REL_BUCKETS - 1, bucket)
    return jnp.where((dist < 0) | (d == N_BIAS_TILES), -1, bucket).astype(jnp.int32)


def _bias_tiles_kernel(idx_ref, rb_ref, o_ref):
    h = pl.program_id(0)
    idx = idx_ref[0]
    acc = jnp.where(idx < 0, NEG_INF, 0.0)
    for bkt in range(REL_BUCKETS):
        acc = jnp.where(idx == bkt, rb_ref[bkt, h] * LOG2E, acc)
    o_ref[0, 0] = acc


def _bias_tiles(rel_bias):
    idx = _bucket_tiles()
    return pl.pallas_call(
        _bias_tiles_kernel,
        grid=(HEADS, N_BIAS_TILES + 1),
        in_specs=[
            pl.BlockSpec((1, MOBA_BLOCK, MOBA_BLOCK), lambda h, d: (d, 0, 0)),
            pl.BlockSpec(memory_space=pltpu.SMEM),
        ],
        out_specs=pl.BlockSpec((1, 1, MOBA_BLOCK, MOBA_BLOCK), lambda h, d: (h, d, 0, 0)),
        out_shape=jax.ShapeDtypeStruct((HEADS, N_BIAS_TILES + 1, MOBA_BLOCK, MOBA_BLOCK), F32),
        compiler_params=_cparams(("parallel", "parallel")),
        name="bias_tiles",
    )(idx, rel_bias)


def _moba_kernel(q_ref, ka_ref, vt_ref, r_ref, bias_ref, o_ref, s_ref, p_ref, acc_ref):
    qb = pl.program_id(1)
    B = MOBA_BLOCK
    q_tr = q_ref[...].astype(F32).T
    q_tr_bf = q_tr.astype(BF16)
    row = lax.broadcasted_iota(jnp.int32, (LANES, B), 0)
    big = jnp.int32(1 << 20)
    q_t, q_own_t = [], []
    for h2 in range(2):
        off = HEAD_DIM * (1 - h2)
        inr = (row >= off) & (row < off + HEAD_DIM)
        valid = inr & (row - off < qb)
        sc = jnp.dot(r_ref[0, h2], q_tr_bf, preferred_element_type=F32)
        s = jnp.where(valid, sc, NEG_INF)
        s = jnp.where(inr, s, -jnp.inf)
        sel = jnp.zeros((LANES, B), jnp.bool_)
        for _ in range(MOBA_TOP):
            m = jnp.max(s, axis=0, keepdims=True)
            idx = jnp.min(jnp.where(s == m, row, big), axis=0, keepdims=True)
            pick = row == idx
            sel = jnp.logical_or(sel, pick)
            s = jnp.where(pick, -jnp.inf, s)
        sel = jnp.logical_and(sel, valid)
        q_t.append(jnp.where(inr, jnp.where(sel, 0.0, NEG_INF), q_tr).astype(BF16))
        q_own_t.append(jnp.where(inr, 0.0, q_tr).astype(BF16))

    carry = []
    for h2 in range(2):
        k_own = ka_ref[h2, pl.ds(pl.multiple_of(qb * B, B), B), :]
        s0 = jnp.dot(k_own, q_own_t[h2], preferred_element_type=F32) + bias_ref[h2, 0]
        m0 = jnp.max(s0, axis=0, keepdims=True)
        p0 = jnp.exp2(s0 - m0).astype(BF16)
        carry += [m0, jnp.dot(vt_ref[h2, qb], p0, preferred_element_type=F32)]

    n_tiles = (qb + 1) // 2
    last = jnp.maximum(n_tiles - 1, 0)
    for h2 in range(2):
        acc_ref[h2] = carry[2 * h2 + 1]
        s_ref[1, h2] = jnp.zeros((2 * B, B), F32)
        p_ref[1, h2] = jnp.zeros((2 * B, B), BF16)

    def trip(t, w, stats):
        r = 1 - w
        ok = (t >= 1) & (t <= n_tiles)
        d_a = jnp.where(ok, jnp.clip(qb - 2 * (t - 1), 0, N_BIAS_TILES - 1), N_BIAS_TILES)
        d_b = jnp.where(ok, jnp.clip(qb - 2 * (t - 1) - 1, 0, N_BIAS_TILES - 1), N_BIAS_TILES)
        start = pl.multiple_of(jnp.clip(t, 0, last) * (2 * B), 2 * B)
        t_pv = jnp.clip(t - 2, 0, last)
        out = []
        for h2 in range(2):
            m_prev, alpha_p = stats[2 * h2], stats[2 * h2 + 1]
            acc_ref[h2] = (acc_ref[h2] * alpha_p
                           + jnp.dot(vt_ref[h2, 2 * t_pv], p_ref[r, h2, 0:B], preferred_element_type=F32)
                           + jnp.dot(vt_ref[h2, 2 * t_pv + 1], p_ref[r, h2, B:2 * B],
                                     preferred_element_type=F32))
            s_a = s_ref[r, h2, 0:B] + bias_ref[h2, d_a]
            s_b = s_ref[r, h2, B:2 * B] + bias_ref[h2, d_b]
            m_new = jnp.maximum(m_prev, jnp.max(jnp.maximum(s_a, s_b), axis=0, keepdims=True))
            out += [m_new, jnp.exp2(m_prev - m_new)]
            p_ref[w, h2, 0:B] = jnp.exp2(s_a - m_new).astype(BF16)
            p_ref[w, h2, B:2 * B] = jnp.exp2(s_b - m_new).astype(BF16)
            s_ref[w, h2] = jnp.dot(ka_ref[h2, pl.ds(start, 2 * B), :], q_t[h2],
                                   preferred_element_type=F32)
        return out

    def body(u, stats):
        stats = trip(2 * u, 0, list(stats))
        return tuple(trip(2 * u + 1, 1, stats))

    stats = []
    for h2 in range(2):
        stats += [carry[2 * h2], jnp.ones_like(carry[2 * h2])]
    lax.fori_loop(0, (n_tiles + 3) // 2, body, tuple(stats))
    acc0, acc1 = acc_ref[0], acc_ref[1]
    row = lax.broadcasted_iota(jnp.int32, acc0.shape, 0)
    o_t = jnp.where(row < HEAD_DIM, acc0 / acc0[HEAD_DIM:HEAD_DIM + 1, :], acc1 / acc1[0:1, :])
    o_ref[...] = o_t.T.astype(BF16)


def _moba(qkv, k_aug, vt_aug, r_mats, bias_tiles):
    T = qkv.shape[0]
    nb = T // MOBA_BLOCK
    npair = HEADS // 2
    once = pl.Buffered(1)
    return pl.pallas_call(
        _moba_kernel,
        grid=(npair, nb),
        in_specs=[
            pl.BlockSpec((MOBA_BLOCK, LANES), lambda p, qb: (qb, p)),
            pl.BlockSpec((2, T, LANES), lambda p, qb: (p, 0, 0), pipeline_mode=once),
            pl.BlockSpec((2, nb, LANES, MOBA_BLOCK), lambda p, qb: (p, 0, 0, 0), pipeline_mode=once),
            pl.BlockSpec((1, 2, LANES, LANES), lambda p, qb: (p, 0, 0, 0)),
            pl.BlockSpec((2, N_BIAS_TILES + 1, MOBA_BLOCK, MOBA_BLOCK), lambda p, qb: (p, 0, 0, 0),
                         pipeline_mode=once),
        ],
        out_specs=pl.BlockSpec((MOBA_BLOCK, LANES), lambda p, qb: (qb, p)),
        out_shape=jax.ShapeDtypeStruct((T, WIDTH), BF16),
        scratch_shapes=[pltpu.VMEM((2, 2, 2 * MOBA_BLOCK, MOBA_BLOCK), F32),
                        pltpu.VMEM((2, 2, 2 * MOBA_BLOCK, MOBA_BLOCK), BF16),
                        pltpu.VMEM((2, LANES, MOBA_BLOCK), F32)],
        compiler_params=_cparams(("arbitrary", "arbitrary")),
        name="moba",
    )(qkv, k_aug, vt_aug, r_mats, bias_tiles)


def _moba_prep_kernel(k_ref, v_ref, ka_ref, vt_ref):
    B = MOBA_BLOCK
    lane = lax.broadcasted_iota(jnp.int32, (B, LANES), 1)
    row = lax.broadcasted_iota(jnp.int32, (LANES, B), 0)
    for jj in range(PREP_BLOCKS):
        j = pl.program_id(1) * PREP_BLOCKS + jj
        kblk = k_ref[jj * B:(jj + 1) * B, :]
        v_t = v_ref[jj * B:(jj + 1) * B, :].astype(F32).T
        for h2 in range(2):
            off = HEAD_DIM * (1 - h2)
            inr = (lane >= off) & (lane < off + HEAD_DIM)
            hot = jnp.where(lane == off + j, 1.0, 0.0).astype(BF16)
            ka_ref[h2, jj * B:(jj + 1) * B, :] = jnp.where(inr, hot, kblk)
            ones_rows = (row >= off) & (row < off + HEAD_DIM)
            vt_ref[h2, jj] = jnp.where(ones_rows, 1.0, v_t).astype(BF16)


def _moba_prep(qkv):
    T = qkv.shape[0]
    nb = T // MOBA_BLOCK
    npair = HEADS // 2
    kcol = WIDTH // LANES
    return pl.pallas_call(
        _moba_prep_kernel,
        grid=(npair, nb // PREP_BLOCKS),
        in_specs=[
            pl.BlockSpec((PREP_BLOCKS * MOBA_BLOCK, LANES), lambda p, j: (j, kcol + p)),
            pl.BlockSpec((PREP_BLOCKS * MOBA_BLOCK, LANES), lambda p, j: (j, 2 * kcol + p)),
        ],
        out_specs=[
            pl.BlockSpec((2, PREP_BLOCKS * MOBA_BLOCK, LANES), lambda p, j: (p, j, 0)),
            pl.BlockSpec((2, PREP_BLOCKS, LANES, MOBA_BLOCK), lambda p, j: (p, j, 0, 0)),
        ],
        out_shape=[
            jax.ShapeDtypeStruct((HEADS, T, LANES), BF16),
            jax.ShapeDtypeStruct((HEADS, nb, LANES, MOBA_BLOCK), BF16),
        ],
        compiler_params=_cparams(("parallel", "parallel")),
        name="moba_prep",
    )(qkv, qkv)


def _score_mats(kmean):
    nb = kmean.shape[0]
    km = kmean.reshape(nb, HEADS, HEAD_DIM).transpose(1, 2, 0)
    km = jnp.pad(km, ((0, 0), (0, 0), (0, HEAD_DIM - nb)))
    z = jnp.zeros((HEADS // 2, HEAD_DIM, HEAD_DIM), F32)
    even = jnp.concatenate([jnp.concatenate([z, km[0::2]], axis=2),
                            jnp.concatenate([z, z], axis=2)], axis=1)
    odd = jnp.concatenate([jnp.concatenate([z, z], axis=2),
                           jnp.concatenate([km[1::2], z], axis=2)], axis=1)
    return jnp.swapaxes(jnp.stack([even, odd], axis=1), -1, -2).astype(BF16)


def _memkv_kernel(m_ref, g_ref, w_ref, o_ref):
    h = _rms(m_ref[...], g_ref[...]).astype(BF16)
    o_ref[...] = jnp.dot(h, w_ref[...], preferred_element_type=F32).astype(BF16)


def _memkv(mem2d, g, w_bf):
    return pl.pallas_call(
        _memkv_kernel,
        out_shape=jax.ShapeDtypeStruct((N_MEM, 2 * XATTN_WIDTH), BF16),
        compiler_params=pltpu.CompilerParams(vmem_limit_bytes=VMEM_LIMIT),
        name="memkv",
    )(mem2d, g, w_bf)


def _merge_kernel(x_ref, oa_ref, ob_ref, gt_ref, woa_ref, wob_ref, wout_ref, nx_ref, wq_ref,
                  kv_ref, wo_ref, nf_ref, wr_ref, x2_ref, h3_ref, rt_ref):
    pa = jnp.dot(oa_ref[...], woa_ref[...], preferred_element_type=F32)
    pb = jnp.dot(ob_ref[...], wob_ref[...], preferred_element_type=F32)
    ga = gt_ref[:, 0:D_MODEL].astype(F32)
    gb = gt_ref[:, D_MODEL:2 * D_MODEL].astype(F32)
    merged = _sigmoid(ga) * pa + _sigmoid(gb) * pb
    x1 = x_ref[...] + jnp.dot(merged.astype(BF16), wout_ref[...], preferred_element_type=F32)
    h2 = _rms(x1, nx_ref[...]).astype(BF16)
    q = jnp.dot(h2, wq_ref[...], preferred_element_type=F32).astype(BF16)
    heads = []
    for h in range(XATTN_HEADS):
        sl = slice(h * XATTN_HEAD_DIM, (h + 1) * XATTN_HEAD_DIM)
        km = kv_ref[:, sl]
        vm = kv_ref[:, XATTN_WIDTH + h * XATTN_HEAD_DIM:XATTN_WIDTH + (h + 1) * XATTN_HEAD_DIM]
        lg = lax.dot_general(q[:, sl], km, (((1,), (1,)), ((), ())),
                             preferred_element_type=F32) * (XATTN_HEAD_DIM ** -0.5)
        e = jnp.exp(lg - jnp.max(lg, axis=-1, keepdims=True))
        p = e / jnp.sum(e, axis=-1, keepdims=True)
        heads.append(jnp.dot(p.astype(BF16), vm, preferred_element_type=F32))
    o = jnp.concatenate(heads, axis=-1).astype(BF16)
    x2 = x1 + jnp.dot(o, wo_ref[...], preferred_element_type=F32)
    x2_ref[...] = x2
    h3 = _rms(x2, nf_ref[...])
    h3_ref[...] = h3
    lg = _bdot(h3, wr_ref[...])
    lane = lax.broadcasted_iota(jnp.int32, lg.shape, 1)
    big = jnp.int32(1 << 20)
    is_g = lane < N_GROUPS
    gl = jnp.where(is_g, lg, -jnp.inf)
    gmax = jnp.max(gl, axis=-1, keepdims=True)
    gsel = jnp.min(jnp.where(gl == gmax, lane, big), axis=-1, keepdims=True)
    p_top = 1.0 / jnp.sum(jnp.where(is_g, jnp.exp(gl - gmax), 0.0), axis=-1, keepdims=True)
    e_id = lane - N_GROUPS
    in_grp = (e_id >= gsel * EXPERTS_PER_GROUP) & (e_id < (gsel + 1) * EXPERTS_PER_GROUP)
    el = jnp.where(in_grp, lg, -jnp.inf)
    v1 = jnp.max(el, axis=-1, keepdims=True)
    i1 = jnp.min(jnp.where(el == v1, lane, big), axis=-1, keepdims=True)
    el2 = jnp.where(lane == i1, -jnp.inf, el)
    v2 = jnp.max(el2, axis=-1, keepdims=True)
    i2 = jnp.min(jnp.where(el2 == v2, lane, big), axis=-1, keepdims=True)
    e21 = jnp.exp(v2 - v1)
    g1 = p_top / (1.0 + e21)
    g2 = p_top * e21 / (1.0 + e21)
    rt = jnp.where(lane == 0, (i1 - N_GROUPS).astype(F32),
                   jnp.where(lane == 1, (i2 - N_GROUPS).astype(F32),
                             jnp.where(lane == 2, g1, jnp.where(lane == 3, g2, 0.0))))
    rt_ref[...] = rt


def _merge(x2d, o_a, o_b, gates, w_oa, w_ob, w_out, n_x, w_q, kv, w_o, n_f, w_r):
    T = x2d.shape[0]
    tm = 256
    full = lambda a: pl.BlockSpec(a.shape, lambda i: (0,) * a.ndim)
    tile = lambda w: pl.BlockSpec((tm, w), lambda i: (i, 0))
    return pl.pallas_call(
        _merge_kernel,
        grid=(T // tm,),
        in_specs=[tile(D_MODEL), tile(WIDTH), tile(WIDTH), tile(GATE_COLS), full(w_oa), full(w_ob),
                  full(w_out), full(n_x), full(w_q), full(kv), full(w_o), full(n_f), full(w_r)],
        out_specs=[tile(D_MODEL), tile(D_MODEL), tile(LANES)],
        out_shape=[jax.ShapeDtypeStruct((T, D_MODEL), F32),
                   jax.ShapeDtypeStruct((T, D_MODEL), F32),
                   jax.ShapeDtypeStruct((T, LANES), F32)],
        compiler_params=_cparams(("parallel",)),
        name="merge",
    )(x2d, o_a, o_b, gates, w_oa, w_ob, w_out, n_x, w_q, kv, w_o, n_f, w_r)


def _moe_kernel(be_ref, nu_ref, tok_ref, h_hbm, wg_ref, wu_ref, wd_ref, o_ref, xb_ref, sem):
    i = pl.program_id(0)
    n_used = nu_ref[0]
    slot = i % 2

    def row_copy(blk, r, s):
        t = tok_ref[blk * MOE_BLOCK + r]
        return pltpu.make_async_copy(h_hbm.at[pl.ds(t, 1)], xb_ref.at[s, pl.ds(r, 1)], sem.at[s])

    last_step = pl.num_programs(0) - 1

    def gather(blk, s):
        for r in range(MOE_BLOCK):
            row_copy(blk, r, s).start()

    def wait_rows(s):
        for _ in range(MOE_BLOCK):
            row_copy(0, 0, s).wait()

    @pl.when((i == 0) & (n_used > 0))
    def _():
        gather(0, 0)

    @pl.when((i == n_used) & (i > 0))
    def _():
        wait_rows(slot)

    @pl.when(i < n_used)
    def _():
        wait_rows(slot)
        gather(jnp.minimum(i + 1, last_step), 1 - slot)
        xb = xb_ref[slot].astype(BF16)
        gate = jnp.dot(xb, wg_ref[0], preferred_element_type=F32)
        up = jnp.dot(xb, wu_ref[0], preferred_element_type=F32)
        hid = gate * _sigmoid(gate) * up
        o_ref[...] = jnp.dot(hid.astype(BF16), wd_ref[0], preferred_element_type=F32)

    @pl.when((i == last_step) & (i < n_used))
    def _():
        wait_rows(1 - slot)

    @pl.when(i >= n_used)
    def _():
        o_ref[...] = jnp.zeros_like(o_ref)


def _moe(block_expert, n_used, buf_tok, h3, wg, wu, wd):
    cap = buf_tok.shape[0]
    n_blocks = cap // MOE_BLOCK
    grid_spec = pltpu.PrefetchScalarGridSpec(
        num_scalar_prefetch=3,
        grid=(n_blocks,),
        in_specs=[
            pl.BlockSpec(memory_space=pl.ANY),
            pl.BlockSpec((1, D_MODEL, D_EXPERT), lambda i, be, nu, tk: (be[i], 0, 0)),
            pl.BlockSpec((1, D_MODEL, D_EXPERT), lambda i, be, nu, tk: (be[i], 0, 0)),
            pl.BlockSpec((1, D_EXPERT, D_MODEL), lambda i, be, nu, tk: (be[i], 0, 0)),
        ],
        out_specs=pl.BlockSpec((MOE_BLOCK, D_MODEL), lambda i, be, nu, tk: (i, 0)),
        scratch_shapes=[pltpu.VMEM((2, MOE_BLOCK, D_MODEL), F32), pltpu.SemaphoreType.DMA((2,))],
    )
    return pl.pallas_call(
        _moe_kernel,
        grid_spec=grid_spec,
        out_shape=jax.ShapeDtypeStruct((cap, D_MODEL), F32),
        compiler_params=_cparams(("arbitrary",)),
        name="moe",
    )(block_expert, n_used, buf_tok, h3, wg, wu, wd)


def _rank_kernel(rt_ref, rank_ref, cnt_ref, run_ref):
    i = pl.program_id(0)

    @pl.when(i == 0)
    def _():
        run_ref[...] = jnp.zeros_like(run_ref)

    rt = rt_ref[...]
    tm = rt.shape[0]
    lane = lax.broadcasted_iota(jnp.int32, rt.shape, 1).astype(F32)
    oh1 = jnp.where(lane == rt[:, 0:1], 1.0, 0.0)
    oh2 = jnp.where(lane == rt[:, 1:2], 1.0, 0.0)
    both = oh1 + oh2
    ri = lax.broadcasted_iota(jnp.int32, (tm, tm), 0)
    ci = lax.broadcasted_iota(jnp.int32, (tm, tm), 1)
    before = _bdot(jnp.where(ri > ci, 1.0, 0.0), both) + run_ref[0:1, :]
    r1 = jnp.sum(before * oh1, axis=-1, keepdims=True)
    r2 = jnp.sum(before * oh2, axis=-1, keepdims=True)
    lane_i = lax.broadcasted_iota(jnp.int32, rt.shape, 1)
    rank_ref[...] = jnp.where(lane_i == 0, r1, jnp.where(lane_i == 1, r2, 0.0))
    run_ref[...] = run_ref[...] + jnp.sum(both, axis=0, keepdims=True)
    cnt_ref[...] = run_ref[...]


def _rank(rt):
    T = rt.shape[0]
    tm = 256
    return pl.pallas_call(
        _rank_kernel,
        grid=(T // tm,),
        in_specs=[pl.BlockSpec((tm, LANES), lambda i: (i, 0))],
        out_specs=[pl.BlockSpec((tm, LANES), lambda i: (i, 0)),
                   pl.BlockSpec((8, LANES), lambda i: (0, 0))],
        out_shape=[jax.ShapeDtypeStruct((T, LANES), F32), jax.ShapeDtypeStruct((8, LANES), F32)],
        scratch_shapes=[pltpu.VMEM((8, LANES), F32)],
        compiler_params=_cparams(("arbitrary",)),
        name="rank",
    )(rt)


def _moe_plan(expert, rank, counts, n_tok):
    n_assign = n_tok * 2
    e_flat = expert.reshape(n_assign)
    rank = rank.reshape(n_assign)
    padded = (counts + MOE_BLOCK - 1) // MOE_BLOCK * MOE_BLOCK
    p_end = jnp.cumsum(padded)
    p_start = p_end - padded
    pos = (p_start[e_flat] + rank).astype(jnp.int32)
    n_blocks = (n_assign + N_EXPERTS * (MOE_BLOCK - 1) + MOE_BLOCK - 1) // MOE_BLOCK
    cap = n_blocks * MOE_BLOCK
    tok_flat = jnp.repeat(jnp.arange(n_tok, dtype=jnp.int32), 2)
    buf_tok = jnp.zeros((cap,), jnp.int32).at[pos].set(tok_flat)
    block_expert = jnp.minimum(
        jnp.searchsorted(p_end, jnp.arange(n_blocks, dtype=jnp.int32) * MOE_BLOCK, side='right'),
        N_EXPERTS - 1).astype(jnp.int32)
    n_used = (p_end[-1] // MOE_BLOCK).astype(jnp.int32).reshape(1)
    return pos, buf_tok, block_expert, n_used


def _final_kernel(pos_ref, x_ref, rt_ref, y_hbm, g_ref, o_ref, y1_ref, y2_ref, sem):
    i = pl.program_id(0)
    n = pl.num_programs(0)
    tm = x_ref.shape[0]
    slot = i % 2

    def copies(tile, r, s):
        p1 = pos_ref[2 * (tile * tm + r)]
        p2 = pos_ref[2 * (tile * tm + r) + 1]
        return (pltpu.make_async_copy(y_hbm.at[pl.ds(p1, 1)], y1_ref.at[s, pl.ds(r, 1)], sem.at[s]),
                pltpu.make_async_copy(y_hbm.at[pl.ds(p2, 1)], y2_ref.at[s, pl.ds(r, 1)], sem.at[s]))

    def gather(tile, s):
        for r in range(tm):
            c1, c2 = copies(tile, r, s)
            c1.start()
            c2.start()

    def wait_rows(s):
        for _ in range(tm):
            c1, c2 = copies(0, 0, s)
            c1.wait()
            c2.wait()

    @pl.when(i == 0)
    def _():
        gather(0, 0)

    wait_rows(slot)
    gather(jnp.minimum(i + 1, n - 1), 1 - slot)
    rt = rt_ref[...]
    g1 = rt[:, 2:3]
    g2 = rt[:, 3:4]
    x3 = x_ref[...] + y1_ref[slot] * g1 + y2_ref[slot] * g2
    o_ref[...] = _rms(x3, g_ref[...])

    @pl.when(i == n - 1)
    def _():
        wait_rows(1 - slot)


def _final(pos, x2, rt, y_sorted, g):
    T = x2.shape[0]
    tm = 256
    grid_spec = pltpu.PrefetchScalarGridSpec(
        num_scalar_prefetch=1,
        grid=(T // tm,),
        in_specs=[
            pl.BlockSpec((tm, D_MODEL), lambda i, ps: (i, 0)),
            pl.BlockSpec((tm, LANES), lambda i, ps: (i, 0)),
            pl.BlockSpec(memory_space=pl.ANY),
            pl.BlockSpec((1, D_MODEL), lambda i, ps: (0, 0)),
        ],
        out_specs=pl.BlockSpec((tm, D_MODEL), lambda i, ps: (i, 0)),
        scratch_shapes=[pltpu.VMEM((2, tm, D_MODEL), F32), pltpu.VMEM((2, tm, D_MODEL), F32),
                        pltpu.SemaphoreType.DMA((2,))],
    )
    return pl.pallas_call(
        _final_kernel,
        grid_spec=grid_spec,
        out_shape=jax.ShapeDtypeStruct((T, D_MODEL), F32),
        compiler_params=_cparams(("arbitrary",)),
        name="final",
    )(pos, x2, rt, y_sorted, g)


def _lora_weight(decay_up, iclr_up, gate_up):
    w = jnp.zeros((LORA_COLS, 3 * WIDTH), F32)
    w = w.at[0:DECAY_LORA, 0:WIDTH].set(decay_up)
    w = w.at[DECAY_LORA:DECAY_LORA + ICLR_LORA, WIDTH:2 * WIDTH].set(iclr_up)
    w = w.at[DECAY_LORA + ICLR_LORA:, 2 * WIDTH:].set(gate_up)
    return w


def _router_weight(w_group, w_expert):
    w = jnp.zeros((D_MODEL, LANES), F32)
    w = w.at[:, 0:N_GROUPS].set(w_group)
    return w.at[:, N_GROUPS:N_GROUPS + N_EXPERTS].set(w_expert)


def kernel(x, mem, rel_bias, mem_norm, norm_mix, w_in, tshift_mu, decay_w0, decay_up, iclr_a0,
           iclr_up, gate_up, k_k, k_a, r_k, ln_x_w, ln_x_b, w_o_rwkv, w_o_moba, w_out,
           norm_xattn, w_q_x, w_kv_x, w_o_x, norm_ffn, w_router_group, w_router_expert,
           w_exp_gate, w_exp_up, w_exp_down, norm_final):
    B, T, D = x.shape
    assert B == 1 and D == D_MODEL and T % MOBA_BLOCK == 0 and T // MOBA_BLOCK <= HEAD_DIM
    assert norm_mix.shape[0] == 1 and (T // MOBA_BLOCK) % PREP_BLOCKS == 0
    row = lambda a: a.reshape(1, -1)
    x2d = x.reshape(T, D)
    kv = _memkv(mem.reshape(N_MEM, D), row(mem_norm), w_kv_x[0].astype(BF16))

    ur, qkv, gates = _inproj(x2d, row(norm_mix[0]), w_in[0].astype(BF16))
    prep = _rwkv_prep(ur, row(tshift_mu[0]), row(decay_w0[0]), row(iclr_a0[0]), row(k_k[0]),
                      row(k_a[0]), row(r_k[0]),
                      _lora_weight(decay_up[0], iclr_up[0], gate_up[0]).astype(BF16))
    o_a = _rwkv_core(*prep, row(ln_x_w[0]), row(ln_x_b[0]))

    kmean = _kmean(qkv).reshape(T // MOBA_BLOCK, WIDTH)
    k_aug, vt_aug = _moba_prep(qkv)
    o_b = _moba(qkv, k_aug, vt_aug, _score_mats(kmean), _bias_tiles(rel_bias))

    x2, h3, rt = _merge(x2d, o_a, o_b, gates, w_o_rwkv[0].astype(BF16), w_o_moba[0].astype(BF16),
                        w_out[0].astype(BF16), row(norm_xattn[0]), w_q_x[0].astype(BF16), kv,
                        w_o_x[0].astype(BF16), row(norm_ffn[0]),
                        _router_weight(w_router_group[0], w_router_expert[0]).astype(BF16))

    expert = rt[:, 0:2].astype(jnp.int32)
    rank, counts = _rank(rt)
    pos, buf_tok, block_expert, n_used = _moe_plan(expert, rank[:, 0:2].astype(jnp.int32),
                                                   counts[0, 0:N_EXPERTS].astype(jnp.int32), T)
    y_sorted = _moe(block_expert, n_used, buf_tok, h3, w_exp_gate[0].astype(BF16),
                    w_exp_up[0].astype(BF16), w_exp_down[0].astype(BF16))
    out = _final(pos, x2, rt, y_sorted, row(norm_final))
    return out.reshape(B, T, D)
```

```python
import functools
import math

import jax
import jax.numpy as jnp
import numpy as np
from jax import lax
from jax.experimental import pallas as pl
from jax.experimental.pallas import tpu as pltpu

F32 = jnp.float32
BF16 = jnp.bfloat16

D_MODEL = 1024
N_MEM = 256
NORM_EPS = 1e-6
NEG_INF = -1e30

HEADS = 8
HEAD_DIM = 64
WIDTH = HEADS * HEAD_DIM
DECAY_LORA = 64
ICLR_LORA = 64
GATE_LORA = 128
LORA_COLS = DECAY_LORA + ICLR_LORA + GATE_LORA
RWKV_COLS = 3 * WIDTH + LORA_COLS
QKV_COLS = 3 * WIDTH
GATE_COLS = 2 * D_MODEL
IN_COLS = RWKV_COLS + QKV_COLS + GATE_COLS
LN_X_EPS = 64e-5
KK_EPS = 1e-12

MOBA_BLOCK = 256
MOBA_TOP = 3
REL_BUCKETS = 32
REL_MAX_DISTANCE = 4096
N_BIAS_TILES = 14

XATTN_HEADS = 4
XATTN_HEAD_DIM = 128
XATTN_WIDTH = XATTN_HEADS * XATTN_HEAD_DIM

N_GROUPS = 4
EXPERTS_PER_GROUP = 8
N_EXPERTS = N_GROUPS * EXPERTS_PER_GROUP
D_EXPERT = 512
MOE_BLOCK = 128

PREP_BLOCKS = 4
CHUNK = 64
CHUNKS_PER_STEP = 4
LANES = 128
VMEM_LIMIT = 48 * 1024 * 1024

LOG2E = math.log2(math.e)


def _cparams(sem):
    return pltpu.CompilerParams(dimension_semantics=sem, vmem_limit_bytes=VMEM_LIMIT)


def _bdot(a, b):
    return jnp.dot(a.astype(BF16), b.astype(BF16), preferred_element_type=F32)


def _bdot_nt(a, b):
    return lax.dot_general(a.astype(BF16), b.astype(BF16), (((1,), (1,)), ((), ())),
                           preferred_element_type=F32)


def _bdot_tn(a, b):
    return lax.dot_general(a.astype(BF16), b.astype(BF16), (((0,), (0,)), ((), ())),
                           preferred_element_type=F32)


def _rms(x, g):
    return x * lax.rsqrt(jnp.mean(x * x, axis=-1, keepdims=True) + NORM_EPS) * g


def _sigmoid(x):
    return 1.0 / (1.0 + jnp.exp(-x))


ROW_TILES = D_MODEL // 128


def _store_token_tiles(ref, x):
    n = x.shape[0]
    for s in range(ROW_TILES):
        ref[pl.ds(s, n, stride=ROW_TILES), :] = x[:, s * LANES:(s + 1) * LANES]


def _load_token_tiles(ref, n):
    return [ref[pl.ds(s, n, stride=ROW_TILES), :] for s in range(ROW_TILES)]


def _inproj_kernel(x_ref, g_ref, w_ref, ur_ref, qkv_ref, gate_ref):
    h = _rms(x_ref[...], g_ref[...]).astype(BF16)
    step = 256
    for c0 in range(0, RWKV_COLS, step):
        ur_ref[:, c0:c0 + step] = jnp.dot(h, w_ref[:, c0:c0 + step], preferred_element_type=F32)
    for c0 in range(0, QKV_COLS, step):
        o = jnp.dot(h, w_ref[:, RWKV_COLS + c0:RWKV_COLS + c0 + step], preferred_element_type=F32)
        if c0 < WIDTH:
            o = o * (HEAD_DIM ** -0.5 * LOG2E)
        qkv_ref[:, c0:c0 + step] = o.astype(BF16)
    base = RWKV_COLS + QKV_COLS
    for c0 in range(0, GATE_COLS, step):
        o = jnp.dot(h, w_ref[:, base + c0:base + c0 + step], preferred_element_type=F32)
        gate_ref[:, c0:c0 + step] = o.astype(BF16)


def _inproj(x2d, g, w_bf):
    T = x2d.shape[0]
    tm = 256
    return pl.pallas_call(
        _inproj_kernel,
        grid=(T // tm,),
        in_specs=[
            pl.BlockSpec((tm, D_MODEL), lambda i: (i, 0)),
            pl.BlockSpec((1, D_MODEL), lambda i: (0, 0)),
            pl.BlockSpec((D_MODEL, IN_COLS), lambda i: (0, 0)),
        ],
        out_specs=[
            pl.BlockSpec((tm, RWKV_COLS), lambda i: (i, 0)),
            pl.BlockSpec((tm, QKV_COLS), lambda i: (i, 0)),
            pl.BlockSpec((tm, GATE_COLS), lambda i: (i, 0)),
        ],
        out_shape=[
            jax.ShapeDtypeStruct((T, RWKV_COLS), F32),
            jax.ShapeDtypeStruct((T, QKV_COLS), BF16),
            jax.ShapeDtypeStruct((T, GATE_COLS), BF16),
        ],
        compiler_params=_cparams(("parallel",)),
        name="inproj",
    )(x2d, g, w_bf)


def _head_ones():
    r = lax.broadcasted_iota(jnp.int32, (LANES, LANES), 0) // HEAD_DIM
    c = lax.broadcasted_iota(jnp.int32, (LANES, LANES), 1) // HEAD_DIM
    return jnp.where(r == c, 1.0, 0.0).astype(BF16)


def _head_sum(x, ones):
    parts = [_bdot(x[:, c:c + LANES], ones) for c in range(0, WIDTH, LANES)]
    return jnp.concatenate(parts, axis=-1)


def _rwkv_prep_kernel(u_ref, up_ref, mu_ref, w0_ref, a0_ref, kk_ref, ka_ref, rk_ref, wl_ref,
                      r_ref, lw_ref, k_ref, v_ref, a_ref, b_ref, g_ref, bo_ref):
    i = pl.program_id(0)
    u = u_ref[...]
    tm = u.shape[0]
    prev_last = up_ref[7:8, :] * jnp.where(i > 0, 1.0, 0.0)
    rolled = pltpu.roll(u, 1, 0)
    row = lax.broadcasted_iota(jnp.int32, u.shape, 0)
    u_prev = jnp.where(row == 0, prev_last, rolled)
    u = u + mu_ref[...] * (u_prev - u)
    r = u[:, 0:WIDTH]
    k = u[:, WIDTH:2 * WIDTH]
    v = u[:, 2 * WIDTH:3 * WIDTH]
    lo = u[:, 3 * WIDTH:3 * WIDTH + LORA_COLS]
    lane = lax.broadcasted_iota(jnp.int32, lo.shape, 1)
    act = jnp.where(lane < DECAY_LORA, jnp.tanh(lo),
                    jnp.where(lane < DECAY_LORA + ICLR_LORA, lo, _sigmoid(lo)))
    up = _bdot(act, wl_ref[...])
    z = -(w0_ref[...] + up[:, 0:WIDTH])
    softplus = jnp.maximum(z, 0.0) + jnp.log(1.0 + jnp.exp(-jnp.abs(z)))
    w_log = -softplus - 0.5
    lw = -jnp.exp(w_log)
    iclr = _sigmoid(a0_ref[...] + up[:, WIDTH:2 * WIDTH])
    g = up[:, 2 * WIDTH:3 * WIDTH]
    ones = _head_ones()
    kk = k * kk_ref[...]
    kk = kk * lax.rsqrt(_head_sum(kk * kk, ones) + KK_EPS)
    k2 = k * (1.0 + (iclr - 1.0) * ka_ref[...])
    bonus = _head_sum(r * k2 * rk_ref[...], ones) * v
    r_ref[...] = r
    lw_ref[...] = lw
    k_ref[...] = k2
    v_ref[...] = v
    a_ref[...] = -kk
    b_ref[...] = kk * iclr
    g_ref[...] = g
    bo_ref[...] = bonus


def _rwkv_prep(ur, mu, w0, a0, k_k, k_a, r_k, w_lora):
    T = ur.shape[0]
    tm = 256
    row = lambda w: pl.BlockSpec((1, w), lambda i: (0, 0))
    out = pl.BlockSpec((tm, WIDTH), lambda i: (i, 0))
    return pl.pallas_call(
        _rwkv_prep_kernel,
        grid=(T // tm,),
        in_specs=[
            pl.BlockSpec((tm, RWKV_COLS), lambda i: (i, 0)),
            pl.BlockSpec((8, RWKV_COLS), lambda i: (jnp.maximum(i * (tm // 8) - 1, 0), 0)),
            row(RWKV_COLS), row(WIDTH), row(WIDTH), row(WIDTH), row(WIDTH), row(WIDTH),
            pl.BlockSpec((LORA_COLS, 3 * WIDTH), lambda i: (0, 0)),
        ],
        out_specs=[out] * 8,
        out_shape=[jax.ShapeDtypeStruct((T, WIDTH), F32)] * 8,
        compiler_params=_cparams(("parallel",)),
        name="rwkv_prep",
    )(ur, ur, mu, w0, a0, k_k, k_a, r_k, w_lora)


def _rwkv_core_kernel(r_ref, lw_ref, k_ref, v_ref, a_ref, b_ref, g_ref, bo_ref, lnw_ref, lnb_ref,
                      o_ref, s_ref):
    c = pl.program_id(0)

    @pl.when(c == 0)
    def _():
        s_ref[...] = jnp.zeros_like(s_ref)

    C = CHUNK
    G = CHUNKS_PER_STEP
    ri = lax.broadcasted_iota(jnp.int32, (C, C), 0)
    ci = lax.broadcasted_iota(jnp.int32, (C, C), 1)
    rg = lax.broadcasted_iota(jnp.int32, (G * C, G * C), 0)
    cg = lax.broadcasted_iota(jnp.int32, (G * C, G * C), 1)
    lw = lw_ref[...]
    tri = jnp.where((rg >= cg) & (rg // C == cg // C), 1.0, 0.0).astype(BF16)
    lw_hi = lw.astype(BF16)
    lw_r1 = lw - lw_hi.astype(F32)
    lw_mid = lw_r1.astype(BF16)
    lw_lo = (lw_r1 - lw_mid.astype(F32)).astype(BF16)
    cum = (jnp.dot(tri, lw_hi, preferred_element_type=F32)
           + jnp.dot(tri, lw_mid, preferred_element_type=F32)
           + jnp.dot(tri, lw_lo, preferred_element_type=F32))
    lam = jnp.exp(cum)
    inv_lam = jnp.exp(-cum)
    r_t = r_ref[...] * lam
    a_t = a_ref[...] * jnp.exp(cum - lw)
    b_t = b_ref[...] * inv_lam
    k_t = k_ref[...] * inv_lam
    tots = [cum[g * C + C - 1:g * C + C, :] for g in range(G)]
    rowg = lax.broadcasted_iota(jnp.int32, cum.shape, 0) // C
    tot = tots[G - 1]
    for g in range(G - 2, -1, -1):
        tot = jnp.where(rowg == g, tots[g], tot)
    rest = jnp.exp(tot - cum)
    b_h = b_ref[...] * rest
    k_h = k_ref[...] * rest
    lam_c = [jnp.exp(t) for t in tots]
    v_all = v_ref[...]
    eye = jnp.where(ri == ci, 1.0, 0.0)
    H = range(G * HEADS)
    sls = [slice(h * HEAD_DIM, (h + 1) * HEAD_DIM) for h in range(HEADS)]
    bf = lambda x: x.astype(BF16)
    part = lambda x, i: x[(i // HEADS) * C:(i // HEADS + 1) * C, sls[i % HEADS]]
    at = [bf(part(a_t, i)) for i in H]
    rt = [part(r_t, i) for i in H]
    bt = [bf(part(b_t, i)) for i in H]
    kt = [bf(part(k_t, i)) for i in H]
    bh = [bf(part(b_h, i)) for i in H]
    kh = [bf(part(k_h, i)) for i in H]
    vv = [bf(part(v_all, i)) for i in H]
    ci2 = lax.broadcasted_iota(jnp.int32, (C, 2 * C), 1)
    ri2 = lax.broadcasted_iota(jnp.int32, (C, 2 * C), 0)
    cm2 = jnp.where(ci2 >= C, ci2 - C, ci2)
    left = ci2 < C
    ar = [jnp.concatenate([at[h], bf(rt[h])], axis=0) for h in H]
    bk = [jnp.concatenate([bt[h], kt[h]], axis=0) for h in H]
    g = [_bdot_nt(ar[h], bk[h]) for h in H]
    top = [jnp.where(ri2 > cm2, g[h][0:C], 0.0) for h in H]
    bot = [bf(jnp.where(ri2 >= cm2, g[h][C:2 * C], 0.0)) for h in H]
    a_ab = [top[h][:, 0:C] for h in H]
    akv = [_bdot(top[h][:, C:2 * C], vv[h]) for h in H]
    z = [jnp.concatenate([a_ab[h], eye], axis=1) for h in H]
    for _ in range(6):
        z = [_bdot(z[h][:, 0:C], z[h]) + jnp.where(left, 0.0, z[h]) for h in H]
    tinv = [bf(z[h][:, C:2 * C]) for h in H]
    wu = [_bdot(tinv[h], jnp.concatenate([at[h], bf(akv[h])], axis=1)) for h in H]
    w_m = [bf(wu[h][:, 0:C]) for h in H]
    uv = [jnp.concatenate([bf(wu[h][:, C:2 * C]), vv[h]], axis=0) for h in H]
    q_m = [rt[h] + _bdot(bot[h][:, 0:C], w_m[h]) for h in H]
    y0 = [_bdot(bot[h], uv[h]) for h in H]
    m_k = [_bdot_tn(w_m[h], bh[h]) for h in H]
    n0 = [_bdot_tn(uv[h], jnp.concatenate([bh[h], kh[h]], axis=0)) for h in H]
    state = [s_ref[h] for h in range(HEADS)]
    blocks = []
    for g in range(G):
        outs = []
        for h in range(HEADS):
            i = g * HEADS + h
            y = _bdot_nt(q_m[i], state[h]) + y0[i]
            state[h] = state[h] * lam_c[g][:, sls[h]] + _bdot(state[h], m_k[i]) + n0[i]
            mean = jnp.mean(y, axis=-1, keepdims=True)
            var = jnp.mean(jnp.square(y - mean), axis=-1, keepdims=True)
            outs.append((y - mean) * lax.rsqrt(var + LN_X_EPS))
        blocks.append(jnp.concatenate(outs, axis=-1))
    for h in range(HEADS):
        s_ref[h] = state[h]
    yn = jnp.concatenate(blocks, axis=0)
    yn = yn * lnw_ref[...] + lnb_ref[...] + bo_ref[...]
    o_ref[...] = (yn * g_ref[...]).astype(BF16)


def _rwkv_core(r, lw, k, v, a, b, g, bonus, ln_w, ln_b):
    T = r.shape[0]
    rows = CHUNK * CHUNKS_PER_STEP
    blk = pl.BlockSpec((rows, WIDTH), lambda c: (c, 0))
    row = pl.BlockSpec((1, WIDTH), lambda c: (0, 0))
    return pl.pallas_call(
        _rwkv_core_kernel,
        grid=(T // rows,),
        in_specs=[blk] * 8 + [row, row],
        out_specs=blk,
        out_shape=jax.ShapeDtypeStruct((T, WIDTH), BF16),
        scratch_shapes=[pltpu.VMEM((HEADS, HEAD_DIM, HEAD_DIM), F32)],
        compiler_params=_cparams(("arbitrary",)),
        name="rwkv_core",
    )(r, lw, k, v, a, b, g, bonus, ln_w, ln_b)


def _kmean_kernel(k_ref, o_ref):
    o_ref[0] = jnp.mean(k_ref[...].astype(F32), axis=0, keepdims=True)


def _kmean(qkv):
    T = qkv.shape[0]
    nb = T // MOBA_BLOCK
    return pl.pallas_call(
        _kmean_kernel,
        grid=(nb,),
        in_specs=[pl.BlockSpec((MOBA_BLOCK, WIDTH), lambda j: (j, 1))],
        out_specs=pl.BlockSpec((1, 1, WIDTH), lambda j: (j, 0, 0)),
        out_shape=jax.ShapeDtypeStruct((nb, 1, WIDTH), F32),
        compiler_params=_cparams(("parallel",)),
        name="kmean",
    )(qkv)


def _t5_bucket(dist):
    n = jnp.maximum(dist, 0)
    max_exact = REL_BUCKETS // 2
    nf = jnp.maximum(n, max_exact).astype(jnp.float32)
    large = max_exact + (jnp.log(nf / max_exact) / math.log(REL_MAX_DISTANCE / max_exact)
                         * (REL_BUCKETS - max_exact)).astype(jnp.int32)
    large = jnp.minimum(large, REL_BUCKETS - 1)
    return jnp.where(n < max_exact, n, large)


def _bucket_tiles():
    i = jnp.arange(MOBA_BLOCK)[None, :]
    j = jnp.arange(MOBA_BLOCK)[:, None]
    d = jnp.arange(N_BIAS_TILES + 1)[:, None, None]
    dist = d * MOBA_BLOCK + i - j
    bucket = _t5_bucket(dist)
    bucket = jnp.where(d == N_BIAS_TILES - 1, REL_BUCKETS - 1, bucket)
    return jnp.where((dist < 0) | (d == N_BIAS_TILES), -1, bucket).astype(jnp.int32)


def _bias_tiles_kernel(idx_ref, rb_ref, o_ref):
    h = pl.program_id(0)
    idx = idx_ref[0]
    acc = jnp.where(idx < 0, NEG_INF, 0.0)
    for bkt in range(REL_BUCKETS):
        acc = jnp.where(idx == bkt, rb_ref[bkt, h] * LOG2E, acc)
    o_ref[0, 0] = acc


def _bias_tiles(rel_bias):
    idx = _bucket_tiles()
    return pl.pallas_call(
        _bias_tiles_kernel,
        grid=(HEADS, N_BIAS_TILES + 1),
        in_specs=[
            pl.BlockSpec((1, MOBA_BLOCK, MOBA_BLOCK), lambda h, d: (d, 0, 0)),
            pl.BlockSpec(memory_space=pltpu.SMEM),
        ],
        out_specs=pl.BlockSpec((1, 1, MOBA_BLOCK, MOBA_BLOCK), lambda h, d: (h, d, 0, 0)),
        out_shape=jax.ShapeDtypeStruct((HEADS, N_BIAS_TILES + 1, MOBA_BLOCK, MOBA_BLOCK), F32),
        compiler_params=_cparams(("parallel", "parallel")),
        name="bias_tiles",
    )(idx, rel_bias)


def _moba_kernel(q_ref, ka_ref, vt_ref, r_ref, bias_ref, o_ref, s_ref, p_ref, acc_ref):
    qb = pl.program_id(1)
    B = MOBA_BLOCK
    q_tr = q_ref[...].astype(F32).T
    q_tr_bf = q_tr.astype(BF16)
    row = lax.broadcasted_iota(jnp.int32, (LANES, B), 0)
    big = jnp.int32(1 << 20)
    q_t, q_own_t = [], []
    for h2 in range(2):
        off = HEAD_DIM * (1 - h2)
        inr = (row >= off) & (row < off + HEAD_DIM)
        valid = inr & (row - off < qb)
        sc = jnp.dot(r_ref[0, h2], q_tr_bf, preferred_element_type=F32)
        s = jnp.where(valid, sc, NEG_INF)
        s = jnp.where(inr, s, -jnp.inf)
        sel = jnp.zeros((LANES, B), jnp.bool_)
        for _ in range(MOBA_TOP):
            m = jnp.max(s, axis=0, keepdims=True)
            idx = jnp.min(jnp.where(s == m, row, big), axis=0, keepdims=True)
            pick = row == idx
            sel = jnp.logical_or(sel, pick)
            s = jnp.where(pick, -jnp.inf, s)
        sel = jnp.logical_and(sel, valid)
        q_t.append(jnp.where(inr, jnp.where(sel, 0.0, NEG_INF), q_tr).astype(BF16))
        q_own_t.append(jnp.where(inr, 0.0, q_tr).astype(BF16))

    carry = []
    for h2 in range(2):
        k_own = ka_ref[h2, pl.ds(pl.multiple_of(qb * B, B), B), :]
        s0 = jnp.dot(k_own, q_own_t[h2], preferred_element_type=F32) + bias_ref[h2, 0]
        m0 = jnp.max(s0, axis=0, keepdims=True)
        p0 = jnp.exp2(s0 - m0).astype(BF16)
        carry += [m0, jnp.dot(vt_ref[h2, qb], p0, preferred_element_type=F32)]

    n_tiles = (qb + 1) // 2
    last = jnp.maximum(n_tiles - 1, 0)
    for h2 in range(2):
        acc_ref[h2] = carry[2 * h2 + 1]
        s_ref[1, h2] = jnp.zeros((2 * B, B), F32)
        p_ref[1, h2] = jnp.zeros((2 * B, B), BF16)

    def trip(t, w, stats):
        r = 1 - w
        ok = (t >= 1) & (t <= n_tiles)
        d_a = jnp.where(ok, jnp.clip(qb - 2 * (t - 1), 0, N_BIAS_TILES - 1), N_BIAS_TILES)
        d_b = jnp.where(ok, jnp.clip(qb - 2 * (t - 1) - 1, 0, N_BIAS_TILES - 1), N_BIAS_TILES)
        start = pl.multiple_of(jnp.clip(t, 0, last) * (2 * B), 2 * B)
        t_pv = jnp.clip(t - 2, 0, last)
        out = []
        for h2 in range(2):
            m_prev, alpha_p = stats[2 * h2], stats[2 * h2 + 1]
            acc_ref[h2] = (acc_ref[h2] * alpha_p
                           + jnp.dot(vt_ref[h2, 2 * t_pv], p_ref[r, h2, 0:B], preferred_element_type=F32)
                           + jnp.dot(vt_ref[h2, 2 * t_pv + 1], p_ref[r, h2, B:2 * B],
                                     preferred_element_type=F32))
            s_a = s_ref[r, h2, 0:B] + bias_ref[h2, d_a]
            s_b = s_ref[r, h2, B:2 * B] + bias_ref[h2, d_b]
            m_new = jnp.maximum(m_prev, jnp.max(jnp.maximum(s_a, s_b), axis=0, keepdims=True))
            out += [m_new, jnp.exp2(m_prev - m_new)]
            p_ref[w, h2, 0:B] = jnp.exp2(s_a - m_new).astype(BF16)
            p_ref[w, h2, B:2 * B] = jnp.exp2(s_b - m_new).astype(BF16)
            s_ref[w, h2] = jnp.dot(ka_ref[h2, pl.ds(start, 2 * B), :], q_t[h2],
                                   preferred_element_type=F32)
        return out

    def body(u, stats):
        stats = trip(2 * u, 0, list(stats))
        return tuple(trip(2 * u + 1, 1, stats))

    stats = []
    for h2 in range(2):
        stats += [carry[2 * h2], jnp.ones_like(carry[2 * h2])]
    lax.fori_loop(0, (n_tiles + 3) // 2, body, tuple(stats))
    acc0, acc1 = acc_ref[0], acc_ref[1]
    row = lax.broadcasted_iota(jnp.int32, acc0.shape, 0)
    o_t = jnp.where(row < HEAD_DIM, acc0 / acc0[HEAD_DIM:HEAD_DIM + 1, :], acc1 / acc1[0:1, :])
    o_ref[...] = o_t.T.astype(BF16)


def _moba(qkv, k_aug, vt_aug, r_mats, bias_tiles):
    T = qkv.shape[0]
    nb = T // MOBA_BLOCK
    npair = HEADS // 2
    once = pl.Buffered(1)
    return pl.pallas_call(
        _moba_kernel,
        grid=(npair, nb),
        in_specs=[
            pl.BlockSpec((MOBA_BLOCK, LANES), lambda p, qb: (qb, p)),
            pl.BlockSpec((2, T, LANES), lambda p, qb: (p, 0, 0), pipeline_mode=once),
            pl.BlockSpec((2, nb, LANES, MOBA_BLOCK), lambda p, qb: (p, 0, 0, 0), pipeline_mode=once),
            pl.BlockSpec((1, 2, LANES, LANES), lambda p, qb: (p, 0, 0, 0)),
            pl.BlockSpec((2, N_BIAS_TILES + 1, MOBA_BLOCK, MOBA_BLOCK), lambda p, qb: (p, 0, 0, 0),
                         pipeline_mode=once),
        ],
        out_specs=pl.BlockSpec((MOBA_BLOCK, LANES), lambda p, qb: (qb, p)),
        out_shape=jax.ShapeDtypeStruct((T, WIDTH), BF16),
        scratch_shapes=[pltpu.VMEM((2, 2, 2 * MOBA_BLOCK, MOBA_BLOCK), F32),
                        pltpu.VMEM((2, 2, 2 * MOBA_BLOCK, MOBA_BLOCK), BF16),
                        pltpu.VMEM((2, LANES, MOBA_BLOCK), F32)],
        compiler_params=_cparams(("arbitrary", "arbitrary")),
        name="moba",
    )(qkv, k_aug, vt_aug, r_mats, bias_tiles)


def _moba_prep_kernel(k_ref, v_ref, ka_ref, vt_ref):
    B = MOBA_BLOCK
    lane = lax.broadcasted_iota(jnp.int32, (B, LANES), 1)
    row = lax.broadcasted_iota(jnp.int32, (LANES, B), 0)
    for jj in range(PREP_BLOCKS):
        j = pl.program_id(1) * PREP_BLOCKS + jj
        kblk = k_ref[jj * B:(jj + 1) * B, :]
        v_t = v_ref[jj * B:(jj + 1) * B, :].astype(F32).T
        for h2 in range(2):
            off = HEAD_DIM * (1 - h2)
            inr = (lane >= off) & (lane < off + HEAD_DIM)
            hot = jnp.where(lane == off + j, 1.0, 0.0).astype(BF16)
            ka_ref[h2, jj * B:(jj + 1) * B, :] = jnp.where(inr, hot, kblk)
            ones_rows = (row >= off) & (row < off + HEAD_DIM)
            vt_ref[h2, jj] = jnp.where(ones_rows, 1.0, v_t).astype(BF16)


def _moba_prep(qkv):
    T = qkv.shape[0]
    nb = T // MOBA_BLOCK
    npair = HEADS // 2
    kcol = WIDTH // LANES
    return pl.pallas_call(
        _moba_prep_kernel,
        grid=(npair, nb // PREP_BLOCKS),
        in_specs=[
            pl.BlockSpec((PREP_BLOCKS * MOBA_BLOCK, LANES), lambda p, j: (j, kcol + p)),
            pl.BlockSpec((PREP_BLOCKS * MOBA_BLOCK, LANES), lambda p, j: (j, 2 * kcol + p)),
        ],
        out_specs=[
            pl.BlockSpec((2, PREP_BLOCKS * MOBA_BLOCK, LANES), lambda p, j: (p, j, 0)),
            pl.BlockSpec((2, PREP_BLOCKS, LANES, MOBA_BLOCK), lambda p, j: (p, j, 0, 0)),
        ],
        out_shape=[
            jax.ShapeDtypeStruct((HEADS, T, LANES), BF16),
            jax.ShapeDtypeStruct((HEADS, nb, LANES, MOBA_BLOCK), BF16),
        ],
        compiler_params=_cparams(("parallel", "parallel")),
        name="moba_prep",
    )(qkv, qkv)


def _score_mats(kmean):
    nb = kmean.shape[0]
    km = kmean.reshape(nb, HEADS, HEAD_DIM).transpose(1, 2, 0)
    km = jnp.pad(km, ((0, 0), (0, 0), (0, HEAD_DIM - nb)))
    z = jnp.zeros((HEADS // 2, HEAD_DIM, HEAD_DIM), F32)
    even = jnp.concatenate([jnp.concatenate([z, km[0::2]], axis=2),
                            jnp.concatenate([z, z], axis=2)], axis=1)
    odd = jnp.concatenate([jnp.concatenate([z, z], axis=2),
                           jnp.concatenate([km[1::2], z], axis=2)], axis=1)
    return jnp.swapaxes(jnp.stack([even, odd], axis=1), -1, -2).astype(BF16)


def _memkv_kernel(m_ref, g_ref, w_ref, o_ref):
    h = _rms(m_ref[...], g_ref[...]).astype(BF16)
    o_ref[...] = jnp.dot(h, w_ref[...], preferred_element_type=F32).astype(BF16)


def _memkv(mem2d, g, w_bf):
    return pl.pallas_call(
        _memkv_kernel,
        out_shape=jax.ShapeDtypeStruct((N_MEM, 2 * XATTN_WIDTH), BF16),
        compiler_params=pltpu.CompilerParams(vmem_limit_bytes=VMEM_LIMIT),
        name="memkv",
    )(mem2d, g, w_bf)


def _merge_kernel(x_ref, oa_ref, ob_ref, gt_ref, woa_ref, wob_ref, wout_ref, nx_ref, wq_ref,
                  kv_ref, wo_ref, nf_ref, wr_ref, x2_ref, h3_ref, rt_ref):
    pa = jnp.dot(oa_ref[...], woa_ref[...], preferred_element_type=F32)
    pb = jnp.dot(ob_ref[...], wob_ref[...], preferred_element_type=F32)
    ga = gt_ref[:, 0:D_MODEL].astype(F32)
    gb = gt_ref[:, D_MODEL:2 * D_MODEL].astype(F32)
    merged = _sigmoid(ga) * pa + _sigmoid(gb) * pb
    x1 = x_ref[...] + jnp.dot(merged.astype(BF16), wout_ref[...], preferred_element_type=F32)
    h2 = _rms(x1, nx_ref[...]).astype(BF16)
    q = jnp.dot(h2, wq_ref[...], preferred_element_type=F32).astype(BF16)
    heads = []
    for h in range(XATTN_HEADS):
        sl = slice(h * XATTN_HEAD_DIM, (h + 1) * XATTN_HEAD_DIM)
        km = kv_ref[:, sl]
        vm = kv_ref[:, XATTN_WIDTH + h * XATTN_HEAD_DIM:XATTN_WIDTH + (h + 1) * XATTN_HEAD_DIM]
        lg = lax.dot_general(q[:, sl], km, (((1,), (1,)), ((), ())),
                             preferred_element_type=F32) * (XATTN_HEAD_DIM ** -0.5)
        e = jnp.exp(lg - jnp.max(lg, axis=-1, keepdims=True))
        p = e / jnp.sum(e, axis=-1, keepdims=True)
        heads.append(jnp.dot(p.astype(BF16), vm, preferred_element_type=F32))
    o = jnp.concatenate(heads, axis=-1).astype(BF16)
    x2 = x1 + jnp.dot(o, wo_ref[...], preferred_element_type=F32)
    x2_ref[...] = x2
    h3 = _rms(x2, nf_ref[...])
    _store_token_tiles(h3_ref, h3)
    lg = _bdot(h3, wr_ref[...])
    lane = lax.broadcasted_iota(jnp.int32, lg.shape, 1)
    big = jnp.int32(1 << 20)
    is_g = lane < N_GROUPS
    gl = jnp.where(is_g, lg, -jnp.inf)
    gmax = jnp.max(gl, axis=-1, keepdims=True)
    gsel = jnp.min(jnp.where(gl == gmax, lane, big), axis=-1, keepdims=True)
    p_top = 1.0 / jnp.sum(jnp.where(is_g, jnp.exp(gl - gmax), 0.0), axis=-1, keepdims=True)
    e_id = lane - N_GROUPS
    in_grp = (e_id >= gsel * EXPERTS_PER_GROUP) & (e_id < (gsel + 1) * EXPERTS_PER_GROUP)
    el = jnp.where(in_grp, lg, -jnp.inf)
    v1 = jnp.max(el, axis=-1, keepdims=True)
    i1 = jnp.min(jnp.where(el == v1, lane, big), axis=-1, keepdims=True)
    el2 = jnp.where(lane == i1, -jnp.inf, el)
    v2 = jnp.max(el2, axis=-1, keepdims=True)
    i2 = jnp.min(jnp.where(el2 == v2, lane, big), axis=-1, keepdims=True)
    e21 = jnp.exp(v2 - v1)
    g1 = p_top / (1.0 + e21)
    g2 = p_top * e21 / (1.0 + e21)
    rt = jnp.where(lane == 0, (i1 - N_GROUPS).astype(F32),
                   jnp.where(lane == 1, (i2 - N_GROUPS).astype(F32),
                             jnp.where(lane == 2, g1, jnp.where(lane == 3, g2, 0.0))))
    rt_ref[...] = rt


def _merge(x2d, o_a, o_b, gates, w_oa, w_ob, w_out, n_x, w_q, kv, w_o, n_f, w_r):
    T = x2d.shape[0]
    tm = 256
    full = lambda a: pl.BlockSpec(a.shape, lambda i: (0,) * a.ndim)
    tile = lambda w: pl.BlockSpec((tm, w), lambda i: (i, 0))
    return pl.pallas_call(
        _merge_kernel,
        grid=(T // tm,),
        in_specs=[tile(D_MODEL), tile(WIDTH), tile(WIDTH), tile(GATE_COLS), full(w_oa), full(w_ob),
                  full(w_out), full(n_x), full(w_q), full(kv), full(w_o), full(n_f), full(w_r)],
        out_specs=[tile(D_MODEL), pl.BlockSpec((tm * ROW_TILES, LANES), lambda i: (i, 0)), tile(LANES)],
        out_shape=[jax.ShapeDtypeStruct((T, D_MODEL), F32),
                   jax.ShapeDtypeStruct((T * ROW_TILES, LANES), F32),
                   jax.ShapeDtypeStruct((T, LANES), F32)],
        compiler_params=_cparams(("parallel",)),
        name="merge",
    )(x2d, o_a, o_b, gates, w_oa, w_ob, w_out, n_x, w_q, kv, w_o, n_f, w_r)


def _moe_kernel(be_ref, nu_ref, tok_ref, h_hbm, wg_ref, wu_ref, wd_ref, o_ref, xb_ref, sem):
    i = pl.program_id(0)
    n_used = nu_ref[0]
    slot = i % 2

    def row_copy(blk, r, s):
        t = tok_ref[blk * MOE_BLOCK + r]
        src = h_hbm.at[pl.ds(pl.multiple_of(t * ROW_TILES, ROW_TILES), ROW_TILES), :]
        return pltpu.make_async_copy(src, xb_ref.at[s, pl.ds(r * ROW_TILES, ROW_TILES), :], sem.at[s])

    last_step = pl.num_programs(0) - 1

    def gather(blk, s):
        for r in range(MOE_BLOCK):
            row_copy(blk, r, s).start()

    def wait_rows(s):
        for _ in range(MOE_BLOCK):
            row_copy(0, 0, s).wait()

    @pl.when((i == 0) & (n_used > 0))
    def _():
        gather(0, 0)

    @pl.when((i == n_used) & (i > 0))
    def _():
        wait_rows(slot)

    @pl.when(i < n_used)
    def _():
        wait_rows(slot)
        gather(jnp.minimum(i + 1, last_step), 1 - slot)
        xb = jnp.concatenate([x.astype(BF16) for x in _load_token_tiles(xb_ref.at[slot], MOE_BLOCK)],
                             axis=-1)
        gate = jnp.dot(xb, wg_ref[0], preferred_element_type=F32)
        up = jnp.dot(xb, wu_ref[0], preferred_element_type=F32)
        hid = gate * _sigmoid(gate) * up
        _store_token_tiles(o_ref, jnp.dot(hid.astype(BF16), wd_ref[0], preferred_element_type=F32))

    @pl.when((i == last_step) & (i < n_used))
    def _():
        wait_rows(1 - slot)

    @pl.when(i >= n_used)
    def _():
        o_ref[...] = jnp.zeros_like(o_ref)


def _moe(block_expert, n_used, buf_tok, h3, wg, wu, wd):
    cap = buf_tok.shape[0]
    n_blocks = cap // MOE_BLOCK
    grid_spec = pltpu.PrefetchScalarGridSpec(
        num_scalar_prefetch=3,
        grid=(n_blocks,),
        in_specs=[
            pl.BlockSpec(memory_space=pl.ANY),
            pl.BlockSpec((1, D_MODEL, D_EXPERT), lambda i, be, nu, tk: (be[i], 0, 0)),
            pl.BlockSpec((1, D_MODEL, D_EXPERT), lambda i, be, nu, tk: (be[i], 0, 0)),
            pl.BlockSpec((1, D_EXPERT, D_MODEL), lambda i, be, nu, tk: (be[i], 0, 0)),
        ],
        out_specs=pl.BlockSpec((MOE_BLOCK * ROW_TILES, LANES), lambda i, be, nu, tk: (i, 0)),
        scratch_shapes=[pltpu.VMEM((2, MOE_BLOCK * ROW_TILES, LANES), F32),
                        pltpu.SemaphoreType.DMA((2,))],
    )
    return pl.pallas_call(
        _moe_kernel,
        grid_spec=grid_spec,
        out_shape=jax.ShapeDtypeStruct((cap * ROW_TILES, LANES), F32),
        compiler_params=_cparams(("arbitrary",)),
        name="moe",
    )(block_expert, n_used, buf_tok, h3, wg, wu, wd)


def _rank_kernel(rt_ref, rank_ref, cnt_ref, run_ref):
    i = pl.program_id(0)

    @pl.when(i == 0)
    def _():
        run_ref[...] = jnp.zeros_like(run_ref)

    rt = rt_ref[...]
    tm = rt.shape[0]
    lane = lax.broadcasted_iota(jnp.int32, rt.shape, 1).astype(F32)
    oh1 = jnp.where(lane == rt[:, 0:1], 1.0, 0.0)
    oh2 = jnp.where(lane == rt[:, 1:2], 1.0, 0.0)
    both = oh1 + oh2
    ri = lax.broadcasted_iota(jnp.int32, (tm, tm), 0)
    ci = lax.broadcasted_iota(jnp.int32, (tm, tm), 1)
    before = _bdot(jnp.where(ri > ci, 1.0, 0.0), both) + run_ref[0:1, :]
    r1 = jnp.sum(before * oh1, axis=-1, keepdims=True)
    r2 = jnp.sum(before * oh2, axis=-1, keepdims=True)
    lane_i = lax.broadcasted_iota(jnp.int32, rt.shape, 1)
    rank_ref[...] = jnp.where(lane_i == 0, r1, jnp.where(lane_i == 1, r2, 0.0))
    run_ref[...] = run_ref[...] + jnp.sum(both, axis=0, keepdims=True)
    cnt_ref[...] = run_ref[...]


def _rank(rt):
    T = rt.shape[0]
    tm = 256
    return pl.pallas_call(
        _rank_kernel,
        grid=(T // tm,),
        in_specs=[pl.BlockSpec((tm, LANES), lambda i: (i, 0))],
        out_specs=[pl.BlockSpec((tm, LANES), lambda i: (i, 0)),
                   pl.BlockSpec((8, LANES), lambda i: (0, 0))],
        out_shape=[jax.ShapeDtypeStruct((T, LANES), F32), jax.ShapeDtypeStruct((8, LANES), F32)],
        scratch_shapes=[pltpu.VMEM((8, LANES), F32)],
        compiler_params=_cparams(("arbitrary",)),
        name="rank",
    )(rt)


def _moe_plan(expert, rank, counts, n_tok):
    n_assign = n_tok * 2
    e_flat = expert.reshape(n_assign)
    rank = rank.reshape(n_assign)
    padded = (counts + MOE_BLOCK - 1) // MOE_BLOCK * MOE_BLOCK
    p_end = jnp.cumsum(padded)
    p_start = p_end - padded
    pos = (p_start[e_flat] + rank).astype(jnp.int32)
    n_blocks = (n_assign + N_EXPERTS * (MOE_BLOCK - 1) + MOE_BLOCK - 1) // MOE_BLOCK
    cap = n_blocks * MOE_BLOCK
    tok_flat = jnp.repeat(jnp.arange(n_tok, dtype=jnp.int32), 2)
    buf_tok = jnp.zeros((cap,), jnp.int32).at[pos].set(tok_flat)
    block_start = jnp.arange(n_blocks, dtype=jnp.int32) * MOE_BLOCK
    block_expert = jnp.minimum(jnp.sum(p_end[None, :] <= block_start[:, None], axis=1),
                               N_EXPERTS - 1).astype(jnp.int32)
    n_used = (p_end[-1] // MOE_BLOCK).astype(jnp.int32).reshape(1)
    return pos, buf_tok, block_expert, n_used


def _final_kernel(pos_ref, x_ref, rt_ref, y_hbm, g_ref, o_ref, y1_ref, y2_ref, sem):
    i = pl.program_id(0)
    n = pl.num_programs(0)
    tm = x_ref.shape[0]
    slot = i % 2

    def copies(tile, r, s):
        p1 = pos_ref[2 * (tile * tm + r)]
        p2 = pos_ref[2 * (tile * tm + r) + 1]
        src = lambda p: y_hbm.at[pl.ds(pl.multiple_of(p * ROW_TILES, ROW_TILES), ROW_TILES), :]
        dst = lambda ref: ref.at[s, pl.ds(r * ROW_TILES, ROW_TILES), :]
        return (pltpu.make_async_copy(src(p1), dst(y1_ref), sem.at[s]),
                pltpu.make_async_copy(src(p2), dst(y2_ref), sem.at[s]))

    def gather(tile, s):
        for r in range(tm):
            c1, c2 = copies(tile, r, s)
            c1.start()
            c2.start()

    def wait_rows(s):
        for _ in range(tm):
            c1, c2 = copies(0, 0, s)
            c1.wait()
            c2.wait()

    @pl.when(i == 0)
    def _():
        gather(0, 0)

    wait_rows(slot)
    gather(jnp.minimum(i + 1, n - 1), 1 - slot)
    rt = rt_ref[...]
    g1 = rt[:, 2:3]
    g2 = rt[:, 3:4]
    y1 = jnp.concatenate(_load_token_tiles(y1_ref.at[slot], tm), axis=-1)
    y2 = jnp.concatenate(_load_token_tiles(y2_ref.at[slot], tm), axis=-1)
    x3 = x_ref[...] + y1 * g1 + y2 * g2
    o_ref[...] = _rms(x3, g_ref[...])

    @pl.when(i == n - 1)
    def _():
        wait_rows(1 - slot)


def _final(pos, x2, rt, y_sorted, g):
    T = x2.shape[0]
    tm = 256
    grid_spec = pltpu.PrefetchScalarGridSpec(
        num_scalar_prefetch=1,
        grid=(T // tm,),
        in_specs=[
            pl.BlockSpec((tm, D_MODEL), lambda i, ps: (i, 0)),
            pl.BlockSpec((tm, LANES), lambda i, ps: (i, 0)),
            pl.BlockSpec(memory_space=pl.ANY),
            pl.BlockSpec((1, D_MODEL), lambda i, ps: (0, 0)),
        ],
        out_specs=pl.BlockSpec((tm, D_MODEL), lambda i, ps: (i, 0)),
        scratch_shapes=[pltpu.VMEM((2, tm * ROW_TILES, LANES), F32),
                        pltpu.VMEM((2, tm * ROW_TILES, LANES), F32),
                        pltpu.SemaphoreType.DMA((2,))],
    )
    return pl.pallas_call(
        _final_kernel,
        grid_spec=grid_spec,
        out_shape=jax.ShapeDtypeStruct((T, D_MODEL), F32),
        compiler_params=_cparams(("arbitrary",)),
        name="final",
    )(pos, x2, rt, y_sorted, g)


def _lora_weight(decay_up, iclr_up, gate_up):
    w = jnp.zeros((LORA_COLS, 3 * WIDTH), F32)
    w = w.at[0:DECAY_LORA, 0:WIDTH].set(decay_up)
    w = w.at[DECAY_LORA:DECAY_LORA + ICLR_LORA, WIDTH:2 * WIDTH].set(iclr_up)
    w = w.at[DECAY_LORA + ICLR_LORA:, 2 * WIDTH:].set(gate_up)
    return w


def _router_weight(w_group, w_expert):
    w = jnp.zeros((D_MODEL, LANES), F32)
    w = w.at[:, 0:N_GROUPS].set(w_group)
    return w.at[:, N_GROUPS:N_GROUPS + N_EXPERTS].set(w_expert)


def kernel(x, mem, rel_bias, mem_norm, norm_mix, w_in, tshift_mu, decay_w0, decay_up, iclr_a0,
           iclr_up, gate_up, k_k, k_a, r_k, ln_x_w, ln_x_b, w_o_rwkv, w_o_moba, w_out,
           norm_xattn, w_q_x, w_kv_x, w_o_x, norm_ffn, w_router_group, w_router_expert,
           w_exp_gate, w_exp_up, w_exp_down, norm_final):
    B, T, D = x.shape
    assert B == 1 and D == D_MODEL and T % MOBA_BLOCK == 0 and T // MOBA_BLOCK <= HEAD_DIM
    assert norm_mix.shape[0] == 1 and (T // MOBA_BLOCK) % PREP_BLOCKS == 0
    row = lambda a: a.reshape(1, -1)
    x2d = x.reshape(T, D)
    kv = _memkv(mem.reshape(N_MEM, D), row(mem_norm), w_kv_x[0].astype(BF16))

    ur, qkv, gates = _inproj(x2d, row(norm_mix[0]), w_in[0].astype(BF16))
    prep = _rwkv_prep(ur, row(tshift_mu[0]), row(decay_w0[0]), row(iclr_a0[0]), row(k_k[0]),
                      row(k_a[0]), row(r_k[0]),
                      _lora_weight(decay_up[0], iclr_up[0], gate_up[0]).astype(BF16))
    o_a = _rwkv_core(*prep, row(ln_x_w[0]), row(ln_x_b[0]))

    kmean = _kmean(qkv).reshape(T // MOBA_BLOCK, WIDTH)
    k_aug, vt_aug = _moba_prep(qkv)
    o_b = _moba(qkv, k_aug, vt_aug, _score_mats(kmean), _bias_tiles(rel_bias))

    x2, h3, rt = _merge(x2d, o_a, o_b, gates, w_o_rwkv[0].astype(BF16), w_o_moba[0].astype(BF16),
                        w_out[0].astype(BF16), row(norm_xattn[0]), w_q_x[0].astype(BF16), kv,
                        w_o_x[0].astype(BF16), row(norm_ffn[0]),
                        _router_weight(w_router_group[0], w_router_expert[0]).astype(BF16))

    expert = rt[:, 0:2].astype(jnp.int32)
    rank, counts = _rank(rt)
    pos, buf_tok, block_expert, n_used = _moe_plan(expert, rank[:, 0:2].astype(jnp.int32),
                                                   counts[0, 0:N_EXPERTS].astype(jnp.int32), T)
    y_sorted = _moe(block_expert, n_used, buf_tok, h3, w_exp_gate[0].astype(BF16),
                    w_exp_up[0].astype(BF16), w_exp_down[0].astype(BF16))
    out = _final(pos, x2, rt, y_sorted, row(norm_final))
    return out.reshape(B, T, D)
```

```python
import functools
import math

import jax
import jax.numpy as jnp
import numpy as np
from jax import lax
from jax.experimental import pallas as pl
from jax.experimental.pallas import tpu as pltpu

F32 = jnp.float32
BF16 = jnp.bfloat16

D_MODEL = 1024
N_MEM = 256
NORM_EPS = 1e-6
NEG_INF = -1e30

HEADS = 8
HEAD_DIM = 64
WIDTH = HEADS * HEAD_DIM
DECAY_LORA = 64
ICLR_LORA = 64
GATE_LORA = 128
LORA_COLS = DECAY_LORA + ICLR_LORA + GATE_LORA
RWKV_COLS = 3 * WIDTH + LORA_COLS
QKV_COLS = 3 * WIDTH
GATE_COLS = 2 * D_MODEL
IN_COLS = RWKV_COLS + QKV_COLS + GATE_COLS
LN_X_EPS = 64e-5
KK_EPS = 1e-12

MOBA_BLOCK = 256
MOBA_TOP = 3
REL_BUCKETS = 32
REL_MAX_DISTANCE = 4096
N_BIAS_TILES = 14

XATTN_HEADS = 4
XATTN_HEAD_DIM = 128
XATTN_WIDTH = XATTN_HEADS * XATTN_HEAD_DIM

N_GROUPS = 4
EXPERTS_PER_GROUP = 8
N_EXPERTS = N_GROUPS * EXPERTS_PER_GROUP
D_EXPERT = 512
MOE_BLOCK = 256

PREP_BLOCKS = 4
VT_ROWS = HEAD_DIM + 16
CHUNK = 64
CHUNKS_PER_STEP = 4
LANES = 128
VMEM_LIMIT = 48 * 1024 * 1024

LOG2E = math.log2(math.e)


def _cparams(sem):
    return pltpu.CompilerParams(dimension_semantics=sem, vmem_limit_bytes=VMEM_LIMIT)


def _bdot(a, b):
    return jnp.dot(a.astype(BF16), b.astype(BF16), preferred_element_type=F32)


def _bdot_nt(a, b):
    return lax.dot_general(a.astype(BF16), b.astype(BF16), (((1,), (1,)), ((), ())),
                           preferred_element_type=F32)


def _bdot_tn(a, b):
    return lax.dot_general(a.astype(BF16), b.astype(BF16), (((0,), (0,)), ((), ())),
                           preferred_element_type=F32)


def _rms(x, g):
    return x * lax.rsqrt(jnp.mean(x * x, axis=-1, keepdims=True) + NORM_EPS) * g


def _sigmoid(x):
    return 1.0 / (1.0 + jnp.exp(-x))


ROW_TILES = D_MODEL // 128


def _store_token_tiles(ref, x):
    n = x.shape[0]
    for s in range(ROW_TILES):
        ref[pl.ds(s, n, stride=ROW_TILES), :] = x[:, s * LANES:(s + 1) * LANES]


def _load_token_tiles(ref, n):
    return [ref[pl.ds(s, n, stride=ROW_TILES), :] for s in range(ROW_TILES)]


def _inproj_kernel(x_ref, g_ref, w_ref, ur_ref, qkv_ref, gate_ref):
    h = _rms(x_ref[...], g_ref[...]).astype(BF16)
    step = 256
    for c0 in range(0, RWKV_COLS, step):
        ur_ref[:, c0:c0 + step] = jnp.dot(h, w_ref[:, c0:c0 + step], preferred_element_type=F32)
    for c0 in range(0, QKV_COLS, step):
        o = jnp.dot(h, w_ref[:, RWKV_COLS + c0:RWKV_COLS + c0 + step], preferred_element_type=F32)
        if c0 < WIDTH:
            o = o * (HEAD_DIM ** -0.5 * LOG2E)
        qkv_ref[:, c0:c0 + step] = o.astype(BF16)
    base = RWKV_COLS + QKV_COLS
    for c0 in range(0, GATE_COLS, step):
        o = jnp.dot(h, w_ref[:, base + c0:base + c0 + step], preferred_element_type=F32)
        gate_ref[:, c0:c0 + step] = o.astype(BF16)


def _inproj(x2d, g, w_bf):
    T = x2d.shape[0]
    tm = 256
    return pl.pallas_call(
        _inproj_kernel,
        grid=(T // tm,),
        in_specs=[
            pl.BlockSpec((tm, D_MODEL), lambda i: (i, 0)),
            pl.BlockSpec((1, D_MODEL), lambda i: (0, 0)),
            pl.BlockSpec((D_MODEL, IN_COLS), lambda i: (0, 0)),
        ],
        out_specs=[
            pl.BlockSpec((tm, RWKV_COLS), lambda i: (i, 0)),
            pl.BlockSpec((tm, QKV_COLS), lambda i: (i, 0)),
            pl.BlockSpec((tm, GATE_COLS), lambda i: (i, 0)),
        ],
        out_shape=[
            jax.ShapeDtypeStruct((T, RWKV_COLS), F32),
            jax.ShapeDtypeStruct((T, QKV_COLS), BF16),
            jax.ShapeDtypeStruct((T, GATE_COLS), BF16),
        ],
        compiler_params=_cparams(("parallel",)),
        name="inproj",
    )(x2d, g, w_bf)


def _head_ones():
    r = lax.broadcasted_iota(jnp.int32, (LANES, LANES), 0) // HEAD_DIM
    c = lax.broadcasted_iota(jnp.int32, (LANES, LANES), 1) // HEAD_DIM
    return jnp.where(r == c, 1.0, 0.0).astype(BF16)


def _head_sum(x, ones):
    parts = [_bdot(x[:, c:c + LANES], ones) for c in range(0, WIDTH, LANES)]
    return jnp.concatenate(parts, axis=-1)


def _rwkv_prep_kernel(u_ref, up_ref, mu_ref, w0_ref, a0_ref, kk_ref, ka_ref, rk_ref, wl_ref,
                      r_ref, lw_ref, k_ref, v_ref, a_ref, b_ref, g_ref, bo_ref):
    i = pl.program_id(0)
    u = u_ref[...]
    tm = u.shape[0]
    prev_last = up_ref[7:8, :] * jnp.where(i > 0, 1.0, 0.0)
    rolled = pltpu.roll(u, 1, 0)
    row = lax.broadcasted_iota(jnp.int32, u.shape, 0)
    u_prev = jnp.where(row == 0, prev_last, rolled)
    u = u + mu_ref[...] * (u_prev - u)
    r = u[:, 0:WIDTH]
    k = u[:, WIDTH:2 * WIDTH]
    v = u[:, 2 * WIDTH:3 * WIDTH]
    lo = u[:, 3 * WIDTH:3 * WIDTH + LORA_COLS]
    lane = lax.broadcasted_iota(jnp.int32, lo.shape, 1)
    act = jnp.where(lane < DECAY_LORA, jnp.tanh(lo),
                    jnp.where(lane < DECAY_LORA + ICLR_LORA, lo, _sigmoid(lo)))
    up = _bdot(act, wl_ref[...])
    z = -(w0_ref[...] + up[:, 0:WIDTH])
    softplus = jnp.maximum(z, 0.0) + jnp.log(1.0 + jnp.exp(-jnp.abs(z)))
    w_log = -softplus - 0.5
    lw = -jnp.exp(w_log)
    iclr = _sigmoid(a0_ref[...] + up[:, WIDTH:2 * WIDTH])
    g = up[:, 2 * WIDTH:3 * WIDTH]
    ones = _head_ones()
    kk = k * kk_ref[...]
    kk = kk * lax.rsqrt(_head_sum(kk * kk, ones) + KK_EPS)
    k2 = k * (1.0 + (iclr - 1.0) * ka_ref[...])
    bonus = _head_sum(r * k2 * rk_ref[...], ones) * v
    r_ref[...] = r
    lw_ref[...] = lw
    k_ref[...] = k2
    v_ref[...] = v
    a_ref[...] = -kk
    b_ref[...] = kk * iclr
    g_ref[...] = g
    bo_ref[...] = bonus


def _rwkv_prep(ur, mu, w0, a0, k_k, k_a, r_k, w_lora):
    T = ur.shape[0]
    tm = 256
    row = lambda w: pl.BlockSpec((1, w), lambda i: (0, 0))
    out = pl.BlockSpec((tm, WIDTH), lambda i: (i, 0))
    return pl.pallas_call(
        _rwkv_prep_kernel,
        grid=(T // tm,),
        in_specs=[
            pl.BlockSpec((tm, RWKV_COLS), lambda i: (i, 0)),
            pl.BlockSpec((8, RWKV_COLS), lambda i: (jnp.maximum(i * (tm // 8) - 1, 0), 0)),
            row(RWKV_COLS), row(WIDTH), row(WIDTH), row(WIDTH), row(WIDTH), row(WIDTH),
            pl.BlockSpec((LORA_COLS, 3 * WIDTH), lambda i: (0, 0)),
        ],
        out_specs=[out] * 8,
        out_shape=[jax.ShapeDtypeStruct((T, WIDTH), F32)] * 8,
        compiler_params=_cparams(("parallel",)),
        name="rwkv_prep",
    )(ur, ur, mu, w0, a0, k_k, k_a, r_k, w_lora)


def _rwkv_core_kernel(r_ref, lw_ref, k_ref, v_ref, a_ref, b_ref, g_ref, bo_ref, lnw_ref, lnb_ref,
                      o_ref, s_ref):
    c = pl.program_id(0)

    @pl.when(c == 0)
    def _():
        s_ref[...] = jnp.zeros_like(s_ref)

    C = CHUNK
    G = CHUNKS_PER_STEP
    ri = lax.broadcasted_iota(jnp.int32, (C, C), 0)
    ci = lax.broadcasted_iota(jnp.int32, (C, C), 1)
    rg = lax.broadcasted_iota(jnp.int32, (G * C, G * C), 0)
    cg = lax.broadcasted_iota(jnp.int32, (G * C, G * C), 1)
    lw = lw_ref[...]
    tri = jnp.where((rg >= cg) & (rg // C == cg // C), 1.0, 0.0).astype(BF16)
    lw_hi = lw.astype(BF16)
    lw_r1 = lw - lw_hi.astype(F32)
    lw_mid = lw_r1.astype(BF16)
    lw_lo = (lw_r1 - lw_mid.astype(F32)).astype(BF16)
    cum = (jnp.dot(tri, lw_hi, preferred_element_type=F32)
           + jnp.dot(tri, lw_mid, preferred_element_type=F32)
           + jnp.dot(tri, lw_lo, preferred_element_type=F32))
    lam = jnp.exp(cum)
    inv_lam = jnp.exp(-cum)
    r_t = r_ref[...] * lam
    a_t = a_ref[...] * jnp.exp(cum - lw)
    b_t = b_ref[...] * inv_lam
    k_t = k_ref[...] * inv_lam
    tots = [cum[g * C + C - 1:g * C + C, :] for g in range(G)]
    rowg = lax.broadcasted_iota(jnp.int32, cum.shape, 0) // C
    tot = tots[G - 1]
    for g in range(G - 2, -1, -1):
        tot = jnp.where(rowg == g, tots[g], tot)
    rest = jnp.exp(tot - cum)
    b_h = b_ref[...] * rest
    k_h = k_ref[...] * rest
    lam_c = [jnp.exp(t) for t in tots]
    v_all = v_ref[...]
    eye = jnp.where(ri == ci, 1.0, 0.0)
    H = range(G * HEADS)
    sls = [slice(h * HEAD_DIM, (h + 1) * HEAD_DIM) for h in range(HEADS)]
    bf = lambda x: x.astype(BF16)
    part = lambda x, i: x[(i // HEADS) * C:(i // HEADS + 1) * C, sls[i % HEADS]]
    at = [bf(part(a_t, i)) for i in H]
    rt = [part(r_t, i) for i in H]
    bt = [bf(part(b_t, i)) for i in H]
    kt = [bf(part(k_t, i)) for i in H]
    bh = [bf(part(b_h, i)) for i in H]
    kh = [bf(part(k_h, i)) for i in H]
    vv = [bf(part(v_all, i)) for i in H]
    ci2 = lax.broadcasted_iota(jnp.int32, (C, 2 * C), 1)
    ri2 = lax.broadcasted_iota(jnp.int32, (C, 2 * C), 0)
    cm2 = jnp.where(ci2 >= C, ci2 - C, ci2)
    left = ci2 < C
    ar = [jnp.concatenate([at[h], bf(rt[h])], axis=0) for h in H]
    bk = [jnp.concatenate([bt[h], kt[h]], axis=0) for h in H]
    g = [_bdot_nt(ar[h], bk[h]) for h in H]
    top = [jnp.where(ri2 > cm2, g[h][0:C], 0.0) for h in H]
    bot = [bf(jnp.where(ri2 >= cm2, g[h][C:2 * C], 0.0)) for h in H]
    a_ab = [top[h][:, 0:C] for h in H]
    akv = [_bdot(top[h][:, C:2 * C], vv[h]) for h in H]
    z = [jnp.concatenate([a_ab[h], eye], axis=1) for h in H]
    for _ in range(6):
        z = [_bdot(z[h][:, 0:C], z[h]) + jnp.where(left, 0.0, z[h]) for h in H]
    tinv = [bf(z[h][:, C:2 * C]) for h in H]
    wu = [_bdot(tinv[h], jnp.concatenate([at[h], bf(akv[h])], axis=1)) for h in H]
    w_m = [bf(wu[h][:, 0:C]) for h in H]
    uv = [jnp.concatenate([bf(wu[h][:, C:2 * C]), vv[h]], axis=0) for h in H]
    q_m = [rt[h] + _bdot(bot[h][:, 0:C], w_m[h]) for h in H]
    y0 = [_bdot(bot[h], uv[h]) for h in H]
    m_k = [_bdot_tn(w_m[h], bh[h]) for h in H]
    n0 = [_bdot_tn(uv[h], jnp.concatenate([bh[h], kh[h]], axis=0)) for h in H]
    state = [s_ref[h] for h in range(HEADS)]
    blocks = []
    for g in range(G):
        outs = []
        for h in range(HEADS):
            i = g * HEADS + h
            y = _bdot_nt(q_m[i], state[h]) + y0[i]
            state[h] = state[h] * lam_c[g][:, sls[h]] + _bdot(state[h], m_k[i]) + n0[i]
            mean = jnp.mean(y, axis=-1, keepdims=True)
            var = jnp.mean(jnp.square(y - mean), axis=-1, keepdims=True)
            outs.append((y - mean) * lax.rsqrt(var + LN_X_EPS))
        blocks.append(jnp.concatenate(outs, axis=-1))
    for h in range(HEADS):
        s_ref[h] = state[h]
    yn = jnp.concatenate(blocks, axis=0)
    yn = yn * lnw_ref[...] + lnb_ref[...] + bo_ref[...]
    o_ref[...] = (yn * g_ref[...]).astype(BF16)


def _rwkv_core(r, lw, k, v, a, b, g, bonus, ln_w, ln_b):
    T = r.shape[0]
    rows = CHUNK * CHUNKS_PER_STEP
    blk = pl.BlockSpec((rows, WIDTH), lambda c: (c, 0))
    row = pl.BlockSpec((1, WIDTH), lambda c: (0, 0))
    return pl.pallas_call(
        _rwkv_core_kernel,
        grid=(T // rows,),
        in_specs=[blk] * 8 + [row, row],
        out_specs=blk,
        out_shape=jax.ShapeDtypeStruct((T, WIDTH), BF16),
        scratch_shapes=[pltpu.VMEM((HEADS, HEAD_DIM, HEAD_DIM), F32)],
        compiler_params=_cparams(("arbitrary",)),
        name="rwkv_core",
    )(r, lw, k, v, a, b, g, bonus, ln_w, ln_b)


def _kmean_kernel(k_ref, o_ref):
    o_ref[0] = jnp.mean(k_ref[...].astype(F32), axis=0, keepdims=True)


def _kmean(qkv):
    T = qkv.shape[0]
    nb = T // MOBA_BLOCK
    return pl.pallas_call(
        _kmean_kernel,
        grid=(nb,),
        in_specs=[pl.BlockSpec((MOBA_BLOCK, WIDTH), lambda j: (j, 1))],
        out_specs=pl.BlockSpec((1, 1, WIDTH), lambda j: (j, 0, 0)),
        out_shape=jax.ShapeDtypeStruct((nb, 1, WIDTH), F32),
        compiler_params=_cparams(("parallel",)),
        name="kmean",
    )(qkv)


def _t5_bucket(dist):
    n = jnp.maximum(dist, 0)
    max_exact = REL_BUCKETS // 2
    nf = jnp.maximum(n, max_exact).astype(jnp.float32)
    large = max_exact + (jnp.log(nf / max_exact) / math.log(REL_MAX_DISTANCE / max_exact)
                         * (REL_BUCKETS - max_exact)).astype(jnp.int32)
    large = jnp.minimum(large, REL_BUCKETS - 1)
    return jnp.where(n < max_exact, n, large)


def _bucket_tiles():
    i = jnp.arange(MOBA_BLOCK)[None, :]
    j = jnp.arange(MOBA_BLOCK)[:, None]
    d = jnp.arange(N_BIAS_TILES + 1)[:, None, None]
    dist = d * MOBA_BLOCK + i - j
    bucket = _t5_bucket(dist)
    bucket = jnp.where(d == N_BIAS_TILES - 1, REL_BUCKETS - 1, bucket)
    return jnp.where((dist < 0) | (d == N_BIAS_TILES), -1, bucket).astype(jnp.int32)


def _bias_tiles_kernel(idx_ref, rb_ref, o_ref):
    h = pl.program_id(0)
    idx = idx_ref[0]
    acc = jnp.where(idx < 0, NEG_INF, 0.0)
    for bkt in range(REL_BUCKETS):
        acc = jnp.where(idx == bkt, rb_ref[bkt, h] * LOG2E, acc)
    o_ref[0, 0] = acc


def _bias_tiles(rel_bias):
    idx = _bucket_tiles()
    return pl.pallas_call(
        _bias_tiles_kernel,
        grid=(HEADS, N_BIAS_TILES + 1),
        in_specs=[
            pl.BlockSpec((1, MOBA_BLOCK, MOBA_BLOCK), lambda h, d: (d, 0, 0)),
            pl.BlockSpec(memory_space=pltpu.SMEM),
        ],
        out_specs=pl.BlockSpec((1, 1, MOBA_BLOCK, MOBA_BLOCK), lambda h, d: (h, d, 0, 0)),
        out_shape=jax.ShapeDtypeStruct((HEADS, N_BIAS_TILES + 1, MOBA_BLOCK, MOBA_BLOCK), F32),
        compiler_params=_cparams(("parallel", "parallel")),
        name="bias_tiles",
    )(idx, rel_bias)


def _moba_kernel(q_ref, ka_ref, vt_ref, r_ref, bias_ref, o_ref, s_ref, p_ref, acc_ref):
    qb = pl.program_id(1)
    B = MOBA_BLOCK
    q_tr = q_ref[...].astype(F32).T
    q_tr_bf = q_tr.astype(BF16)
    row = lax.broadcasted_iota(jnp.int32, (LANES, B), 0)
    big = jnp.int32(1 << 20)
    q_t, q_own_t = [], []
    for h2 in range(2):
        off = HEAD_DIM * (1 - h2)
        inr = (row >= off) & (row < off + HEAD_DIM)
        valid = inr & (row - off < qb)
        sc = jnp.dot(r_ref[0, h2], q_tr_bf, preferred_element_type=F32)
        s = jnp.where(valid, sc, NEG_INF)
        s = jnp.where(inr, s, -jnp.inf)
        sel = jnp.zeros((LANES, B), jnp.bool_)
        for _ in range(MOBA_TOP):
            m = jnp.max(s, axis=0, keepdims=True)
            idx = jnp.min(jnp.where(s == m, row, big), axis=0, keepdims=True)
            pick = row == idx
            sel = jnp.logical_or(sel, pick)
            s = jnp.where(pick, -jnp.inf, s)
        sel = jnp.logical_and(sel, valid)
        q_t.append(jnp.where(inr, jnp.where(sel, 0.0, NEG_INF), q_tr).astype(BF16))
        q_own_t.append(jnp.where(inr, 0.0, q_tr).astype(BF16))

    carry = []
    for h2 in range(2):
        k_own = ka_ref[h2, pl.ds(pl.multiple_of(qb * B, B), B), :]
        s0 = jnp.dot(k_own, q_own_t[h2], preferred_element_type=F32) + bias_ref[h2, 0]
        m0 = jnp.max(s0, axis=0, keepdims=True)
        p0 = jnp.exp2(s0 - m0).astype(BF16)
        carry += [m0, jnp.dot(vt_ref[h2, qb], p0, preferred_element_type=F32)]

    n_tiles = (qb + 1) // 2
    last = jnp.maximum(n_tiles - 1, 0)
    nb = vt_ref.shape[1] - PREP_BLOCKS
    for h2 in range(2):
        acc_ref[h2] = carry[2 * h2 + 1]

    @pl.when((pl.program_id(0) == 0) & (qb == 0))
    def _():
        s_ref[...] = jnp.zeros_like(s_ref)
        p_ref[...] = jnp.zeros_like(p_ref)

    def trip(t, w, stats):
        r = 1 - w
        ok = (t >= 1) & (t <= n_tiles)
        d_a = jnp.where(ok, jnp.clip(qb - 2 * (t - 1), 0, N_BIAS_TILES - 1), N_BIAS_TILES)
        d_b = jnp.where(ok, jnp.clip(qb - 2 * (t - 1) - 1, 0, N_BIAS_TILES - 1), N_BIAS_TILES)
        start = pl.multiple_of(jnp.clip(t, 0, last) * (2 * B), 2 * B)
        ok_pv = (t >= 2) & (t <= n_tiles + 1)
        v_a = jnp.where(ok_pv, 2 * (t - 2), nb)
        v_b = jnp.where(ok_pv, 2 * (t - 2) + 1, nb)
        out = []
        for h2 in range(2):
            m_prev, alpha_p = stats[2 * h2], stats[2 * h2 + 1]
            acc_ref[h2] = (acc_ref[h2] * alpha_p
                           + jnp.dot(vt_ref[h2, v_a], p_ref[r, h2, 0:B], preferred_element_type=F32)
                           + jnp.dot(vt_ref[h2, v_b], p_ref[r, h2, B:2 * B],
                                     preferred_element_type=F32))
            s_a = s_ref[r, h2, 0:B] + bias_ref[h2, d_a]
            s_b = s_ref[r, h2, B:2 * B] + bias_ref[h2, d_b]
            m_new = jnp.maximum(m_prev, jnp.max(jnp.maximum(s_a, s_b), axis=0, keepdims=True))
            out += [m_new, jnp.exp2(m_prev - m_new)]
            p_ref[w, h2, 0:B] = jnp.exp2(s_a - m_new).astype(BF16)
            p_ref[w, h2, B:2 * B] = jnp.exp2(s_b - m_new).astype(BF16)
            s_ref[w, h2] = jnp.dot(ka_ref[h2, pl.ds(start, 2 * B), :], q_t[h2],
                                   preferred_element_type=F32)
        return out

    def body(u, stats):
        stats = trip(2 * u, 0, list(stats))
        return tuple(trip(2 * u + 1, 1, stats))

    stats = []
    for h2 in range(2):
        stats += [carry[2 * h2], jnp.ones_like(carry[2 * h2])]
    lax.fori_loop(0, (n_tiles + 3) // 2, body, tuple(stats))
    outs = [acc_ref[h2, 0:HEAD_DIM] / acc_ref[h2, HEAD_DIM:HEAD_DIM + 1] for h2 in range(2)]
    o_ref[...] = jnp.concatenate(outs, axis=0).T.astype(BF16)


def _moba(qkv, k_aug, vt_aug, r_mats, bias_tiles):
    T = qkv.shape[0]
    nb = T // MOBA_BLOCK
    npair = HEADS // 2
    once = pl.Buffered(1)
    return pl.pallas_call(
        _moba_kernel,
        grid=(npair, nb),
        in_specs=[
            pl.BlockSpec((MOBA_BLOCK, LANES), lambda p, qb: (qb, p)),
            pl.BlockSpec((2, T, LANES), lambda p, qb: (p, 0, 0), pipeline_mode=once),
            pl.BlockSpec((2, nb + PREP_BLOCKS, VT_ROWS, MOBA_BLOCK), lambda p, qb: (p, 0, 0, 0),
                         pipeline_mode=once),
            pl.BlockSpec((1, 2, LANES, LANES), lambda p, qb: (p, 0, 0, 0)),
            pl.BlockSpec((2, N_BIAS_TILES + 1, MOBA_BLOCK, MOBA_BLOCK), lambda p, qb: (p, 0, 0, 0),
                         pipeline_mode=once),
        ],
        out_specs=pl.BlockSpec((MOBA_BLOCK, LANES), lambda p, qb: (qb, p)),
        out_shape=jax.ShapeDtypeStruct((T, WIDTH), BF16),
        scratch_shapes=[pltpu.VMEM((2, 2, 2 * MOBA_BLOCK, MOBA_BLOCK), F32),
                        pltpu.VMEM((2, 2, 2 * MOBA_BLOCK, MOBA_BLOCK), BF16),
                        pltpu.VMEM((2, VT_ROWS, MOBA_BLOCK), F32)],
        compiler_params=_cparams(("arbitrary", "arbitrary")),
        name="moba",
    )(qkv, k_aug, vt_aug, r_mats, bias_tiles)


def _moba_prep_kernel(k_ref, v_ref, ka_ref, vt_ref):
    B = MOBA_BLOCK
    step = pl.program_id(1)
    extra = step == pl.num_programs(1) - 1
    lane = lax.broadcasted_iota(jnp.int32, (B, LANES), 1)
    ones = jnp.ones((VT_ROWS - HEAD_DIM, B), F32)
    for jj in range(PREP_BLOCKS):
        j = jnp.minimum(step, pl.num_programs(1) - 2) * PREP_BLOCKS + jj
        kblk = k_ref[jj * B:(jj + 1) * B, :]
        v_t = v_ref[jj * B:(jj + 1) * B, :].astype(F32).T
        for h2 in range(2):
            off = HEAD_DIM * (1 - h2)
            inr = (lane >= off) & (lane < off + HEAD_DIM)
            hot = jnp.where(lane == off + j, 1.0, 0.0).astype(BF16)
            ka_ref[h2, jj * B:(jj + 1) * B, :] = jnp.where(inr, hot, kblk)
            tile = jnp.concatenate([v_t[h2 * HEAD_DIM:(h2 + 1) * HEAD_DIM], ones], axis=0)
            vt_ref[h2, jj] = jnp.where(extra, 0.0, tile).astype(BF16)


def _moba_prep(qkv):
    T = qkv.shape[0]
    nb = T // MOBA_BLOCK
    npair = HEADS // 2
    kcol = WIDTH // LANES
    steps = nb // PREP_BLOCKS
    return pl.pallas_call(
        _moba_prep_kernel,
        grid=(npair, steps + 1),
        in_specs=[
            pl.BlockSpec((PREP_BLOCKS * MOBA_BLOCK, LANES), lambda p, j: (jnp.minimum(j, steps - 1), kcol + p)),
            pl.BlockSpec((PREP_BLOCKS * MOBA_BLOCK, LANES),
                         lambda p, j: (jnp.minimum(j, steps - 1), 2 * kcol + p)),
        ],
        out_specs=[
            pl.BlockSpec((2, PREP_BLOCKS * MOBA_BLOCK, LANES), lambda p, j: (p, jnp.minimum(j, steps - 1), 0)),
            pl.BlockSpec((2, PREP_BLOCKS, VT_ROWS, MOBA_BLOCK), lambda p, j: (p, j, 0, 0)),
        ],
        out_shape=[
            jax.ShapeDtypeStruct((HEADS, T, LANES), BF16),
            jax.ShapeDtypeStruct((HEADS, nb + PREP_BLOCKS, VT_ROWS, MOBA_BLOCK), BF16),
        ],
        compiler_params=_cparams(("arbitrary", "arbitrary")),
        name="moba_prep",
    )(qkv, qkv)


def _score_mats(kmean):
    nb = kmean.shape[0]
    km = kmean.reshape(nb, HEADS, HEAD_DIM).transpose(1, 2, 0)
    km = jnp.pad(km, ((0, 0), (0, 0), (0, HEAD_DIM - nb)))
    z = jnp.zeros((HEADS // 2, HEAD_DIM, HEAD_DIM), F32)
    even = jnp.concatenate([jnp.concatenate([z, km[0::2]], axis=2),
                            jnp.concatenate([z, z], axis=2)], axis=1)
    odd = jnp.concatenate([jnp.concatenate([z, z], axis=2),
                           jnp.concatenate([km[1::2], z], axis=2)], axis=1)
    return jnp.swapaxes(jnp.stack([even, odd], axis=1), -1, -2).astype(BF16)


def _memkv_kernel(m_ref, g_ref, w_ref, o_ref):
    h = _rms(m_ref[...], g_ref[...]).astype(BF16)
    o_ref[...] = jnp.dot(h, w_ref[...], preferred_element_type=F32).astype(BF16)


def _memkv(mem2d, g, w_bf):
    return pl.pallas_call(
        _memkv_kernel,
        out_shape=jax.ShapeDtypeStruct((N_MEM, 2 * XATTN_WIDTH), BF16),
        compiler_params=pltpu.CompilerParams(vmem_limit_bytes=VMEM_LIMIT),
        name="memkv",
    )(mem2d, g, w_bf)


def _merge_kernel(x_ref, oa_ref, ob_ref, gt_ref, woa_ref, wob_ref, wout_ref, nx_ref, wq_ref,
                  kv_ref, wo_ref, nf_ref, wr_ref, x2_ref, h3_ref, rt_ref):
    pa = jnp.dot(oa_ref[...], woa_ref[...], preferred_element_type=F32)
    pb = jnp.dot(ob_ref[...], wob_ref[...], preferred_element_type=F32)
    ga = gt_ref[:, 0:D_MODEL].astype(F32)
    gb = gt_ref[:, D_MODEL:2 * D_MODEL].astype(F32)
    merged = _sigmoid(ga) * pa + _sigmoid(gb) * pb
    x1 = x_ref[...] + jnp.dot(merged.astype(BF16), wout_ref[...], preferred_element_type=F32)
    h2 = _rms(x1, nx_ref[...]).astype(BF16)
    q = jnp.dot(h2, wq_ref[...], preferred_element_type=F32).astype(BF16)
    heads = []
    for h in range(XATTN_HEADS):
        sl = slice(h * XATTN_HEAD_DIM, (h + 1) * XATTN_HEAD_DIM)
        km = kv_ref[:, sl]
        vm = kv_ref[:, XATTN_WIDTH + h * XATTN_HEAD_DIM:XATTN_WIDTH + (h + 1) * XATTN_HEAD_DIM]
        lg = lax.dot_general(q[:, sl], km, (((1,), (1,)), ((), ())),
                             preferred_element_type=F32) * (XATTN_HEAD_DIM ** -0.5)
        e = jnp.exp(lg - jnp.max(lg, axis=-1, keepdims=True))
        p = e / jnp.sum(e, axis=-1, keepdims=True)
        heads.append(jnp.dot(p.astype(BF16), vm, preferred_element_type=F32))
    o = jnp.concatenate(heads, axis=-1).astype(BF16)
    x2 = x1 + jnp.dot(o, wo_ref[...], preferred_element_type=F32)
    x2_ref[...] = x2
    h3 = _rms(x2, nf_ref[...])
    _store_token_tiles(h3_ref, h3)
    lg = _bdot(h3, wr_ref[...])
    lane = lax.broadcasted_iota(jnp.int32, lg.shape, 1)
    big = jnp.int32(1 << 20)
    is_g = lane < N_GROUPS
    gl = jnp.where(is_g, lg, -jnp.inf)
    gmax = jnp.max(gl, axis=-1, keepdims=True)
    gsel = jnp.min(jnp.where(gl == gmax, lane, big), axis=-1, keepdims=True)
    p_top = 1.0 / jnp.sum(jnp.where(is_g, jnp.exp(gl - gmax), 0.0), axis=-1, keepdims=True)
    e_id = lane - N_GROUPS
    in_grp = (e_id >= gsel * EXPERTS_PER_GROUP) & (e_id < (gsel + 1) * EXPERTS_PER_GROUP)
    el = jnp.where(in_grp, lg, -jnp.inf)
    v1 = jnp.max(el, axis=-1, keepdims=True)
    i1 = jnp.min(jnp.where(el == v1, lane, big), axis=-1, keepdims=True)
    el2 = jnp.where(lane == i1, -jnp.inf, el)
    v2 = jnp.max(el2, axis=-1, keepdims=True)
    i2 = jnp.min(jnp.where(el2 == v2, lane, big), axis=-1, keepdims=True)
    e21 = jnp.exp(v2 - v1)
    g1 = p_top / (1.0 + e21)
    g2 = p_top * e21 / (1.0 + e21)
    rt = jnp.where(lane == 0, (i1 - N_GROUPS).astype(F32),
                   jnp.where(lane == 1, (i2 - N_GROUPS).astype(F32),
                             jnp.where(lane == 2, g1, jnp.where(lane == 3, g2, 0.0))))
    rt_ref[...] = rt


def _merge(x2d, o_a, o_b, gates, w_oa, w_ob, w_out, n_x, w_q, kv, w_o, n_f, w_r):
    T = x2d.shape[0]
    tm = 256
    full = lambda a: pl.BlockSpec(a.shape, lambda i: (0,) * a.ndim)
    tile = lambda w: pl.BlockSpec((tm, w), lambda i: (i, 0))
    return pl.pallas_call(
        _merge_kernel,
        grid=(T // tm,),
        in_specs=[tile(D_MODEL), tile(WIDTH), tile(WIDTH), tile(GATE_COLS), full(w_oa), full(w_ob),
                  full(w_out), full(n_x), full(w_q), full(kv), full(w_o), full(n_f), full(w_r)],
        out_specs=[tile(D_MODEL), pl.BlockSpec((tm * ROW_TILES, LANES), lambda i: (i, 0)), tile(LANES)],
        out_shape=[jax.ShapeDtypeStruct((T, D_MODEL), F32),
                   jax.ShapeDtypeStruct((T * ROW_TILES, LANES), F32),
                   jax.ShapeDtypeStruct((T, LANES), F32)],
        compiler_params=_cparams(("parallel",)),
        name="merge",
    )(x2d, o_a, o_b, gates, w_oa, w_ob, w_out, n_x, w_q, kv, w_o, n_f, w_r)


def _moe_kernel(be_ref, nu_ref, tok_ref, h_hbm, wg_ref, wu_ref, wd_ref, o_ref, xb_ref, sem):
    i = pl.program_id(0)
    n_used = nu_ref[0]
    slot = i % 2

    def row_copy(blk, r, s):
        t = tok_ref[blk * MOE_BLOCK + r]
        src = h_hbm.at[pl.ds(pl.multiple_of(t * ROW_TILES, ROW_TILES), ROW_TILES), :]
        return pltpu.make_async_copy(src, xb_ref.at[s, pl.ds(r * ROW_TILES, ROW_TILES), :], sem.at[s])

    last_step = pl.num_programs(0) - 1

    def gather(blk, s):
        for r in range(MOE_BLOCK):
            row_copy(blk, r, s).start()

    def wait_rows(s):
        for _ in range(MOE_BLOCK):
            row_copy(0, 0, s).wait()

    @pl.when((i == 0) & (n_used > 0))
    def _():
        gather(0, 0)

    @pl.when((i == n_used) & (i > 0))
    def _():
        wait_rows(slot)

    @pl.when(i < n_used)
    def _():
        wait_rows(slot)
        gather(jnp.minimum(i + 1, last_step), 1 - slot)
        xb = jnp.concatenate([x.astype(BF16) for x in _load_token_tiles(xb_ref.at[slot], MOE_BLOCK)],
                             axis=-1)
        gate = jnp.dot(xb, wg_ref[0], preferred_element_type=F32)
        up = jnp.dot(xb, wu_ref[0], preferred_element_type=F32)
        hid = gate * _sigmoid(gate) * up
        _store_token_tiles(o_ref, jnp.dot(hid.astype(BF16), wd_ref[0], preferred_element_type=F32))

    @pl.when((i == last_step) & (i < n_used))
    def _():
        wait_rows(1 - slot)

    @pl.when(i >= n_used)
    def _():
        o_ref[...] = jnp.zeros_like(o_ref)


def _moe(block_expert, n_used, buf_tok, h3, wg, wu, wd):
    cap = buf_tok.shape[0]
    n_blocks = cap // MOE_BLOCK
    grid_spec = pltpu.PrefetchScalarGridSpec(
        num_scalar_prefetch=3,
        grid=(n_blocks,),
        in_specs=[
            pl.BlockSpec(memory_space=pl.ANY),
            pl.BlockSpec((1, D_MODEL, D_EXPERT), lambda i, be, nu, tk: (be[i], 0, 0)),
            pl.BlockSpec((1, D_MODEL, D_EXPERT), lambda i, be, nu, tk: (be[i], 0, 0)),
            pl.BlockSpec((1, D_EXPERT, D_MODEL), lambda i, be, nu, tk: (be[i], 0, 0)),
        ],
        out_specs=pl.BlockSpec((MOE_BLOCK * ROW_TILES, LANES), lambda i, be, nu, tk: (i, 0)),
        scratch_shapes=[pltpu.VMEM((2, MOE_BLOCK * ROW_TILES, LANES), F32),
                        pltpu.SemaphoreType.DMA((2,))],
    )
    return pl.pallas_call(
        _moe_kernel,
        grid_spec=grid_spec,
        out_shape=jax.ShapeDtypeStruct((cap * ROW_TILES, LANES), F32),
        compiler_params=_cparams(("arbitrary",)),
        name="moe",
    )(block_expert, n_used, buf_tok, h3, wg, wu, wd)


def _rank_kernel(rt_ref, rank_ref, cnt_ref, run_ref):
    i = pl.program_id(0)

    @pl.when(i == 0)
    def _():
        run_ref[...] = jnp.zeros_like(run_ref)

    rt = rt_ref[...]
    tm = rt.shape[0]
    lane = lax.broadcasted_iota(jnp.int32, rt.shape, 1).astype(F32)
    oh1 = jnp.where(lane == rt[:, 0:1], 1.0, 0.0)
    oh2 = jnp.where(lane == rt[:, 1:2], 1.0, 0.0)
    both = oh1 + oh2
    ri = lax.broadcasted_iota(jnp.int32, (tm, tm), 0)
    ci = lax.broadcasted_iota(jnp.int32, (tm, tm), 1)
    before = _bdot(jnp.where(ri > ci, 1.0, 0.0), both) + run_ref[0:1, :]
    r1 = jnp.sum(before * oh1, axis=-1, keepdims=True)
    r2 = jnp.sum(before * oh2, axis=-1, keepdims=True)
    lane_i = lax.broadcasted_iota(jnp.int32, rt.shape, 1)
    rank_ref[...] = jnp.where(lane_i == 0, r1, jnp.where(lane_i == 1, r2, 0.0))
    run_ref[...] = run_ref[...] + jnp.sum(both, axis=0, keepdims=True)
    cnt_ref[...] = run_ref[...]


def _rank(rt):
    T = rt.shape[0]
    tm = 256
    return pl.pallas_call(
        _rank_kernel,
        grid=(T // tm,),
        in_specs=[pl.BlockSpec((tm, LANES), lambda i: (i, 0))],
        out_specs=[pl.BlockSpec((tm, LANES), lambda i: (i, 0)),
                   pl.BlockSpec((8, LANES), lambda i: (0, 0))],
        out_shape=[jax.ShapeDtypeStruct((T, LANES), F32), jax.ShapeDtypeStruct((8, LANES), F32)],
        scratch_shapes=[pltpu.VMEM((8, LANES), F32)],
        compiler_params=_cparams(("arbitrary",)),
        name="rank",
    )(rt)


def _moe_plan(expert, rank, counts, n_tok):
    n_assign = n_tok * 2
    e_flat = expert.reshape(n_assign)
    rank = rank.reshape(n_assign)
    padded = (counts + MOE_BLOCK - 1) // MOE_BLOCK * MOE_BLOCK
    p_end = jnp.cumsum(padded)
    p_start = p_end - padded
    pos = (p_start[e_flat] + rank).astype(jnp.int32)
    n_blocks = (n_assign + N_EXPERTS * (MOE_BLOCK - 1) + MOE_BLOCK - 1) // MOE_BLOCK
    cap = n_blocks * MOE_BLOCK
    tok_flat = jnp.repeat(jnp.arange(n_tok, dtype=jnp.int32), 2)
    buf_tok = jnp.zeros((cap,), jnp.int32).at[pos].set(tok_flat)
    block_start = jnp.arange(n_blocks, dtype=jnp.int32) * MOE_BLOCK
    block_expert = jnp.minimum(jnp.sum(p_end[None, :] <= block_start[:, None], axis=1),
                               N_EXPERTS - 1).astype(jnp.int32)
    n_used = (p_end[-1] // MOE_BLOCK).astype(jnp.int32).reshape(1)
    return pos, buf_tok, block_expert, n_used


def _final_kernel(pos_ref, x_ref, rt_ref, y_hbm, g_ref, o_ref, y1_ref, y2_ref, sem):
    i = pl.program_id(0)
    n = pl.num_programs(0)
    tm = x_ref.shape[0]
    slot = i % 2

    def copies(tile, r, s):
        p1 = pos_ref[2 * (tile * tm + r)]
        p2 = pos_ref[2 * (tile * tm + r) + 1]
        src = lambda p: y_hbm.at[pl.ds(pl.multiple_of(p * ROW_TILES, ROW_TILES), ROW_TILES), :]
        dst = lambda ref: ref.at[s, pl.ds(r * ROW_TILES, ROW_TILES), :]
        return (pltpu.make_async_copy(src(p1), dst(y1_ref), sem.at[s]),
                pltpu.make_async_copy(src(p2), dst(y2_ref), sem.at[s]))

    def gather(tile, s):
        for r in range(tm):
            c1, c2 = copies(tile, r, s)
            c1.start(priority=0)
            c2.start(priority=1)

    def wait_rows(s):
        for _ in range(tm):
            c1, c2 = copies(0, 0, s)
            c1.wait()
            c2.wait()

    @pl.when(i == 0)
    def _():
        gather(0, 0)

    wait_rows(slot)
    gather(jnp.minimum(i + 1, n - 1), 1 - slot)
    rt = rt_ref[...]
    g1 = rt[:, 2:3]
    g2 = rt[:, 3:4]
    y1 = jnp.concatenate(_load_token_tiles(y1_ref.at[slot], tm), axis=-1)
    y2 = jnp.concatenate(_load_token_tiles(y2_ref.at[slot], tm), axis=-1)
    x3 = x_ref[...] + y1 * g1 + y2 * g2
    o_ref[...] = _rms(x3, g_ref[...])

    @pl.when(i == n - 1)
    def _():
        wait_rows(1 - slot)


def _final(pos, x2, rt, y_sorted, g):
    T = x2.shape[0]
    tm = 256
    grid_spec = pltpu.PrefetchScalarGridSpec(
        num_scalar_prefetch=1,
        grid=(T // tm,),
        in_specs=[
            pl.BlockSpec((tm, D_MODEL), lambda i, ps: (i, 0)),
            pl.BlockSpec((tm, LANES), lambda i, ps: (i, 0)),
            pl.BlockSpec(memory_space=pl.ANY),
            pl.BlockSpec((1, D_MODEL), lambda i, ps: (0, 0)),
        ],
        out_specs=pl.BlockSpec((tm, D_MODEL), lambda i, ps: (i, 0)),
        scratch_shapes=[pltpu.VMEM((2, tm * ROW_TILES, LANES), F32),
                        pltpu.VMEM((2, tm * ROW_TILES, LANES), F32),
                        pltpu.SemaphoreType.DMA((2,))],
    )
    return pl.pallas_call(
        _final_kernel,
        grid_spec=grid_spec,
        out_shape=jax.ShapeDtypeStruct((T, D_MODEL), F32),
        compiler_params=_cparams(("arbitrary",)),
        name="final",
    )(pos, x2, rt, y_sorted, g)


def _lora_weight(decay_up, iclr_up, gate_up):
    w = jnp.zeros((LORA_COLS, 3 * WIDTH), F32)
    w = w.at[0:DECAY_LORA, 0:WIDTH].set(decay_up)
    w = w.at[DECAY_LORA:DECAY_LORA + ICLR_LORA, WIDTH:2 * WIDTH].set(iclr_up)
    w = w.at[DECAY_LORA + ICLR_LORA:, 2 * WIDTH:].set(gate_up)
    return w


def _router_weight(w_group, w_expert):
    w = jnp.zeros((D_MODEL, LANES), F32)
    w = w.at[:, 0:N_GROUPS].set(w_group)
    return w.at[:, N_GROUPS:N_GROUPS + N_EXPERTS].set(w_expert)


def kernel(x, mem, rel_bias, mem_norm, norm_mix, w_in, tshift_mu, decay_w0, decay_up, iclr_a0,
           iclr_up, gate_up, k_k, k_a, r_k, ln_x_w, ln_x_b, w_o_rwkv, w_o_moba, w_out,
           norm_xattn, w_q_x, w_kv_x, w_o_x, norm_ffn, w_router_group, w_router_expert,
           w_exp_gate, w_exp_up, w_exp_down, norm_final):
    B, T, D = x.shape
    assert B == 1 and D == D_MODEL and T % MOBA_BLOCK == 0 and T // MOBA_BLOCK <= HEAD_DIM
    assert norm_mix.shape[0] == 1 and (T // MOBA_BLOCK) % PREP_BLOCKS == 0
    row = lambda a: a.reshape(1, -1)
    x2d = x.reshape(T, D)
    kv = _memkv(mem.reshape(N_MEM, D), row(mem_norm), w_kv_x[0].astype(BF16))

    ur, qkv, gates = _inproj(x2d, row(norm_mix[0]), w_in[0].astype(BF16))
    prep = _rwkv_prep(ur, row(tshift_mu[0]), row(decay_w0[0]), row(iclr_a0[0]), row(k_k[0]),
                      row(k_a[0]), row(r_k[0]),
                      _lora_weight(decay_up[0], iclr_up[0], gate_up[0]).astype(BF16))
    o_a = _rwkv_core(*prep, row(ln_x_w[0]), row(ln_x_b[0]))

    kmean = _kmean(qkv).reshape(T // MOBA_BLOCK, WIDTH)
    k_aug, vt_aug = _moba_prep(qkv)
    o_b = _moba(qkv, k_aug, vt_aug, _score_mats(kmean), _bias_tiles(rel_bias))

    x2, h3, rt = _merge(x2d, o_a, o_b, gates, w_o_rwkv[0].astype(BF16), w_o_moba[0].astype(BF16),
                        w_out[0].astype(BF16), row(norm_xattn[0]), w_q_x[0].astype(BF16), kv,
                        w_o_x[0].astype(BF16), row(norm_ffn[0]),
                        _router_weight(w_router_group[0], w_router_expert[0]).astype(BF16))

    expert = rt[:, 0:2].astype(jnp.int32)
    rank, counts = _rank(rt)
    pos, buf_tok, block_expert, n_used = _moe_plan(expert, rank[:, 0:2].astype(jnp.int32),
                                                   counts[0, 0:N_EXPERTS].astype(jnp.int32), T)
    y_sorted = _moe(block_expert, n_used, buf_tok, h3, w_exp_gate[0].astype(BF16),
                    w_exp_up[0].astype(BF16), w_exp_down[0].astype(BF16))
    out = _final(pos, x2, rt, y_sorted, row(norm_final))
    return out.reshape(B, T, D)
```

```python
import functools
import math

import jax
import jax.numpy as jnp
import numpy as np
from jax import lax
from jax.experimental import pallas as pl
from jax.experimental.pallas import tpu as pltpu

F32 = jnp.float32
BF16 = jnp.bfloat16

D_MODEL = 1024
N_MEM = 256
NORM_EPS = 1e-6
NEG_INF = -1e30

HEADS = 8
HEAD_DIM = 64
WIDTH = HEADS * HEAD_DIM
DECAY_LORA = 64
ICLR_LORA = 64
GATE_LORA = 128
LORA_COLS = DECAY_LORA + ICLR_LORA + GATE_LORA
RWKV_COLS = 3 * WIDTH + LORA_COLS
QKV_COLS = 3 * WIDTH
GATE_COLS = 2 * D_MODEL
IN_COLS = RWKV_COLS + QKV_COLS + GATE_COLS
LN_X_EPS = 64e-5
KK_EPS = 1e-12

MOBA_BLOCK = 256
MOBA_TOP = 3
REL_BUCKETS = 32
REL_MAX_DISTANCE = 4096
N_BIAS_TILES = 14

XATTN_HEADS = 4
XATTN_HEAD_DIM = 128
XATTN_WIDTH = XATTN_HEADS * XATTN_HEAD_DIM

N_GROUPS = 4
EXPERTS_PER_GROUP = 8
N_EXPERTS = N_GROUPS * EXPERTS_PER_GROUP
D_EXPERT = 512
MOE_BLOCK = 256

PREP_BLOCKS = 4
VT_ROWS = HEAD_DIM + 16
CHUNK = 64
CHUNKS_PER_STEP = 4
LANES = 128
VMEM_LIMIT = 48 * 1024 * 1024

LOG2E = math.log2(math.e)


def _cparams(sem):
    return pltpu.CompilerParams(dimension_semantics=sem, vmem_limit_bytes=VMEM_LIMIT)


def _bdot(a, b):
    return jnp.dot(a.astype(BF16), b.astype(BF16), preferred_element_type=F32)


def _bdot_nt(a, b):
    return lax.dot_general(a.astype(BF16), b.astype(BF16), (((1,), (1,)), ((), ())),
                           preferred_element_type=F32)


def _bdot_tn(a, b):
    return lax.dot_general(a.astype(BF16), b.astype(BF16), (((0,), (0,)), ((), ())),
                           preferred_element_type=F32)


def _rms(x, g):
    return x * lax.rsqrt(jnp.mean(x * x, axis=-1, keepdims=True) + NORM_EPS) * g


def _sigmoid(x):
    return 1.0 / (1.0 + jnp.exp(-x))


ROW_TILES = D_MODEL // 128


def _store_token_tiles(ref, x):
    n = x.shape[0]
    for s in range(ROW_TILES):
        ref[pl.ds(s, n, stride=ROW_TILES), :] = x[:, s * LANES:(s + 1) * LANES]


def _load_token_tiles(ref, n):
    return [ref[pl.ds(s, n, stride=ROW_TILES), :] for s in range(ROW_TILES)]


def _inproj_kernel(x_ref, g_ref, w_ref, ur_ref, qkv_ref, gate_ref):
    h = _rms(x_ref[...], g_ref[...]).astype(BF16)
    step = 256
    for c0 in range(0, RWKV_COLS, step):
        ur_ref[:, c0:c0 + step] = jnp.dot(h, w_ref[:, c0:c0 + step], preferred_element_type=F32)
    for c0 in range(0, QKV_COLS, step):
        o = jnp.dot(h, w_ref[:, RWKV_COLS + c0:RWKV_COLS + c0 + step], preferred_element_type=F32)
        if c0 < WIDTH:
            o = o * (HEAD_DIM ** -0.5 * LOG2E)
        qkv_ref[:, c0:c0 + step] = o.astype(BF16)
    base = RWKV_COLS + QKV_COLS
    for c0 in range(0, GATE_COLS, step):
        o = jnp.dot(h, w_ref[:, base + c0:base + c0 + step], preferred_element_type=F32)
        gate_ref[:, c0:c0 + step] = o.astype(BF16)


def _inproj(x2d, g, w_bf):
    T = x2d.shape[0]
    tm = 512
    return pl.pallas_call(
        _inproj_kernel,
        grid=(T // tm,),
        in_specs=[
            pl.BlockSpec((tm, D_MODEL), lambda i: (i, 0)),
            pl.BlockSpec((1, D_MODEL), lambda i: (0, 0)),
            pl.BlockSpec((D_MODEL, IN_COLS), lambda i: (0, 0), pipeline_mode=pl.Buffered(1)),
        ],
        out_specs=[
            pl.BlockSpec((tm, RWKV_COLS), lambda i: (i, 0)),
            pl.BlockSpec((tm, QKV_COLS), lambda i: (i, 0)),
            pl.BlockSpec((tm, GATE_COLS), lambda i: (i, 0)),
        ],
        out_shape=[
            jax.ShapeDtypeStruct((T, RWKV_COLS), F32),
            jax.ShapeDtypeStruct((T, QKV_COLS), BF16),
            jax.ShapeDtypeStruct((T, GATE_COLS), BF16),
        ],
        compiler_params=_cparams(("parallel",)),
        name="inproj",
    )(x2d, g, w_bf)


def _head_ones():
    r = lax.broadcasted_iota(jnp.int32, (LANES, LANES), 0) // HEAD_DIM
    c = lax.broadcasted_iota(jnp.int32, (LANES, LANES), 1) // HEAD_DIM
    return jnp.where(r == c, 1.0, 0.0).astype(BF16)


def _head_sum(x, ones):
    parts = [_bdot(x[:, c:c + LANES], ones) for c in range(0, WIDTH, LANES)]
    return jnp.concatenate(parts, axis=-1)


def _rwkv_prep_kernel(u_ref, up_ref, mu_ref, w0_ref, a0_ref, kk_ref, ka_ref, rk_ref, wl_ref,
                      r_ref, lw_ref, k_ref, v_ref, a_ref, b_ref, g_ref, bo_ref):
    i = pl.program_id(0)
    u = u_ref[...]
    tm = u.shape[0]
    prev_last = up_ref[7:8, :] * jnp.where(i > 0, 1.0, 0.0)
    rolled = pltpu.roll(u, 1, 0)
    row = lax.broadcasted_iota(jnp.int32, u.shape, 0)
    u_prev = jnp.where(row == 0, prev_last, rolled)
    u = u + mu_ref[...] * (u_prev - u)
    r = u[:, 0:WIDTH]
    k = u[:, WIDTH:2 * WIDTH]
    v = u[:, 2 * WIDTH:3 * WIDTH]
    lo = u[:, 3 * WIDTH:3 * WIDTH + LORA_COLS]
    lane = lax.broadcasted_iota(jnp.int32, lo.shape, 1)
    act = jnp.where(lane < DECAY_LORA, jnp.tanh(lo),
                    jnp.where(lane < DECAY_LORA + ICLR_LORA, lo, _sigmoid(lo)))
    up = _bdot(act, wl_ref[...])
    z = -(w0_ref[...] + up[:, 0:WIDTH])
    softplus = jnp.maximum(z, 0.0) + jnp.log(1.0 + jnp.exp(-jnp.abs(z)))
    w_log = -softplus - 0.5
    lw = -jnp.exp(w_log)
    iclr = _sigmoid(a0_ref[...] + up[:, WIDTH:2 * WIDTH])
    g = up[:, 2 * WIDTH:3 * WIDTH]
    ones = _head_ones()
    kk = k * kk_ref[...]
    kk = kk * lax.rsqrt(_head_sum(kk * kk, ones) + KK_EPS)
    k2 = k * (1.0 + (iclr - 1.0) * ka_ref[...])
    bonus = _head_sum(r * k2 * rk_ref[...], ones) * v
    r_ref[...] = r
    lw_ref[...] = lw
    k_ref[...] = k2
    v_ref[...] = v
    a_ref[...] = -kk
    b_ref[...] = kk * iclr
    g_ref[...] = g
    bo_ref[...] = bonus


def _rwkv_prep(ur, mu, w0, a0, k_k, k_a, r_k, w_lora):
    T = ur.shape[0]
    tm = 256
    row = lambda w: pl.BlockSpec((1, w), lambda i: (0, 0))
    out = pl.BlockSpec((tm, WIDTH), lambda i: (i, 0))
    return pl.pallas_call(
        _rwkv_prep_kernel,
        grid=(T // tm,),
        in_specs=[
            pl.BlockSpec((tm, RWKV_COLS), lambda i: (i, 0)),
            pl.BlockSpec((8, RWKV_COLS), lambda i: (jnp.maximum(i * (tm // 8) - 1, 0), 0)),
            row(RWKV_COLS), row(WIDTH), row(WIDTH), row(WIDTH), row(WIDTH), row(WIDTH),
            pl.BlockSpec((LORA_COLS, 3 * WIDTH), lambda i: (0, 0)),
        ],
        out_specs=[out] * 8,
        out_shape=[jax.ShapeDtypeStruct((T, WIDTH), F32)] * 8,
        compiler_params=_cparams(("parallel",)),
        name="rwkv_prep",
    )(ur, ur, mu, w0, a0, k_k, k_a, r_k, w_lora)


def _rwkv_core_kernel(r_ref, lw_ref, k_ref, v_ref, a_ref, b_ref, g_ref, bo_ref, lnw_ref, lnb_ref,
                      o_ref, s_ref):
    c = pl.program_id(0)

    @pl.when(c == 0)
    def _():
        s_ref[...] = jnp.zeros_like(s_ref)

    C = CHUNK
    G = CHUNKS_PER_STEP
    ri = lax.broadcasted_iota(jnp.int32, (C, C), 0)
    ci = lax.broadcasted_iota(jnp.int32, (C, C), 1)
    rg = lax.broadcasted_iota(jnp.int32, (G * C, G * C), 0)
    cg = lax.broadcasted_iota(jnp.int32, (G * C, G * C), 1)
    lw = lw_ref[...]
    tri = jnp.where((rg >= cg) & (rg // C == cg // C), 1.0, 0.0).astype(BF16)
    lw_hi = lw.astype(BF16)
    lw_r1 = lw - lw_hi.astype(F32)
    lw_mid = lw_r1.astype(BF16)
    lw_lo = (lw_r1 - lw_mid.astype(F32)).astype(BF16)
    cum = (jnp.dot(tri, lw_hi, preferred_element_type=F32)
           + jnp.dot(tri, lw_mid, preferred_element_type=F32)
           + jnp.dot(tri, lw_lo, preferred_element_type=F32))
    lam = jnp.exp(cum)
    inv_lam = jnp.exp(-cum)
    r_t = r_ref[...] * lam
    a_t = a_ref[...] * jnp.exp(cum - lw)
    b_t = b_ref[...] * inv_lam
    k_t = k_ref[...] * inv_lam
    tots = [cum[g * C + C - 1:g * C + C, :] for g in range(G)]
    rowg = lax.broadcasted_iota(jnp.int32, cum.shape, 0) // C
    tot = tots[G - 1]
    for g in range(G - 2, -1, -1):
        tot = jnp.where(rowg == g, tots[g], tot)
    rest = jnp.exp(tot - cum)
    b_h = b_ref[...] * rest
    k_h = k_ref[...] * rest
    lam_c = [jnp.exp(t) for t in tots]
    v_all = v_ref[...]
    eye = jnp.where(ri == ci, 1.0, 0.0)
    H = range(G * HEADS)
    sls = [slice(h * HEAD_DIM, (h + 1) * HEAD_DIM) for h in range(HEADS)]
    bf = lambda x: x.astype(BF16)
    part = lambda x, i: x[(i // HEADS) * C:(i // HEADS + 1) * C, sls[i % HEADS]]
    at = [bf(part(a_t, i)) for i in H]
    rt = [part(r_t, i) for i in H]
    bt = [bf(part(b_t, i)) for i in H]
    kt = [bf(part(k_t, i)) for i in H]
    bh = [bf(part(b_h, i)) for i in H]
    kh = [bf(part(k_h, i)) for i in H]
    vv = [bf(part(v_all, i)) for i in H]
    ci2 = lax.broadcasted_iota(jnp.int32, (C, 2 * C), 1)
    ri2 = lax.broadcasted_iota(jnp.int32, (C, 2 * C), 0)
    cm2 = jnp.where(ci2 >= C, ci2 - C, ci2)
    left = ci2 < C
    ar = [jnp.concatenate([at[h], bf(rt[h])], axis=0) for h in H]
    bk = [jnp.concatenate([bt[h], kt[h]], axis=0) for h in H]
    g = [_bdot_nt(ar[h], bk[h]) for h in H]
    top = [jnp.where(ri2 > cm2, g[h][0:C], 0.0) for h in H]
    bot = [bf(jnp.where(ri2 >= cm2, g[h][C:2 * C], 0.0)) for h in H]
    a_ab = [top[h][:, 0:C] for h in H]
    akv = [_bdot(top[h][:, C:2 * C], vv[h]) for h in H]
    z = [jnp.concatenate([a_ab[h], eye], axis=1) for h in H]
    for _ in range(6):
        z = [_bdot(z[h][:, 0:C], z[h]) + jnp.where(left, 0.0, z[h]) for h in H]
    tinv = [bf(z[h][:, C:2 * C]) for h in H]
    wu = [_bdot(tinv[h], jnp.concatenate([at[h], bf(akv[h])], axis=1)) for h in H]
    w_m = [bf(wu[h][:, 0:C]) for h in H]
    uv = [jnp.concatenate([bf(wu[h][:, C:2 * C]), vv[h]], axis=0) for h in H]
    q_m = [rt[h] + _bdot(bot[h][:, 0:C], w_m[h]) for h in H]
    y0 = [_bdot(bot[h], uv[h]) for h in H]
    m_k = [_bdot_tn(w_m[h], bh[h]) for h in H]
    n0 = [_bdot_tn(uv[h], jnp.concatenate([bh[h], kh[h]], axis=0)) for h in H]
    state = [s_ref[h] for h in range(HEADS)]
    blocks = []
    for g in range(G):
        outs = []
        for h in range(HEADS):
            i = g * HEADS + h
            y = _bdot_nt(q_m[i], state[h]) + y0[i]
            state[h] = state[h] * lam_c[g][:, sls[h]] + _bdot(state[h], m_k[i]) + n0[i]
            mean = jnp.mean(y, axis=-1, keepdims=True)
            var = jnp.mean(jnp.square(y - mean), axis=-1, keepdims=True)
            outs.append((y - mean) * lax.rsqrt(var + LN_X_EPS))
        blocks.append(jnp.concatenate(outs, axis=-1))
    for h in range(HEADS):
        s_ref[h] = state[h]
    yn = jnp.concatenate(blocks, axis=0)
    yn = yn * lnw_ref[...] + lnb_ref[...] + bo_ref[...]
    o_ref[...] = (yn * g_ref[...]).astype(BF16)


def _rwkv_core(r, lw, k, v, a, b, g, bonus, ln_w, ln_b):
    T = r.shape[0]
    rows = CHUNK * CHUNKS_PER_STEP
    blk = pl.BlockSpec((rows, WIDTH), lambda c: (c, 0))
    row = pl.BlockSpec((1, WIDTH), lambda c: (0, 0))
    return pl.pallas_call(
        _rwkv_core_kernel,
        grid=(T // rows,),
        in_specs=[blk] * 8 + [row, row],
        out_specs=blk,
        out_shape=jax.ShapeDtypeStruct((T, WIDTH), BF16),
        scratch_shapes=[pltpu.VMEM((HEADS, HEAD_DIM, HEAD_DIM), F32)],
        compiler_params=_cparams(("arbitrary",)),
        name="rwkv_core",
    )(r, lw, k, v, a, b, g, bonus, ln_w, ln_b)


def _kmean_kernel(k_ref, o_ref):
    o_ref[0] = jnp.mean(k_ref[...].astype(F32), axis=0, keepdims=True)


def _kmean(qkv):
    T = qkv.shape[0]
    nb = T // MOBA_BLOCK
    return pl.pallas_call(
        _kmean_kernel,
        grid=(nb,),
        in_specs=[pl.BlockSpec((MOBA_BLOCK, WIDTH), lambda j: (j, 1))],
        out_specs=pl.BlockSpec((1, 1, WIDTH), lambda j: (j, 0, 0)),
        out_shape=jax.ShapeDtypeStruct((nb, 1, WIDTH), F32),
        compiler_params=_cparams(("parallel",)),
        name="kmean",
    )(qkv)


def _t5_bucket(dist):
    n = jnp.maximum(dist, 0)
    max_exact = REL_BUCKETS // 2
    nf = jnp.maximum(n, max_exact).astype(jnp.float32)
    large = max_exact + (jnp.log(nf / max_exact) / math.log(REL_MAX_DISTANCE / max_exact)
                         * (REL_BUCKETS - max_exact)).astype(jnp.int32)
    large = jnp.minimum(large, REL_BUCKETS - 1)
    return jnp.where(n < max_exact, n, large)


def _bucket_tiles():
    i = jnp.arange(MOBA_BLOCK)[None, :]
    j = jnp.arange(MOBA_BLOCK)[:, None]
    d = jnp.arange(N_BIAS_TILES + 1)[:, None, None]
    dist = d * MOBA_BLOCK + i - j
    bucket = _t5_bucket(dist)
    bucket = jnp.where(d == N_BIAS_TILES - 1, REL_BUCKETS - 1, bucket)
    return jnp.where((dist < 0) | (d == N_BIAS_TILES), -1, bucket).astype(jnp.int32)


def _bias_tiles_kernel(idx_ref, rb_ref, o_ref):
    h = pl.program_id(0)

    def tile(d, c):
        idx = idx_ref[d]
        acc = jnp.where(idx < 0, NEG_INF, 0.0)
        for bkt in range(REL_BUCKETS):
            acc = jnp.where(idx == bkt, rb_ref[bkt, h] * LOG2E, acc)
        o_ref[0, d] = acc
        return c

    lax.fori_loop(0, N_BIAS_TILES + 1, tile, 0)


def _bias_tiles(rel_bias):
    idx = _bucket_tiles()
    n = N_BIAS_TILES + 1
    return pl.pallas_call(
        _bias_tiles_kernel,
        grid=(HEADS,),
        in_specs=[
            pl.BlockSpec((n, MOBA_BLOCK, MOBA_BLOCK), lambda h: (0, 0, 0)),
            pl.BlockSpec(memory_space=pltpu.SMEM),
        ],
        out_specs=pl.BlockSpec((1, n, MOBA_BLOCK, MOBA_BLOCK), lambda h: (h, 0, 0, 0)),
        out_shape=jax.ShapeDtypeStruct((HEADS, n, MOBA_BLOCK, MOBA_BLOCK), F32),
        compiler_params=_cparams(("parallel",)),
        name="bias_tiles",
    )(idx, rel_bias)


def _moba_kernel(q_ref, ka_ref, vt_ref, r_ref, bias_ref, o_ref, s_ref, p_ref, acc_ref):
    qb = pl.program_id(1)
    B = MOBA_BLOCK
    q_tr = q_ref[...].astype(F32).T
    q_tr_bf = q_tr.astype(BF16)
    row = lax.broadcasted_iota(jnp.int32, (LANES, B), 0)
    big = jnp.int32(1 << 20)
    q_t, q_own_t = [], []
    for h2 in range(2):
        off = HEAD_DIM * (1 - h2)
        inr = (row >= off) & (row < off + HEAD_DIM)
        valid = inr & (row - off < qb)
        sc = jnp.dot(r_ref[0, h2], q_tr_bf, preferred_element_type=F32)
        s = jnp.where(valid, sc, NEG_INF)
        s = jnp.where(inr, s, -jnp.inf)
        sel = jnp.zeros((LANES, B), jnp.bool_)
        for _ in range(MOBA_TOP):
            m = jnp.max(s, axis=0, keepdims=True)
            idx = jnp.min(jnp.where(s == m, row, big), axis=0, keepdims=True)
            pick = row == idx
            sel = jnp.logical_or(sel, pick)
            s = jnp.where(pick, -jnp.inf, s)
        sel = jnp.logical_and(sel, valid)
        q_t.append(jnp.where(inr, jnp.where(sel, 0.0, NEG_INF), q_tr).astype(BF16))
        q_own_t.append(jnp.where(inr, 0.0, q_tr).astype(BF16))

    carry = []
    for h2 in range(2):
        k_own = ka_ref[h2, pl.ds(pl.multiple_of(qb * B, B), B), :]
        s0 = jnp.dot(k_own, q_own_t[h2], preferred_element_type=F32) + bias_ref[h2, 0]
        m0 = jnp.max(s0, axis=0, keepdims=True)
        p0 = jnp.exp2(s0 - m0).astype(BF16)
        carry += [m0, jnp.dot(vt_ref[h2, qb], p0, preferred_element_type=F32)]

    n_tiles = (qb + 1) // 2
    last = jnp.maximum(n_tiles - 1, 0)
    nb = vt_ref.shape[1] - PREP_BLOCKS
    for h2 in range(2):
        acc_ref[h2] = carry[2 * h2 + 1]

    @pl.when((pl.program_id(0) == 0) & (qb == 0))
    def _():
        s_ref[...] = jnp.zeros_like(s_ref)
        p_ref[...] = jnp.zeros_like(p_ref)

    def trip(t, w, stats):
        r = 1 - w
        ok = (t >= 1) & (t <= n_tiles)
        d_a = jnp.where(ok, jnp.clip(qb - 2 * (t - 1), 0, N_BIAS_TILES - 1), N_BIAS_TILES)
        d_b = jnp.where(ok, jnp.clip(qb - 2 * (t - 1) - 1, 0, N_BIAS_TILES - 1), N_BIAS_TILES)
        start = pl.multiple_of(jnp.clip(t, 0, last) * (2 * B), 2 * B)
        ok_pv = (t >= 2) & (t <= n_tiles + 1)
        v_a = jnp.where(ok_pv, 2 * (t - 2), nb)
        v_b = jnp.where(ok_pv, 2 * (t - 2) + 1, nb)
        out = []
        for h2 in range(2):
            m_prev, alpha_p = stats[2 * h2], stats[2 * h2 + 1]
            acc_ref[h2] = (acc_ref[h2] * alpha_p
                           + jnp.dot(vt_ref[h2, v_a], p_ref[r, h2, 0:B], preferred_element_type=F32)
                           + jnp.dot(vt_ref[h2, v_b], p_ref[r, h2, B:2 * B],
                                     preferred_element_type=F32))
            s_a = s_ref[r, h2, 0:B] + bias_ref[h2, d_a]
            s_b = s_ref[r, h2, B:2 * B] + bias_ref[h2, d_b]
            m_new = jnp.maximum(m_prev, jnp.max(jnp.maximum(s_a, s_b), axis=0, keepdims=True))
            out += [m_new, jnp.exp2(m_prev - m_new)]
            p_ref[w, h2, 0:B] = jnp.exp2(s_a - m_new).astype(BF16)
            p_ref[w, h2, B:2 * B] = jnp.exp2(s_b - m_new).astype(BF16)
            s_ref[w, h2] = jnp.dot(ka_ref[h2, pl.ds(start, 2 * B), :], q_t[h2],
                                   preferred_element_type=F32)
        return out

    def body(u, stats):
        stats = trip(2 * u, 0, list(stats))
        return tuple(trip(2 * u + 1, 1, stats))

    stats = []
    for h2 in range(2):
        stats += [carry[2 * h2], jnp.ones_like(carry[2 * h2])]
    lax.fori_loop(0, (n_tiles + 3) // 2, body, tuple(stats))
    outs = [acc_ref[h2, 0:HEAD_DIM] / acc_ref[h2, HEAD_DIM:HEAD_DIM + 1] for h2 in range(2)]
    o_ref[...] = jnp.concatenate(outs, axis=0).T.astype(BF16)


def _moba(qkv, k_aug, vt_aug, r_mats, bias_tiles):
    T = qkv.shape[0]
    nb = T // MOBA_BLOCK
    npair = HEADS // 2
    once = pl.Buffered(1)
    return pl.pallas_call(
        _moba_kernel,
        grid=(npair, nb),
        in_specs=[
            pl.BlockSpec((MOBA_BLOCK, LANES), lambda p, qb: (qb, p)),
            pl.BlockSpec((2, T, LANES), lambda p, qb: (p, 0, 0), pipeline_mode=once),
            pl.BlockSpec((2, nb + PREP_BLOCKS, VT_ROWS, MOBA_BLOCK), lambda p, qb: (p, 0, 0, 0),
                         pipeline_mode=once),
            pl.BlockSpec((1, 2, LANES, LANES), lambda p, qb: (p, 0, 0, 0)),
            pl.BlockSpec((2, N_BIAS_TILES + 1, MOBA_BLOCK, MOBA_BLOCK), lambda p, qb: (p, 0, 0, 0),
                         pipeline_mode=once),
        ],
        out_specs=pl.BlockSpec((MOBA_BLOCK, LANES), lambda p, qb: (qb, p)),
        out_shape=jax.ShapeDtypeStruct((T, WIDTH), BF16),
        scratch_shapes=[pltpu.VMEM((2, 2, 2 * MOBA_BLOCK, MOBA_BLOCK), F32),
                        pltpu.VMEM((2, 2, 2 * MOBA_BLOCK, MOBA_BLOCK), BF16),
                        pltpu.VMEM((2, VT_ROWS, MOBA_BLOCK), F32)],
        compiler_params=_cparams(("arbitrary", "arbitrary")),
        name="moba",
    )(qkv, k_aug, vt_aug, r_mats, bias_tiles)


def _moba_prep_kernel(k_ref, v_ref, ka_ref, vt_ref):
    B = MOBA_BLOCK
    step = pl.program_id(1)
    extra = step == pl.num_programs(1) - 1
    lane = lax.broadcasted_iota(jnp.int32, (B, LANES), 1)
    ones = jnp.ones((VT_ROWS - HEAD_DIM, B), F32)
    for jj in range(PREP_BLOCKS):
        j = jnp.minimum(step, pl.num_programs(1) - 2) * PREP_BLOCKS + jj
        kblk = k_ref[jj * B:(jj + 1) * B, :]
        v_t = v_ref[jj * B:(jj + 1) * B, :].astype(F32).T
        for h2 in range(2):
            off = HEAD_DIM * (1 - h2)
            inr = (lane >= off) & (lane < off + HEAD_DIM)
            hot = jnp.where(lane == off + j, 1.0, 0.0).astype(BF16)
            ka_ref[h2, jj * B:(jj + 1) * B, :] = jnp.where(inr, hot, kblk)
            tile = jnp.concatenate([v_t[h2 * HEAD_DIM:(h2 + 1) * HEAD_DIM], ones], axis=0)
            vt_ref[h2, jj] = jnp.where(extra, 0.0, tile).astype(BF16)


def _moba_prep(qkv):
    T = qkv.shape[0]
    nb = T // MOBA_BLOCK
    npair = HEADS // 2
    kcol = WIDTH // LANES
    steps = nb // PREP_BLOCKS
    return pl.pallas_call(
        _moba_prep_kernel,
        grid=(npair, steps + 1),
        in_specs=[
            pl.BlockSpec((PREP_BLOCKS * MOBA_BLOCK, LANES), lambda p, j: (jnp.minimum(j, steps - 1), kcol + p)),
            pl.BlockSpec((PREP_BLOCKS * MOBA_BLOCK, LANES),
                         lambda p, j: (jnp.minimum(j, steps - 1), 2 * kcol + p)),
        ],
        out_specs=[
            pl.BlockSpec((2, PREP_BLOCKS * MOBA_BLOCK, LANES), lambda p, j: (p, jnp.minimum(j, steps - 1), 0)),
            pl.BlockSpec((2, PREP_BLOCKS, VT_ROWS, MOBA_BLOCK), lambda p, j: (p, j, 0, 0)),
        ],
        out_shape=[
            jax.ShapeDtypeStruct((HEADS, T, LANES), BF16),
            jax.ShapeDtypeStruct((HEADS, nb + PREP_BLOCKS, VT_ROWS, MOBA_BLOCK), BF16),
        ],
        compiler_params=_cparams(("arbitrary", "arbitrary")),
        name="moba_prep",
    )(qkv, qkv)


def _score_mats(kmean):
    nb = kmean.shape[0]
    km = kmean.reshape(nb, HEADS, HEAD_DIM).transpose(1, 2, 0)
    km = jnp.pad(km, ((0, 0), (0, 0), (0, HEAD_DIM - nb)))
    z = jnp.zeros((HEADS // 2, HEAD_DIM, HEAD_DIM), F32)
    even = jnp.concatenate([jnp.concatenate([z, km[0::2]], axis=2),
                            jnp.concatenate([z, z], axis=2)], axis=1)
    odd = jnp.concatenate([jnp.concatenate([z, z], axis=2),
                           jnp.concatenate([km[1::2], z], axis=2)], axis=1)
    return jnp.swapaxes(jnp.stack([even, odd], axis=1), -1, -2).astype(BF16)


def _memkv_kernel(m_ref, g_ref, w_ref, o_ref):
    h = _rms(m_ref[...], g_ref[...]).astype(BF16)
    o_ref[...] = jnp.dot(h, w_ref[...], preferred_element_type=F32).astype(BF16)


def _memkv(mem2d, g, w_bf):
    return pl.pallas_call(
        _memkv_kernel,
        out_shape=jax.ShapeDtypeStruct((N_MEM, 2 * XATTN_WIDTH), BF16),
        compiler_params=pltpu.CompilerParams(vmem_limit_bytes=VMEM_LIMIT),
        name="memkv",
    )(mem2d, g, w_bf)


def _merge_kernel(x_ref, oa_ref, ob_ref, gt_ref, woa_ref, wob_ref, wout_ref, nx_ref, wq_ref,
                  kv_ref, wo_ref, nf_ref, wr_ref, x2_ref, h3_ref, rt_ref):
    pa = jnp.dot(oa_ref[...], woa_ref[...], preferred_element_type=F32)
    pb = jnp.dot(ob_ref[...], wob_ref[...], preferred_element_type=F32)
    ga = gt_ref[:, 0:D_MODEL].astype(F32)
    gb = gt_ref[:, D_MODEL:2 * D_MODEL].astype(F32)
    merged = _sigmoid(ga) * pa + _sigmoid(gb) * pb
    x1 = x_ref[...] + jnp.dot(merged.astype(BF16), wout_ref[...], preferred_element_type=F32)
    h2 = _rms(x1, nx_ref[...]).astype(BF16)
    q = jnp.dot(h2, wq_ref[...], preferred_element_type=F32).astype(BF16)
    heads = []
    for h in range(XATTN_HEADS):
        sl = slice(h * XATTN_HEAD_DIM, (h + 1) * XATTN_HEAD_DIM)
        km = kv_ref[:, sl]
        vm = kv_ref[:, XATTN_WIDTH + h * XATTN_HEAD_DIM:XATTN_WIDTH + (h + 1) * XATTN_HEAD_DIM]
        lg = lax.dot_general(q[:, sl], km, (((1,), (1,)), ((), ())),
                             preferred_element_type=F32) * (XATTN_HEAD_DIM ** -0.5)
        e = jnp.exp(lg - jnp.max(lg, axis=-1, keepdims=True))
        p = e / jnp.sum(e, axis=-1, keepdims=True)
        heads.append(jnp.dot(p.astype(BF16), vm, preferred_element_type=F32))
    o = jnp.concatenate(heads, axis=-1).astype(BF16)
    x2 = x1 + jnp.dot(o, wo_ref[...], preferred_element_type=F32)
    x2_ref[...] = x2
    h3 = _rms(x2, nf_ref[...])
    _store_token_tiles(h3_ref, h3)
    lg = _bdot(h3, wr_ref[...])
    lane = lax.broadcasted_iota(jnp.int32, lg.shape, 1)
    big = jnp.int32(1 << 20)
    is_g = lane < N_GROUPS
    gl = jnp.where(is_g, lg, -jnp.inf)
    gmax = jnp.max(gl, axis=-1, keepdims=True)
    gsel = jnp.min(jnp.where(gl == gmax, lane, big), axis=-1, keepdims=True)
    p_top = 1.0 / jnp.sum(jnp.where(is_g, jnp.exp(gl - gmax), 0.0), axis=-1, keepdims=True)
    e_id = lane - N_GROUPS
    in_grp = (e_id >= gsel * EXPERTS_PER_GROUP) & (e_id < (gsel + 1) * EXPERTS_PER_GROUP)
    el = jnp.where(in_grp, lg, -jnp.inf)
    v1 = jnp.max(el, axis=-1, keepdims=True)
    i1 = jnp.min(jnp.where(el == v1, lane, big), axis=-1, keepdims=True)
    el2 = jnp.where(lane == i1, -jnp.inf, el)
    v2 = jnp.max(el2, axis=-1, keepdims=True)
    i2 = jnp.min(jnp.where(el2 == v2, lane, big), axis=-1, keepdims=True)
    e21 = jnp.exp(v2 - v1)
    g1 = p_top / (1.0 + e21)
    g2 = p_top * e21 / (1.0 + e21)
    rt = jnp.where(lane == 0, (i1 - N_GROUPS).astype(F32),
                   jnp.where(lane == 1, (i2 - N_GROUPS).astype(F32),
                             jnp.where(lane == 2, g1, jnp.where(lane == 3, g2, 0.0))))
    rt_ref[...] = rt


def _merge(x2d, o_a, o_b, gates, w_oa, w_ob, w_out, n_x, w_q, kv, w_o, n_f, w_r):
    T = x2d.shape[0]
    tm = 512
    full = lambda a: pl.BlockSpec(a.shape, lambda i: (0,) * a.ndim, pipeline_mode=pl.Buffered(1))
    tile = lambda w: pl.BlockSpec((tm, w), lambda i: (i, 0))
    return pl.pallas_call(
        _merge_kernel,
        grid=(T // tm,),
        in_specs=[tile(D_MODEL), tile(WIDTH), tile(WIDTH), tile(GATE_COLS), full(w_oa), full(w_ob),
                  full(w_out), full(n_x), full(w_q), full(kv), full(w_o), full(n_f), full(w_r)],
        out_specs=[tile(D_MODEL), pl.BlockSpec((tm * ROW_TILES, LANES), lambda i: (i, 0)), tile(LANES)],
        out_shape=[jax.ShapeDtypeStruct((T, D_MODEL), F32),
                   jax.ShapeDtypeStruct((T * ROW_TILES, LANES), F32),
                   jax.ShapeDtypeStruct((T, LANES), F32)],
        compiler_params=_cparams(("parallel",)),
        name="merge",
    )(x2d, o_a, o_b, gates, w_oa, w_ob, w_out, n_x, w_q, kv, w_o, n_f, w_r)


def _dispatch_kernel(pos_ref, zs_ref, zv_ref, nu_ref, h_ref, xs_hbm, zero_ref, sem):
    i = pl.program_id(0)
    tm = h_ref.shape[0] // ROW_TILES

    @pl.when(i == 0)
    def _():
        zero_ref[...] = jnp.zeros_like(zero_ref)

        def clear(e):
            dst = xs_hbm.at[pl.ds(pl.multiple_of(zs_ref[e] * ROW_TILES, ROW_TILES),
                                  MOE_BLOCK * ROW_TILES), :]
            return pltpu.make_async_copy(zero_ref, dst, sem)

        for e in range(N_EXPERTS):
            @pl.when(zv_ref[e] > 0)
            def _():
                clear(e).start()

        for e in range(N_EXPERTS):
            @pl.when(zv_ref[e] > 0)
            def _():
                clear(e).wait()

        def tail(b):
            dst = xs_hbm.at[pl.ds(pl.multiple_of(b * (MOE_BLOCK * ROW_TILES), MOE_BLOCK * ROW_TILES),
                                  MOE_BLOCK * ROW_TILES), :]
            return pltpu.make_async_copy(zero_ref, dst, sem)

        n_blocks = xs_hbm.shape[0] // (MOE_BLOCK * ROW_TILES)

        def start_tail(b, c):
            tail(b).start()
            return c

        def wait_tail(b, c):
            tail(b).wait()
            return c

        lax.fori_loop(nu_ref[0], n_blocks, start_tail, 0)
        lax.fori_loop(nu_ref[0], n_blocks, wait_tail, 0)

    def row_copy(r, k):
        p = pos_ref[2 * (i * tm + r) + k]
        dst = xs_hbm.at[pl.ds(pl.multiple_of(p * ROW_TILES, ROW_TILES), ROW_TILES), :]
        return pltpu.make_async_copy(h_ref.at[pl.ds(r * ROW_TILES, ROW_TILES), :], dst, sem)

    for r in range(tm):
        row_copy(r, 0).start(priority=0)
        row_copy(r, 1).start(priority=1)
    for _ in range(tm):
        row_copy(0, 0).wait()
        row_copy(0, 1).wait()


def _dispatch(pos, zstart, zvalid, n_used, h3t, cap):
    T = h3t.shape[0] // ROW_TILES
    tm = 256
    grid_spec = pltpu.PrefetchScalarGridSpec(
        num_scalar_prefetch=4,
        grid=(T // tm,),
        in_specs=[pl.BlockSpec((tm * ROW_TILES, LANES), lambda i, ps, zs, zv, nu: (i, 0))],
        out_specs=pl.BlockSpec(memory_space=pl.ANY),
        scratch_shapes=[pltpu.VMEM((MOE_BLOCK * ROW_TILES, LANES), F32), pltpu.SemaphoreType.DMA],
    )
    return pl.pallas_call(
        _dispatch_kernel,
        grid_spec=grid_spec,
        out_shape=jax.ShapeDtypeStruct((cap * ROW_TILES, LANES), F32),
        compiler_params=_cparams(("arbitrary",)),
        name="dispatch",
    )(pos, zstart, zvalid, n_used, h3t)


def _moe_kernel(be_ref, nu_ref, x_ref, wg_ref, wu_ref, wd_ref, o_ref):
    i = pl.program_id(0)

    @pl.when(i < nu_ref[0])
    def _():
        xb = jnp.concatenate([x.astype(BF16) for x in _load_token_tiles(x_ref, MOE_BLOCK)], axis=-1)
        gate = jnp.dot(xb, wg_ref[0], preferred_element_type=F32)
        up = jnp.dot(xb, wu_ref[0], preferred_element_type=F32)
        hid = gate * _sigmoid(gate) * up
        _store_token_tiles(o_ref, jnp.dot(hid.astype(BF16), wd_ref[0], preferred_element_type=F32))

    @pl.when(i >= nu_ref[0])
    def _():
        o_ref[...] = jnp.zeros_like(o_ref)


def _moe(block_expert, n_used, xs, wg, wu, wd):
    n_blocks = xs.shape[0] // (MOE_BLOCK * ROW_TILES)
    used = lambda i, nu: jnp.minimum(i, jnp.maximum(nu[0] - 1, 0))
    grid_spec = pltpu.PrefetchScalarGridSpec(
        num_scalar_prefetch=2,
        grid=(n_blocks,),
        in_specs=[
            pl.BlockSpec((MOE_BLOCK * ROW_TILES, LANES), lambda i, be, nu: (used(i, nu), 0)),
            pl.BlockSpec((1, D_MODEL, D_EXPERT), lambda i, be, nu: (be[i], 0, 0)),
            pl.BlockSpec((1, D_MODEL, D_EXPERT), lambda i, be, nu: (be[i], 0, 0)),
            pl.BlockSpec((1, D_EXPERT, D_MODEL), lambda i, be, nu: (be[i], 0, 0)),
        ],
        out_specs=pl.BlockSpec((MOE_BLOCK * ROW_TILES, LANES), lambda i, be, nu: (i, 0)),
    )
    return pl.pallas_call(
        _moe_kernel,
        grid_spec=grid_spec,
        out_shape=jax.ShapeDtypeStruct(xs.shape, F32),
        compiler_params=_cparams(("arbitrary",)),
        name="moe",
    )(block_expert, n_used, xs, wg, wu, wd)


def _rank_kernel(rt_ref, rank_ref, cnt_ref, run_ref):
    i = pl.program_id(0)

    @pl.when(i == 0)
    def _():
        run_ref[...] = jnp.zeros_like(run_ref)

    rt = rt_ref[...]
    tm = rt.shape[0]
    lane = lax.broadcasted_iota(jnp.int32, rt.shape, 1).astype(F32)
    oh1 = jnp.where(lane == rt[:, 0:1], 1.0, 0.0)
    oh2 = jnp.where(lane == rt[:, 1:2], 1.0, 0.0)
    both = oh1 + oh2
    ri = lax.broadcasted_iota(jnp.int32, (tm, tm), 0)
    ci = lax.broadcasted_iota(jnp.int32, (tm, tm), 1)
    before = _bdot(jnp.where(ri > ci, 1.0, 0.0), both) + run_ref[0:1, :]
    r1 = jnp.sum(before * oh1, axis=-1, keepdims=True)
    r2 = jnp.sum(before * oh2, axis=-1, keepdims=True)
    lane_i = lax.broadcasted_iota(jnp.int32, rt.shape, 1)
    rank_ref[...] = jnp.where(lane_i == 0, r1, jnp.where(lane_i == 1, r2, 0.0))
    run_ref[...] = run_ref[...] + jnp.sum(both, axis=0, keepdims=True)
    cnt_ref[...] = run_ref[...]


def _rank(rt):
    T = rt.shape[0]
    tm = 256
    return pl.pallas_call(
        _rank_kernel,
        grid=(T // tm,),
        in_specs=[pl.BlockSpec((tm, LANES), lambda i: (i, 0))],
        out_specs=[pl.BlockSpec((tm, LANES), lambda i: (i, 0)),
                   pl.BlockSpec((8, LANES), lambda i: (0, 0))],
        out_shape=[jax.ShapeDtypeStruct((T, LANES), F32), jax.ShapeDtypeStruct((8, LANES), F32)],
        scratch_shapes=[pltpu.VMEM((8, LANES), F32)],
        compiler_params=_cparams(("arbitrary",)),
        name="rank",
    )(rt)


def _moe_plan(expert, rank, counts, n_tok):
    n_assign = n_tok * 2
    e_flat = expert.reshape(n_assign)
    rank = rank.reshape(n_assign)
    padded = (counts + MOE_BLOCK - 1) // MOE_BLOCK * MOE_BLOCK
    p_end = jnp.cumsum(padded)
    p_start = p_end - padded
    pos = (p_start[e_flat] + rank).astype(jnp.int32)
    n_blocks = (n_assign + N_EXPERTS * (MOE_BLOCK - 1) + MOE_BLOCK - 1) // MOE_BLOCK
    cap = n_blocks * MOE_BLOCK
    block_start = jnp.arange(n_blocks, dtype=jnp.int32) * MOE_BLOCK
    block_expert = jnp.minimum(jnp.sum(p_end[None, :] <= block_start[:, None], axis=1),
                               N_EXPERTS - 1).astype(jnp.int32)
    n_used = (p_end[-1] // MOE_BLOCK).astype(jnp.int32).reshape(1)
    last_block = jnp.maximum(p_end - MOE_BLOCK, 0).astype(jnp.int32)
    nonempty = (padded > 0).astype(jnp.int32)
    return pos, cap, block_expert, n_used, last_block, nonempty


def _final_kernel(pos_ref, x_ref, rt_ref, y_hbm, g_ref, o_ref, y1_ref, y2_ref, sem):
    i = pl.program_id(0)
    n = pl.num_programs(0)
    tm = x_ref.shape[0]
    slot = i % 2

    def copies(tile, r, s):
        p1 = pos_ref[2 * (tile * tm + r)]
        p2 = pos_ref[2 * (tile * tm + r) + 1]
        src = lambda p: y_hbm.at[pl.ds(pl.multiple_of(p * ROW_TILES, ROW_TILES), ROW_TILES), :]
        dst = lambda ref: ref.at[s, pl.ds(r * ROW_TILES, ROW_TILES), :]
        return (pltpu.make_async_copy(src(p1), dst(y1_ref), sem.at[s]),
                pltpu.make_async_copy(src(p2), dst(y2_ref), sem.at[s]))

    def gather(tile, s):
        for r in range(tm):
            c1, c2 = copies(tile, r, s)
            c1.start(priority=0)
            c2.start(priority=1)

    def wait_rows(s):
        for _ in range(tm):
            c1, c2 = copies(0, 0, s)
            c1.wait()
            c2.wait()

    @pl.when(i == 0)
    def _():
        gather(0, 0)

    wait_rows(slot)
    gather(jnp.minimum(i + 1, n - 1), 1 - slot)
    rt = rt_ref[...]
    g1 = rt[:, 2:3]
    g2 = rt[:, 3:4]
    y1 = jnp.concatenate(_load_token_tiles(y1_ref.at[slot], tm), axis=-1)
    y2 = jnp.concatenate(_load_token_tiles(y2_ref.at[slot], tm), axis=-1)
    x3 = x_ref[...] + y1 * g1 + y2 * g2
    o_ref[...] = _rms(x3, g_ref[...])

    @pl.when(i == n - 1)
    def _():
        wait_rows(1 - slot)


def _final(pos, x2, rt, y_sorted, g):
    T = x2.shape[0]
    tm = 256
    grid_spec = pltpu.PrefetchScalarGridSpec(
        num_scalar_prefetch=1,
        grid=(T // tm,),
        in_specs=[
            pl.BlockSpec((tm, D_MODEL), lambda i, ps: (i, 0)),
            pl.BlockSpec((tm, LANES), lambda i, ps: (i, 0)),
            pl.BlockSpec(memory_space=pl.ANY),
            pl.BlockSpec((1, D_MODEL), lambda i, ps: (0, 0)),
        ],
        out_specs=pl.BlockSpec((tm, D_MODEL), lambda i, ps: (i, 0)),
        scratch_shapes=[pltpu.VMEM((2, tm * ROW_TILES, LANES), F32),
                        pltpu.VMEM((2, tm * ROW_TILES, LANES), F32),
                        pltpu.SemaphoreType.DMA((2,))],
    )
    return pl.pallas_call(
        _final_kernel,
        grid_spec=grid_spec,
        out_shape=jax.ShapeDtypeStruct((T, D_MODEL), F32),
        compiler_params=_cparams(("arbitrary",)),
        name="final",
    )(pos, x2, rt, y_sorted, g)


def _lora_weight(decay_up, iclr_up, gate_up):
    w = jnp.zeros((LORA_COLS, 3 * WIDTH), F32)
    w = w.at[0:DECAY_LORA, 0:WIDTH].set(decay_up)
    w = w.at[DECAY_LORA:DECAY_LORA + ICLR_LORA, WIDTH:2 * WIDTH].set(iclr_up)
    w = w.at[DECAY_LORA + ICLR_LORA:, 2 * WIDTH:].set(gate_up)
    return w


def _router_weight(w_group, w_expert):
    w = jnp.zeros((D_MODEL, LANES), F32)
    w = w.at[:, 0:N_GROUPS].set(w_group)
    return w.at[:, N_GROUPS:N_GROUPS + N_EXPERTS].set(w_expert)


def kernel(x, mem, rel_bias, mem_norm, norm_mix, w_in, tshift_mu, decay_w0, decay_up, iclr_a0,
           iclr_up, gate_up, k_k, k_a, r_k, ln_x_w, ln_x_b, w_o_rwkv, w_o_moba, w_out,
           norm_xattn, w_q_x, w_kv_x, w_o_x, norm_ffn, w_router_group, w_router_expert,
           w_exp_gate, w_exp_up, w_exp_down, norm_final):
    B, T, D = x.shape
    assert B == 1 and D == D_MODEL and T % MOBA_BLOCK == 0 and T // MOBA_BLOCK <= HEAD_DIM
    assert norm_mix.shape[0] == 1 and (T // MOBA_BLOCK) % PREP_BLOCKS == 0
    row = lambda a: a.reshape(1, -1)
    x2d = x.reshape(T, D)
    kv = _memkv(mem.reshape(N_MEM, D), row(mem_norm), w_kv_x[0].astype(BF16))

    ur, qkv, gates = _inproj(x2d, row(norm_mix[0]), w_in[0].astype(BF16))
    prep = _rwkv_prep(ur, row(tshift_mu[0]), row(decay_w0[0]), row(iclr_a0[0]), row(k_k[0]),
                      row(k_a[0]), row(r_k[0]),
                      _lora_weight(decay_up[0], iclr_up[0], gate_up[0]).astype(BF16))
    o_a = _rwkv_core(*prep, row(ln_x_w[0]), row(ln_x_b[0]))

    kmean = _kmean(qkv).reshape(T // MOBA_BLOCK, WIDTH)
    k_aug, vt_aug = _moba_prep(qkv)
    o_b = _moba(qkv, k_aug, vt_aug, _score_mats(kmean), _bias_tiles(rel_bias))

    x2, h3, rt = _merge(x2d, o_a, o_b, gates, w_o_rwkv[0].astype(BF16), w_o_moba[0].astype(BF16),
                        w_out[0].astype(BF16), row(norm_xattn[0]), w_q_x[0].astype(BF16), kv,
                        w_o_x[0].astype(BF16), row(norm_ffn[0]),
                        _router_weight(w_router_group[0], w_router_expert[0]).astype(BF16))

    expert = rt[:, 0:2].astype(jnp.int32)
    rank, counts = _rank(rt)
    pos, cap, block_expert, n_used, last_block, nonempty = _moe_plan(
        expert, rank[:, 0:2].astype(jnp.int32), counts[0, 0:N_EXPERTS].astype(jnp.int32), T)
    x_sorted = _dispatch(pos, last_block, nonempty, n_used, h3, cap)
    y_sorted = _moe(block_expert, n_used, x_sorted, w_exp_gate[0].astype(BF16),
                    w_exp_up[0].astype(BF16), w_exp_down[0].astype(BF16))
    out = _final(pos, x2, rt, y_sorted, row(norm_final))
    return out.reshape(B, T, D)
```

```python
import functools
import math

import jax
import jax.numpy as jnp
import numpy as np
from jax import lax
from jax.experimental import pallas as pl
from jax.experimental.pallas import tpu as pltpu

F32 = jnp.float32
BF16 = jnp.bfloat16

D_MODEL = 1024
N_MEM = 256
NORM_EPS = 1e-6
NEG_INF = -1e30

HEADS = 8
HEAD_DIM = 64
WIDTH = HEADS * HEAD_DIM
DECAY_LORA = 64
ICLR_LORA = 64
GATE_LORA = 128
LORA_COLS = DECAY_LORA + ICLR_LORA + GATE_LORA
RWKV_COLS = 3 * WIDTH + LORA_COLS
QKV_COLS = 3 * WIDTH
GATE_COLS = 2 * D_MODEL
IN_COLS = RWKV_COLS + QKV_COLS + GATE_COLS
LN_X_EPS = 64e-5
KK_EPS = 1e-12

MOBA_BLOCK = 256
MOBA_TOP = 3
REL_BUCKETS = 32
REL_MAX_DISTANCE = 4096
N_BIAS_TILES = 14

XATTN_HEADS = 4
XATTN_HEAD_DIM = 128
XATTN_WIDTH = XATTN_HEADS * XATTN_HEAD_DIM

N_GROUPS = 4
EXPERTS_PER_GROUP = 8
N_EXPERTS = N_GROUPS * EXPERTS_PER_GROUP
D_EXPERT = 512
MOE_BLOCK = 256

PREP_BLOCKS = 4
VT_ROWS = HEAD_DIM + 16
CHUNK = 64
CHUNKS_PER_STEP = 4
LANES = 128
VMEM_LIMIT = 48 * 1024 * 1024

LOG2E = math.log2(math.e)


def _cparams(sem):
    return pltpu.CompilerParams(dimension_semantics=sem, vmem_limit_bytes=VMEM_LIMIT)


def _bdot(a, b):
    return jnp.dot(a.astype(BF16), b.astype(BF16), preferred_element_type=F32)


def _bdot_nt(a, b):
    return lax.dot_general(a.astype(BF16), b.astype(BF16), (((1,), (1,)), ((), ())),
                           preferred_element_type=F32)


def _bdot_tn(a, b):
    return lax.dot_general(a.astype(BF16), b.astype(BF16), (((0,), (0,)), ((), ())),
                           preferred_element_type=F32)


def _rms(x, g):
    return x * lax.rsqrt(jnp.mean(x * x, axis=-1, keepdims=True) + NORM_EPS) * g


def _sigmoid(x):
    return 1.0 / (1.0 + jnp.exp(-x))


ROW_TILES = D_MODEL // 128


def _store_token_tiles(ref, x):
    n = x.shape[0]
    for s in range(ROW_TILES):
        ref[pl.ds(s, n, stride=ROW_TILES), :] = x[:, s * LANES:(s + 1) * LANES]


def _load_token_tiles(ref, n):
    return [ref[pl.ds(s, n, stride=ROW_TILES), :] for s in range(ROW_TILES)]


def _inproj_kernel(x_ref, g_ref, w_ref, ur_ref, qkv_ref, gate_ref):
    h = _rms(x_ref[...], g_ref[...]).astype(BF16)
    step = 256
    for c0 in range(0, RWKV_COLS, step):
        ur_ref[:, c0:c0 + step] = jnp.dot(h, w_ref[:, c0:c0 + step], preferred_element_type=F32)
    for c0 in range(0, QKV_COLS, step):
        o = jnp.dot(h, w_ref[:, RWKV_COLS + c0:RWKV_COLS + c0 + step], preferred_element_type=F32)
        if c0 < WIDTH:
            o = o * (HEAD_DIM ** -0.5 * LOG2E)
        qkv_ref[:, c0:c0 + step] = o.astype(BF16)
    base = RWKV_COLS + QKV_COLS
    for c0 in range(0, GATE_COLS, step):
        o = jnp.dot(h, w_ref[:, base + c0:base + c0 + step], preferred_element_type=F32)
        gate_ref[:, c0:c0 + step] = o.astype(BF16)


def _inproj(x2d, g, w_bf):
    T = x2d.shape[0]
    tm = 512
    return pl.pallas_call(
        _inproj_kernel,
        grid=(T // tm,),
        in_specs=[
            pl.BlockSpec((tm, D_MODEL), lambda i: (i, 0)),
            pl.BlockSpec((1, D_MODEL), lambda i: (0, 0)),
            pl.BlockSpec((D_MODEL, IN_COLS), lambda i: (0, 0), pipeline_mode=pl.Buffered(1)),
        ],
        out_specs=[
            pl.BlockSpec((tm, RWKV_COLS), lambda i: (i, 0)),
            pl.BlockSpec((tm, QKV_COLS), lambda i: (i, 0)),
            pl.BlockSpec((tm, GATE_COLS), lambda i: (i, 0)),
        ],
        out_shape=[
            jax.ShapeDtypeStruct((T, RWKV_COLS), F32),
            jax.ShapeDtypeStruct((T, QKV_COLS), BF16),
            jax.ShapeDtypeStruct((T, GATE_COLS), BF16),
        ],
        compiler_params=_cparams(("parallel",)),
        name="inproj",
    )(x2d, g, w_bf)


def _head_ones():
    r = lax.broadcasted_iota(jnp.int32, (LANES, LANES), 0) // HEAD_DIM
    c = lax.broadcasted_iota(jnp.int32, (LANES, LANES), 1) // HEAD_DIM
    return jnp.where(r == c, 1.0, 0.0).astype(BF16)


def _head_sum(x, ones):
    parts = [_bdot(x[:, c:c + LANES], ones) for c in range(0, WIDTH, LANES)]
    return jnp.concatenate(parts, axis=-1)


def _rwkv_prep_values(u_ref, up_ref, mu_ref, w0_ref, a0_ref, kk_ref, ka_ref, rk_ref, wl_ref):
    i = pl.program_id(0)
    u = u_ref[...]
    tm = u.shape[0]
    prev_last = up_ref[7:8, :] * jnp.where(i > 0, 1.0, 0.0)
    rolled = pltpu.roll(u, 1, 0)
    row = lax.broadcasted_iota(jnp.int32, u.shape, 0)
    u_prev = jnp.where(row == 0, prev_last, rolled)
    u = u + mu_ref[...] * (u_prev - u)
    r = u[:, 0:WIDTH]
    k = u[:, WIDTH:2 * WIDTH]
    v = u[:, 2 * WIDTH:3 * WIDTH]
    lo = u[:, 3 * WIDTH:3 * WIDTH + LORA_COLS]
    lane = lax.broadcasted_iota(jnp.int32, lo.shape, 1)
    act = jnp.where(lane < DECAY_LORA, jnp.tanh(lo),
                    jnp.where(lane < DECAY_LORA + ICLR_LORA, lo, _sigmoid(lo)))
    up = _bdot(act, wl_ref[...])
    z = -(w0_ref[...] + up[:, 0:WIDTH])
    softplus = jnp.maximum(z, 0.0) + jnp.log(1.0 + jnp.exp(-jnp.abs(z)))
    w_log = -softplus - 0.5
    lw = -jnp.exp(w_log)
    iclr = _sigmoid(a0_ref[...] + up[:, WIDTH:2 * WIDTH])
    g = up[:, 2 * WIDTH:3 * WIDTH]
    ones = _head_ones()
    kk = k * kk_ref[...]
    kk = kk * lax.rsqrt(_head_sum(kk * kk, ones) + KK_EPS)
    k2 = k * (1.0 + (iclr - 1.0) * ka_ref[...])
    bonus = _head_sum(r * k2 * rk_ref[...], ones) * v
    return r, lw, k2, v, -kk, kk * iclr, g, bonus


def _rwkv_core_kernel(u_ref, up_ref, mu_ref, w0_ref, a0_ref, kk_ref, ka_ref, rk_ref, wl_ref,
                      lnw_ref, lnb_ref, o_ref, s_ref):
    c = pl.program_id(0)

    @pl.when(c == 0)
    def _():
        s_ref[...] = jnp.zeros_like(s_ref)

    C = CHUNK
    G = CHUNKS_PER_STEP
    ri = lax.broadcasted_iota(jnp.int32, (C, C), 0)
    ci = lax.broadcasted_iota(jnp.int32, (C, C), 1)
    rg = lax.broadcasted_iota(jnp.int32, (G * C, G * C), 0)
    cg = lax.broadcasted_iota(jnp.int32, (G * C, G * C), 1)
    r_in, lw, k_in, v_all, a_in, b_in, g_in, bonus = _rwkv_prep_values(
        u_ref, up_ref, mu_ref, w0_ref, a0_ref, kk_ref, ka_ref, rk_ref, wl_ref)
    tri = jnp.where((rg >= cg) & (rg // C == cg // C), 1.0, 0.0).astype(BF16)
    lw_hi = lw.astype(BF16)
    lw_r1 = lw - lw_hi.astype(F32)
    lw_mid = lw_r1.astype(BF16)
    lw_lo = (lw_r1 - lw_mid.astype(F32)).astype(BF16)
    cum = (jnp.dot(tri, lw_hi, preferred_element_type=F32)
           + jnp.dot(tri, lw_mid, preferred_element_type=F32)
           + jnp.dot(tri, lw_lo, preferred_element_type=F32))
    lam = jnp.exp(cum)
    inv_lam = jnp.exp(-cum)
    r_t = r_in * lam
    a_t = a_in * jnp.exp(cum - lw)
    b_t = b_in * inv_lam
    k_t = k_in * inv_lam
    tots = [cum[g * C + C - 1:g * C + C, :] for g in range(G)]
    rowg = lax.broadcasted_iota(jnp.int32, cum.shape, 0) // C
    tot = tots[G - 1]
    for g in range(G - 2, -1, -1):
        tot = jnp.where(rowg == g, tots[g], tot)
    rest = jnp.exp(tot - cum)
    b_h = b_in * rest
    k_h = k_in * rest
    lam_c = [jnp.exp(t) for t in tots]
    eye = jnp.where(ri == ci, 1.0, 0.0)
    H = range(G * HEADS)
    sls = [slice(h * HEAD_DIM, (h + 1) * HEAD_DIM) for h in range(HEADS)]
    bf = lambda x: x.astype(BF16)
    part = lambda x, i: x[(i // HEADS) * C:(i // HEADS + 1) * C, sls[i % HEADS]]
    at = [bf(part(a_t, i)) for i in H]
    rt = [part(r_t, i) for i in H]
    bt = [bf(part(b_t, i)) for i in H]
    kt = [bf(part(k_t, i)) for i in H]
    bh = [bf(part(b_h, i)) for i in H]
    kh = [bf(part(k_h, i)) for i in H]
    vv = [bf(part(v_all, i)) for i in H]
    ci2 = lax.broadcasted_iota(jnp.int32, (C, 2 * C), 1)
    ri2 = lax.broadcasted_iota(jnp.int32, (C, 2 * C), 0)
    cm2 = jnp.where(ci2 >= C, ci2 - C, ci2)
    left = ci2 < C
    ar = [jnp.concatenate([at[h], bf(rt[h])], axis=0) for h in H]
    bk = [jnp.concatenate([bt[h], kt[h]], axis=0) for h in H]
    g = [_bdot_nt(ar[h], bk[h]) for h in H]
    top = [jnp.where(ri2 > cm2, g[h][0:C], 0.0) for h in H]
    bot = [bf(jnp.where(ri2 >= cm2, g[h][C:2 * C], 0.0)) for h in H]
    a_ab = [top[h][:, 0:C] for h in H]
    akv = [_bdot(top[h][:, C:2 * C], vv[h]) for h in H]
    z = [jnp.concatenate([a_ab[h], eye], axis=1) for h in H]
    for _ in range(6):
        z = [_bdot(z[h][:, 0:C], z[h]) + jnp.where(left, 0.0, z[h]) for h in H]
    tinv = [bf(z[h][:, C:2 * C]) for h in H]
    wu = [_bdot(tinv[h], jnp.concatenate([at[h], bf(akv[h])], axis=1)) for h in H]
    w_m = [bf(wu[h][:, 0:C]) for h in H]
    uv = [jnp.concatenate([bf(wu[h][:, C:2 * C]), vv[h]], axis=0) for h in H]
    q_m = [rt[h] + _bdot(bot[h][:, 0:C], w_m[h]) for h in H]
    y0 = [_bdot(bot[h], uv[h]) for h in H]
    m_k = [_bdot_tn(w_m[h], bh[h]) for h in H]
    n0 = [_bdot_tn(uv[h], jnp.concatenate([bh[h], kh[h]], axis=0)) for h in H]
    state = [s_ref[h] for h in range(HEADS)]
    blocks = []
    for g in range(G):
        outs = []
        for h in range(HEADS):
            i = g * HEADS + h
            y = _bdot_nt(q_m[i], state[h]) + y0[i]
            state[h] = state[h] * lam_c[g][:, sls[h]] + _bdot(state[h], m_k[i]) + n0[i]
            mean = jnp.mean(y, axis=-1, keepdims=True)
            var = jnp.mean(jnp.square(y - mean), axis=-1, keepdims=True)
            outs.append((y - mean) * lax.rsqrt(var + LN_X_EPS))
        blocks.append(jnp.concatenate(outs, axis=-1))
    for h in range(HEADS):
        s_ref[h] = state[h]
    yn = jnp.concatenate(blocks, axis=0)
    yn = yn * lnw_ref[...] + lnb_ref[...] + bonus
    o_ref[...] = (yn * g_in).astype(BF16)


def _rwkv_core(ur, mu, w0, a0, k_k, k_a, r_k, w_lora, ln_w, ln_b):
    T = ur.shape[0]
    rows = CHUNK * CHUNKS_PER_STEP
    row = lambda w: pl.BlockSpec((1, w), lambda c: (0, 0))
    return pl.pallas_call(
        _rwkv_core_kernel,
        grid=(T // rows,),
        in_specs=[
            pl.BlockSpec((rows, RWKV_COLS), lambda c: (c, 0)),
            pl.BlockSpec((8, RWKV_COLS), lambda c: (jnp.maximum(c * (rows // 8) - 1, 0), 0)),
            row(RWKV_COLS), row(WIDTH), row(WIDTH), row(WIDTH), row(WIDTH), row(WIDTH),
            pl.BlockSpec((LORA_COLS, 3 * WIDTH), lambda c: (0, 0)),
            row(WIDTH), row(WIDTH),
        ],
        out_specs=pl.BlockSpec((rows, WIDTH), lambda c: (c, 0)),
        out_shape=jax.ShapeDtypeStruct((T, WIDTH), BF16),
        scratch_shapes=[pltpu.VMEM((HEADS, HEAD_DIM, HEAD_DIM), F32)],
        compiler_params=_cparams(("arbitrary",)),
        name="rwkv_core",
    )(ur, ur, mu, w0, a0, k_k, k_a, r_k, w_lora, ln_w, ln_b)


def _kmean_kernel(k_ref, o_ref):
    o_ref[0] = jnp.mean(k_ref[...].astype(F32), axis=0, keepdims=True)


def _kmean(qkv):
    T = qkv.shape[0]
    nb = T // MOBA_BLOCK
    return pl.pallas_call(
        _kmean_kernel,
        grid=(nb,),
        in_specs=[pl.BlockSpec((MOBA_BLOCK, WIDTH), lambda j: (j, 1))],
        out_specs=pl.BlockSpec((1, 1, WIDTH), lambda j: (j, 0, 0)),
        out_shape=jax.ShapeDtypeStruct((nb, 1, WIDTH), F32),
        compiler_params=_cparams(("parallel",)),
        name="kmean",
    )(qkv)


def _t5_bucket(dist):
    n = jnp.maximum(dist, 0)
    max_exact = REL_BUCKETS // 2
    nf = jnp.maximum(n, max_exact).astype(jnp.float32)
    large = max_exact + (jnp.log(nf / max_exact) / math.log(REL_MAX_DISTANCE / max_exact)
                         * (REL_BUCKETS - max_exact)).astype(jnp.int32)
    large = jnp.minimum(large, REL_BUCKETS - 1)
    return jnp.where(n < max_exact, n, large)


def _bucket_tiles():
    i = jnp.arange(MOBA_BLOCK)[None, :]
    j = jnp.arange(MOBA_BLOCK)[:, None]
    d = jnp.arange(N_BIAS_TILES + 1)[:, None, None]
    dist = d * MOBA_BLOCK + i - j
    bucket = _t5_bucket(dist)
    bucket = jnp.where(d == N_BIAS_TILES - 1, REL_BUCKETS - 1, bucket)
    return jnp.where((dist < 0) | (d == N_BIAS_TILES), -1, bucket).astype(jnp.int32)


def _bias_tiles_kernel(idx_ref, rb_ref, o_ref):
    h = pl.program_id(0)

    def tile(d, c):
        idx = idx_ref[d]
        acc = jnp.where(idx < 0, NEG_INF, 0.0)
        for bkt in range(REL_BUCKETS):
            acc = jnp.where(idx == bkt, rb_ref[bkt, h] * LOG2E, acc)
        o_ref[0, d] = acc
        return c

    lax.fori_loop(0, N_BIAS_TILES + 1, tile, 0)


def _bias_tiles(rel_bias):
    idx = _bucket_tiles()
    n = N_BIAS_TILES + 1
    return pl.pallas_call(
        _bias_tiles_kernel,
        grid=(HEADS,),
        in_specs=[
            pl.BlockSpec((n, MOBA_BLOCK, MOBA_BLOCK), lambda h: (0, 0, 0)),
            pl.BlockSpec(memory_space=pltpu.SMEM),
        ],
        out_specs=pl.BlockSpec((1, n, MOBA_BLOCK, MOBA_BLOCK), lambda h: (h, 0, 0, 0)),
        out_shape=jax.ShapeDtypeStruct((HEADS, n, MOBA_BLOCK, MOBA_BLOCK), F32),
        compiler_params=_cparams(("parallel",)),
        name="bias_tiles",
    )(idx, rel_bias)


def _moba_kernel(q_ref, ka_ref, vt_ref, r_ref, bias_ref, o_ref, s_ref, p_ref, acc_ref):
    qb = pl.program_id(1)
    B = MOBA_BLOCK
    q_tr = q_ref[...].astype(F32).T
    q_tr_bf = q_tr.astype(BF16)
    row = lax.broadcasted_iota(jnp.int32, (LANES, B), 0)
    big = jnp.int32(1 << 20)
    q_t, q_own_t = [], []
    for h2 in range(2):
        off = HEAD_DIM * (1 - h2)
        inr = (row >= off) & (row < off + HEAD_DIM)
        valid = inr & (row - off < qb)
        sc = jnp.dot(r_ref[0, h2], q_tr_bf, preferred_element_type=F32)
        s = jnp.where(valid, sc, NEG_INF)
        s = jnp.where(inr, s, -jnp.inf)
        sel = jnp.zeros((LANES, B), jnp.bool_)
        for _ in range(MOBA_TOP):
            m = jnp.max(s, axis=0, keepdims=True)
            idx = jnp.min(jnp.where(s == m, row, big), axis=0, keepdims=True)
            pick = row == idx
            sel = jnp.logical_or(sel, pick)
            s = jnp.where(pick, -jnp.inf, s)
        sel = jnp.logical_and(sel, valid)
        q_t.append(jnp.where(inr, jnp.where(sel, 0.0, NEG_INF), q_tr).astype(BF16))
        q_own_t.append(jnp.where(inr, 0.0, q_tr).astype(BF16))

    carry = []
    for h2 in range(2):
        k_own = ka_ref[h2, pl.ds(pl.multiple_of(qb * B, B), B), :]
        s0 = jnp.dot(k_own, q_own_t[h2], preferred_element_type=F32) + bias_ref[h2, 0]
        m0 = jnp.max(s0, axis=0, keepdims=True)
        p0 = jnp.exp2(s0 - m0).astype(BF16)
        carry += [m0, jnp.dot(vt_ref[h2, qb], p0, preferred_element_type=F32)]

    n_tiles = (qb + 1) // 2
    last = jnp.maximum(n_tiles - 1, 0)
    nb = vt_ref.shape[1] - PREP_BLOCKS
    for h2 in range(2):
        acc_ref[h2] = carry[2 * h2 + 1]

    @pl.when((pl.program_id(0) == 0) & (qb == 0))
    def _():
        s_ref[...] = jnp.zeros_like(s_ref)
        p_ref[...] = jnp.zeros_like(p_ref)

    def trip(t, w, stats):
        r = 1 - w
        ok = (t >= 1) & (t <= n_tiles)
        d_a = jnp.where(ok, jnp.clip(qb - 2 * (t - 1), 0, N_BIAS_TILES - 1), N_BIAS_TILES)
        d_b = jnp.where(ok, jnp.clip(qb - 2 * (t - 1) - 1, 0, N_BIAS_TILES - 1), N_BIAS_TILES)
        start = pl.multiple_of(jnp.clip(t, 0, last) * (2 * B), 2 * B)
        ok_pv = (t >= 2) & (t <= n_tiles + 1)
        v_a = jnp.where(ok_pv, 2 * (t - 2), nb)
        v_b = jnp.where(ok_pv, 2 * (t - 2) + 1, nb)
        out = []
        for h2 in range(2):
            m_prev, alpha_p = stats[2 * h2], stats[2 * h2 + 1]
            acc_ref[h2] = (acc_ref[h2] * alpha_p
                           + jnp.dot(vt_ref[h2, v_a], p_ref[r, h2, 0:B], preferred_element_type=F32)
                           + jnp.dot(vt_ref[h2, v_b], p_ref[r, h2, B:2 * B],
                                     preferred_element_type=F32))
            s_a = s_ref[r, h2, 0:B] + bias_ref[h2, d_a]
            s_b = s_ref[r, h2, B:2 * B] + bias_ref[h2, d_b]
            m_new = jnp.maximum(m_prev, jnp.max(jnp.maximum(s_a, s_b), axis=0, keepdims=True))
            out += [m_new, jnp.exp2(m_prev - m_new)]
            p_ref[w, h2, 0:B] = jnp.exp2(s_a - m_new).astype(BF16)
            p_ref[w, h2, B:2 * B] = jnp.exp2(s_b - m_new).astype(BF16)
            s_ref[w, h2] = jnp.dot(ka_ref[h2, pl.ds(start, 2 * B), :], q_t[h2],
                                   preferred_element_type=F32)
        return out

    def body(u, stats):
        stats = trip(2 * u, 0, list(stats))
        return tuple(trip(2 * u + 1, 1, stats))

    stats = []
    for h2 in range(2):
        stats += [carry[2 * h2], jnp.ones_like(carry[2 * h2])]
    lax.fori_loop(0, (n_tiles + 3) // 2, body, tuple(stats))
    outs = [acc_ref[h2, 0:HEAD_DIM] / acc_ref[h2, HEAD_DIM:HEAD_DIM + 1] for h2 in range(2)]
    o_ref[...] = jnp.concatenate(outs, axis=0).T.astype(BF16)


def _moba(qkv, k_aug, vt_aug, r_mats, bias_tiles):
    T = qkv.shape[0]
    nb = T // MOBA_BLOCK
    npair = HEADS // 2
    once = pl.Buffered(1)
    return pl.pallas_call(
        _moba_kernel,
        grid=(npair, nb),
        in_specs=[
            pl.BlockSpec((MOBA_BLOCK, LANES), lambda p, qb: (qb, p)),
            pl.BlockSpec((2, T, LANES), lambda p, qb: (p, 0, 0), pipeline_mode=once),
            pl.BlockSpec((2, nb + PREP_BLOCKS, VT_ROWS, MOBA_BLOCK), lambda p, qb: (p, 0, 0, 0),
                         pipeline_mode=once),
            pl.BlockSpec((1, 2, LANES, LANES), lambda p, qb: (p, 0, 0, 0)),
            pl.BlockSpec((2, N_BIAS_TILES + 1, MOBA_BLOCK, MOBA_BLOCK), lambda p, qb: (p, 0, 0, 0),
                         pipeline_mode=once),
        ],
        out_specs=pl.BlockSpec((MOBA_BLOCK, LANES), lambda p, qb: (qb, p)),
        out_shape=jax.ShapeDtypeStruct((T, WIDTH), BF16),
        scratch_shapes=[pltpu.VMEM((2, 2, 2 * MOBA_BLOCK, MOBA_BLOCK), F32),
                        pltpu.VMEM((2, 2, 2 * MOBA_BLOCK, MOBA_BLOCK), BF16),
                        pltpu.VMEM((2, VT_ROWS, MOBA_BLOCK), F32)],
        compiler_params=_cparams(("arbitrary", "arbitrary")),
        name="moba",
    )(qkv, k_aug, vt_aug, r_mats, bias_tiles)


def _moba_prep_kernel(k_ref, v_ref, ka_ref, vt_ref):
    B = MOBA_BLOCK
    step = pl.program_id(1)
    extra = step == pl.num_programs(1) - 1
    lane = lax.broadcasted_iota(jnp.int32, (B, LANES), 1)
    ones = jnp.ones((VT_ROWS - HEAD_DIM, B), F32)
    for jj in range(PREP_BLOCKS):
        j = jnp.minimum(step, pl.num_programs(1) - 2) * PREP_BLOCKS + jj
        kblk = k_ref[jj * B:(jj + 1) * B, :]
        v_t = v_ref[jj * B:(jj + 1) * B, :].astype(F32).T
        for h2 in range(2):
            off = HEAD_DIM * (1 - h2)
            inr = (lane >= off) & (lane < off + HEAD_DIM)
            hot = jnp.where(lane == off + j, 1.0, 0.0).astype(BF16)
            ka_ref[h2, jj * B:(jj + 1) * B, :] = jnp.where(inr, hot, kblk)
            tile = jnp.concatenate([v_t[h2 * HEAD_DIM:(h2 + 1) * HEAD_DIM], ones], axis=0)
            vt_ref[h2, jj] = jnp.where(extra, 0.0, tile).astype(BF16)


def _moba_prep(qkv):
    T = qkv.shape[0]
    nb = T // MOBA_BLOCK
    npair = HEADS // 2
    kcol = WIDTH // LANES
    steps = nb // PREP_BLOCKS
    return pl.pallas_call(
        _moba_prep_kernel,
        grid=(npair, steps + 1),
        in_specs=[
            pl.BlockSpec((PREP_BLOCKS * MOBA_BLOCK, LANES), lambda p, j: (jnp.minimum(j, steps - 1), kcol + p)),
            pl.BlockSpec((PREP_BLOCKS * MOBA_BLOCK, LANES),
                         lambda p, j: (jnp.minimum(j, steps - 1), 2 * kcol + p)),
        ],
        out_specs=[
            pl.BlockSpec((2, PREP_BLOCKS * MOBA_BLOCK, LANES), lambda p, j: (p, jnp.minimum(j, steps - 1), 0)),
            pl.BlockSpec((2, PREP_BLOCKS, VT_ROWS, MOBA_BLOCK), lambda p, j: (p, j, 0, 0)),
        ],
        out_shape=[
            jax.ShapeDtypeStruct((HEADS, T, LANES), BF16),
            jax.ShapeDtypeStruct((HEADS, nb + PREP_BLOCKS, VT_ROWS, MOBA_BLOCK), BF16),
        ],
        compiler_params=_cparams(("arbitrary", "arbitrary")),
        name="moba_prep",
    )(qkv, qkv)


def _score_mats(kmean):
    nb = kmean.shape[0]
    km = kmean.reshape(nb, HEADS, HEAD_DIM).transpose(1, 2, 0)
    km = jnp.pad(km, ((0, 0), (0, 0), (0, HEAD_DIM - nb)))
    z = jnp.zeros((HEADS // 2, HEAD_DIM, HEAD_DIM), F32)
    even = jnp.concatenate([jnp.concatenate([z, km[0::2]], axis=2),
                            jnp.concatenate([z, z], axis=2)], axis=1)
    odd = jnp.concatenate([jnp.concatenate([z, z], axis=2),
                           jnp.concatenate([km[1::2], z], axis=2)], axis=1)
    return jnp.swapaxes(jnp.stack([even, odd], axis=1), -1, -2).astype(BF16)


def _memkv_kernel(m_ref, g_ref, w_ref, o_ref):
    h = _rms(m_ref[...], g_ref[...]).astype(BF16)
    o_ref[...] = jnp.dot(h, w_ref[...], preferred_element_type=F32).astype(BF16)


def _memkv(mem2d, g, w_bf):
    return pl.pallas_call(
        _memkv_kernel,
        out_shape=jax.ShapeDtypeStruct((N_MEM, 2 * XATTN_WIDTH), BF16),
        compiler_params=pltpu.CompilerParams(vmem_limit_bytes=VMEM_LIMIT),
        name="memkv",
    )(mem2d, g, w_bf)


def _merge_kernel(x_ref, oa_ref, ob_ref, gt_ref, woa_ref, wob_ref, wout_ref, nx_ref, wq_ref,
                  kv_ref, wo_ref, nf_ref, wr_ref, x2_ref, h3_ref, rt_ref):
    pa = jnp.dot(oa_ref[...], woa_ref[...], preferred_element_type=F32)
    pb = jnp.dot(ob_ref[...], wob_ref[...], preferred_element_type=F32)
    ga = gt_ref[:, 0:D_MODEL].astype(F32)
    gb = gt_ref[:, D_MODEL:2 * D_MODEL].astype(F32)
    merged = _sigmoid(ga) * pa + _sigmoid(gb) * pb
    x1 = x_ref[...] + jnp.dot(merged.astype(BF16), wout_ref[...], preferred_element_type=F32)
    h2 = _rms(x1, nx_ref[...]).astype(BF16)
    q = jnp.dot(h2, wq_ref[...], preferred_element_type=F32).astype(BF16)
    heads = []
    for h in range(XATTN_HEADS):
        sl = slice(h * XATTN_HEAD_DIM, (h + 1) * XATTN_HEAD_DIM)
        km = kv_ref[:, sl]
        vm = kv_ref[:, XATTN_WIDTH + h * XATTN_HEAD_DIM:XATTN_WIDTH + (h + 1) * XATTN_HEAD_DIM]
        lg = lax.dot_general(q[:, sl], km, (((1,), (1,)), ((), ())),
                             preferred_element_type=F32) * (XATTN_HEAD_DIM ** -0.5)
        e = jnp.exp(lg - jnp.max(lg, axis=-1, keepdims=True))
        p = e / jnp.sum(e, axis=-1, keepdims=True)
        heads.append(jnp.dot(p.astype(BF16), vm, preferred_element_type=F32))
    o = jnp.concatenate(heads, axis=-1).astype(BF16)
    x2 = x1 + jnp.dot(o, wo_ref[...], preferred_element_type=F32)
    x2_ref[...] = x2
    h3 = _rms(x2, nf_ref[...])
    _store_token_tiles(h3_ref, h3)
    lg = _bdot(h3, wr_ref[...])
    lane = lax.broadcasted_iota(jnp.int32, lg.shape, 1)
    big = jnp.int32(1 << 20)
    is_g = lane < N_GROUPS
    gl = jnp.where(is_g, lg, -jnp.inf)
    gmax = jnp.max(gl, axis=-1, keepdims=True)
    gsel = jnp.min(jnp.where(gl == gmax, lane, big), axis=-1, keepdims=True)
    p_top = 1.0 / jnp.sum(jnp.where(is_g, jnp.exp(gl - gmax), 0.0), axis=-1, keepdims=True)
    e_id = lane - N_GROUPS
    in_grp = (e_id >= gsel * EXPERTS_PER_GROUP) & (e_id < (gsel + 1) * EXPERTS_PER_GROUP)
    el = jnp.where(in_grp, lg, -jnp.inf)
    v1 = jnp.max(el, axis=-1, keepdims=True)
    i1 = jnp.min(jnp.where(el == v1, lane, big), axis=-1, keepdims=True)
    el2 = jnp.where(lane == i1, -jnp.inf, el)
    v2 = jnp.max(el2, axis=-1, keepdims=True)
    i2 = jnp.min(jnp.where(el2 == v2, lane, big), axis=-1, keepdims=True)
    e21 = jnp.exp(v2 - v1)
    g1 = p_top / (1.0 + e21)
    g2 = p_top * e21 / (1.0 + e21)
    rt = jnp.where(lane == 0, (i1 - N_GROUPS).astype(F32),
                   jnp.where(lane == 1, (i2 - N_GROUPS).astype(F32),
                             jnp.where(lane == 2, g1, jnp.where(lane == 3, g2, 0.0))))
    rt_ref[...] = rt


def _merge(x2d, o_a, o_b, gates, w_oa, w_ob, w_out, n_x, w_q, kv, w_o, n_f, w_r):
    T = x2d.shape[0]
    tm = 512
    full = lambda a: pl.BlockSpec(a.shape, lambda i: (0,) * a.ndim, pipeline_mode=pl.Buffered(1))
    tile = lambda w: pl.BlockSpec((tm, w), lambda i: (i, 0))
    return pl.pallas_call(
        _merge_kernel,
        grid=(T // tm,),
        in_specs=[tile(D_MODEL), tile(WIDTH), tile(WIDTH), tile(GATE_COLS), full(w_oa), full(w_ob),
                  full(w_out), full(n_x), full(w_q), full(kv), full(w_o), full(n_f), full(w_r)],
        out_specs=[tile(D_MODEL), pl.BlockSpec((tm * ROW_TILES, LANES), lambda i: (i, 0)), tile(LANES)],
        out_shape=[jax.ShapeDtypeStruct((T, D_MODEL), F32),
                   jax.ShapeDtypeStruct((T * ROW_TILES, LANES), F32),
                   jax.ShapeDtypeStruct((T, LANES), F32)],
        compiler_params=_cparams(("parallel",)),
        name="merge",
    )(x2d, o_a, o_b, gates, w_oa, w_ob, w_out, n_x, w_q, kv, w_o, n_f, w_r)


def _dispatch_kernel(pos_ref, zs_ref, zv_ref, nu_ref, h_ref, xs_hbm, zero_ref, sem):
    i = pl.program_id(0)
    tm = h_ref.shape[0] // ROW_TILES

    @pl.when(i == 0)
    def _():
        zero_ref[...] = jnp.zeros_like(zero_ref)

        def clear(e):
            dst = xs_hbm.at[pl.ds(pl.multiple_of(zs_ref[e] * ROW_TILES, ROW_TILES),
                                  MOE_BLOCK * ROW_TILES), :]
            return pltpu.make_async_copy(zero_ref, dst, sem)

        for e in range(N_EXPERTS):
            @pl.when(zv_ref[e] > 0)
            def _():
                clear(e).start()

        for e in range(N_EXPERTS):
            @pl.when(zv_ref[e] > 0)
            def _():
                clear(e).wait()

        def tail(b):
            dst = xs_hbm.at[pl.ds(pl.multiple_of(b * (MOE_BLOCK * ROW_TILES), MOE_BLOCK * ROW_TILES),
                                  MOE_BLOCK * ROW_TILES), :]
            return pltpu.make_async_copy(zero_ref, dst, sem)

        n_blocks = xs_hbm.shape[0] // (MOE_BLOCK * ROW_TILES)

        def start_tail(b, c):
            tail(b).start()
            return c

        def wait_tail(b, c):
            tail(b).wait()
            return c

        lax.fori_loop(nu_ref[0], n_blocks, start_tail, 0)
        lax.fori_loop(nu_ref[0], n_blocks, wait_tail, 0)

    def row_copy(r, k):
        p = pos_ref[2 * (i * tm + r) + k]
        dst = xs_hbm.at[pl.ds(pl.multiple_of(p * ROW_TILES, ROW_TILES), ROW_TILES), :]
        return pltpu.make_async_copy(h_ref.at[pl.ds(r * ROW_TILES, ROW_TILES), :], dst, sem)

    for r in range(tm):
        row_copy(r, 0).start(priority=0)
        row_copy(r, 1).start(priority=1)
    for _ in range(tm):
        row_copy(0, 0).wait()
        row_copy(0, 1).wait()


def _dispatch(pos, zstart, zvalid, n_used, h3t, cap):
    T = h3t.shape[0] // ROW_TILES
    tm = 256
    grid_spec = pltpu.PrefetchScalarGridSpec(
        num_scalar_prefetch=4,
        grid=(T // tm,),
        in_specs=[pl.BlockSpec((tm * ROW_TILES, LANES), lambda i, ps, zs, zv, nu: (i, 0))],
        out_specs=pl.BlockSpec(memory_space=pl.ANY),
        scratch_shapes=[pltpu.VMEM((MOE_BLOCK * ROW_TILES, LANES), F32), pltpu.SemaphoreType.DMA],
    )
    return pl.pallas_call(
        _dispatch_kernel,
        grid_spec=grid_spec,
        out_shape=jax.ShapeDtypeStruct((cap * ROW_TILES, LANES), F32),
        compiler_params=_cparams(("arbitrary",)),
        name="dispatch",
    )(pos, zstart, zvalid, n_used, h3t)


def _moe_kernel(be_ref, nu_ref, x_ref, wg_ref, wu_ref, wd_ref, o_ref):
    i = pl.program_id(0)

    @pl.when(i < nu_ref[0])
    def _():
        xb = jnp.concatenate([x.astype(BF16) for x in _load_token_tiles(x_ref, MOE_BLOCK)], axis=-1)
        gate = _bdot(xb, wg_ref[0])
        up = _bdot(xb, wu_ref[0])
        hid = gate * _sigmoid(gate) * up
        _store_token_tiles(o_ref, _bdot(hid, wd_ref[0]))

    @pl.when(i >= nu_ref[0])
    def _():
        o_ref[...] = jnp.zeros_like(o_ref)


def _moe(block_expert, n_used, xs, wg, wu, wd):
    n_blocks = xs.shape[0] // (MOE_BLOCK * ROW_TILES)
    used = lambda i, nu: jnp.minimum(i, jnp.maximum(nu[0] - 1, 0))
    grid_spec = pltpu.PrefetchScalarGridSpec(
        num_scalar_prefetch=2,
        grid=(n_blocks,),
        in_specs=[
            pl.BlockSpec((MOE_BLOCK * ROW_TILES, LANES), lambda i, be, nu: (used(i, nu), 0)),
            pl.BlockSpec((1, D_MODEL, D_EXPERT), lambda i, be, nu: (be[i], 0, 0)),
            pl.BlockSpec((1, D_MODEL, D_EXPERT), lambda i, be, nu: (be[i], 0, 0)),
            pl.BlockSpec((1, D_EXPERT, D_MODEL), lambda i, be, nu: (be[i], 0, 0)),
        ],
        out_specs=pl.BlockSpec((MOE_BLOCK * ROW_TILES, LANES), lambda i, be, nu: (i, 0)),
    )
    return pl.pallas_call(
        _moe_kernel,
        grid_spec=grid_spec,
        out_shape=jax.ShapeDtypeStruct(xs.shape, F32),
        compiler_params=_cparams(("arbitrary",)),
        name="moe",
    )(block_expert, n_used, xs, wg, wu, wd)


def _rank_kernel(rt_ref, rank_ref, cnt_ref, run_ref):
    i = pl.program_id(0)

    @pl.when(i == 0)
    def _():
        run_ref[...] = jnp.zeros_like(run_ref)

    rt = rt_ref[...]
    tm = rt.shape[0]
    lane = lax.broadcasted_iota(jnp.int32, rt.shape, 1).astype(F32)
    oh1 = jnp.where(lane == rt[:, 0:1], 1.0, 0.0)
    oh2 = jnp.where(lane == rt[:, 1:2], 1.0, 0.0)
    both = oh1 + oh2
    ri = lax.broadcasted_iota(jnp.int32, (tm, tm), 0)
    ci = lax.broadcasted_iota(jnp.int32, (tm, tm), 1)
    before = _bdot(jnp.where(ri > ci, 1.0, 0.0), both) + run_ref[0:1, :]
    r1 = jnp.sum(before * oh1, axis=-1, keepdims=True)
    r2 = jnp.sum(before * oh2, axis=-1, keepdims=True)
    lane_i = lax.broadcasted_iota(jnp.int32, rt.shape, 1)
    rank_ref[...] = jnp.where(lane_i == 0, r1, jnp.where(lane_i == 1, r2, 0.0))
    run_ref[...] = run_ref[...] + jnp.sum(both, axis=0, keepdims=True)
    cnt_ref[...] = run_ref[...]


def _rank(rt):
    T = rt.shape[0]
    tm = 256
    return pl.pallas_call(
        _rank_kernel,
        grid=(T // tm,),
        in_specs=[pl.BlockSpec((tm, LANES), lambda i: (i, 0))],
        out_specs=[pl.BlockSpec((tm, LANES), lambda i: (i, 0)),
                   pl.BlockSpec((8, LANES), lambda i: (0, 0))],
        out_shape=[jax.ShapeDtypeStruct((T, LANES), F32), jax.ShapeDtypeStruct((8, LANES), F32)],
        scratch_shapes=[pltpu.VMEM((8, LANES), F32)],
        compiler_params=_cparams(("arbitrary",)),
        name="rank",
    )(rt)


def _moe_plan(expert, rank, counts, n_tok):
    n_assign = n_tok * 2
    e_flat = expert.reshape(n_assign)
    rank = rank.reshape(n_assign)
    padded = (counts + MOE_BLOCK - 1) // MOE_BLOCK * MOE_BLOCK
    p_end = jnp.cumsum(padded)
    p_start = p_end - padded
    pos = (p_start[e_flat] + rank).astype(jnp.int32)
    n_blocks = (n_assign + N_EXPERTS * (MOE_BLOCK - 1) + MOE_BLOCK - 1) // MOE_BLOCK
    cap = n_blocks * MOE_BLOCK
    block_start = jnp.arange(n_blocks, dtype=jnp.int32) * MOE_BLOCK
    block_expert = jnp.minimum(jnp.sum(p_end[None, :] <= block_start[:, None], axis=1),
                               N_EXPERTS - 1).astype(jnp.int32)
    n_used = (p_end[-1] // MOE_BLOCK).astype(jnp.int32).reshape(1)
    last_block = jnp.maximum(p_end - MOE_BLOCK, 0).astype(jnp.int32)
    nonempty = (padded > 0).astype(jnp.int32)
    return pos, cap, block_expert, n_used, last_block, nonempty


def _final_kernel(pos_ref, x_ref, rt_ref, y_hbm, g_ref, o_ref, y1_ref, y2_ref, sem):
    i = pl.program_id(0)
    n = pl.num_programs(0)
    tm = x_ref.shape[0]
    slot = i % 2

    def copies(tile, r, s):
        p1 = pos_ref[2 * (tile * tm + r)]
        p2 = pos_ref[2 * (tile * tm + r) + 1]
        src = lambda p: y_hbm.at[pl.ds(pl.multiple_of(p * ROW_TILES, ROW_TILES), ROW_TILES), :]
        dst = lambda ref: ref.at[s, pl.ds(r * ROW_TILES, ROW_TILES), :]
        return (pltpu.make_async_copy(src(p1), dst(y1_ref), sem.at[s]),
                pltpu.make_async_copy(src(p2), dst(y2_ref), sem.at[s]))

    def gather(tile, s):
        for r in range(tm):
            c1, c2 = copies(tile, r, s)
            c1.start(priority=0)
            c2.start(priority=1)

    def wait_rows(s):
        for _ in range(tm):
            c1, c2 = copies(0, 0, s)
            c1.wait()
            c2.wait()

    @pl.when(i == 0)
    def _():
        gather(0, 0)

    wait_rows(slot)
    gather(jnp.minimum(i + 1, n - 1), 1 - slot)
    rt = rt_ref[...]
    g1 = rt[:, 2:3]
    g2 = rt[:, 3:4]
    y1 = jnp.concatenate(_load_token_tiles(y1_ref.at[slot], tm), axis=-1)
    y2 = jnp.concatenate(_load_token_tiles(y2_ref.at[slot], tm), axis=-1)
    x3 = x_ref[...] + y1 * g1 + y2 * g2
    o_ref[...] = _rms(x3, g_ref[...])

    @pl.when(i == n - 1)
    def _():
        wait_rows(1 - slot)


def _final(pos, x2, rt, y_sorted, g):
    T = x2.shape[0]
    tm = 256
    grid_spec = pltpu.PrefetchScalarGridSpec(
        num_scalar_prefetch=1,
        grid=(T // tm,),
        in_specs=[
            pl.BlockSpec((tm, D_MODEL), lambda i, ps: (i, 0)),
            pl.BlockSpec((tm, LANES), lambda i, ps: (i, 0)),
            pl.BlockSpec(memory_space=pl.ANY),
            pl.BlockSpec((1, D_MODEL), lambda i, ps: (0, 0)),
        ],
        out_specs=pl.BlockSpec((tm, D_MODEL), lambda i, ps: (i, 0)),
        scratch_shapes=[pltpu.VMEM((2, tm * ROW_TILES, LANES), F32),
                        pltpu.VMEM((2, tm * ROW_TILES, LANES), F32),
                        pltpu.SemaphoreType.DMA((2,))],
    )
    return pl.pallas_call(
        _final_kernel,
        grid_spec=grid_spec,
        out_shape=jax.ShapeDtypeStruct((T, D_MODEL), F32),
        compiler_params=_cparams(("arbitrary",)),
        name="final",
    )(pos, x2, rt, y_sorted, g)


def _lora_weight(decay_up, iclr_up, gate_up):
    w = jnp.zeros((LORA_COLS, 3 * WIDTH), F32)
    w = w.at[0:DECAY_LORA, 0:WIDTH].set(decay_up)
    w = w.at[DECAY_LORA:DECAY_LORA + ICLR_LORA, WIDTH:2 * WIDTH].set(iclr_up)
    w = w.at[DECAY_LORA + ICLR_LORA:, 2 * WIDTH:].set(gate_up)
    return w


def _router_weight(w_group, w_expert):
    w = jnp.zeros((D_MODEL, LANES), F32)
    w = w.at[:, 0:N_GROUPS].set(w_group)
    return w.at[:, N_GROUPS:N_GROUPS + N_EXPERTS].set(w_expert)


def kernel(x, mem, rel_bias, mem_norm, norm_mix, w_in, tshift_mu, decay_w0, decay_up, iclr_a0,
           iclr_up, gate_up, k_k, k_a, r_k, ln_x_w, ln_x_b, w_o_rwkv, w_o_moba, w_out,
           norm_xattn, w_q_x, w_kv_x, w_o_x, norm_ffn, w_router_group, w_router_expert,
           w_exp_gate, w_exp_up, w_exp_down, norm_final):
    B, T, D = x.shape
    assert B == 1 and D == D_MODEL and T % MOBA_BLOCK == 0 and T // MOBA_BLOCK <= HEAD_DIM
    assert norm_mix.shape[0] == 1 and (T // MOBA_BLOCK) % PREP_BLOCKS == 0
    row = lambda a: a.reshape(1, -1)
    x2d = x.reshape(T, D)
    kv = _memkv(mem.reshape(N_MEM, D), row(mem_norm), w_kv_x[0].astype(BF16))

    ur, qkv, gates = _inproj(x2d, row(norm_mix[0]), w_in[0].astype(BF16))
    o_a = _rwkv_core(ur, row(tshift_mu[0]), row(decay_w0[0]), row(iclr_a0[0]), row(k_k[0]),
                     row(k_a[0]), row(r_k[0]),
                     _lora_weight(decay_up[0], iclr_up[0], gate_up[0]).astype(BF16),
                     row(ln_x_w[0]), row(ln_x_b[0]))

    kmean = _kmean(qkv).reshape(T // MOBA_BLOCK, WIDTH)
    k_aug, vt_aug = _moba_prep(qkv)
    o_b = _moba(qkv, k_aug, vt_aug, _score_mats(kmean), _bias_tiles(rel_bias))

    x2, h3, rt = _merge(x2d, o_a, o_b, gates, w_o_rwkv[0].astype(BF16), w_o_moba[0].astype(BF16),
                        w_out[0].astype(BF16), row(norm_xattn[0]), w_q_x[0].astype(BF16), kv,
                        w_o_x[0].astype(BF16), row(norm_ffn[0]),
                        _router_weight(w_router_group[0], w_router_expert[0]).astype(BF16))

    expert = rt[:, 0:2].astype(jnp.int32)
    rank, counts = _rank(rt)
    pos, cap, block_expert, n_used, last_block, nonempty = _moe_plan(
        expert, rank[:, 0:2].astype(jnp.int32), counts[0, 0:N_EXPERTS].astype(jnp.int32), T)
    x_sorted = _dispatch(pos, last_block, nonempty, n_used, h3, cap)
    y_sorted = _moe(block_expert, n_used, x_sorted, w_exp_gate[0], w_exp_up[0], w_exp_down[0])
    out = _final(pos, x2, rt, y_sorted, row(norm_final))
    return out.reshape(B, T, D)
```

```python
import functools
import math

import jax
import jax.numpy as jnp
import numpy as np
from jax import lax
from jax.experimental import pallas as pl
from jax.experimental.pallas import tpu as pltpu

F32 = jnp.float32
BF16 = jnp.bfloat16

D_MODEL = 1024
N_MEM = 256
NORM_EPS = 1e-6
NEG_INF = -1e30

HEADS = 8
HEAD_DIM = 64
WIDTH = HEADS * HEAD_DIM
DECAY_LORA = 64
ICLR_LORA = 64
GATE_LORA = 128
LORA_COLS = DECAY_LORA + ICLR_LORA + GATE_LORA
RWKV_COLS = 3 * WIDTH + LORA_COLS
QKV_COLS = 3 * WIDTH
GATE_COLS = 2 * D_MODEL
IN_COLS = RWKV_COLS + QKV_COLS + GATE_COLS
LN_X_EPS = 64e-5
KK_EPS = 1e-12

MOBA_BLOCK = 256
MOBA_TOP = 3
REL_BUCKETS = 32
REL_MAX_DISTANCE = 4096
N_BIAS_TILES = 14

XATTN_HEADS = 4
XATTN_HEAD_DIM = 128
XATTN_WIDTH = XATTN_HEADS * XATTN_HEAD_DIM

N_GROUPS = 4
EXPERTS_PER_GROUP = 8
N_EXPERTS = N_GROUPS * EXPERTS_PER_GROUP
D_EXPERT = 512
MOE_BLOCK = 256

PREP_BLOCKS = 4
VT_ROWS = HEAD_DIM + 16
CHUNK = 64
CHUNKS_PER_STEP = 4
LANES = 128
VMEM_LIMIT = 48 * 1024 * 1024

LOG2E = math.log2(math.e)


def _cparams(sem):
    return pltpu.CompilerParams(dimension_semantics=sem, vmem_limit_bytes=VMEM_LIMIT)


def _bdot(a, b):
    return jnp.dot(a.astype(BF16), b.astype(BF16), preferred_element_type=F32)


def _bdot_nt(a, b):
    return lax.dot_general(a.astype(BF16), b.astype(BF16), (((1,), (1,)), ((), ())),
                           preferred_element_type=F32)


def _bdot_tn(a, b):
    return lax.dot_general(a.astype(BF16), b.astype(BF16), (((0,), (0,)), ((), ())),
                           preferred_element_type=F32)


def _rms(x, g):
    return x * lax.rsqrt(jnp.mean(x * x, axis=-1, keepdims=True) + NORM_EPS) * g


def _sigmoid(x):
    return 1.0 / (1.0 + jnp.exp(-x))


ROW_TILES = D_MODEL // 128


def _store_token_tiles(ref, x):
    n = x.shape[0]
    for s in range(ROW_TILES):
        ref[pl.ds(s, n, stride=ROW_TILES), :] = x[:, s * LANES:(s + 1) * LANES]


def _load_token_tiles(ref, n):
    return [ref[pl.ds(s, n, stride=ROW_TILES), :] for s in range(ROW_TILES)]


def _inproj_kernel(x_ref, g_ref, w_ref, ur_ref, qkv_ref, gate_ref):
    h = _rms(x_ref[...], g_ref[...]).astype(BF16)
    step = 256
    for c0 in range(0, RWKV_COLS, step):
        ur_ref[:, c0:c0 + step] = jnp.dot(h, w_ref[:, c0:c0 + step], preferred_element_type=F32)
    for c0 in range(0, QKV_COLS, step):
        o = jnp.dot(h, w_ref[:, RWKV_COLS + c0:RWKV_COLS + c0 + step], preferred_element_type=F32)
        if c0 < WIDTH:
            o = o * (HEAD_DIM ** -0.5 * LOG2E)
        qkv_ref[:, c0:c0 + step] = o.astype(BF16)
    base = RWKV_COLS + QKV_COLS
    for c0 in range(0, GATE_COLS, step):
        o = jnp.dot(h, w_ref[:, base + c0:base + c0 + step], preferred_element_type=F32)
        gate_ref[:, c0:c0 + step] = o.astype(BF16)


def _inproj(x2d, g, w_bf):
    T = x2d.shape[0]
    tm = 512
    return pl.pallas_call(
        _inproj_kernel,
        grid=(T // tm,),
        in_specs=[
            pl.BlockSpec((tm, D_MODEL), lambda i: (i, 0)),
            pl.BlockSpec((1, D_MODEL), lambda i: (0, 0)),
            pl.BlockSpec((D_MODEL, IN_COLS), lambda i: (0, 0), pipeline_mode=pl.Buffered(1)),
        ],
        out_specs=[
            pl.BlockSpec((tm, RWKV_COLS), lambda i: (i, 0)),
            pl.BlockSpec((tm, QKV_COLS), lambda i: (i, 0)),
            pl.BlockSpec((tm, GATE_COLS), lambda i: (i, 0)),
        ],
        out_shape=[
            jax.ShapeDtypeStruct((T, RWKV_COLS), F32),
            jax.ShapeDtypeStruct((T, QKV_COLS), BF16),
            jax.ShapeDtypeStruct((T, GATE_COLS), BF16),
        ],
        compiler_params=_cparams(("parallel",)),
        name="inproj",
    )(x2d, g, w_bf)


def _head_ones():
    r = lax.broadcasted_iota(jnp.int32, (LANES, LANES), 0) // HEAD_DIM
    c = lax.broadcasted_iota(jnp.int32, (LANES, LANES), 1) // HEAD_DIM
    return jnp.where(r == c, 1.0, 0.0).astype(BF16)


def _head_sum(x, ones):
    parts = [_bdot(x[:, c:c + LANES], ones) for c in range(0, WIDTH, LANES)]
    return jnp.concatenate(parts, axis=-1)


def _rwkv_prep_values(u_ref, up_ref, mu_ref, w0_ref, a0_ref, kk_ref, ka_ref, rk_ref, wl_ref):
    i = pl.program_id(0)
    u = u_ref[...]
    tm = u.shape[0]
    prev_last = up_ref[7:8, :] * jnp.where(i > 0, 1.0, 0.0)
    rolled = pltpu.roll(u, 1, 0)
    row = lax.broadcasted_iota(jnp.int32, u.shape, 0)
    u_prev = jnp.where(row == 0, prev_last, rolled)
    u = u + mu_ref[...] * (u_prev - u)
    r = u[:, 0:WIDTH]
    k = u[:, WIDTH:2 * WIDTH]
    v = u[:, 2 * WIDTH:3 * WIDTH]
    lo = u[:, 3 * WIDTH:3 * WIDTH + LORA_COLS]
    lane = lax.broadcasted_iota(jnp.int32, lo.shape, 1)
    act = jnp.where(lane < DECAY_LORA, jnp.tanh(lo),
                    jnp.where(lane < DECAY_LORA + ICLR_LORA, lo, _sigmoid(lo)))
    up = _bdot(act, wl_ref[...])
    z = -(w0_ref[...] + up[:, 0:WIDTH])
    softplus = jnp.maximum(z, 0.0) + jnp.log(1.0 + jnp.exp(-jnp.abs(z)))
    w_log = -softplus - 0.5
    lw = -jnp.exp(w_log)
    iclr = _sigmoid(a0_ref[...] + up[:, WIDTH:2 * WIDTH])
    g = up[:, 2 * WIDTH:3 * WIDTH]
    ones = _head_ones()
    kk = k * kk_ref[...]
    kk = kk * lax.rsqrt(_head_sum(kk * kk, ones) + KK_EPS)
    k2 = k * (1.0 + (iclr - 1.0) * ka_ref[...])
    bonus = _head_sum(r * k2 * rk_ref[...], ones) * v
    return r, lw, k2, v, -kk, kk * iclr, g, bonus


def _rwkv_core_kernel(u_ref, up_ref, mu_ref, w0_ref, a0_ref, kk_ref, ka_ref, rk_ref, wl_ref,
                      lnw_ref, lnb_ref, o_ref, s_ref):
    c = pl.program_id(0)

    @pl.when(c == 0)
    def _():
        s_ref[...] = jnp.zeros_like(s_ref)

    C = CHUNK
    G = CHUNKS_PER_STEP
    ri = lax.broadcasted_iota(jnp.int32, (C, C), 0)
    ci = lax.broadcasted_iota(jnp.int32, (C, C), 1)
    rg = lax.broadcasted_iota(jnp.int32, (G * C, G * C), 0)
    cg = lax.broadcasted_iota(jnp.int32, (G * C, G * C), 1)
    r_in, lw, k_in, v_all, a_in, b_in, g_in, bonus = _rwkv_prep_values(
        u_ref, up_ref, mu_ref, w0_ref, a0_ref, kk_ref, ka_ref, rk_ref, wl_ref)
    tri = jnp.where((rg >= cg) & (rg // C == cg // C), 1.0, 0.0).astype(BF16)
    lw_hi = lw.astype(BF16)
    lw_r1 = lw - lw_hi.astype(F32)
    lw_mid = lw_r1.astype(BF16)
    lw_lo = (lw_r1 - lw_mid.astype(F32)).astype(BF16)
    cum = (jnp.dot(tri, lw_hi, preferred_element_type=F32)
           + jnp.dot(tri, lw_mid, preferred_element_type=F32)
           + jnp.dot(tri, lw_lo, preferred_element_type=F32))
    lam = jnp.exp(cum)
    inv_lam = jnp.exp(-cum)
    r_t = r_in * lam
    a_t = a_in * jnp.exp(cum - lw)
    b_t = b_in * inv_lam
    k_t = k_in * inv_lam
    tots = [cum[g * C + C - 1:g * C + C, :] for g in range(G)]
    rowg = lax.broadcasted_iota(jnp.int32, cum.shape, 0) // C
    tot = tots[G - 1]
    for g in range(G - 2, -1, -1):
        tot = jnp.where(rowg == g, tots[g], tot)
    rest = jnp.exp(tot - cum)
    b_h = b_in * rest
    k_h = k_in * rest
    lam_c = [jnp.exp(t) for t in tots]
    eye = jnp.where(ri == ci, 1.0, 0.0)
    H = range(G * HEADS)
    sls = [slice(h * HEAD_DIM, (h + 1) * HEAD_DIM) for h in range(HEADS)]
    bf = lambda x: x.astype(BF16)
    part = lambda x, i: x[(i // HEADS) * C:(i // HEADS + 1) * C, sls[i % HEADS]]
    at = [bf(part(a_t, i)) for i in H]
    rt = [part(r_t, i) for i in H]
    bt = [bf(part(b_t, i)) for i in H]
    kt = [bf(part(k_t, i)) for i in H]
    bh = [bf(part(b_h, i)) for i in H]
    kh = [bf(part(k_h, i)) for i in H]
    vv = [bf(part(v_all, i)) for i in H]
    ci2 = lax.broadcasted_iota(jnp.int32, (C, 2 * C), 1)
    ri2 = lax.broadcasted_iota(jnp.int32, (C, 2 * C), 0)
    cm2 = jnp.where(ci2 >= C, ci2 - C, ci2)
    left = ci2 < C
    ar = [jnp.concatenate([at[h], bf(rt[h])], axis=0) for h in H]
    bk = [jnp.concatenate([bt[h], kt[h]], axis=0) for h in H]
    g = [_bdot_nt(ar[h], bk[h]) for h in H]
    top = [jnp.where(ri2 > cm2, g[h][0:C], 0.0) for h in H]
    bot = [bf(jnp.where(ri2 >= cm2, g[h][C:2 * C], 0.0)) for h in H]
    a_ab = [top[h][:, 0:C] for h in H]
    akv = [_bdot(top[h][:, C:2 * C], vv[h]) for h in H]
    z = [jnp.concatenate([a_ab[h], eye], axis=1) for h in H]
    for _ in range(6):
        z = [_bdot(z[h][:, 0:C], z[h]) + jnp.where(left, 0.0, z[h]) for h in H]
    tinv = [bf(z[h][:, C:2 * C]) for h in H]
    wu = [_bdot(tinv[h], jnp.concatenate([at[h], bf(akv[h])], axis=1)) for h in H]
    w_m = [bf(wu[h][:, 0:C]) for h in H]
    uv = [jnp.concatenate([bf(wu[h][:, C:2 * C]), vv[h]], axis=0) for h in H]
    q_m = [rt[h] + _bdot(bot[h][:, 0:C], w_m[h]) for h in H]
    y0 = [_bdot(bot[h], uv[h]) for h in H]
    m_k = [_bdot_tn(w_m[h], bh[h]) for h in H]
    n0 = [_bdot_tn(uv[h], jnp.concatenate([bh[h], kh[h]], axis=0)) for h in H]
    state = [s_ref[h] for h in range(HEADS)]
    blocks = []
    for g in range(G):
        outs = []
        for h in range(HEADS):
            i = g * HEADS + h
            y = _bdot_nt(q_m[i], state[h]) + y0[i]
            state[h] = state[h] * lam_c[g][:, sls[h]] + _bdot(state[h], m_k[i]) + n0[i]
            mean = jnp.mean(y, axis=-1, keepdims=True)
            var = jnp.mean(jnp.square(y - mean), axis=-1, keepdims=True)
            outs.append((y - mean) * lax.rsqrt(var + LN_X_EPS))
        blocks.append(jnp.concatenate(outs, axis=-1))
    for h in range(HEADS):
        s_ref[h] = state[h]
    yn = jnp.concatenate(blocks, axis=0)
    yn = yn * lnw_ref[...] + lnb_ref[...] + bonus
    o_ref[...] = (yn * g_in).astype(BF16)


def _rwkv_core(ur, mu, w0, a0, k_k, k_a, r_k, w_lora, ln_w, ln_b):
    T = ur.shape[0]
    rows = CHUNK * CHUNKS_PER_STEP
    row = lambda w: pl.BlockSpec((1, w), lambda c: (0, 0))
    return pl.pallas_call(
        _rwkv_core_kernel,
        grid=(T // rows,),
        in_specs=[
            pl.BlockSpec((rows, RWKV_COLS), lambda c: (c, 0)),
            pl.BlockSpec((8, RWKV_COLS), lambda c: (jnp.maximum(c * (rows // 8) - 1, 0), 0)),
            row(RWKV_COLS), row(WIDTH), row(WIDTH), row(WIDTH), row(WIDTH), row(WIDTH),
            pl.BlockSpec((LORA_COLS, 3 * WIDTH), lambda c: (0, 0)),
            row(WIDTH), row(WIDTH),
        ],
        out_specs=pl.BlockSpec((rows, WIDTH), lambda c: (c, 0)),
        out_shape=jax.ShapeDtypeStruct((T, WIDTH), BF16),
        scratch_shapes=[pltpu.VMEM((HEADS, HEAD_DIM, HEAD_DIM), F32)],
        compiler_params=_cparams(("arbitrary",)),
        name="rwkv_core",
    )(ur, ur, mu, w0, a0, k_k, k_a, r_k, w_lora, ln_w, ln_b)


def _t5_bucket(dist):
    n = jnp.maximum(dist, 0)
    max_exact = REL_BUCKETS // 2
    nf = jnp.maximum(n, max_exact).astype(jnp.float32)
    large = max_exact + (jnp.log(nf / max_exact) / math.log(REL_MAX_DISTANCE / max_exact)
                         * (REL_BUCKETS - max_exact)).astype(jnp.int32)
    large = jnp.minimum(large, REL_BUCKETS - 1)
    return jnp.where(n < max_exact, n, large)


def _bucket_tiles():
    i = jnp.arange(MOBA_BLOCK)[None, :]
    j = jnp.arange(MOBA_BLOCK)[:, None]
    d = jnp.arange(N_BIAS_TILES + 1)[:, None, None]
    dist = d * MOBA_BLOCK + i - j
    bucket = _t5_bucket(dist)
    bucket = jnp.where(d == N_BIAS_TILES - 1, REL_BUCKETS - 1, bucket)
    return jnp.where((dist < 0) | (d == N_BIAS_TILES), -1, bucket).astype(jnp.int32)


def _bias_tiles_kernel(idx_ref, rb_ref, o_ref):
    h = pl.program_id(0)

    def tile(d, c):
        idx = idx_ref[d]
        acc = jnp.where(idx < 0, NEG_INF, 0.0)
        for bkt in range(REL_BUCKETS):
            acc = jnp.where(idx == bkt, rb_ref[bkt, h] * LOG2E, acc)
        o_ref[0, d] = acc
        return c

    lax.fori_loop(0, N_BIAS_TILES + 1, tile, 0)


def _bias_tiles(rel_bias):
    idx = _bucket_tiles()
    n = N_BIAS_TILES + 1
    return pl.pallas_call(
        _bias_tiles_kernel,
        grid=(HEADS,),
        in_specs=[
            pl.BlockSpec((n, MOBA_BLOCK, MOBA_BLOCK), lambda h: (0, 0, 0)),
            pl.BlockSpec(memory_space=pltpu.SMEM),
        ],
        out_specs=pl.BlockSpec((1, n, MOBA_BLOCK, MOBA_BLOCK), lambda h: (h, 0, 0, 0)),
        out_shape=jax.ShapeDtypeStruct((HEADS, n, MOBA_BLOCK, MOBA_BLOCK), F32),
        compiler_params=_cparams(("parallel",)),
        name="bias_tiles",
    )(idx, rel_bias)


def _moba_kernel(q_ref, ka_ref, vt_ref, r_ref, bias_ref, o_ref, s_ref, p_ref, acc_ref):
    qb = pl.program_id(1)
    B = MOBA_BLOCK
    q_tr = q_ref[...].astype(F32).T
    q_tr_bf = q_tr.astype(BF16)
    row = lax.broadcasted_iota(jnp.int32, (LANES, B), 0)
    big = jnp.int32(1 << 20)
    q_t, q_own_t = [], []
    for h2 in range(2):
        off = HEAD_DIM * (1 - h2)
        inr = (row >= off) & (row < off + HEAD_DIM)
        valid = inr & (row - off < qb)
        sc = jnp.dot(r_ref[0, h2], q_tr_bf, preferred_element_type=F32)
        s = jnp.where(valid, sc, NEG_INF)
        s = jnp.where(inr, s, -jnp.inf)
        sel = jnp.zeros((LANES, B), jnp.bool_)
        for _ in range(MOBA_TOP):
            m = jnp.max(s, axis=0, keepdims=True)
            idx = jnp.min(jnp.where(s == m, row, big), axis=0, keepdims=True)
            pick = row == idx
            sel = jnp.logical_or(sel, pick)
            s = jnp.where(pick, -jnp.inf, s)
        sel = jnp.logical_and(sel, valid)
        q_t.append(jnp.where(inr, jnp.where(sel, 0.0, NEG_INF), q_tr).astype(BF16))
        q_own_t.append(jnp.where(inr, 0.0, q_tr).astype(BF16))

    carry = []
    for h2 in range(2):
        k_own = ka_ref[h2, pl.ds(pl.multiple_of(qb * B, B), B), :]
        s0 = jnp.dot(k_own, q_own_t[h2], preferred_element_type=F32) + bias_ref[h2, 0]
        m0 = jnp.max(s0, axis=0, keepdims=True)
        p0 = jnp.exp2(s0 - m0).astype(BF16)
        carry += [m0, jnp.dot(vt_ref[h2, qb], p0, preferred_element_type=F32)]

    n_tiles = (qb + 1) // 2
    last = jnp.maximum(n_tiles - 1, 0)
    nb = vt_ref.shape[1] - PREP_BLOCKS
    for h2 in range(2):
        acc_ref[h2] = carry[2 * h2 + 1]

    @pl.when((pl.program_id(0) == 0) & (qb == 0))
    def _():
        s_ref[...] = jnp.zeros_like(s_ref)
        p_ref[...] = jnp.zeros_like(p_ref)

    def trip(t, w, stats):
        r = 1 - w
        ok = (t >= 1) & (t <= n_tiles)
        d_a = jnp.where(ok, jnp.clip(qb - 2 * (t - 1), 0, N_BIAS_TILES - 1), N_BIAS_TILES)
        d_b = jnp.where(ok, jnp.clip(qb - 2 * (t - 1) - 1, 0, N_BIAS_TILES - 1), N_BIAS_TILES)
        start = pl.multiple_of(jnp.clip(t, 0, last) * (2 * B), 2 * B)
        ok_pv = (t >= 2) & (t <= n_tiles + 1)
        v_a = jnp.where(ok_pv, 2 * (t - 2), nb)
        v_b = jnp.where(ok_pv, 2 * (t - 2) + 1, nb)
        out = []
        for h2 in range(2):
            m_prev, alpha_p = stats[2 * h2], stats[2 * h2 + 1]
            acc_ref[h2] = (acc_ref[h2] * alpha_p
                           + jnp.dot(vt_ref[h2, v_a], p_ref[r, h2, 0:B], preferred_element_type=F32)
                           + jnp.dot(vt_ref[h2, v_b], p_ref[r, h2, B:2 * B],
                                     preferred_element_type=F32))
            s_a = s_ref[r, h2, 0:B] + bias_ref[h2, d_a]
            s_b = s_ref[r, h2, B:2 * B] + bias_ref[h2, d_b]
            m_new = jnp.maximum(m_prev, jnp.max(jnp.maximum(s_a, s_b), axis=0, keepdims=True))
            out += [m_new, jnp.exp2(m_prev - m_new)]
            p_ref[w, h2, 0:B] = jnp.exp2(s_a - m_new).astype(BF16)
            p_ref[w, h2, B:2 * B] = jnp.exp2(s_b - m_new).astype(BF16)
            s_ref[w, h2] = jnp.dot(ka_ref[h2, pl.ds(start, 2 * B), :], q_t[h2],
                                   preferred_element_type=F32)
        return out

    def body(u, stats):
        stats = trip(2 * u, 0, list(stats))
        return tuple(trip(2 * u + 1, 1, stats))

    stats = []
    for h2 in range(2):
        stats += [carry[2 * h2], jnp.ones_like(carry[2 * h2])]
    lax.fori_loop(0, (n_tiles + 3) // 2, body, tuple(stats))
    outs = [acc_ref[h2, 0:HEAD_DIM] / acc_ref[h2, HEAD_DIM:HEAD_DIM + 1] for h2 in range(2)]
    o_ref[...] = jnp.concatenate(outs, axis=0).T.astype(BF16)


def _moba(qkv, k_aug, vt_aug, r_mats, bias_tiles):
    T = qkv.shape[0]
    nb = T // MOBA_BLOCK
    npair = HEADS // 2
    once = pl.Buffered(1)
    return pl.pallas_call(
        _moba_kernel,
        grid=(npair, nb),
        in_specs=[
            pl.BlockSpec((MOBA_BLOCK, LANES), lambda p, qb: (qb, p)),
            pl.BlockSpec((2, T, LANES), lambda p, qb: (p, 0, 0), pipeline_mode=once),
            pl.BlockSpec((2, nb + PREP_BLOCKS, VT_ROWS, MOBA_BLOCK), lambda p, qb: (p, 0, 0, 0),
                         pipeline_mode=once),
            pl.BlockSpec((1, 2, LANES, LANES), lambda p, qb: (p, 0, 0, 0)),
            pl.BlockSpec((2, N_BIAS_TILES + 1, MOBA_BLOCK, MOBA_BLOCK), lambda p, qb: (p, 0, 0, 0),
                         pipeline_mode=once),
        ],
        out_specs=pl.BlockSpec((MOBA_BLOCK, LANES), lambda p, qb: (qb, p)),
        out_shape=jax.ShapeDtypeStruct((T, WIDTH), BF16),
        scratch_shapes=[pltpu.VMEM((2, 2, 2 * MOBA_BLOCK, MOBA_BLOCK), F32),
                        pltpu.VMEM((2, 2, 2 * MOBA_BLOCK, MOBA_BLOCK), BF16),
                        pltpu.VMEM((2, VT_ROWS, MOBA_BLOCK), F32)],
        compiler_params=_cparams(("arbitrary", "arbitrary")),
        name="moba",
    )(qkv, k_aug, vt_aug, r_mats, bias_tiles)


def _moba_prep_kernel(k_ref, v_ref, ka_ref, vt_ref, km_ref):
    B = MOBA_BLOCK
    step = pl.program_id(1)
    extra = step == pl.num_programs(1) - 1
    lane = lax.broadcasted_iota(jnp.int32, (B, LANES), 1)
    ones = jnp.ones((VT_ROWS - HEAD_DIM, B), F32)
    for jj in range(PREP_BLOCKS):
        j = jnp.minimum(step, pl.num_programs(1) - 2) * PREP_BLOCKS + jj
        kblk = k_ref[jj * B:(jj + 1) * B, :]
        km_ref[0, 0, jj:jj + 1, :] = jnp.mean(kblk.astype(F32), axis=0, keepdims=True)
        v_t = v_ref[jj * B:(jj + 1) * B, :].astype(F32).T
        for h2 in range(2):
            off = HEAD_DIM * (1 - h2)
            inr = (lane >= off) & (lane < off + HEAD_DIM)
            hot = jnp.where(lane == off + j, 1.0, 0.0).astype(BF16)
            ka_ref[h2, jj * B:(jj + 1) * B, :] = jnp.where(inr, hot, kblk)
            tile = jnp.concatenate([v_t[h2 * HEAD_DIM:(h2 + 1) * HEAD_DIM], ones], axis=0)
            vt_ref[h2, jj] = jnp.where(extra, 0.0, tile).astype(BF16)


def _moba_prep(qkv):
    T = qkv.shape[0]
    nb = T // MOBA_BLOCK
    npair = HEADS // 2
    kcol = WIDTH // LANES
    steps = nb // PREP_BLOCKS
    k_aug, vt_aug, km = pl.pallas_call(
        _moba_prep_kernel,
        grid=(npair, steps + 1),
        in_specs=[
            pl.BlockSpec((PREP_BLOCKS * MOBA_BLOCK, LANES), lambda p, j: (jnp.minimum(j, steps - 1), kcol + p)),
            pl.BlockSpec((PREP_BLOCKS * MOBA_BLOCK, LANES),
                         lambda p, j: (jnp.minimum(j, steps - 1), 2 * kcol + p)),
        ],
        out_specs=[
            pl.BlockSpec((2, PREP_BLOCKS * MOBA_BLOCK, LANES), lambda p, j: (p, jnp.minimum(j, steps - 1), 0)),
            pl.BlockSpec((2, PREP_BLOCKS, VT_ROWS, MOBA_BLOCK), lambda p, j: (p, j, 0, 0)),
            pl.BlockSpec((1, 1, PREP_BLOCKS, LANES), lambda p, j: (p, j, 0, 0)),
        ],
        out_shape=[
            jax.ShapeDtypeStruct((HEADS, T, LANES), BF16),
            jax.ShapeDtypeStruct((HEADS, nb + PREP_BLOCKS, VT_ROWS, MOBA_BLOCK), BF16),
            jax.ShapeDtypeStruct((npair, steps + 1, PREP_BLOCKS, LANES), F32),
        ],
        compiler_params=_cparams(("arbitrary", "arbitrary")),
        name="moba_prep",
    )(qkv, qkv)
    kmean = km[:, :steps].reshape(npair, nb, LANES).transpose(1, 0, 2).reshape(nb, WIDTH)
    return k_aug, vt_aug, kmean


def _score_mats(kmean):
    nb = kmean.shape[0]
    km = kmean.reshape(nb, HEADS, HEAD_DIM).transpose(1, 2, 0)
    km = jnp.pad(km, ((0, 0), (0, 0), (0, HEAD_DIM - nb)))
    z = jnp.zeros((HEADS // 2, HEAD_DIM, HEAD_DIM), F32)
    even = jnp.concatenate([jnp.concatenate([z, km[0::2]], axis=2),
                            jnp.concatenate([z, z], axis=2)], axis=1)
    odd = jnp.concatenate([jnp.concatenate([z, z], axis=2),
                           jnp.concatenate([km[1::2], z], axis=2)], axis=1)
    return jnp.swapaxes(jnp.stack([even, odd], axis=1), -1, -2).astype(BF16)


def _memkv_kernel(m_ref, g_ref, w_ref, o_ref):
    h = _rms(m_ref[...], g_ref[...]).astype(BF16)
    o_ref[...] = jnp.dot(h, w_ref[...], preferred_element_type=F32).astype(BF16)


def _memkv(mem2d, g, w_bf):
    return pl.pallas_call(
        _memkv_kernel,
        out_shape=jax.ShapeDtypeStruct((N_MEM, 2 * XATTN_WIDTH), BF16),
        compiler_params=pltpu.CompilerParams(vmem_limit_bytes=VMEM_LIMIT),
        name="memkv",
    )(mem2d, g, w_bf)


def _merge_kernel(x_ref, oa_ref, ob_ref, gt_ref, woa_ref, wob_ref, wout_ref, nx_ref, wq_ref,
                  kv_ref, wo_ref, nf_ref, wr_ref, x2_ref, h3_ref, rt_ref, cnt_ref, run_ref):
    pa = jnp.dot(oa_ref[...], woa_ref[...], preferred_element_type=F32)
    pb = jnp.dot(ob_ref[...], wob_ref[...], preferred_element_type=F32)
    ga = gt_ref[:, 0:D_MODEL].astype(F32)
    gb = gt_ref[:, D_MODEL:2 * D_MODEL].astype(F32)
    merged = _sigmoid(ga) * pa + _sigmoid(gb) * pb
    x1 = x_ref[...] + jnp.dot(merged.astype(BF16), wout_ref[...], preferred_element_type=F32)
    h2 = _rms(x1, nx_ref[...]).astype(BF16)
    q = jnp.dot(h2, wq_ref[...], preferred_element_type=F32).astype(BF16)
    heads = []
    for h in range(XATTN_HEADS):
        sl = slice(h * XATTN_HEAD_DIM, (h + 1) * XATTN_HEAD_DIM)
        km = kv_ref[:, sl]
        vm = kv_ref[:, XATTN_WIDTH + h * XATTN_HEAD_DIM:XATTN_WIDTH + (h + 1) * XATTN_HEAD_DIM]
        lg = lax.dot_general(q[:, sl], km, (((1,), (1,)), ((), ())),
                             preferred_element_type=F32) * (XATTN_HEAD_DIM ** -0.5)
        e = jnp.exp(lg - jnp.max(lg, axis=-1, keepdims=True))
        p = e / jnp.sum(e, axis=-1, keepdims=True)
        heads.append(jnp.dot(p.astype(BF16), vm, preferred_element_type=F32))
    o = jnp.concatenate(heads, axis=-1).astype(BF16)
    x2 = x1 + jnp.dot(o, wo_ref[...], preferred_element_type=F32)
    x2_ref[...] = x2
    h3 = _rms(x2, nf_ref[...])
    _store_token_tiles(h3_ref, h3)
    lg = _bdot(h3, wr_ref[...])
    lane = lax.broadcasted_iota(jnp.int32, lg.shape, 1)
    big = jnp.int32(1 << 20)
    is_g = lane < N_GROUPS
    gl = jnp.where(is_g, lg, -jnp.inf)
    gmax = jnp.max(gl, axis=-1, keepdims=True)
    gsel = jnp.min(jnp.where(gl == gmax, lane, big), axis=-1, keepdims=True)
    p_top = 1.0 / jnp.sum(jnp.where(is_g, jnp.exp(gl - gmax), 0.0), axis=-1, keepdims=True)
    e_id = lane - N_GROUPS
    in_grp = (e_id >= gsel * EXPERTS_PER_GROUP) & (e_id < (gsel + 1) * EXPERTS_PER_GROUP)
    el = jnp.where(in_grp, lg, -jnp.inf)
    v1 = jnp.max(el, axis=-1, keepdims=True)
    i1 = jnp.min(jnp.where(el == v1, lane, big), axis=-1, keepdims=True)
    el2 = jnp.where(lane == i1, -jnp.inf, el)
    v2 = jnp.max(el2, axis=-1, keepdims=True)
    i2 = jnp.min(jnp.where(el2 == v2, lane, big), axis=-1, keepdims=True)
    e21 = jnp.exp(v2 - v1)
    g1 = p_top / (1.0 + e21)
    g2 = p_top * e21 / (1.0 + e21)
    step = pl.program_id(0)

    @pl.when(step == 0)
    def _():
        run_ref[...] = jnp.zeros_like(run_ref)

    tm = lg.shape[0]
    oh1 = jnp.where(lane == i1, 1.0, 0.0)
    oh2 = jnp.where(lane == i2, 1.0, 0.0)
    both = oh1 + oh2
    ri = lax.broadcasted_iota(jnp.int32, (tm, tm), 0)
    ci = lax.broadcasted_iota(jnp.int32, (tm, tm), 1)
    before = _bdot(jnp.where(ri > ci, 1.0, 0.0), both) + run_ref[0:1, :]
    r1 = jnp.sum(before * oh1, axis=-1, keepdims=True)
    r2 = jnp.sum(before * oh2, axis=-1, keepdims=True)
    run_ref[...] = run_ref[...] + jnp.sum(both, axis=0, keepdims=True)
    cnt_ref[...] = run_ref[...]
    vals = [(i1 - N_GROUPS).astype(F32), (i2 - N_GROUPS).astype(F32), g1, g2, r1, r2]
    rt = jnp.zeros_like(lg)
    for k, v in enumerate(vals):
        rt = jnp.where(lane == k, v, rt)
    rt_ref[...] = rt


def _merge(x2d, o_a, o_b, gates, w_oa, w_ob, w_out, n_x, w_q, kv, w_o, n_f, w_r):
    T = x2d.shape[0]
    tm = 512
    full = lambda a: pl.BlockSpec(a.shape, lambda i: (0,) * a.ndim, pipeline_mode=pl.Buffered(1))
    tile = lambda w: pl.BlockSpec((tm, w), lambda i: (i, 0))
    return pl.pallas_call(
        _merge_kernel,
        grid=(T // tm,),
        in_specs=[tile(D_MODEL), tile(WIDTH), tile(WIDTH), tile(GATE_COLS), full(w_oa), full(w_ob),
                  full(w_out), full(n_x), full(w_q), full(kv), full(w_o), full(n_f), full(w_r)],
        out_specs=[tile(D_MODEL), pl.BlockSpec((tm * ROW_TILES, LANES), lambda i: (i, 0)), tile(LANES),
                   pl.BlockSpec((8, LANES), lambda i: (0, 0))],
        out_shape=[jax.ShapeDtypeStruct((T, D_MODEL), F32),
                   jax.ShapeDtypeStruct((T * ROW_TILES, LANES), F32),
                   jax.ShapeDtypeStruct((T, LANES), F32),
                   jax.ShapeDtypeStruct((8, LANES), F32)],
        scratch_shapes=[pltpu.VMEM((8, LANES), F32)],
        compiler_params=_cparams(("arbitrary",)),
        name="merge",
    )(x2d, o_a, o_b, gates, w_oa, w_ob, w_out, n_x, w_q, kv, w_o, n_f, w_r)


def _dispatch_kernel(pos_ref, zs_ref, zv_ref, nu_ref, h_ref, xs_hbm, zero_ref, sem):
    i = pl.program_id(0)
    tm = h_ref.shape[0] // ROW_TILES

    @pl.when(i == 0)
    def _():
        zero_ref[...] = jnp.zeros_like(zero_ref)

        def clear(e):
            dst = xs_hbm.at[pl.ds(pl.multiple_of(zs_ref[e] * ROW_TILES, ROW_TILES),
                                  MOE_BLOCK * ROW_TILES), :]
            return pltpu.make_async_copy(zero_ref, dst, sem)

        for e in range(N_EXPERTS):
            @pl.when(zv_ref[e] > 0)
            def _():
                clear(e).start()

        for e in range(N_EXPERTS):
            @pl.when(zv_ref[e] > 0)
            def _():
                clear(e).wait()

        def tail(b):
            dst = xs_hbm.at[pl.ds(pl.multiple_of(b * (MOE_BLOCK * ROW_TILES), MOE_BLOCK * ROW_TILES),
                                  MOE_BLOCK * ROW_TILES), :]
            return pltpu.make_async_copy(zero_ref, dst, sem)

        n_blocks = xs_hbm.shape[0] // (MOE_BLOCK * ROW_TILES)

        def start_tail(b, c):
            tail(b).start()
            return c

        def wait_tail(b, c):
            tail(b).wait()
            return c

        lax.fori_loop(nu_ref[0], n_blocks, start_tail, 0)
        lax.fori_loop(nu_ref[0], n_blocks, wait_tail, 0)

    def row_copy(r, k):
        p = pos_ref[2 * (i * tm + r) + k]
        dst = xs_hbm.at[pl.ds(pl.multiple_of(p * ROW_TILES, ROW_TILES), ROW_TILES), :]
        return pltpu.make_async_copy(h_ref.at[pl.ds(r * ROW_TILES, ROW_TILES), :], dst, sem)

    for r in range(tm):
        row_copy(r, 0).start(priority=0)
        row_copy(r, 1).start(priority=1)
    for _ in range(tm):
        row_copy(0, 0).wait()
        row_copy(0, 1).wait()


def _dispatch(pos, zstart, zvalid, n_used, h3t, cap):
    T = h3t.shape[0] // ROW_TILES
    tm = 256
    grid_spec = pltpu.PrefetchScalarGridSpec(
        num_scalar_prefetch=4,
        grid=(T // tm,),
        in_specs=[pl.BlockSpec((tm * ROW_TILES, LANES), lambda i, ps, zs, zv, nu: (i, 0))],
        out_specs=pl.BlockSpec(memory_space=pl.ANY),
        scratch_shapes=[pltpu.VMEM((MOE_BLOCK * ROW_TILES, LANES), F32), pltpu.SemaphoreType.DMA],
    )
    return pl.pallas_call(
        _dispatch_kernel,
        grid_spec=grid_spec,
        out_shape=jax.ShapeDtypeStruct((cap * ROW_TILES, LANES), F32),
        compiler_params=_cparams(("arbitrary",)),
        name="dispatch",
    )(pos, zstart, zvalid, n_used, h3t)


def _moe_kernel(be_ref, nu_ref, x_ref, wg_ref, wu_ref, wd_ref, o_ref):
    i = pl.program_id(0)

    @pl.when(i < nu_ref[0])
    def _():
        xb = jnp.concatenate([x.astype(BF16) for x in _load_token_tiles(x_ref, MOE_BLOCK)], axis=-1)
        gate = _bdot(xb, wg_ref[0])
        up = _bdot(xb, wu_ref[0])
        hid = gate * _sigmoid(gate) * up
        _store_token_tiles(o_ref, _bdot(hid, wd_ref[0]))

    @pl.when(i >= nu_ref[0])
    def _():
        o_ref[...] = jnp.zeros_like(o_ref)


def _moe(block_expert, n_used, xs, wg, wu, wd):
    n_blocks = xs.shape[0] // (MOE_BLOCK * ROW_TILES)
    used = lambda i, nu: jnp.minimum(i, jnp.maximum(nu[0] - 1, 0))
    grid_spec = pltpu.PrefetchScalarGridSpec(
        num_scalar_prefetch=2,
        grid=(n_blocks,),
        in_specs=[
            pl.BlockSpec((MOE_BLOCK * ROW_TILES, LANES), lambda i, be, nu: (used(i, nu), 0)),
            pl.BlockSpec((1, D_MODEL, D_EXPERT), lambda i, be, nu: (be[i], 0, 0)),
            pl.BlockSpec((1, D_MODEL, D_EXPERT), lambda i, be, nu: (be[i], 0, 0)),
            pl.BlockSpec((1, D_EXPERT, D_MODEL), lambda i, be, nu: (be[i], 0, 0)),
        ],
        out_specs=pl.BlockSpec((MOE_BLOCK * ROW_TILES, LANES), lambda i, be, nu: (i, 0)),
    )
    return pl.pallas_call(
        _moe_kernel,
        grid_spec=grid_spec,
        out_shape=jax.ShapeDtypeStruct(xs.shape, F32),
        compiler_params=_cparams(("arbitrary",)),
        name="moe",
    )(block_expert, n_used, xs, wg, wu, wd)


def _moe_plan(expert, rank, counts, n_tok):
    n_assign = n_tok * 2
    e_flat = expert.reshape(n_assign)
    rank = rank.reshape(n_assign)
    padded = (counts + MOE_BLOCK - 1) // MOE_BLOCK * MOE_BLOCK
    p_end = jnp.cumsum(padded)
    p_start = p_end - padded
    pos = (p_start[e_flat] + rank).astype(jnp.int32)
    n_blocks = (n_assign + N_EXPERTS * (MOE_BLOCK - 1) + MOE_BLOCK - 1) // MOE_BLOCK
    cap = n_blocks * MOE_BLOCK
    block_start = jnp.arange(n_blocks, dtype=jnp.int32) * MOE_BLOCK
    block_expert = jnp.minimum(jnp.sum(p_end[None, :] <= block_start[:, None], axis=1),
                               N_EXPERTS - 1).astype(jnp.int32)
    n_used = (p_end[-1] // MOE_BLOCK).astype(jnp.int32).reshape(1)
    last_block = jnp.maximum(p_end - MOE_BLOCK, 0).astype(jnp.int32)
    nonempty = (padded > 0).astype(jnp.int32)
    return pos, cap, block_expert, n_used, last_block, nonempty


def _final_kernel(pos_ref, x_ref, rt_ref, y_hbm, g_ref, o_ref, y1_ref, y2_ref, sem):
    i = pl.program_id(0)
    n = pl.num_programs(0)
    tm = x_ref.shape[0]
    slot = i % 2

    def copies(tile, r, s):
        p1 = pos_ref[2 * (tile * tm + r)]
        p2 = pos_ref[2 * (tile * tm + r) + 1]
        src = lambda p: y_hbm.at[pl.ds(pl.multiple_of(p * ROW_TILES, ROW_TILES), ROW_TILES), :]
        dst = lambda ref: ref.at[s, pl.ds(r * ROW_TILES, ROW_TILES), :]
        return (pltpu.make_async_copy(src(p1), dst(y1_ref), sem.at[s]),
                pltpu.make_async_copy(src(p2), dst(y2_ref), sem.at[s]))

    def gather(tile, s):
        for r in range(tm):
            c1, c2 = copies(tile, r, s)
            c1.start(priority=0)
            c2.start(priority=1)

    def wait_rows(s):
        for _ in range(tm):
            c1, c2 = copies(0, 0, s)
            c1.wait()
            c2.wait()

    @pl.when(i == 0)
    def _():
        gather(0, 0)

    wait_rows(slot)
    gather(jnp.minimum(i + 1, n - 1), 1 - slot)
    rt = rt_ref[...]
    g1 = rt[:, 2:3]
    g2 = rt[:, 3:4]
    y1 = jnp.concatenate(_load_token_tiles(y1_ref.at[slot], tm), axis=-1)
    y2 = jnp.concatenate(_load_token_tiles(y2_ref.at[slot], tm), axis=-1)
    x3 = x_ref[...] + y1 * g1 + y2 * g2
    o_ref[...] = _rms(x3, g_ref[...])

    @pl.when(i == n - 1)
    def _():
        wait_rows(1 - slot)


def _final(pos, x2, rt, y_sorted, g):
    T = x2.shape[0]
    tm = 256
    grid_spec = pltpu.PrefetchScalarGridSpec(
        num_scalar_prefetch=1,
        grid=(T // tm,),
        in_specs=[
            pl.BlockSpec((tm, D_MODEL), lambda i, ps: (i, 0)),
            pl.BlockSpec((tm, LANES), lambda i, ps: (i, 0)),
            pl.BlockSpec(memory_space=pl.ANY),
            pl.BlockSpec((1, D_MODEL), lambda i, ps: (0, 0)),
        ],
        out_specs=pl.BlockSpec((tm, D_MODEL), lambda i, ps: (i, 0)),
        scratch_shapes=[pltpu.VMEM((2, tm * ROW_TILES, LANES), F32),
                        pltpu.VMEM((2, tm * ROW_TILES, LANES), F32),
                        pltpu.SemaphoreType.DMA((2,))],
    )
    return pl.pallas_call(
        _final_kernel,
        grid_spec=grid_spec,
        out_shape=jax.ShapeDtypeStruct((T, D_MODEL), F32),
        compiler_params=_cparams(("arbitrary",)),
        name="final",
    )(pos, x2, rt, y_sorted, g)


def _lora_weight(decay_up, iclr_up, gate_up):
    w = jnp.zeros((LORA_COLS, 3 * WIDTH), F32)
    w = w.at[0:DECAY_LORA, 0:WIDTH].set(decay_up)
    w = w.at[DECAY_LORA:DECAY_LORA + ICLR_LORA, WIDTH:2 * WIDTH].set(iclr_up)
    w = w.at[DECAY_LORA + ICLR_LORA:, 2 * WIDTH:].set(gate_up)
    return w


def _router_weight(w_group, w_expert):
    w = jnp.zeros((D_MODEL, LANES), F32)
    w = w.at[:, 0:N_GROUPS].set(w_group)
    return w.at[:, N_GROUPS:N_GROUPS + N_EXPERTS].set(w_expert)


def kernel(x, mem, rel_bias, mem_norm, norm_mix, w_in, tshift_mu, decay_w0, decay_up, iclr_a0,
           iclr_up, gate_up, k_k, k_a, r_k, ln_x_w, ln_x_b, w_o_rwkv, w_o_moba, w_out,
           norm_xattn, w_q_x, w_kv_x, w_o_x, norm_ffn, w_router_group, w_router_expert,
           w_exp_gate, w_exp_up, w_exp_down, norm_final):
    B, T, D = x.shape
    assert B == 1 and D == D_MODEL and T % MOBA_BLOCK == 0 and T // MOBA_BLOCK <= HEAD_DIM
    assert norm_mix.shape[0] == 1 and (T // MOBA_BLOCK) % PREP_BLOCKS == 0
    row = lambda a: a.reshape(1, -1)
    x2d = x.reshape(T, D)
    kv = _memkv(mem.reshape(N_MEM, D), row(mem_norm), w_kv_x[0].astype(BF16))

    ur, qkv, gates = _inproj(x2d, row(norm_mix[0]), w_in[0].astype(BF16))
    o_a = _rwkv_core(ur, row(tshift_mu[0]), row(decay_w0[0]), row(iclr_a0[0]), row(k_k[0]),
                     row(k_a[0]), row(r_k[0]),
                     _lora_weight(decay_up[0], iclr_up[0], gate_up[0]).astype(BF16),
                     row(ln_x_w[0]), row(ln_x_b[0]))

    k_aug, vt_aug, kmean = _moba_prep(qkv)
    o_b = _moba(qkv, k_aug, vt_aug, _score_mats(kmean), _bias_tiles(rel_bias))

    x2, h3, rt, counts = _merge(
        x2d, o_a, o_b, gates, w_o_rwkv[0].astype(BF16), w_o_moba[0].astype(BF16),
        w_out[0].astype(BF16), row(norm_xattn[0]), w_q_x[0].astype(BF16), kv,
        w_o_x[0].astype(BF16), row(norm_ffn[0]),
        _router_weight(w_router_group[0], w_router_expert[0]).astype(BF16))

    expert = rt[:, 0:2].astype(jnp.int32)
    rank = rt[:, 4:6].astype(jnp.int32)
    pos, cap, block_expert, n_used, last_block, nonempty = _moe_plan(
        expert, rank, counts[0, N_GROUPS:N_GROUPS + N_EXPERTS].astype(jnp.int32), T)
    x_sorted = _dispatch(pos, last_block, nonempty, n_used, h3, cap)
    y_sorted = _moe(block_expert, n_used, x_sorted, w_exp_gate[0], w_exp_up[0], w_exp_down[0])
    out = _final(pos, x2, rt, y_sorted, row(norm_final))
    return out.reshape(B, T, D)
```

```python
import math

import jax
import jax.numpy as jnp
from jax import lax
from jax.experimental import pallas as pl
from jax.experimental.pallas import tpu as pltpu

F32 = jnp.float32
BF16 = jnp.bfloat16

D_MODEL = 1024
N_MEM = 256
NORM_EPS = 1e-6
NEG_INF = -1e30

HEADS = 8
HEAD_DIM = 64
WIDTH = HEADS * HEAD_DIM
DECAY_LORA = 64
ICLR_LORA = 64
GATE_LORA = 128
LORA_COLS = DECAY_LORA + ICLR_LORA + GATE_LORA
RWKV_COLS = 3 * WIDTH + LORA_COLS
QKV_COLS = 3 * WIDTH
GATE_COLS = 2 * D_MODEL
IN_COLS = RWKV_COLS + QKV_COLS + GATE_COLS
LN_X_EPS = 64e-5
KK_EPS = 1e-12

MOBA_BLOCK = 256
MOBA_TOP = 3
REL_BUCKETS = 32
REL_MAX_DISTANCE = 4096
N_BIAS_TILES = 14

XATTN_HEADS = 4
XATTN_HEAD_DIM = 128
XATTN_WIDTH = XATTN_HEADS * XATTN_HEAD_DIM

N_GROUPS = 4
EXPERTS_PER_GROUP = 8
N_EXPERTS = N_GROUPS * EXPERTS_PER_GROUP
D_EXPERT = 512
MOE_BLOCK = 256

TOKEN_TILE = 512
ROW_DMA_TILE = 256
PREP_BLOCKS = 4
VT_ROWS = HEAD_DIM + 16
CHUNK = 64
CHUNKS_PER_STEP = 4
LANES = 128
VMEM_LIMIT = 48 * 1024 * 1024

LOG2E = math.log2(math.e)


def _cparams(sem):
    return pltpu.CompilerParams(dimension_semantics=sem, vmem_limit_bytes=VMEM_LIMIT)


def _bdot(a, b):
    return jnp.dot(a.astype(BF16), b.astype(BF16), preferred_element_type=F32)


def _bdot_nt(a, b):
    return lax.dot_general(a.astype(BF16), b.astype(BF16), (((1,), (1,)), ((), ())),
                           preferred_element_type=F32)


def _bdot_tn(a, b):
    return lax.dot_general(a.astype(BF16), b.astype(BF16), (((0,), (0,)), ((), ())),
                           preferred_element_type=F32)


def _rms(x, g):
    return x * lax.rsqrt(jnp.mean(x * x, axis=-1, keepdims=True) + NORM_EPS) * g


def _sigmoid(x):
    return 1.0 / (1.0 + jnp.exp(-x))


ROW_TILES = D_MODEL // (2 * LANES)
U32 = jnp.uint32
HI_MASK = 0xFFFF0000


def _bits(x):
    return pltpu.bitcast(x.astype(BF16).astype(F32), U32)


def _store_token_tiles(ref, x):
    n = x.shape[0]
    for s in range(ROW_TILES):
        lo = x[:, 2 * s * LANES:(2 * s + 1) * LANES]
        hi = x[:, (2 * s + 1) * LANES:(2 * s + 2) * LANES]
        word = (_bits(hi) & U32(HI_MASK)) | lax.shift_right_logical(_bits(lo), U32(16))
        ref[pl.ds(s, n, stride=ROW_TILES), :] = word


def _load_token_tiles(ref, n):
    out = []
    for s in range(ROW_TILES):
        word = ref[pl.ds(s, n, stride=ROW_TILES), :]
        out.append(pltpu.bitcast(lax.shift_left(word, U32(16)), F32))
        out.append(pltpu.bitcast(word & U32(HI_MASK), F32))
    return out


def _inproj_kernel(x_ref, g_ref, w_ref, ur_ref, qkv_ref, gate_ref):
    h = _rms(x_ref[...], g_ref[...]).astype(BF16)
    step = 256
    for c0 in range(0, RWKV_COLS, step):
        ur_ref[:, c0:c0 + step] = jnp.dot(h, w_ref[:, c0:c0 + step], preferred_element_type=F32)
    for c0 in range(0, QKV_COLS, step):
        o = jnp.dot(h, w_ref[:, RWKV_COLS + c0:RWKV_COLS + c0 + step], preferred_element_type=F32)
        if c0 < WIDTH:
            o = o * (HEAD_DIM ** -0.5 * LOG2E)
        qkv_ref[:, c0:c0 + step] = o.astype(BF16)
    base = RWKV_COLS + QKV_COLS
    for c0 in range(0, GATE_COLS, step):
        o = jnp.dot(h, w_ref[:, base + c0:base + c0 + step], preferred_element_type=F32)
        gate_ref[:, c0:c0 + step] = o.astype(BF16)


def _inproj(x2d, g, w_bf):
    T = x2d.shape[0]
    tm = TOKEN_TILE
    return pl.pallas_call(
        _inproj_kernel,
        grid=(T // tm,),
        in_specs=[
            pl.BlockSpec((tm, D_MODEL), lambda i: (i, 0)),
            pl.BlockSpec((1, D_MODEL), lambda i: (0, 0)),
            pl.BlockSpec((D_MODEL, IN_COLS), lambda i: (0, 0), pipeline_mode=pl.Buffered(1)),
        ],
        out_specs=[
            pl.BlockSpec((tm, RWKV_COLS), lambda i: (i, 0)),
            pl.BlockSpec((tm, QKV_COLS), lambda i: (i, 0)),
            pl.BlockSpec((tm, GATE_COLS), lambda i: (i, 0)),
        ],
        out_shape=[
            jax.ShapeDtypeStruct((T, RWKV_COLS), F32),
            jax.ShapeDtypeStruct((T, QKV_COLS), BF16),
            jax.ShapeDtypeStruct((T, GATE_COLS), BF16),
        ],
        compiler_params=_cparams(("parallel",)),
        name="inproj",
    )(x2d, g, w_bf)


def _head_ones():
    r = lax.broadcasted_iota(jnp.int32, (LANES, LANES), 0) // HEAD_DIM
    c = lax.broadcasted_iota(jnp.int32, (LANES, LANES), 1) // HEAD_DIM
    return jnp.where(r == c, 1.0, 0.0).astype(BF16)


def _head_sum(x, ones):
    parts = [_bdot(x[:, c:c + LANES], ones) for c in range(0, WIDTH, LANES)]
    return jnp.concatenate(parts, axis=-1)


def _rwkv_prep_values(u_ref, up_ref, mu_ref, w0_ref, a0_ref, kk_ref, ka_ref, rk_ref, wl_ref):
    i = pl.program_id(0)
    u = u_ref[...]
    tm = u.shape[0]
    prev_last = up_ref[7:8, :] * jnp.where(i > 0, 1.0, 0.0)
    rolled = pltpu.roll(u, 1, 0)
    row = lax.broadcasted_iota(jnp.int32, u.shape, 0)
    u_prev = jnp.where(row == 0, prev_last, rolled)
    u = u + mu_ref[...] * (u_prev - u)
    r = u[:, 0:WIDTH]
    k = u[:, WIDTH:2 * WIDTH]
    v = u[:, 2 * WIDTH:3 * WIDTH]
    lo = u[:, 3 * WIDTH:3 * WIDTH + LORA_COLS]
    lane = lax.broadcasted_iota(jnp.int32, lo.shape, 1)
    act = jnp.where(lane < DECAY_LORA, jnp.tanh(lo),
                    jnp.where(lane < DECAY_LORA + ICLR_LORA, lo, _sigmoid(lo)))
    up = _bdot(act, wl_ref[...])
    z = -(w0_ref[...] + up[:, 0:WIDTH])
    softplus = jnp.maximum(z, 0.0) + jnp.log(1.0 + jnp.exp(-jnp.abs(z)))
    w_log = -softplus - 0.5
    lw = -jnp.exp(w_log)
    iclr = _sigmoid(a0_ref[...] + up[:, WIDTH:2 * WIDTH])
    g = up[:, 2 * WIDTH:3 * WIDTH]
    ones = _head_ones()
    kk = k * kk_ref[...]
    kk = kk * lax.rsqrt(_head_sum(kk * kk, ones) + KK_EPS)
    k2 = k * (1.0 + (iclr - 1.0) * ka_ref[...])
    bonus = _head_sum(r * k2 * rk_ref[...], ones) * v
    return r, lw, k2, v, -kk, kk * iclr, g, bonus


def _rwkv_core_kernel(u_ref, up_ref, mu_ref, w0_ref, a0_ref, kk_ref, ka_ref, rk_ref, wl_ref,
                      lnw_ref, lnb_ref, o_ref, s_ref):
    c = pl.program_id(0)

    @pl.when(c == 0)
    def _():
        s_ref[...] = jnp.zeros_like(s_ref)

    C = CHUNK
    G = CHUNKS_PER_STEP
    ri = lax.broadcasted_iota(jnp.int32, (C, C), 0)
    ci = lax.broadcasted_iota(jnp.int32, (C, C), 1)
    rg = lax.broadcasted_iota(jnp.int32, (G * C, G * C), 0)
    cg = lax.broadcasted_iota(jnp.int32, (G * C, G * C), 1)
    r_in, lw, k_in, v_all, a_in, b_in, g_in, bonus = _rwkv_prep_values(
        u_ref, up_ref, mu_ref, w0_ref, a0_ref, kk_ref, ka_ref, rk_ref, wl_ref)
    tri = jnp.where((rg >= cg) & (rg // C == cg // C), 1.0, 0.0).astype(BF16)
    lw_hi = lw.astype(BF16)
    lw_r1 = lw - lw_hi.astype(F32)
    lw_mid = lw_r1.astype(BF16)
    lw_lo = (lw_r1 - lw_mid.astype(F32)).astype(BF16)
    cum = (jnp.dot(tri, lw_hi, preferred_element_type=F32)
           + jnp.dot(tri, lw_mid, preferred_element_type=F32)
           + jnp.dot(tri, lw_lo, preferred_element_type=F32))
    lam = jnp.exp(cum)
    inv_lam = jnp.exp(-cum)
    r_t = r_in * lam
    a_t = a_in * jnp.exp(cum - lw)
    b_t = b_in * inv_lam
    k_t = k_in * inv_lam
    tots = [cum[g * C + C - 1:g * C + C, :] for g in range(G)]
    rowg = lax.broadcasted_iota(jnp.int32, cum.shape, 0) // C
    tot = tots[G - 1]
    for g in range(G - 2, -1, -1):
        tot = jnp.where(rowg == g, tots[g], tot)
    rest = jnp.exp(tot - cum)
    b_h = b_in * rest
    k_h = k_in * rest
    lam_c = [jnp.exp(t) for t in tots]
    eye = jnp.where(ri == ci, 1.0, 0.0)
    H = range(G * HEADS)
    sls = [slice(h * HEAD_DIM, (h + 1) * HEAD_DIM) for h in range(HEADS)]
    bf = lambda x: x.astype(BF16)
    part = lambda x, i: x[(i // HEADS) * C:(i // HEADS + 1) * C, sls[i % HEADS]]
    at = [bf(part(a_t, i)) for i in H]
    rt = [part(r_t, i) for i in H]
    bt = [bf(part(b_t, i)) for i in H]
    kt = [bf(part(k_t, i)) for i in H]
    bh = [bf(part(b_h, i)) for i in H]
    kh = [bf(part(k_h, i)) for i in H]
    vv = [bf(part(v_all, i)) for i in H]
    ci2 = lax.broadcasted_iota(jnp.int32, (C, 2 * C), 1)
    ri2 = lax.broadcasted_iota(jnp.int32, (C, 2 * C), 0)
    cm2 = jnp.where(ci2 >= C, ci2 - C, ci2)
    left = ci2 < C
    ar = [jnp.concatenate([at[h], bf(rt[h])], axis=0) for h in H]
    bk = [jnp.concatenate([bt[h], kt[h]], axis=0) for h in H]
    g = [_bdot_nt(ar[h], bk[h]) for h in H]
    top = [jnp.where(ri2 > cm2, g[h][0:C], 0.0) for h in H]
    bot = [bf(jnp.where(ri2 >= cm2, g[h][C:2 * C], 0.0)) for h in H]
    a_ab = [top[h][:, 0:C] for h in H]
    akv = [_bdot(top[h][:, C:2 * C], vv[h]) for h in H]
    z = [jnp.concatenate([a_ab[h], eye], axis=1) for h in H]
    for _ in range(6):
        z = [_bdot(z[h][:, 0:C], z[h]) + jnp.where(left, 0.0, z[h]) for h in H]
    tinv = [bf(z[h][:, C:2 * C]) for h in H]
    wu = [_bdot(tinv[h], jnp.concatenate([at[h], bf(akv[h])], axis=1)) for h in H]
    w_m = [bf(wu[h][:, 0:C]) for h in H]
    uv = [jnp.concatenate([bf(wu[h][:, C:2 * C]), vv[h]], axis=0) for h in H]
    q_m = [rt[h] + _bdot(bot[h][:, 0:C], w_m[h]) for h in H]
    y0 = [_bdot(bot[h], uv[h]) for h in H]
    m_k = [_bdot_tn(w_m[h], bh[h]) for h in H]
    n0 = [_bdot_tn(uv[h], jnp.concatenate([bh[h], kh[h]], axis=0)) for h in H]
    state = [s_ref[h] for h in range(HEADS)]
    blocks = []
    for g in range(G):
        outs = []
        for h in range(HEADS):
            i = g * HEADS + h
            y = _bdot_nt(q_m[i], state[h]) + y0[i]
            state[h] = state[h] * lam_c[g][:, sls[h]] + _bdot(state[h], m_k[i]) + n0[i]
            mean = jnp.mean(y, axis=-1, keepdims=True)
            var = jnp.mean(jnp.square(y - mean), axis=-1, keepdims=True)
            outs.append((y - mean) * lax.rsqrt(var + LN_X_EPS))
        blocks.append(jnp.concatenate(outs, axis=-1))
    for h in range(HEADS):
        s_ref[h] = state[h]
    yn = jnp.concatenate(blocks, axis=0)
    yn = yn * lnw_ref[...] + lnb_ref[...] + bonus
    o_ref[...] = (yn * g_in).astype(BF16)


def _rwkv_core(ur, mu, w0, a0, k_k, k_a, r_k, w_lora, ln_w, ln_b):
    T = ur.shape[0]
    rows = CHUNK * CHUNKS_PER_STEP
    row = lambda w: pl.BlockSpec((1, w), lambda c: (0, 0))
    return pl.pallas_call(
        _rwkv_core_kernel,
        grid=(T // rows,),
        in_specs=[
            pl.BlockSpec((rows, RWKV_COLS), lambda c: (c, 0)),
            pl.BlockSpec((8, RWKV_COLS), lambda c: (jnp.maximum(c * (rows // 8) - 1, 0), 0)),
            row(RWKV_COLS), row(WIDTH), row(WIDTH), row(WIDTH), row(WIDTH), row(WIDTH),
            pl.BlockSpec((LORA_COLS, 3 * WIDTH), lambda c: (0, 0)),
            row(WIDTH), row(WIDTH),
        ],
        out_specs=pl.BlockSpec((rows, WIDTH), lambda c: (c, 0)),
        out_shape=jax.ShapeDtypeStruct((T, WIDTH), BF16),
        scratch_shapes=[pltpu.VMEM((HEADS, HEAD_DIM, HEAD_DIM), F32)],
        compiler_params=_cparams(("arbitrary",)),
        name="rwkv_core",
    )(ur, ur, mu, w0, a0, k_k, k_a, r_k, w_lora, ln_w, ln_b)


def _t5_bucket(dist):
    n = jnp.maximum(dist, 0)
    max_exact = REL_BUCKETS // 2
    nf = jnp.maximum(n, max_exact).astype(jnp.float32)
    large = max_exact + (jnp.log(nf / max_exact) / math.log(REL_MAX_DISTANCE / max_exact)
                         * (REL_BUCKETS - max_exact)).astype(jnp.int32)
    large = jnp.minimum(large, REL_BUCKETS - 1)
    return jnp.where(n < max_exact, n, large)


def _bucket_tiles():
    i = jnp.arange(MOBA_BLOCK)[None, :]
    j = jnp.arange(MOBA_BLOCK)[:, None]
    d = jnp.arange(N_BIAS_TILES + 1)[:, None, None]
    dist = d * MOBA_BLOCK + i - j
    bucket = _t5_bucket(dist)
    bucket = jnp.where(d == N_BIAS_TILES - 1, REL_BUCKETS - 1, bucket)
    return jnp.where((dist < 0) | (d == N_BIAS_TILES), -1, bucket).astype(jnp.int32)


def _bias_tiles_kernel(idx_ref, rb_ref, o_ref):
    h = pl.program_id(0)

    def tile(d, c):
        idx = idx_ref[d]
        acc = jnp.where(idx < 0, NEG_INF, 0.0)
        for bkt in range(REL_BUCKETS):
            acc = jnp.where(idx == bkt, rb_ref[bkt, h] * LOG2E, acc)
        o_ref[0, d] = acc
        return c

    lax.fori_loop(0, N_BIAS_TILES + 1, tile, 0)


def _bias_tiles(rel_bias):
    idx = _bucket_tiles()
    n = N_BIAS_TILES + 1
    return pl.pallas_call(
        _bias_tiles_kernel,
        grid=(HEADS,),
        in_specs=[
            pl.BlockSpec((n, MOBA_BLOCK, MOBA_BLOCK), lambda h: (0, 0, 0)),
            pl.BlockSpec(memory_space=pltpu.SMEM),
        ],
        out_specs=pl.BlockSpec((1, n, MOBA_BLOCK, MOBA_BLOCK), lambda h: (h, 0, 0, 0)),
        out_shape=jax.ShapeDtypeStruct((HEADS, n, MOBA_BLOCK, MOBA_BLOCK), F32),
        compiler_params=_cparams(("parallel",)),
        name="bias_tiles",
    )(idx, rel_bias)


def _moba_kernel(q_ref, ka_ref, vt_ref, r_ref, bias_ref, o_ref, s_ref, p_ref, acc_ref):
    qb = pl.program_id(1)
    B = MOBA_BLOCK
    q_tr = q_ref[...].astype(F32).T
    q_tr_bf = q_tr.astype(BF16)
    blk = lax.broadcasted_iota(jnp.int32, (HEAD_DIM, B), 0)
    big = jnp.int32(1 << 20)
    q_t, q_own_t = [], []
    for h2 in range(2):
        off = HEAD_DIM * (1 - h2)
        valid = blk < qb
        sc = jnp.dot(r_ref[0, h2, off:off + HEAD_DIM, :], q_tr_bf, preferred_element_type=F32)
        s = jnp.where(valid, sc, NEG_INF)
        sel = jnp.zeros((HEAD_DIM, B), jnp.bool_)
        for _ in range(MOBA_TOP):
            m = jnp.max(s, axis=0, keepdims=True)
            idx = jnp.min(jnp.where(s == m, blk, big), axis=0, keepdims=True)
            pick = blk == idx
            sel = jnp.logical_or(sel, pick)
            s = jnp.where(pick, -jnp.inf, s)
        sel = jnp.logical_and(sel, valid)
        choice = jnp.where(sel, 0.0, NEG_INF)
        q_head = q_tr[h2 * HEAD_DIM:(h2 + 1) * HEAD_DIM]
        parts = [q_head, choice] if h2 == 0 else [choice, q_head]
        own = [q_head, jnp.zeros_like(choice)] if h2 == 0 else [jnp.zeros_like(choice), q_head]
        q_t.append(jnp.concatenate(parts, axis=0).astype(BF16))
        q_own_t.append(jnp.concatenate(own, axis=0).astype(BF16))

    carry = []
    for h2 in range(2):
        k_own = ka_ref[h2, pl.ds(pl.multiple_of(qb * B, B), B), :]
        s0 = jnp.dot(k_own, q_own_t[h2], preferred_element_type=F32) + bias_ref[h2, 0]
        m0 = jnp.max(s0, axis=0, keepdims=True)
        p0 = jnp.exp2(s0 - m0).astype(BF16)
        carry += [m0, jnp.dot(vt_ref[h2, qb], p0, preferred_element_type=F32)]

    n_tiles = (qb + 1) // 2
    last = jnp.maximum(n_tiles - 1, 0)
    nb = vt_ref.shape[1] - PREP_BLOCKS
    for h2 in range(2):
        acc_ref[h2] = carry[2 * h2 + 1]

    @pl.when((pl.program_id(0) == 0) & (qb == 0))
    def _():
        s_ref[...] = jnp.zeros_like(s_ref)
        p_ref[...] = jnp.zeros_like(p_ref)

    def trip(t, w, stats):
        r = 1 - w
        ok = (t >= 1) & (t <= n_tiles)
        d_a = jnp.where(ok, jnp.clip(qb - 2 * (t - 1), 0, N_BIAS_TILES - 1), N_BIAS_TILES)
        d_b = jnp.where(ok, jnp.clip(qb - 2 * (t - 1) - 1, 0, N_BIAS_TILES - 1), N_BIAS_TILES)
        start = pl.multiple_of(jnp.clip(t, 0, last) * (2 * B), 2 * B)
        ok_pv = (t >= 2) & (t <= n_tiles + 1)
        v_a = jnp.where(ok_pv, 2 * (t - 2), nb)
        v_b = jnp.where(ok_pv, 2 * (t - 2) + 1, nb)
        out = []
        for h2 in range(2):
            m_prev, alpha_p = stats[2 * h2], stats[2 * h2 + 1]
            acc_ref[h2] = (acc_ref[h2] * alpha_p
                           + jnp.dot(vt_ref[h2, v_a], p_ref[r, h2, 0:B], preferred_element_type=F32)
                           + jnp.dot(vt_ref[h2, v_b], p_ref[r, h2, B:2 * B],
                                     preferred_element_type=F32))
            s_a = s_ref[r, h2, 0:B] + bias_ref[h2, d_a]
            s_b = s_ref[r, h2, B:2 * B] + bias_ref[h2, d_b]
            m_new = jnp.maximum(m_prev, jnp.max(jnp.maximum(s_a, s_b), axis=0, keepdims=True))
            out += [m_new, jnp.exp2(m_prev - m_new)]
            p_ref[w, h2, 0:B] = jnp.exp2(s_a - m_new).astype(BF16)
            p_ref[w, h2, B:2 * B] = jnp.exp2(s_b - m_new).astype(BF16)
            s_ref[w, h2] = jnp.dot(ka_ref[h2, pl.ds(start, 2 * B), :], q_t[h2],
                                   preferred_element_type=F32)
        return out

    def body(u, stats):
        stats = trip(2 * u, 0, list(stats))
        return tuple(trip(2 * u + 1, 1, stats))

    stats = []
    for h2 in range(2):
        stats += [carry[2 * h2], jnp.ones_like(carry[2 * h2])]
    lax.fori_loop(0, (n_tiles + 3) // 2, body, tuple(stats))
    outs = [acc_ref[h2, 0:HEAD_DIM] / acc_ref[h2, HEAD_DIM:HEAD_DIM + 1] for h2 in range(2)]
    o_ref[...] = jnp.concatenate(outs, axis=0).T.astype(BF16)


def _moba(qkv, k_aug, vt_aug, r_mats, bias_tiles):
    T = qkv.shape[0]
    nb = T // MOBA_BLOCK
    npair = HEADS // 2
    once = pl.Buffered(1)
    return pl.pallas_call(
        _moba_kernel,
        grid=(npair, nb),
        in_specs=[
            pl.BlockSpec((MOBA_BLOCK, LANES), lambda p, qb: (qb, p)),
            pl.BlockSpec((2, T, LANES), lambda p, qb: (p, 0, 0), pipeline_mode=once),
            pl.BlockSpec((2, nb + PREP_BLOCKS, VT_ROWS, MOBA_BLOCK), lambda p, qb: (p, 0, 0, 0),
                         pipeline_mode=once),
            pl.BlockSpec((1, 2, LANES, LANES), lambda p, qb: (p, 0, 0, 0)),
            pl.BlockSpec((2, N_BIAS_TILES + 1, MOBA_BLOCK, MOBA_BLOCK), lambda p, qb: (p, 0, 0, 0),
                         pipeline_mode=once),
        ],
        out_specs=pl.BlockSpec((MOBA_BLOCK, LANES), lambda p, qb: (qb, p)),
        out_shape=jax.ShapeDtypeStruct((T, WIDTH), BF16),
        scratch_shapes=[pltpu.VMEM((2, 2, 2 * MOBA_BLOCK, MOBA_BLOCK), F32),
                        pltpu.VMEM((2, 2, 2 * MOBA_BLOCK, MOBA_BLOCK), BF16),
                        pltpu.VMEM((2, VT_ROWS, MOBA_BLOCK), F32)],
        compiler_params=_cparams(("arbitrary", "arbitrary")),
        name="moba",
    )(qkv, k_aug, vt_aug, r_mats, bias_tiles)


def _moba_prep_kernel(k_ref, v_ref, ka_ref, vt_ref, km_ref):
    B = MOBA_BLOCK
    step = pl.program_id(1)
    extra = step == pl.num_programs(1) - 1
    lane = lax.broadcasted_iota(jnp.int32, (B, LANES), 1)
    ones = jnp.ones((VT_ROWS - HEAD_DIM, B), F32)
    for jj in range(PREP_BLOCKS):
        j = jnp.minimum(step, pl.num_programs(1) - 2) * PREP_BLOCKS + jj
        kblk = k_ref[jj * B:(jj + 1) * B, :]
        km_ref[0, 0, jj:jj + 1, :] = jnp.mean(kblk.astype(F32), axis=0, keepdims=True)
        v_t = v_ref[jj * B:(jj + 1) * B, :].astype(F32).T
        for h2 in range(2):
            off = HEAD_DIM * (1 - h2)
            inr = (lane >= off) & (lane < off + HEAD_DIM)
            hot = jnp.where(lane == off + j, 1.0, 0.0).astype(BF16)
            ka_ref[h2, jj * B:(jj + 1) * B, :] = jnp.where(inr, hot, kblk)
            tile = jnp.concatenate([v_t[h2 * HEAD_DIM:(h2 + 1) * HEAD_DIM], ones], axis=0)
            vt_ref[h2, jj] = jnp.where(extra, 0.0, tile).astype(BF16)


def _moba_prep(qkv):
    T = qkv.shape[0]
    nb = T // MOBA_BLOCK
    npair = HEADS // 2
    kcol = WIDTH // LANES
    steps = nb // PREP_BLOCKS
    k_aug, vt_aug, km = pl.pallas_call(
        _moba_prep_kernel,
        grid=(npair, steps + 1),
        in_specs=[
            pl.BlockSpec((PREP_BLOCKS * MOBA_BLOCK, LANES), lambda p, j: (jnp.minimum(j, steps - 1), kcol + p)),
            pl.BlockSpec((PREP_BLOCKS * MOBA_BLOCK, LANES),
                         lambda p, j: (jnp.minimum(j, steps - 1), 2 * kcol + p)),
        ],
        out_specs=[
            pl.BlockSpec((2, PREP_BLOCKS * MOBA_BLOCK, LANES), lambda p, j: (p, jnp.minimum(j, steps - 1), 0)),
            pl.BlockSpec((2, PREP_BLOCKS, VT_ROWS, MOBA_BLOCK), lambda p, j: (p, j, 0, 0)),
            pl.BlockSpec((1, 1, PREP_BLOCKS, LANES), lambda p, j: (p, j, 0, 0)),
        ],
        out_shape=[
            jax.ShapeDtypeStruct((HEADS, T, LANES), BF16),
            jax.ShapeDtypeStruct((HEADS, nb + PREP_BLOCKS, VT_ROWS, MOBA_BLOCK), BF16),
            jax.ShapeDtypeStruct((npair, steps + 1, PREP_BLOCKS, LANES), F32),
        ],
        compiler_params=_cparams(("arbitrary", "arbitrary")),
        name="moba_prep",
    )(qkv, qkv)
    kmean = km[:, :steps].reshape(npair, nb, LANES).transpose(1, 0, 2).reshape(nb, WIDTH)
    return k_aug, vt_aug, kmean


def _score_mats(kmean):
    nb = kmean.shape[0]
    km = kmean.reshape(nb, HEADS, HEAD_DIM).transpose(1, 2, 0)
    km = jnp.pad(km, ((0, 0), (0, 0), (0, HEAD_DIM - nb)))
    z = jnp.zeros((HEADS // 2, HEAD_DIM, HEAD_DIM), F32)
    even = jnp.concatenate([jnp.concatenate([z, km[0::2]], axis=2),
                            jnp.concatenate([z, z], axis=2)], axis=1)
    odd = jnp.concatenate([jnp.concatenate([z, z], axis=2),
                           jnp.concatenate([km[1::2], z], axis=2)], axis=1)
    return jnp.swapaxes(jnp.stack([even, odd], axis=1), -1, -2).astype(BF16)


def _memkv_kernel(m_ref, g_ref, w_ref, o_ref):
    h = _rms(m_ref[...], g_ref[...]).astype(BF16)
    o_ref[...] = jnp.dot(h, w_ref[...], preferred_element_type=F32).astype(BF16)


def _memkv(mem2d, g, w_bf):
    return pl.pallas_call(
        _memkv_kernel,
        out_shape=jax.ShapeDtypeStruct((N_MEM, 2 * XATTN_WIDTH), BF16),
        compiler_params=pltpu.CompilerParams(vmem_limit_bytes=VMEM_LIMIT),
        name="memkv",
    )(mem2d, g, w_bf)


def _merge_kernel(x_ref, oa_ref, ob_ref, gt_ref, woa_ref, wob_ref, wout_ref, nx_ref, wq_ref,
                  kv_ref, wo_ref, nf_ref, wr_ref, x2_ref, h3_ref, rt_ref, cnt_ref, run_ref):
    pa = jnp.dot(oa_ref[...], woa_ref[...], preferred_element_type=F32)
    pb = jnp.dot(ob_ref[...], wob_ref[...], preferred_element_type=F32)
    ga = gt_ref[:, 0:D_MODEL].astype(F32)
    gb = gt_ref[:, D_MODEL:2 * D_MODEL].astype(F32)
    merged = _sigmoid(ga) * pa + _sigmoid(gb) * pb
    x1 = x_ref[...] + jnp.dot(merged.astype(BF16), wout_ref[...], preferred_element_type=F32)
    h2 = _rms(x1, nx_ref[...]).astype(BF16)
    q = jnp.dot(h2, wq_ref[...], preferred_element_type=F32).astype(BF16)
    heads = []
    for h in range(XATTN_HEADS):
        sl = slice(h * XATTN_HEAD_DIM, (h + 1) * XATTN_HEAD_DIM)
        km = kv_ref[:, sl]
        vm = kv_ref[:, XATTN_WIDTH + h * XATTN_HEAD_DIM:XATTN_WIDTH + (h + 1) * XATTN_HEAD_DIM]
        lg = lax.dot_general(q[:, sl], km, (((1,), (1,)), ((), ())),
                             preferred_element_type=F32) * (XATTN_HEAD_DIM ** -0.5)
        e = jnp.exp(lg - jnp.max(lg, axis=-1, keepdims=True))
        p = e / jnp.sum(e, axis=-1, keepdims=True)
        heads.append(jnp.dot(p.astype(BF16), vm, preferred_element_type=F32))
    o = jnp.concatenate(heads, axis=-1).astype(BF16)
    x2 = x1 + jnp.dot(o, wo_ref[...], preferred_element_type=F32)
    x2_ref[...] = x2
    h3 = _rms(x2, nf_ref[...])
    _store_token_tiles(h3_ref, h3)
    lg = _bdot(h3, wr_ref[...])
    lane = lax.broadcasted_iota(jnp.int32, lg.shape, 1)
    big = jnp.int32(1 << 20)
    is_g = lane < N_GROUPS
    gl = jnp.where(is_g, lg, -jnp.inf)
    gmax = jnp.max(gl, axis=-1, keepdims=True)
    gsel = jnp.min(jnp.where(gl == gmax, lane, big), axis=-1, keepdims=True)
    p_top = 1.0 / jnp.sum(jnp.where(is_g, jnp.exp(gl - gmax), 0.0), axis=-1, keepdims=True)
    e_id = lane - N_GROUPS
    in_grp = (e_id >= gsel * EXPERTS_PER_GROUP) & (e_id < (gsel + 1) * EXPERTS_PER_GROUP)
    el = jnp.where(in_grp, lg, -jnp.inf)
    v1 = jnp.max(el, axis=-1, keepdims=True)
    i1 = jnp.min(jnp.where(el == v1, lane, big), axis=-1, keepdims=True)
    el2 = jnp.where(lane == i1, -jnp.inf, el)
    v2 = jnp.max(el2, axis=-1, keepdims=True)
    i2 = jnp.min(jnp.where(el2 == v2, lane, big), axis=-1, keepdims=True)
    e21 = jnp.exp(v2 - v1)
    g1 = p_top / (1.0 + e21)
    g2 = p_top * e21 / (1.0 + e21)
    step = pl.program_id(0)

    @pl.when(step == 0)
    def _():
        run_ref[...] = jnp.zeros_like(run_ref)

    tm = lg.shape[0]
    oh1 = jnp.where(lane == i1, 1.0, 0.0)
    oh2 = jnp.where(lane == i2, 1.0, 0.0)
    both = oh1 + oh2
    ri = lax.broadcasted_iota(jnp.int32, (tm, tm), 0)
    ci = lax.broadcasted_iota(jnp.int32, (tm, tm), 1)
    before = _bdot(jnp.where(ri > ci, 1.0, 0.0), both) + run_ref[0:1, :]
    r1 = jnp.sum(before * oh1, axis=-1, keepdims=True)
    r2 = jnp.sum(before * oh2, axis=-1, keepdims=True)
    run_ref[...] = run_ref[...] + jnp.sum(both, axis=0, keepdims=True)
    cnt_ref[...] = run_ref[...]
    vals = [(i1 - N_GROUPS).astype(F32), (i2 - N_GROUPS).astype(F32), g1, g2, r1, r2]
    rt = jnp.zeros_like(lg)
    for k, v in enumerate(vals):
        rt = jnp.where(lane == k, v, rt)
    rt_ref[...] = rt


def _merge(x2d, o_a, o_b, gates, w_oa, w_ob, w_out, n_x, w_q, kv, w_o, n_f, w_r):
    T = x2d.shape[0]
    tm = TOKEN_TILE
    full = lambda a: pl.BlockSpec(a.shape, lambda i: (0,) * a.ndim, pipeline_mode=pl.Buffered(1))
    tile = lambda w: pl.BlockSpec((tm, w), lambda i: (i, 0))
    return pl.pallas_call(
        _merge_kernel,
        grid=(T // tm,),
        in_specs=[tile(D_MODEL), tile(WIDTH), tile(WIDTH), tile(GATE_COLS), full(w_oa), full(w_ob),
                  full(w_out), full(n_x), full(w_q), full(kv), full(w_o), full(n_f), full(w_r)],
        out_specs=[tile(D_MODEL), pl.BlockSpec((tm * ROW_TILES, LANES), lambda i: (i, 0)), tile(LANES),
                   pl.BlockSpec((8, LANES), lambda i: (0, 0))],
        out_shape=[jax.ShapeDtypeStruct((T, D_MODEL), F32),
                   jax.ShapeDtypeStruct((T * ROW_TILES, LANES), U32),
                   jax.ShapeDtypeStruct((T, LANES), F32),
                   jax.ShapeDtypeStruct((8, LANES), F32)],
        scratch_shapes=[pltpu.VMEM((8, LANES), F32)],
        compiler_params=_cparams(("arbitrary",)),
        name="merge",
    )(x2d, o_a, o_b, gates, w_oa, w_ob, w_out, n_x, w_q, kv, w_o, n_f, w_r)


def _dispatch_kernel(pos_ref, zs_ref, zv_ref, nu_ref, h_ref, xs_hbm, zero_ref, sem):
    i = pl.program_id(0)
    tm = h_ref.shape[0] // ROW_TILES

    @pl.when(i == 0)
    def _():
        zero_ref[...] = jnp.zeros_like(zero_ref)

        def clear(e):
            dst = xs_hbm.at[pl.ds(pl.multiple_of(zs_ref[e] * ROW_TILES, ROW_TILES),
                                  MOE_BLOCK * ROW_TILES), :]
            return pltpu.make_async_copy(zero_ref, dst, sem)

        for e in range(N_EXPERTS):
            @pl.when(zv_ref[e] > 0)
            def _():
                clear(e).start()

        for e in range(N_EXPERTS):
            @pl.when(zv_ref[e] > 0)
            def _():
                clear(e).wait()

        def tail(b):
            dst = xs_hbm.at[pl.ds(pl.multiple_of(b * (MOE_BLOCK * ROW_TILES), MOE_BLOCK * ROW_TILES),
                                  MOE_BLOCK * ROW_TILES), :]
            return pltpu.make_async_copy(zero_ref, dst, sem)

        n_blocks = xs_hbm.shape[0] // (MOE_BLOCK * ROW_TILES)

        def start_tail(b, c):
            tail(b).start()
            return c

        def wait_tail(b, c):
            tail(b).wait()
            return c

        lax.fori_loop(nu_ref[0], n_blocks, start_tail, 0)
        lax.fori_loop(nu_ref[0], n_blocks, wait_tail, 0)

    def row_copy(r, k):
        p = pos_ref[2 * (i * tm + r) + k]
        dst = xs_hbm.at[pl.ds(pl.multiple_of(p * ROW_TILES, ROW_TILES), ROW_TILES), :]
        return pltpu.make_async_copy(h_ref.at[pl.ds(r * ROW_TILES, ROW_TILES), :], dst, sem)

    for r in range(tm):
        row_copy(r, 0).start(priority=0)
        row_copy(r, 1).start(priority=1)
    for _ in range(tm):
        row_copy(0, 0).wait()
        row_copy(0, 1).wait()


def _dispatch(pos, zstart, zvalid, n_used, h3t, cap):
    T = h3t.shape[0] // ROW_TILES
    tm = ROW_DMA_TILE
    grid_spec = pltpu.PrefetchScalarGridSpec(
        num_scalar_prefetch=4,
        grid=(T // tm,),
        in_specs=[pl.BlockSpec((tm * ROW_TILES, LANES), lambda i, ps, zs, zv, nu: (i, 0))],
        out_specs=pl.BlockSpec(memory_space=pl.ANY),
        scratch_shapes=[pltpu.VMEM((MOE_BLOCK * ROW_TILES, LANES), U32), pltpu.SemaphoreType.DMA],
    )
    return pl.pallas_call(
        _dispatch_kernel,
        grid_spec=grid_spec,
        out_shape=jax.ShapeDtypeStruct((cap * ROW_TILES, LANES), U32),
        compiler_params=_cparams(("arbitrary",)),
        name="dispatch",
    )(pos, zstart, zvalid, n_used, h3t)


def _moe_kernel(be_ref, nu_ref, x_ref, wg_ref, wu_ref, wd_ref, o_ref):
    i = pl.program_id(0)

    @pl.when(i < nu_ref[0])
    def _():
        xb = jnp.concatenate([x.astype(BF16) for x in _load_token_tiles(x_ref, MOE_BLOCK)], axis=-1)
        gate = _bdot(xb, wg_ref[0])
        up = _bdot(xb, wu_ref[0])
        hid = gate * _sigmoid(gate) * up
        _store_token_tiles(o_ref, _bdot(hid, wd_ref[0]))

    @pl.when(i >= nu_ref[0])
    def _():
        o_ref[...] = jnp.zeros_like(o_ref)


def _moe(block_expert, n_used, xs, wg, wu, wd):
    n_blocks = xs.shape[0] // (MOE_BLOCK * ROW_TILES)
    used = lambda i, nu: jnp.minimum(i, jnp.maximum(nu[0] - 1, 0))
    grid_spec = pltpu.PrefetchScalarGridSpec(
        num_scalar_prefetch=2,
        grid=(n_blocks,),
        in_specs=[
            pl.BlockSpec((MOE_BLOCK * ROW_TILES, LANES), lambda i, be, nu: (used(i, nu), 0)),
            pl.BlockSpec((1, D_MODEL, D_EXPERT), lambda i, be, nu: (be[i], 0, 0)),
            pl.BlockSpec((1, D_MODEL, D_EXPERT), lambda i, be, nu: (be[i], 0, 0)),
            pl.BlockSpec((1, D_EXPERT, D_MODEL), lambda i, be, nu: (be[i], 0, 0)),
        ],
        out_specs=pl.BlockSpec((MOE_BLOCK * ROW_TILES, LANES), lambda i, be, nu: (i, 0)),
    )
    return pl.pallas_call(
        _moe_kernel,
        grid_spec=grid_spec,
        out_shape=jax.ShapeDtypeStruct(xs.shape, U32),
        compiler_params=_cparams(("arbitrary",)),
        name="moe",
    )(block_expert, n_used, xs, wg, wu, wd)


def _moe_plan(expert, rank, counts, n_tok):
    n_assign = n_tok * 2
    e_flat = expert.reshape(n_assign)
    rank = rank.reshape(n_assign)
    padded = (counts + MOE_BLOCK - 1) // MOE_BLOCK * MOE_BLOCK
    p_end = jnp.cumsum(padded)
    p_start = p_end - padded
    pos = (p_start[e_flat] + rank).astype(jnp.int32)
    n_blocks = (n_assign + N_EXPERTS * (MOE_BLOCK - 1) + MOE_BLOCK - 1) // MOE_BLOCK
    cap = n_blocks * MOE_BLOCK
    block_start = jnp.arange(n_blocks, dtype=jnp.int32) * MOE_BLOCK
    block_expert = jnp.minimum(jnp.sum(p_end[None, :] <= block_start[:, None], axis=1),
                               N_EXPERTS - 1).astype(jnp.int32)
    n_used = (p_end[-1] // MOE_BLOCK).astype(jnp.int32).reshape(1)
    last_block = jnp.maximum(p_end - MOE_BLOCK, 0).astype(jnp.int32)
    nonempty = (padded > 0).astype(jnp.int32)
    return pos, cap, block_expert, n_used, last_block, nonempty


def _final_kernel(pos_ref, x_ref, rt_ref, y_hbm, g_ref, o_ref, y1_ref, y2_ref, sem):
    i = pl.program_id(0)
    n = pl.num_programs(0)
    tm = x_ref.shape[0]
    slot = i % 2

    def copies(tile, r, s):
        p1 = pos_ref[2 * (tile * tm + r)]
        p2 = pos_ref[2 * (tile * tm + r) + 1]
        src = lambda p: y_hbm.at[pl.ds(pl.multiple_of(p * ROW_TILES, ROW_TILES), ROW_TILES), :]
        dst = lambda ref: ref.at[s, pl.ds(r * ROW_TILES, ROW_TILES), :]
        return (pltpu.make_async_copy(src(p1), dst(y1_ref), sem.at[s]),
                pltpu.make_async_copy(src(p2), dst(y2_ref), sem.at[s]))

    def gather(tile, s):
        for r in range(tm):
            c1, c2 = copies(tile, r, s)
            c1.start(priority=0)
            c2.start(priority=1)

    def wait_rows(s):
        for _ in range(tm):
            c1, c2 = copies(0, 0, s)
            c1.wait()
            c2.wait()

    @pl.when(i == 0)
    def _():
        gather(0, 0)

    wait_rows(slot)
    gather(jnp.minimum(i + 1, n - 1), 1 - slot)
    rt = rt_ref[...]
    g1 = rt[:, 2:3]
    g2 = rt[:, 3:4]
    y1 = jnp.concatenate(_load_token_tiles(y1_ref.at[slot], tm), axis=-1)
    y2 = jnp.concatenate(_load_token_tiles(y2_ref.at[slot], tm), axis=-1)
    x3 = x_ref[...] + y1 * g1 + y2 * g2
    o_ref[...] = _rms(x3, g_ref[...])

    @pl.when(i == n - 1)
    def _():
        wait_rows(1 - slot)


def _final(pos, x2, rt, y_sorted, g):
    T = x2.shape[0]
    tm = ROW_DMA_TILE
    grid_spec = pltpu.PrefetchScalarGridSpec(
        num_scalar_prefetch=1,
        grid=(T // tm,),
        in_specs=[
            pl.BlockSpec((tm, D_MODEL), lambda i, ps: (i, 0)),
            pl.BlockSpec((tm, LANES), lambda i, ps: (i, 0)),
            pl.BlockSpec(memory_space=pl.ANY),
            pl.BlockSpec((1, D_MODEL), lambda i, ps: (0, 0)),
        ],
        out_specs=pl.BlockSpec((tm, D_MODEL), lambda i, ps: (i, 0)),
        scratch_shapes=[pltpu.VMEM((2, tm * ROW_TILES, LANES), U32),
                        pltpu.VMEM((2, tm * ROW_TILES, LANES), U32),
                        pltpu.SemaphoreType.DMA((2,))],
    )
    return pl.pallas_call(
        _final_kernel,
        grid_spec=grid_spec,
        out_shape=jax.ShapeDtypeStruct((T, D_MODEL), F32),
        compiler_params=_cparams(("arbitrary",)),
        name="final",
    )(pos, x2, rt, y_sorted, g)


def _lora_weight(decay_up, iclr_up, gate_up):
    w = jnp.zeros((LORA_COLS, 3 * WIDTH), F32)
    w = w.at[0:DECAY_LORA, 0:WIDTH].set(decay_up)
    w = w.at[DECAY_LORA:DECAY_LORA + ICLR_LORA, WIDTH:2 * WIDTH].set(iclr_up)
    w = w.at[DECAY_LORA + ICLR_LORA:, 2 * WIDTH:].set(gate_up)
    return w


def _router_weight(w_group, w_expert):
    w = jnp.zeros((D_MODEL, LANES), F32)
    w = w.at[:, 0:N_GROUPS].set(w_group)
    return w.at[:, N_GROUPS:N_GROUPS + N_EXPERTS].set(w_expert)


def kernel(x, mem, rel_bias, mem_norm, norm_mix, w_in, tshift_mu, decay_w0, decay_up, iclr_a0,
           iclr_up, gate_up, k_k, k_a, r_k, ln_x_w, ln_x_b, w_o_rwkv, w_o_moba, w_out,
           norm_xattn, w_q_x, w_kv_x, w_o_x, norm_ffn, w_router_group, w_router_expert,
           w_exp_gate, w_exp_up, w_exp_down, norm_final):
    B, T, D = x.shape
    assert B == 1 and D == D_MODEL and T % TOKEN_TILE == 0 and T // MOBA_BLOCK <= HEAD_DIM
    assert norm_mix.shape[0] == 1 and (T // MOBA_BLOCK) % PREP_BLOCKS == 0
    row = lambda a: a.reshape(1, -1)
    x2d = x.reshape(T, D)
    kv = _memkv(mem.reshape(N_MEM, D), row(mem_norm), w_kv_x[0].astype(BF16))

    ur, qkv, gates = _inproj(x2d, row(norm_mix[0]), w_in[0].astype(BF16))
    o_a = _rwkv_core(ur, row(tshift_mu[0]), row(decay_w0[0]), row(iclr_a0[0]), row(k_k[0]),
                     row(k_a[0]), row(r_k[0]),
                     _lora_weight(decay_up[0], iclr_up[0], gate_up[0]).astype(BF16),
                     row(ln_x_w[0]), row(ln_x_b[0]))

    k_aug, vt_aug, kmean = _moba_prep(qkv)
    o_b = _moba(qkv, k_aug, vt_aug, _score_mats(kmean), _bias_tiles(rel_bias))

    x2, h3, rt, counts = _merge(
        x2d, o_a, o_b, gates, w_o_rwkv[0].astype(BF16), w_o_moba[0].astype(BF16),
        w_out[0].astype(BF16), row(norm_xattn[0]), w_q_x[0].astype(BF16), kv,
        w_o_x[0].astype(BF16), row(norm_ffn[0]),
        _router_weight(w_router_group[0], w_router_expert[0]).astype(BF16))

    expert = rt[:, 0:2].astype(jnp.int32)
    rank = rt[:, 4:6].astype(jnp.int32)
    pos, cap, block_expert, n_used, last_block, nonempty = _moe_plan(
        expert, rank, counts[0, N_GROUPS:N_GROUPS + N_EXPERTS].astype(jnp.int32), T)
    x_sorted = _dispatch(pos, last_block, nonempty, n_used, h3, cap)
    y_sorted = _moe(block_expert, n_used, x_sorted, w_exp_gate[0], w_exp_up[0], w_exp_down[0])
    out = _final(pos, x2, rt, y_sorted, row(norm_final))
    return out.reshape(B, T, D)
```

```python
import math

import jax
import jax.numpy as jnp
from jax import lax
from jax.experimental import pallas as pl
from jax.experimental.pallas import tpu as pltpu

F32 = jnp.float32
BF16 = jnp.bfloat16

D_MODEL = 1024
N_MEM = 256
NORM_EPS = 1e-6
NEG_INF = -1e30

HEADS = 8
HEAD_DIM = 64
WIDTH = HEADS * HEAD_DIM
DECAY_LORA = 64
ICLR_LORA = 64
GATE_LORA = 128
LORA_COLS = DECAY_LORA + ICLR_LORA + GATE_LORA
RWKV_COLS = 3 * WIDTH + LORA_COLS
QKV_COLS = 3 * WIDTH
GATE_COLS = 2 * D_MODEL
IN_COLS = RWKV_COLS + QKV_COLS + GATE_COLS
LN_X_EPS = 64e-5
KK_EPS = 1e-12

MOBA_BLOCK = 256
MOBA_TOP = 3
REL_BUCKETS = 32
REL_MAX_DISTANCE = 4096
N_BIAS_TILES = 14

XATTN_HEADS = 4
XATTN_HEAD_DIM = 128
XATTN_WIDTH = XATTN_HEADS * XATTN_HEAD_DIM

N_GROUPS = 4
EXPERTS_PER_GROUP = 8
N_EXPERTS = N_GROUPS * EXPERTS_PER_GROUP
D_EXPERT = 512
MOE_BLOCK = 256

TOKEN_TILE = 512
ROW_DMA_TILE = 256
PREP_BLOCKS = 4
VT_ROWS = HEAD_DIM + 16
CHUNK = 64
CHUNKS_PER_STEP = 4
LANES = 128
VMEM_LIMIT = 48 * 1024 * 1024

LOG2E = math.log2(math.e)


def _cparams(sem):
    return pltpu.CompilerParams(dimension_semantics=sem, vmem_limit_bytes=VMEM_LIMIT)


def _bdot(a, b):
    return jnp.dot(a.astype(BF16), b.astype(BF16), preferred_element_type=F32)


def _bdot_nt(a, b):
    return lax.dot_general(a.astype(BF16), b.astype(BF16), (((1,), (1,)), ((), ())),
                           preferred_element_type=F32)


def _bdot_tn(a, b):
    return lax.dot_general(a.astype(BF16), b.astype(BF16), (((0,), (0,)), ((), ())),
                           preferred_element_type=F32)


def _rms(x, g):
    return x * lax.rsqrt(jnp.mean(x * x, axis=-1, keepdims=True) + NORM_EPS) * g


def _sigmoid(x):
    return 1.0 / (1.0 + jnp.exp(-x))


ROW_TILES = D_MODEL // (2 * LANES)
U32 = jnp.uint32
HI_MASK = 0xFFFF0000


def _bits(x):
    return pltpu.bitcast(x.astype(BF16).astype(F32), U32)


def _store_token_tiles(ref, x):
    n = x.shape[0]
    for s in range(ROW_TILES):
        lo = x[:, 2 * s * LANES:(2 * s + 1) * LANES]
        hi = x[:, (2 * s + 1) * LANES:(2 * s + 2) * LANES]
        word = (_bits(hi) & U32(HI_MASK)) | lax.shift_right_logical(_bits(lo), U32(16))
        ref[pl.ds(s, n, stride=ROW_TILES), :] = word


def _load_token_tiles(ref, n):
    out = []
    for s in range(ROW_TILES):
        word = ref[pl.ds(s, n, stride=ROW_TILES), :]
        out.append(pltpu.bitcast(lax.shift_left(word, U32(16)), F32))
        out.append(pltpu.bitcast(word & U32(HI_MASK), F32))
    return out


def _inproj_kernel(x_ref, g_ref, w_ref, ur_ref, qkv_ref, gate_ref):
    h = _rms(x_ref[...], g_ref[...]).astype(BF16)
    step = 256
    for c0 in range(0, RWKV_COLS, step):
        ur_ref[:, c0:c0 + step] = jnp.dot(h, w_ref[:, c0:c0 + step], preferred_element_type=F32)
    for c0 in range(0, QKV_COLS, step):
        o = jnp.dot(h, w_ref[:, RWKV_COLS + c0:RWKV_COLS + c0 + step], preferred_element_type=F32)
        if c0 < WIDTH:
            o = o * (HEAD_DIM ** -0.5 * LOG2E)
        qkv_ref[:, c0:c0 + step] = o.astype(BF16)
    base = RWKV_COLS + QKV_COLS
    for c0 in range(0, GATE_COLS, step):
        o = jnp.dot(h, w_ref[:, base + c0:base + c0 + step], preferred_element_type=F32)
        gate_ref[:, c0:c0 + step] = o.astype(BF16)


def _inproj(x2d, g, w_bf):
    T = x2d.shape[0]
    tm = TOKEN_TILE
    return pl.pallas_call(
        _inproj_kernel,
        grid=(T // tm,),
        in_specs=[
            pl.BlockSpec((tm, D_MODEL), lambda i: (i, 0)),
            pl.BlockSpec((1, D_MODEL), lambda i: (0, 0)),
            pl.BlockSpec((D_MODEL, IN_COLS), lambda i: (0, 0), pipeline_mode=pl.Buffered(1)),
        ],
        out_specs=[
            pl.BlockSpec((tm, RWKV_COLS), lambda i: (i, 0)),
            pl.BlockSpec((tm, QKV_COLS), lambda i: (i, 0)),
            pl.BlockSpec((tm, GATE_COLS), lambda i: (i, 0)),
        ],
        out_shape=[
            jax.ShapeDtypeStruct((T, RWKV_COLS), F32),
            jax.ShapeDtypeStruct((T, QKV_COLS), BF16),
            jax.ShapeDtypeStruct((T, GATE_COLS), BF16),
        ],
        compiler_params=_cparams(("parallel",)),
        name="inproj",
    )(x2d, g, w_bf)


def _head_ones():
    r = lax.broadcasted_iota(jnp.int32, (LANES, LANES), 0) // HEAD_DIM
    c = lax.broadcasted_iota(jnp.int32, (LANES, LANES), 1) // HEAD_DIM
    return jnp.where(r == c, 1.0, 0.0).astype(BF16)


def _head_sum(x, ones):
    parts = [_bdot(x[:, c:c + LANES], ones) for c in range(0, WIDTH, LANES)]
    return jnp.concatenate(parts, axis=-1)


def _rwkv_prep_values(u_ref, up_ref, mu_ref, w0_ref, a0_ref, kk_ref, ka_ref, rk_ref, wl_ref):
    i = pl.program_id(0)
    u = u_ref[...]
    tm = u.shape[0]
    prev_last = up_ref[7:8, :] * jnp.where(i > 0, 1.0, 0.0)
    rolled = pltpu.roll(u, 1, 0)
    row = lax.broadcasted_iota(jnp.int32, u.shape, 0)
    u_prev = jnp.where(row == 0, prev_last, rolled)
    u = u + mu_ref[...] * (u_prev - u)
    r = u[:, 0:WIDTH]
    k = u[:, WIDTH:2 * WIDTH]
    v = u[:, 2 * WIDTH:3 * WIDTH]
    lo = u[:, 3 * WIDTH:3 * WIDTH + LORA_COLS]
    lane = lax.broadcasted_iota(jnp.int32, lo.shape, 1)
    act = jnp.where(lane < DECAY_LORA, jnp.tanh(lo),
                    jnp.where(lane < DECAY_LORA + ICLR_LORA, lo, _sigmoid(lo)))
    up = _bdot(act, wl_ref[...])
    z = -(w0_ref[...] + up[:, 0:WIDTH])
    softplus = jnp.maximum(z, 0.0) + jnp.log(1.0 + jnp.exp(-jnp.abs(z)))
    w_log = -softplus - 0.5
    lw = -jnp.exp(w_log)
    iclr = _sigmoid(a0_ref[...] + up[:, WIDTH:2 * WIDTH])
    g = up[:, 2 * WIDTH:3 * WIDTH]
    ones = _head_ones()
    kk = k * kk_ref[...]
    kk = kk * lax.rsqrt(_head_sum(kk * kk, ones) + KK_EPS)
    k2 = k * (1.0 + (iclr - 1.0) * ka_ref[...])
    bonus = _head_sum(r * k2 * rk_ref[...], ones) * v
    return r, lw, k2, v, -kk, kk * iclr, g, bonus


def _rwkv_core_kernel(u_ref, up_ref, mu_ref, w0_ref, a0_ref, kk_ref, ka_ref, rk_ref, wl_ref,
                      lnw_ref, lnb_ref, o_ref, s_ref):
    c = pl.program_id(0)

    @pl.when(c == 0)
    def _():
        s_ref[...] = jnp.zeros_like(s_ref)

    C = CHUNK
    G = CHUNKS_PER_STEP
    ri = lax.broadcasted_iota(jnp.int32, (C, C), 0)
    ci = lax.broadcasted_iota(jnp.int32, (C, C), 1)
    rg = lax.broadcasted_iota(jnp.int32, (G * C, G * C), 0)
    cg = lax.broadcasted_iota(jnp.int32, (G * C, G * C), 1)
    r_in, lw, k_in, v_all, a_in, b_in, g_in, bonus = _rwkv_prep_values(
        u_ref, up_ref, mu_ref, w0_ref, a0_ref, kk_ref, ka_ref, rk_ref, wl_ref)
    tri = jnp.where((rg >= cg) & (rg // C == cg // C), 1.0, 0.0).astype(BF16)
    lw_hi = lw.astype(BF16)
    lw_r1 = lw - lw_hi.astype(F32)
    lw_mid = lw_r1.astype(BF16)
    lw_lo = (lw_r1 - lw_mid.astype(F32)).astype(BF16)
    cum = (jnp.dot(tri, lw_hi, preferred_element_type=F32)
           + jnp.dot(tri, lw_mid, preferred_element_type=F32)
           + jnp.dot(tri, lw_lo, preferred_element_type=F32))
    lam = jnp.exp(cum)
    inv_lam = jnp.exp(-cum)
    r_t = r_in * lam
    a_t = a_in * jnp.exp(cum - lw)
    b_t = b_in * inv_lam
    k_t = k_in * inv_lam
    tots = [cum[g * C + C - 1:g * C + C, :] for g in range(G)]
    rowg = lax.broadcasted_iota(jnp.int32, cum.shape, 0) // C
    tot = tots[G - 1]
    for g in range(G - 2, -1, -1):
        tot = jnp.where(rowg == g, tots[g], tot)
    rest = jnp.exp(tot - cum)
    b_h = b_in * rest
    k_h = k_in * rest
    lam_c = [jnp.exp(t) for t in tots]
    eye = jnp.where(ri == ci, 1.0, 0.0)
    H = range(G * HEADS)
    sls = [slice(h * HEAD_DIM, (h + 1) * HEAD_DIM) for h in range(HEADS)]
    bf = lambda x: x.astype(BF16)
    part = lambda x, i: x[(i // HEADS) * C:(i // HEADS + 1) * C, sls[i % HEADS]]
    at = [bf(part(a_t, i)) for i in H]
    rt = [part(r_t, i) for i in H]
    bt = [bf(part(b_t, i)) for i in H]
    kt = [bf(part(k_t, i)) for i in H]
    bh = [bf(part(b_h, i)) for i in H]
    kh = [bf(part(k_h, i)) for i in H]
    vv = [bf(part(v_all, i)) for i in H]
    ci2 = lax.broadcasted_iota(jnp.int32, (C, 2 * C), 1)
    ri2 = lax.broadcasted_iota(jnp.int32, (C, 2 * C), 0)
    cm2 = jnp.where(ci2 >= C, ci2 - C, ci2)
    left = ci2 < C
    ar = [jnp.concatenate([at[h], bf(rt[h])], axis=0) for h in H]
    bk = [jnp.concatenate([bt[h], kt[h]], axis=0) for h in H]
    g = [_bdot_nt(ar[h], bk[h]) for h in H]
    top = [jnp.where(ri2 > cm2, g[h][0:C], 0.0) for h in H]
    bot = [bf(jnp.where(ri2 >= cm2, g[h][C:2 * C], 0.0)) for h in H]
    a_ab = [top[h][:, 0:C] for h in H]
    akv = [_bdot(top[h][:, C:2 * C], vv[h]) for h in H]
    z = [jnp.concatenate([a_ab[h], eye], axis=1) for h in H]
    for _ in range(6):
        z = [_bdot(z[h][:, 0:C], z[h]) + jnp.where(left, 0.0, z[h]) for h in H]
    tinv = [bf(z[h][:, C:2 * C]) for h in H]
    wu = [_bdot(tinv[h], jnp.concatenate([at[h], bf(akv[h])], axis=1)) for h in H]
    w_m = [bf(wu[h][:, 0:C]) for h in H]
    uv = [jnp.concatenate([bf(wu[h][:, C:2 * C]), vv[h]], axis=0) for h in H]
    q_m = [rt[h] + _bdot(bot[h][:, 0:C], w_m[h]) for h in H]
    y0 = [_bdot(bot[h], uv[h]) for h in H]
    m_k = [_bdot_tn(w_m[h], bh[h]) for h in H]
    n0 = [_bdot_tn(uv[h], jnp.concatenate([bh[h], kh[h]], axis=0)) for h in H]
    state = [s_ref[h] for h in range(HEADS)]
    blocks = []
    for g in range(G):
        outs = []
        for h in range(HEADS):
            i = g * HEADS + h
            y = _bdot_nt(q_m[i], state[h]) + y0[i]
            state[h] = state[h] * lam_c[g][:, sls[h]] + _bdot(state[h], m_k[i]) + n0[i]
            mean = jnp.mean(y, axis=-1, keepdims=True)
            var = jnp.mean(jnp.square(y - mean), axis=-1, keepdims=True)
            outs.append((y - mean) * lax.rsqrt(var + LN_X_EPS))
        blocks.append(jnp.concatenate(outs, axis=-1))
    for h in range(HEADS):
        s_ref[h] = state[h]
    yn = jnp.concatenate(blocks, axis=0)
    yn = yn * lnw_ref[...] + lnb_ref[...] + bonus
    o_ref[...] = (yn * g_in).astype(BF16)


def _rwkv_core(ur, mu, w0, a0, k_k, k_a, r_k, w_lora, ln_w, ln_b):
    T = ur.shape[0]
    rows = CHUNK * CHUNKS_PER_STEP
    row = lambda w: pl.BlockSpec((1, w), lambda c: (0, 0))
    return pl.pallas_call(
        _rwkv_core_kernel,
        grid=(T // rows,),
        in_specs=[
            pl.BlockSpec((rows, RWKV_COLS), lambda c: (c, 0)),
            pl.BlockSpec((8, RWKV_COLS), lambda c: (jnp.maximum(c * (rows // 8) - 1, 0), 0)),
            row(RWKV_COLS), row(WIDTH), row(WIDTH), row(WIDTH), row(WIDTH), row(WIDTH),
            pl.BlockSpec((LORA_COLS, 3 * WIDTH), lambda c: (0, 0)),
            row(WIDTH), row(WIDTH),
        ],
        out_specs=pl.BlockSpec((rows, WIDTH), lambda c: (c, 0)),
        out_shape=jax.ShapeDtypeStruct((T, WIDTH), BF16),
        scratch_shapes=[pltpu.VMEM((HEADS, HEAD_DIM, HEAD_DIM), F32)],
        compiler_params=_cparams(("arbitrary",)),
        name="rwkv_core",
    )(ur, ur, mu, w0, a0, k_k, k_a, r_k, w_lora, ln_w, ln_b)


def _t5_bucket(dist):
    n = jnp.maximum(dist, 0)
    max_exact = REL_BUCKETS // 2
    nf = jnp.maximum(n, max_exact).astype(jnp.float32)
    large = max_exact + (jnp.log(nf / max_exact) / math.log(REL_MAX_DISTANCE / max_exact)
                         * (REL_BUCKETS - max_exact)).astype(jnp.int32)
    large = jnp.minimum(large, REL_BUCKETS - 1)
    return jnp.where(n < max_exact, n, large)


def _bucket_tiles():
    i = jnp.arange(MOBA_BLOCK)[None, :]
    j = jnp.arange(MOBA_BLOCK)[:, None]
    d = jnp.arange(N_BIAS_TILES + 1)[:, None, None]
    dist = d * MOBA_BLOCK + i - j
    bucket = _t5_bucket(dist)
    bucket = jnp.where(d == N_BIAS_TILES - 1, REL_BUCKETS - 1, bucket)
    return jnp.where((dist < 0) | (d == N_BIAS_TILES), -1, bucket).astype(jnp.int32)


BIAS_ROWS = 16


def _bias_tiles_kernel(idx_ref, rb_ref, o_ref):
    def rows(c, carry):
        r0 = pl.multiple_of(c * BIAS_ROWS, BIAS_ROWS)
        idx = idx_ref[0, pl.ds(r0, BIAS_ROWS), :]
        acc = [jnp.where(idx < 0, NEG_INF, 0.0)] * HEADS
        for bkt in range(REL_BUCKETS):
            hit = idx == bkt
            acc = [jnp.where(hit, rb_ref[bkt, h] * LOG2E, acc[h]) for h in range(HEADS)]
        for h in range(HEADS):
            o_ref[h, 0, pl.ds(r0, BIAS_ROWS), :] = acc[h]
        return carry

    lax.fori_loop(0, MOBA_BLOCK // BIAS_ROWS, rows, 0)


def _bias_tiles(rel_bias):
    idx = _bucket_tiles()
    n = N_BIAS_TILES + 1
    return pl.pallas_call(
        _bias_tiles_kernel,
        grid=(n,),
        in_specs=[
            pl.BlockSpec((1, MOBA_BLOCK, MOBA_BLOCK), lambda d: (d, 0, 0)),
            pl.BlockSpec(memory_space=pltpu.SMEM),
        ],
        out_specs=pl.BlockSpec((HEADS, 1, MOBA_BLOCK, MOBA_BLOCK), lambda d: (0, d, 0, 0)),
        out_shape=jax.ShapeDtypeStruct((HEADS, n, MOBA_BLOCK, MOBA_BLOCK), F32),
        compiler_params=_cparams(("parallel",)),
        name="bias_tiles",
    )(idx, rel_bias)


def _moba_kernel(q_ref, ka_ref, vt_ref, r_ref, bias_ref, o_ref, s_ref, p_ref, acc_ref):
    qb = pl.program_id(1)
    B = MOBA_BLOCK
    q_tr = q_ref[...].astype(F32).T
    q_tr_bf = q_tr.astype(BF16)
    blk = lax.broadcasted_iota(jnp.int32, (HEAD_DIM, B), 0)
    big = jnp.int32(1 << 20)
    q_t, q_own_t = [], []
    for h2 in range(2):
        off = HEAD_DIM * (1 - h2)
        valid = blk < qb
        sc = jnp.dot(r_ref[0, h2, off:off + HEAD_DIM, :], q_tr_bf, preferred_element_type=F32)
        s = jnp.where(valid, sc, NEG_INF)
        sel = jnp.zeros((HEAD_DIM, B), jnp.bool_)
        for _ in range(MOBA_TOP):
            m = jnp.max(s, axis=0, keepdims=True)
            idx = jnp.min(jnp.where(s == m, blk, big), axis=0, keepdims=True)
            pick = blk == idx
            sel = jnp.logical_or(sel, pick)
            s = jnp.where(pick, -jnp.inf, s)
        sel = jnp.logical_and(sel, valid)
        choice = jnp.where(sel, 0.0, NEG_INF)
        q_head = q_tr[h2 * HEAD_DIM:(h2 + 1) * HEAD_DIM]
        parts = [q_head, choice] if h2 == 0 else [choice, q_head]
        own = [q_head, jnp.zeros_like(choice)] if h2 == 0 else [jnp.zeros_like(choice), q_head]
        q_t.append(jnp.concatenate(parts, axis=0).astype(BF16))
        q_own_t.append(jnp.concatenate(own, axis=0).astype(BF16))

    carry = []
    for h2 in range(2):
        k_own = ka_ref[h2, pl.ds(pl.multiple_of(qb * B, B), B), :]
        s0 = jnp.dot(k_own, q_own_t[h2], preferred_element_type=F32) + bias_ref[h2, 0]
        m0 = jnp.max(s0, axis=0, keepdims=True)
        p0 = jnp.exp2(s0 - m0).astype(BF16)
        carry += [m0, jnp.dot(vt_ref[h2, qb], p0, preferred_element_type=F32)]

    n_tiles = (qb + 1) // 2
    last = jnp.maximum(n_tiles - 1, 0)
    nb = vt_ref.shape[1] - PREP_BLOCKS
    for h2 in range(2):
        acc_ref[h2] = carry[2 * h2 + 1]

    @pl.when((pl.program_id(0) == 0) & (qb == 0))
    def _():
        s_ref[...] = jnp.zeros_like(s_ref)
        p_ref[...] = jnp.zeros_like(p_ref)

    def trip(t, w, stats):
        r = 1 - w
        ok = (t >= 1) & (t <= n_tiles)
        d_a = jnp.where(ok, jnp.clip(qb - 2 * (t - 1), 0, N_BIAS_TILES - 1), N_BIAS_TILES)
        d_b = jnp.where(ok, jnp.clip(qb - 2 * (t - 1) - 1, 0, N_BIAS_TILES - 1), N_BIAS_TILES)
        start = pl.multiple_of(jnp.clip(t, 0, last) * (2 * B), 2 * B)
        ok_pv = (t >= 2) & (t <= n_tiles + 1)
        v_a = jnp.where(ok_pv, 2 * (t - 2), nb)
        v_b = jnp.where(ok_pv, 2 * (t - 2) + 1, nb)
        out = []
        for h2 in range(2):
            m_prev, alpha_p = stats[2 * h2], stats[2 * h2 + 1]
            acc_ref[h2] = (acc_ref[h2] * alpha_p
                           + jnp.dot(vt_ref[h2, v_a], p_ref[r, h2, 0:B], preferred_element_type=F32)
                           + jnp.dot(vt_ref[h2, v_b], p_ref[r, h2, B:2 * B],
                                     preferred_element_type=F32))
            s_a = s_ref[r, h2, 0:B] + bias_ref[h2, d_a]
            s_b = s_ref[r, h2, B:2 * B] + bias_ref[h2, d_b]
            m_new = jnp.maximum(m_prev, jnp.max(jnp.maximum(s_a, s_b), axis=0, keepdims=True))
            out += [m_new, jnp.exp2(m_prev - m_new)]
            p_ref[w, h2, 0:B] = jnp.exp2(s_a - m_new).astype(BF16)
            p_ref[w, h2, B:2 * B] = jnp.exp2(s_b - m_new).astype(BF16)
            s_ref[w, h2] = jnp.dot(ka_ref[h2, pl.ds(start, 2 * B), :], q_t[h2],
                                   preferred_element_type=F32)
        return out

    def body(u, stats):
        stats = trip(2 * u, 0, list(stats))
        return tuple(trip(2 * u + 1, 1, stats))

    stats = []
    for h2 in range(2):
        stats += [carry[2 * h2], jnp.ones_like(carry[2 * h2])]
    lax.fori_loop(0, (n_tiles + 3) // 2, body, tuple(stats))
    outs = [acc_ref[h2, 0:HEAD_DIM] / acc_ref[h2, HEAD_DIM:HEAD_DIM + 1] for h2 in range(2)]
    o_ref[...] = jnp.concatenate(outs, axis=0).T.astype(BF16)


def _moba(qkv, k_aug, vt_aug, r_mats, bias_tiles):
    T = qkv.shape[0]
    nb = T // MOBA_BLOCK
    npair = HEADS // 2
    once = pl.Buffered(1)
    return pl.pallas_call(
        _moba_kernel,
        grid=(npair, nb),
        in_specs=[
            pl.BlockSpec((MOBA_BLOCK, LANES), lambda p, qb: (qb, p)),
            pl.BlockSpec((2, T, LANES), lambda p, qb: (p, 0, 0), pipeline_mode=once),
            pl.BlockSpec((2, nb + PREP_BLOCKS, VT_ROWS, MOBA_BLOCK), lambda p, qb: (p, 0, 0, 0),
                         pipeline_mode=once),
            pl.BlockSpec((1, 2, LANES, LANES), lambda p, qb: (p, 0, 0, 0)),
            pl.BlockSpec((2, N_BIAS_TILES + 1, MOBA_BLOCK, MOBA_BLOCK), lambda p, qb: (p, 0, 0, 0),
                         pipeline_mode=once),
        ],
        out_specs=pl.BlockSpec((MOBA_BLOCK, LANES), lambda p, qb: (qb, p)),
        out_shape=jax.ShapeDtypeStruct((T, WIDTH), BF16),
        scratch_shapes=[pltpu.VMEM((2, 2, 2 * MOBA_BLOCK, MOBA_BLOCK), F32),
                        pltpu.VMEM((2, 2, 2 * MOBA_BLOCK, MOBA_BLOCK), BF16),
                        pltpu.VMEM((2, VT_ROWS, MOBA_BLOCK), F32)],
        compiler_params=_cparams(("arbitrary", "arbitrary")),
        name="moba",
    )(qkv, k_aug, vt_aug, r_mats, bias_tiles)


def _moba_prep_kernel(k_ref, v_ref, ka_ref, vt_ref, km_ref):
    B = MOBA_BLOCK
    step = pl.program_id(1)
    extra = step == pl.num_programs(1) - 1
    lane = lax.broadcasted_iota(jnp.int32, (B, LANES), 1)
    ones = jnp.ones((VT_ROWS - HEAD_DIM, B), F32)
    for jj in range(PREP_BLOCKS):
        j = jnp.minimum(step, pl.num_programs(1) - 2) * PREP_BLOCKS + jj
        kblk = k_ref[jj * B:(jj + 1) * B, :]
        km_ref[0, 0, jj:jj + 1, :] = jnp.mean(kblk.astype(F32), axis=0, keepdims=True)
        v_t = v_ref[jj * B:(jj + 1) * B, :].astype(F32).T
        for h2 in range(2):
            off = HEAD_DIM * (1 - h2)
            inr = (lane >= off) & (lane < off + HEAD_DIM)
            hot = jnp.where(lane == off + j, 1.0, 0.0).astype(BF16)
            ka_ref[h2, jj * B:(jj + 1) * B, :] = jnp.where(inr, hot, kblk)
            tile = jnp.concatenate([v_t[h2 * HEAD_DIM:(h2 + 1) * HEAD_DIM], ones], axis=0)
            vt_ref[h2, jj] = jnp.where(extra, 0.0, tile).astype(BF16)


def _moba_prep(qkv):
    T = qkv.shape[0]
    nb = T // MOBA_BLOCK
    npair = HEADS // 2
    kcol = WIDTH // LANES
    steps = nb // PREP_BLOCKS
    k_aug, vt_aug, km = pl.pallas_call(
        _moba_prep_kernel,
        grid=(npair, steps + 1),
        in_specs=[
            pl.BlockSpec((PREP_BLOCKS * MOBA_BLOCK, LANES), lambda p, j: (jnp.minimum(j, steps - 1), kcol + p)),
            pl.BlockSpec((PREP_BLOCKS * MOBA_BLOCK, LANES),
                         lambda p, j: (jnp.minimum(j, steps - 1), 2 * kcol + p)),
        ],
        out_specs=[
            pl.BlockSpec((2, PREP_BLOCKS * MOBA_BLOCK, LANES), lambda p, j: (p, jnp.minimum(j, steps - 1), 0)),
            pl.BlockSpec((2, PREP_BLOCKS, VT_ROWS, MOBA_BLOCK), lambda p, j: (p, j, 0, 0)),
            pl.BlockSpec((1, 1, PREP_BLOCKS, LANES), lambda p, j: (p, j, 0, 0)),
        ],
        out_shape=[
            jax.ShapeDtypeStruct((HEADS, T, LANES), BF16),
            jax.ShapeDtypeStruct((HEADS, nb + PREP_BLOCKS, VT_ROWS, MOBA_BLOCK), BF16),
            jax.ShapeDtypeStruct((npair, steps + 1, PREP_BLOCKS, LANES), F32),
        ],
        compiler_params=_cparams(("arbitrary", "arbitrary")),
        name="moba_prep",
    )(qkv, qkv)
    kmean = km[:, :steps].reshape(npair, nb, LANES).transpose(1, 0, 2).reshape(nb, WIDTH)
    return k_aug, vt_aug, kmean


def _score_mats(kmean):
    nb = kmean.shape[0]
    km = kmean.reshape(nb, HEADS, HEAD_DIM).transpose(1, 2, 0)
    km = jnp.pad(km, ((0, 0), (0, 0), (0, HEAD_DIM - nb)))
    z = jnp.zeros((HEADS // 2, HEAD_DIM, HEAD_DIM), F32)
    even = jnp.concatenate([jnp.concatenate([z, km[0::2]], axis=2),
                            jnp.concatenate([z, z], axis=2)], axis=1)
    odd = jnp.concatenate([jnp.concatenate([z, z], axis=2),
                           jnp.concatenate([km[1::2], z], axis=2)], axis=1)
    return jnp.swapaxes(jnp.stack([even, odd], axis=1), -1, -2).astype(BF16)


def _memkv_kernel(m_ref, g_ref, w_ref, o_ref):
    h = _rms(m_ref[...], g_ref[...]).astype(BF16)
    o_ref[...] = jnp.dot(h, w_ref[...], preferred_element_type=F32).astype(BF16)


def _memkv(mem2d, g, w_bf):
    return pl.pallas_call(
        _memkv_kernel,
        out_shape=jax.ShapeDtypeStruct((N_MEM, 2 * XATTN_WIDTH), BF16),
        compiler_params=pltpu.CompilerParams(vmem_limit_bytes=VMEM_LIMIT),
        name="memkv",
    )(mem2d, g, w_bf)


def _merge_kernel(x_ref, oa_ref, ob_ref, gt_ref, woa_ref, wob_ref, wout_ref, nx_ref, wq_ref,
                  kv_ref, wo_ref, nf_ref, wr_ref, x2_ref, h3_ref, rt_ref, cnt_ref, run_ref):
    pa = jnp.dot(oa_ref[...], woa_ref[...], preferred_element_type=F32)
    pb = jnp.dot(ob_ref[...], wob_ref[...], preferred_element_type=F32)
    ga = gt_ref[:, 0:D_MODEL].astype(F32)
    gb = gt_ref[:, D_MODEL:2 * D_MODEL].astype(F32)
    merged = _sigmoid(ga) * pa + _sigmoid(gb) * pb
    x1 = x_ref[...] + jnp.dot(merged.astype(BF16), wout_ref[...], preferred_element_type=F32)
    h2 = _rms(x1, nx_ref[...]).astype(BF16)
    q = jnp.dot(h2, wq_ref[...], preferred_element_type=F32).astype(BF16)
    heads = []
    for h in range(XATTN_HEADS):
        sl = slice(h * XATTN_HEAD_DIM, (h + 1) * XATTN_HEAD_DIM)
        km = kv_ref[:, sl]
        vm = kv_ref[:, XATTN_WIDTH + h * XATTN_HEAD_DIM:XATTN_WIDTH + (h + 1) * XATTN_HEAD_DIM]
        lg = lax.dot_general(q[:, sl], km, (((1,), (1,)), ((), ())),
                             preferred_element_type=F32) * (XATTN_HEAD_DIM ** -0.5)
        e = jnp.exp(lg - jnp.max(lg, axis=-1, keepdims=True))
        p = e / jnp.sum(e, axis=-1, keepdims=True)
        heads.append(jnp.dot(p.astype(BF16), vm, preferred_element_type=F32))
    o = jnp.concatenate(heads, axis=-1).astype(BF16)
    x2 = x1 + jnp.dot(o, wo_ref[...], preferred_element_type=F32)
    x2_ref[...] = x2
    h3 = _rms(x2, nf_ref[...])
    _store_token_tiles(h3_ref, h3)
    lg = _bdot(h3, wr_ref[...]).T
    tm = lg.shape[1]
    row = lax.broadcasted_iota(jnp.int32, lg.shape, 0)
    big = jnp.int32(1 << 20)
    is_g = row < N_GROUPS
    gl = jnp.where(is_g, lg, -jnp.inf)
    gmax = jnp.max(gl, axis=0, keepdims=True)
    gsel = jnp.min(jnp.where(gl == gmax, row, big), axis=0, keepdims=True)
    p_top = 1.0 / jnp.sum(jnp.where(is_g, jnp.exp(gl - gmax), 0.0), axis=0, keepdims=True)
    e_id = row - N_GROUPS
    in_grp = (e_id >= gsel * EXPERTS_PER_GROUP) & (e_id < (gsel + 1) * EXPERTS_PER_GROUP)
    el = jnp.where(in_grp, lg, -jnp.inf)
    v1 = jnp.max(el, axis=0, keepdims=True)
    i1 = jnp.min(jnp.where(el == v1, row, big), axis=0, keepdims=True)
    el2 = jnp.where(row == i1, -jnp.inf, el)
    v2 = jnp.max(el2, axis=0, keepdims=True)
    i2 = jnp.min(jnp.where(el2 == v2, row, big), axis=0, keepdims=True)
    e21 = jnp.exp(v2 - v1)
    g1 = p_top / (1.0 + e21)
    g2 = p_top * e21 / (1.0 + e21)
    step = pl.program_id(0)

    @pl.when(step == 0)
    def _():
        run_ref[...] = jnp.zeros_like(run_ref)

    oh1 = jnp.where(row == i1, 1.0, 0.0)
    oh2 = jnp.where(row == i2, 1.0, 0.0)
    both = oh1 + oh2
    ri = lax.broadcasted_iota(jnp.int32, (tm, tm), 0)
    ci = lax.broadcasted_iota(jnp.int32, (tm, tm), 1)
    before = jnp.dot(both, jnp.where(ri < ci, 1.0, 0.0), preferred_element_type=F32) + run_ref[:, 0:1]
    r1 = jnp.sum(before * oh1, axis=0, keepdims=True)
    r2 = jnp.sum(before * oh2, axis=0, keepdims=True)
    run_ref[...] = run_ref[...] + jnp.sum(both, axis=1, keepdims=True)
    cnt_ref[...] = run_ref[...]
    vals = [(i1 - N_GROUPS).astype(F32), (i2 - N_GROUPS).astype(F32), g1, g2, r1, r2]
    rt = jnp.zeros_like(lg)
    for k, v in enumerate(vals):
        rt = jnp.where(row == k, v, rt)
    rt_ref[...] = rt.T


def _merge(x2d, o_a, o_b, gates, w_oa, w_ob, w_out, n_x, w_q, kv, w_o, n_f, w_r):
    T = x2d.shape[0]
    tm = TOKEN_TILE
    full = lambda a: pl.BlockSpec(a.shape, lambda i: (0,) * a.ndim, pipeline_mode=pl.Buffered(1))
    tile = lambda w: pl.BlockSpec((tm, w), lambda i: (i, 0))
    return pl.pallas_call(
        _merge_kernel,
        grid=(T // tm,),
        in_specs=[tile(D_MODEL), tile(WIDTH), tile(WIDTH), tile(GATE_COLS), full(w_oa), full(w_ob),
                  full(w_out), full(n_x), full(w_q), full(kv), full(w_o), full(n_f), full(w_r)],
        out_specs=[tile(D_MODEL), pl.BlockSpec((tm * ROW_TILES, LANES), lambda i: (i, 0)), tile(LANES),
                   pl.BlockSpec((LANES, LANES), lambda i: (0, 0))],
        out_shape=[jax.ShapeDtypeStruct((T, D_MODEL), F32),
                   jax.ShapeDtypeStruct((T * ROW_TILES, LANES), U32),
                   jax.ShapeDtypeStruct((T, LANES), F32),
                   jax.ShapeDtypeStruct((LANES, LANES), F32)],
        scratch_shapes=[pltpu.VMEM((LANES, LANES), F32)],
        compiler_params=_cparams(("arbitrary",)),
        name="merge",
    )(x2d, o_a, o_b, gates, w_oa, w_ob, w_out, n_x, w_q, kv, w_o, n_f, w_r)


def _dispatch_kernel(pos_ref, zs_ref, zv_ref, nu_ref, h_ref, xs_hbm, zero_ref, sem):
    i = pl.program_id(0)
    tm = h_ref.shape[0] // ROW_TILES

    @pl.when(i == 0)
    def _():
        zero_ref[...] = jnp.zeros_like(zero_ref)

        def clear(e):
            dst = xs_hbm.at[pl.ds(pl.multiple_of(zs_ref[e] * ROW_TILES, ROW_TILES),
                                  MOE_BLOCK * ROW_TILES), :]
            return pltpu.make_async_copy(zero_ref, dst, sem)

        for e in range(N_EXPERTS):
            @pl.when(zv_ref[e] > 0)
            def _():
                clear(e).start()

        for e in range(N_EXPERTS):
            @pl.when(zv_ref[e] > 0)
            def _():
                clear(e).wait()

        def tail(b):
            dst = xs_hbm.at[pl.ds(pl.multiple_of(b * (MOE_BLOCK * ROW_TILES), MOE_BLOCK * ROW_TILES),
                                  MOE_BLOCK * ROW_TILES), :]
            return pltpu.make_async_copy(zero_ref, dst, sem)

        n_blocks = xs_hbm.shape[0] // (MOE_BLOCK * ROW_TILES)

        def start_tail(b, c):
            tail(b).start()
            return c

        def wait_tail(b, c):
            tail(b).wait()
            return c

        lax.fori_loop(nu_ref[0], n_blocks, start_tail, 0)
        lax.fori_loop(nu_ref[0], n_blocks, wait_tail, 0)

    def row_copy(r, k):
        p = pos_ref[2 * (i * tm + r) + k]
        dst = xs_hbm.at[pl.ds(pl.multiple_of(p * ROW_TILES, ROW_TILES), ROW_TILES), :]
        return pltpu.make_async_copy(h_ref.at[pl.ds(r * ROW_TILES, ROW_TILES), :], dst, sem)

    for r in range(tm):
        row_copy(r, 0).start(priority=0)
        row_copy(r, 1).start(priority=1)
    for _ in range(tm):
        row_copy(0, 0).wait()
        row_copy(0, 1).wait()


def _dispatch(pos, zstart, zvalid, n_used, h3t, cap):
    T = h3t.shape[0] // ROW_TILES
    tm = ROW_DMA_TILE
    grid_spec = pltpu.PrefetchScalarGridSpec(
        num_scalar_prefetch=4,
        grid=(T // tm,),
        in_specs=[pl.BlockSpec((tm * ROW_TILES, LANES), lambda i, ps, zs, zv, nu: (i, 0))],
        out_specs=pl.BlockSpec(memory_space=pl.ANY),
        scratch_shapes=[pltpu.VMEM((MOE_BLOCK * ROW_TILES, LANES), U32), pltpu.SemaphoreType.DMA],
    )
    return pl.pallas_call(
        _dispatch_kernel,
        grid_spec=grid_spec,
        out_shape=jax.ShapeDtypeStruct((cap * ROW_TILES, LANES), U32),
        compiler_params=_cparams(("arbitrary",)),
        name="dispatch",
    )(pos, zstart, zvalid, n_used, h3t)


def _moe_kernel(be_ref, nu_ref, x_ref, wg_ref, wu_ref, wd_ref, o_ref):
    i = pl.program_id(0)

    @pl.when(i < nu_ref[0])
    def _():
        xb = jnp.concatenate([x.astype(BF16) for x in _load_token_tiles(x_ref, MOE_BLOCK)], axis=-1)
        gate = _bdot(xb, wg_ref[0])
        up = _bdot(xb, wu_ref[0])
        hid = gate * _sigmoid(gate) * up
        _store_token_tiles(o_ref, _bdot(hid, wd_ref[0]))

    @pl.when(i >= nu_ref[0])
    def _():
        o_ref[...] = jnp.zeros_like(o_ref)


def _moe(block_expert, n_used, xs, wg, wu, wd):
    n_blocks = xs.shape[0] // (MOE_BLOCK * ROW_TILES)
    used = lambda i, nu: jnp.minimum(i, jnp.maximum(nu[0] - 1, 0))
    grid_spec = pltpu.PrefetchScalarGridSpec(
        num_scalar_prefetch=2,
        grid=(n_blocks,),
        in_specs=[
            pl.BlockSpec((MOE_BLOCK * ROW_TILES, LANES), lambda i, be, nu: (used(i, nu), 0)),
            pl.BlockSpec((1, D_MODEL, D_EXPERT), lambda i, be, nu: (be[i], 0, 0)),
            pl.BlockSpec((1, D_MODEL, D_EXPERT), lambda i, be, nu: (be[i], 0, 0)),
            pl.BlockSpec((1, D_EXPERT, D_MODEL), lambda i, be, nu: (be[i], 0, 0)),
        ],
        out_specs=pl.BlockSpec((MOE_BLOCK * ROW_TILES, LANES), lambda i, be, nu: (i, 0)),
    )
    return pl.pallas_call(
        _moe_kernel,
        grid_spec=grid_spec,
        out_shape=jax.ShapeDtypeStruct(xs.shape, U32),
        compiler_params=_cparams(("arbitrary",)),
        name="moe",
    )(block_expert, n_used, xs, wg, wu, wd)


def _moe_plan(expert, rank, counts, n_tok):
    n_assign = n_tok * 2
    e_flat = expert.reshape(n_assign)
    rank = rank.reshape(n_assign)
    padded = (counts + MOE_BLOCK - 1) // MOE_BLOCK * MOE_BLOCK
    p_end = jnp.cumsum(padded)
    p_start = p_end - padded
    pos = (p_start[e_flat] + rank).astype(jnp.int32)
    n_blocks = (n_assign + N_EXPERTS * (MOE_BLOCK - 1) + MOE_BLOCK - 1) // MOE_BLOCK
    cap = n_blocks * MOE_BLOCK
    block_start = jnp.arange(n_blocks, dtype=jnp.int32) * MOE_BLOCK
    block_expert = jnp.minimum(jnp.sum(p_end[None, :] <= block_start[:, None], axis=1),
                               N_EXPERTS - 1).astype(jnp.int32)
    n_used = (p_end[-1] // MOE_BLOCK).astype(jnp.int32).reshape(1)
    last_block = jnp.maximum(p_end - MOE_BLOCK, 0).astype(jnp.int32)
    nonempty = (padded > 0).astype(jnp.int32)
    return pos, cap, block_expert, n_used, last_block, nonempty


def _final_kernel(pos_ref, x_ref, rt_ref, y_hbm, g_ref, o_ref, y1_ref, y2_ref, sem):
    i = pl.program_id(0)
    n = pl.num_programs(0)
    tm = x_ref.shape[0]
    slot = i % 2

    def copies(tile, r, s):
        p1 = pos_ref[2 * (tile * tm + r)]
        p2 = pos_ref[2 * (tile * tm + r) + 1]
        src = lambda p: y_hbm.at[pl.ds(pl.multiple_of(p * ROW_TILES, ROW_TILES), ROW_TILES), :]
        dst = lambda ref: ref.at[s, pl.ds(r * ROW_TILES, ROW_TILES), :]
        return (pltpu.make_async_copy(src(p1), dst(y1_ref), sem.at[s]),
                pltpu.make_async_copy(src(p2), dst(y2_ref), sem.at[s]))

    def gather(tile, s):
        for r in range(tm):
            c1, c2 = copies(tile, r, s)
            c1.start(priority=0)
            c2.start(priority=1)

    def wait_rows(s):
        for _ in range(tm):
            c1, c2 = copies(0, 0, s)
            c1.wait()
            c2.wait()

    @pl.when(i == 0)
    def _():
        gather(0, 0)

    wait_rows(slot)
    gather(jnp.minimum(i + 1, n - 1), 1 - slot)
    rt = rt_ref[...]
    g1 = rt[:, 2:3]
    g2 = rt[:, 3:4]
    y1 = jnp.concatenate(_load_token_tiles(y1_ref.at[slot], tm), axis=-1)
    y2 = jnp.concatenate(_load_token_tiles(y2_ref.at[slot], tm), axis=-1)
    x3 = x_ref[...] + y1 * g1 + y2 * g2
    o_ref[...] = _rms(x3, g_ref[...])

    @pl.when(i == n - 1)
    def _():
        wait_rows(1 - slot)


def _final(pos, x2, rt, y_sorted, g):
    T = x2.shape[0]
    tm = ROW_DMA_TILE
    grid_spec = pltpu.PrefetchScalarGridSpec(
        num_scalar_prefetch=1,
        grid=(T // tm,),
        in_specs=[
            pl.BlockSpec((tm, D_MODEL), lambda i, ps: (i, 0)),
            pl.BlockSpec((tm, LANES), lambda i, ps: (i, 0)),
            pl.BlockSpec(memory_space=pl.ANY),
            pl.BlockSpec((1, D_MODEL), lambda i, ps: (0, 0)),
        ],
        out_specs=pl.BlockSpec((tm, D_MODEL), lambda i, ps: (i, 0)),
        scratch_shapes=[pltpu.VMEM((2, tm * ROW_TILES, LANES), U32),
                        pltpu.VMEM((2, tm * ROW_TILES, LANES), U32),
                        pltpu.SemaphoreType.DMA((2,))],
    )
    return pl.pallas_call(
        _final_kernel,
        grid_spec=grid_spec,
        out_shape=jax.ShapeDtypeStruct((T, D_MODEL), F32),
        compiler_params=_cparams(("arbitrary",)),
        name="final",
    )(pos, x2, rt, y_sorted, g)


def _lora_weight(decay_up, iclr_up, gate_up):
    w = jnp.zeros((LORA_COLS, 3 * WIDTH), F32)
    w = w.at[0:DECAY_LORA, 0:WIDTH].set(decay_up)
    w = w.at[DECAY_LORA:DECAY_LORA + ICLR_LORA, WIDTH:2 * WIDTH].set(iclr_up)
    w = w.at[DECAY_LORA + ICLR_LORA:, 2 * WIDTH:].set(gate_up)
    return w


def _router_weight(w_group, w_expert):
    w = jnp.zeros((D_MODEL, LANES), F32)
    w = w.at[:, 0:N_GROUPS].set(w_group)
    return w.at[:, N_GROUPS:N_GROUPS + N_EXPERTS].set(w_expert)


def kernel(x, mem, rel_bias, mem_norm, norm_mix, w_in, tshift_mu, decay_w0, decay_up, iclr_a0,
           iclr_up, gate_up, k_k, k_a, r_k, ln_x_w, ln_x_b, w_o_rwkv, w_o_moba, w_out,
           norm_xattn, w_q_x, w_kv_x, w_o_x, norm_ffn, w_router_group, w_router_expert,
           w_exp_gate, w_exp_up, w_exp_down, norm_final):
    B, T, D = x.shape
    assert B == 1 and D == D_MODEL and T % TOKEN_TILE == 0 and T // MOBA_BLOCK <= HEAD_DIM
    assert norm_mix.shape[0] == 1 and (T // MOBA_BLOCK) % PREP_BLOCKS == 0
    row = lambda a: a.reshape(1, -1)
    x2d = x.reshape(T, D)
    kv = _memkv(mem.reshape(N_MEM, D), row(mem_norm), w_kv_x[0].astype(BF16))

    ur, qkv, gates = _inproj(x2d, row(norm_mix[0]), w_in[0].astype(BF16))
    o_a = _rwkv_core(ur, row(tshift_mu[0]), row(decay_w0[0]), row(iclr_a0[0]), row(k_k[0]),
                     row(k_a[0]), row(r_k[0]),
                     _lora_weight(decay_up[0], iclr_up[0], gate_up[0]).astype(BF16),
                     row(ln_x_w[0]), row(ln_x_b[0]))

    k_aug, vt_aug, kmean = _moba_prep(qkv)
    o_b = _moba(qkv, k_aug, vt_aug, _score_mats(kmean), _bias_tiles(rel_bias))

    x2, h3, rt, counts = _merge(
        x2d, o_a, o_b, gates, w_o_rwkv[0].astype(BF16), w_o_moba[0].astype(BF16),
        w_out[0].astype(BF16), row(norm_xattn[0]), w_q_x[0].astype(BF16), kv,
        w_o_x[0].astype(BF16), row(norm_ffn[0]),
        _router_weight(w_router_group[0], w_router_expert[0]).astype(BF16))

    expert = rt[:, 0:2].astype(jnp.int32)
    rank = rt[:, 4:6].astype(jnp.int32)
    pos, cap, block_expert, n_used, last_block, nonempty = _moe_plan(
        expert, rank, counts[N_GROUPS:N_GROUPS + N_EXPERTS, 0].astype(jnp.int32), T)
    x_sorted = _dispatch(pos, last_block, nonempty, n_used, h3, cap)
    y_sorted = _moe(block_expert, n_used, x_sorted, w_exp_gate[0], w_exp_up[0], w_exp_down[0])
    out = _final(pos, x2, rt, y_sorted, row(norm_final))
    return out.reshape(B, T, D)
```

```python
import math

import jax
import jax.numpy as jnp
from jax import lax
from jax.experimental import pallas as pl
from jax.experimental.pallas import tpu as pltpu

F32 = jnp.float32
BF16 = jnp.bfloat16

D_MODEL = 1024
N_MEM = 256
NORM_EPS = 1e-6
NEG_INF = -1e30

HEADS = 8
HEAD_DIM = 64
WIDTH = HEADS * HEAD_DIM
DECAY_LORA = 64
ICLR_LORA = 64
GATE_LORA = 128
LORA_COLS = DECAY_LORA + ICLR_LORA + GATE_LORA
RWKV_COLS = 3 * WIDTH + LORA_COLS
QKV_COLS = 3 * WIDTH
GATE_COLS = 2 * D_MODEL
IN_COLS = RWKV_COLS + QKV_COLS + GATE_COLS
LN_X_EPS = 64e-5
KK_EPS = 1e-12

MOBA_BLOCK = 256
MOBA_TOP = 3
REL_BUCKETS = 32
REL_MAX_DISTANCE = 4096
N_BIAS_TILES = 14

XATTN_HEADS = 4
XATTN_HEAD_DIM = 128
XATTN_WIDTH = XATTN_HEADS * XATTN_HEAD_DIM

N_GROUPS = 4
EXPERTS_PER_GROUP = 8
N_EXPERTS = N_GROUPS * EXPERTS_PER_GROUP
D_EXPERT = 512
MOE_BLOCK = 256

TOKEN_TILE = 512
ROW_DMA_TILE = 256
VT_ROWS = HEAD_DIM + 16
CHUNK = 64
CHUNKS_PER_STEP = 4
LANES = 128
VMEM_LIMIT = 48 * 1024 * 1024

LOG2E = math.log2(math.e)


def _cparams(sem):
    return pltpu.CompilerParams(dimension_semantics=sem, vmem_limit_bytes=VMEM_LIMIT)


def _bdot(a, b):
    return jnp.dot(a.astype(BF16), b.astype(BF16), preferred_element_type=F32)


def _bdot_nt(a, b):
    return lax.dot_general(a.astype(BF16), b.astype(BF16), (((1,), (1,)), ((), ())),
                           preferred_element_type=F32)


def _bdot_tn(a, b):
    return lax.dot_general(a.astype(BF16), b.astype(BF16), (((0,), (0,)), ((), ())),
                           preferred_element_type=F32)


def _rms(x, g):
    return x * lax.rsqrt(jnp.mean(x * x, axis=-1, keepdims=True) + NORM_EPS) * g


def _sigmoid(x):
    return 1.0 / (1.0 + jnp.exp(-x))


ROW_TILES = D_MODEL // (2 * LANES)
U32 = jnp.uint32
HI_MASK = 0xFFFF0000


def _bits(x):
    return pltpu.bitcast(x.astype(BF16).astype(F32), U32)


def _store_token_tiles(ref, x):
    n = x.shape[0]
    for s in range(ROW_TILES):
        lo = x[:, 2 * s * LANES:(2 * s + 1) * LANES]
        hi = x[:, (2 * s + 1) * LANES:(2 * s + 2) * LANES]
        word = (_bits(hi) & U32(HI_MASK)) | lax.shift_right_logical(_bits(lo), U32(16))
        ref[pl.ds(s, n, stride=ROW_TILES), :] = word


def _load_token_tiles(ref, n):
    out = []
    for s in range(ROW_TILES):
        word = ref[pl.ds(s, n, stride=ROW_TILES), :]
        out.append(pltpu.bitcast(lax.shift_left(word, U32(16)), F32))
        out.append(pltpu.bitcast(word & U32(HI_MASK), F32))
    return out


def _inproj_kernel(x_ref, g_ref, w_ref, ur_ref, q_ref, gate_ref, ka_ref, vt_ref, km_ref):
    i = pl.program_id(0)
    B = MOBA_BLOCK
    h = _rms(x_ref[...], g_ref[...]).astype(BF16)
    tm = h.shape[0]
    step = 256
    for c0 in range(0, RWKV_COLS, step):
        ur_ref[:, c0:c0 + step] = jnp.dot(h, w_ref[:, c0:c0 + step], preferred_element_type=F32)
    for c0 in range(0, WIDTH, step):
        o = jnp.dot(h, w_ref[:, RWKV_COLS + c0:RWKV_COLS + c0 + step], preferred_element_type=F32)
        q_ref[:, c0:c0 + step] = (o * (HEAD_DIM ** -0.5 * LOG2E)).astype(BF16)
    lane = lax.broadcasted_iota(jnp.int32, (B, LANES), 1)
    ones = jnp.ones((VT_ROWS - HEAD_DIM, B), F32)
    for c0 in range(0, WIDTH, step):
        kcol = RWKV_COLS + WIDTH + c0
        k = jnp.dot(h, w_ref[:, kcol:kcol + step], preferred_element_type=F32).astype(BF16)
        v = jnp.dot(h, w_ref[:, kcol + WIDTH:kcol + WIDTH + step], preferred_element_type=F32)
        v = v.astype(BF16).astype(F32)
        for pp in range(step // LANES):
            pair = c0 // LANES + pp
            for jj in range(tm // B):
                rows = slice(jj * B, (jj + 1) * B)
                kblk = k[rows, pp * LANES:(pp + 1) * LANES]
                km_ref[0, jj:jj + 1, pair * LANES:(pair + 1) * LANES] = jnp.mean(
                    kblk.astype(F32), axis=0, keepdims=True)
                v_t = v[rows, pp * LANES:(pp + 1) * LANES].T
                for h2 in range(2):
                    off = HEAD_DIM * (1 - h2)
                    inr = (lane >= off) & (lane < off + HEAD_DIM)
                    hot = jnp.where(lane == off + i * (tm // B) + jj, 1.0, 0.0).astype(BF16)
                    ka_ref[2 * pair + h2, rows, :] = jnp.where(inr, hot, kblk)
                    tile = jnp.concatenate([v_t[h2 * HEAD_DIM:(h2 + 1) * HEAD_DIM], ones], axis=0)
                    vt_ref[2 * pair + h2, jj] = tile.astype(BF16)
    base = RWKV_COLS + QKV_COLS
    for c0 in range(0, GATE_COLS, step):
        o = jnp.dot(h, w_ref[:, base + c0:base + c0 + step], preferred_element_type=F32)
        gate_ref[:, c0:c0 + step] = o.astype(BF16)


def _inproj(x2d, g, w_bf):
    T = x2d.shape[0]
    tm = TOKEN_TILE
    bpt = tm // MOBA_BLOCK
    return pl.pallas_call(
        _inproj_kernel,
        grid=(T // tm,),
        in_specs=[
            pl.BlockSpec((tm, D_MODEL), lambda i: (i, 0)),
            pl.BlockSpec((1, D_MODEL), lambda i: (0, 0)),
            pl.BlockSpec((D_MODEL, IN_COLS), lambda i: (0, 0), pipeline_mode=pl.Buffered(1)),
        ],
        out_specs=[
            pl.BlockSpec((tm, RWKV_COLS), lambda i: (i, 0)),
            pl.BlockSpec((tm, WIDTH), lambda i: (i, 0)),
            pl.BlockSpec((tm, GATE_COLS), lambda i: (i, 0)),
            pl.BlockSpec((HEADS, tm, LANES), lambda i: (0, i, 0)),
            pl.BlockSpec((HEADS, bpt, VT_ROWS, MOBA_BLOCK), lambda i: (0, i, 0, 0)),
            pl.BlockSpec((1, bpt, WIDTH), lambda i: (i, 0, 0)),
        ],
        out_shape=[
            jax.ShapeDtypeStruct((T, RWKV_COLS), F32),
            jax.ShapeDtypeStruct((T, WIDTH), BF16),
            jax.ShapeDtypeStruct((T, GATE_COLS), BF16),
            jax.ShapeDtypeStruct((HEADS, T, LANES), BF16),
            jax.ShapeDtypeStruct((HEADS, T // MOBA_BLOCK, VT_ROWS, MOBA_BLOCK), BF16),
            jax.ShapeDtypeStruct((T // tm, bpt, WIDTH), F32),
        ],
        compiler_params=_cparams(("parallel",)),
        name="inproj",
    )(x2d, g, w_bf)


def _head_ones():
    r = lax.broadcasted_iota(jnp.int32, (LANES, LANES), 0) // HEAD_DIM
    c = lax.broadcasted_iota(jnp.int32, (LANES, LANES), 1) // HEAD_DIM
    return jnp.where(r == c, 1.0, 0.0).astype(BF16)


def _head_sum(x, ones):
    parts = [_bdot(x[:, c:c + LANES], ones) for c in range(0, WIDTH, LANES)]
    return jnp.concatenate(parts, axis=-1)


def _rwkv_prep_values(u_ref, up_ref, mu_ref, w0_ref, a0_ref, kk_ref, ka_ref, rk_ref, wl_ref):
    i = pl.program_id(0)
    u = u_ref[...]
    tm = u.shape[0]
    prev_last = up_ref[7:8, :] * jnp.where(i > 0, 1.0, 0.0)
    rolled = pltpu.roll(u, 1, 0)
    row = lax.broadcasted_iota(jnp.int32, u.shape, 0)
    u_prev = jnp.where(row == 0, prev_last, rolled)
    u = u + mu_ref[...] * (u_prev - u)
    r = u[:, 0:WIDTH]
    k = u[:, WIDTH:2 * WIDTH]
    v = u[:, 2 * WIDTH:3 * WIDTH]
    lo = u[:, 3 * WIDTH:3 * WIDTH + LORA_COLS]
    lane = lax.broadcasted_iota(jnp.int32, lo.shape, 1)
    act = jnp.where(lane < DECAY_LORA, jnp.tanh(lo),
                    jnp.where(lane < DECAY_LORA + ICLR_LORA, lo, _sigmoid(lo)))
    up = _bdot(act, wl_ref[...])
    z = -(w0_ref[...] + up[:, 0:WIDTH])
    softplus = jnp.maximum(z, 0.0) + jnp.log(1.0 + jnp.exp(-jnp.abs(z)))
    w_log = -softplus - 0.5
    lw = -jnp.exp(w_log)
    iclr = _sigmoid(a0_ref[...] + up[:, WIDTH:2 * WIDTH])
    g = up[:, 2 * WIDTH:3 * WIDTH]
    ones = _head_ones()
    kk = k * kk_ref[...]
    kk = kk * lax.rsqrt(_head_sum(kk * kk, ones) + KK_EPS)
    k2 = k * (1.0 + (iclr - 1.0) * ka_ref[...])
    bonus = _head_sum(r * k2 * rk_ref[...], ones) * v
    return r, lw, k2, v, -kk, kk * iclr, g, bonus


def _rwkv_core_kernel(u_ref, up_ref, mu_ref, w0_ref, a0_ref, kk_ref, ka_ref, rk_ref, wl_ref,
                      lnw_ref, lnb_ref, o_ref, s_ref):
    c = pl.program_id(0)

    @pl.when(c == 0)
    def _():
        s_ref[...] = jnp.zeros_like(s_ref)

    C = CHUNK
    G = CHUNKS_PER_STEP
    ri = lax.broadcasted_iota(jnp.int32, (C, C), 0)
    ci = lax.broadcasted_iota(jnp.int32, (C, C), 1)
    rg = lax.broadcasted_iota(jnp.int32, (G * C, G * C), 0)
    cg = lax.broadcasted_iota(jnp.int32, (G * C, G * C), 1)
    r_in, lw, k_in, v_all, a_in, b_in, g_in, bonus = _rwkv_prep_values(
        u_ref, up_ref, mu_ref, w0_ref, a0_ref, kk_ref, ka_ref, rk_ref, wl_ref)
    tri = jnp.where((rg >= cg) & (rg // C == cg // C), 1.0, 0.0).astype(BF16)
    lw_hi = lw.astype(BF16)
    lw_r1 = lw - lw_hi.astype(F32)
    lw_mid = lw_r1.astype(BF16)
    lw_lo = (lw_r1 - lw_mid.astype(F32)).astype(BF16)
    cum = (jnp.dot(tri, lw_hi, preferred_element_type=F32)
           + jnp.dot(tri, lw_mid, preferred_element_type=F32)
           + jnp.dot(tri, lw_lo, preferred_element_type=F32))
    lam = jnp.exp(cum)
    inv_lam = jnp.exp(-cum)
    r_t = r_in * lam
    a_t = a_in * jnp.exp(cum - lw)
    b_t = b_in * inv_lam
    k_t = k_in * inv_lam
    tots = [cum[g * C + C - 1:g * C + C, :] for g in range(G)]
    rowg = lax.broadcasted_iota(jnp.int32, cum.shape, 0) // C
    tot = tots[G - 1]
    for g in range(G - 2, -1, -1):
        tot = jnp.where(rowg == g, tots[g], tot)
    rest = jnp.exp(tot - cum)
    b_h = b_in * rest
    k_h = k_in * rest
    lam_c = [jnp.exp(t) for t in tots]
    eye = jnp.where(ri == ci, 1.0, 0.0)
    H = range(G * HEADS)
    sls = [slice(h * HEAD_DIM, (h + 1) * HEAD_DIM) for h in range(HEADS)]
    bf = lambda x: x.astype(BF16)
    part = lambda x, i: x[(i // HEADS) * C:(i // HEADS + 1) * C, sls[i % HEADS]]
    at = [bf(part(a_t, i)) for i in H]
    rt = [part(r_t, i) for i in H]
    bt = [bf(part(b_t, i)) for i in H]
    kt = [bf(part(k_t, i)) for i in H]
    bh = [bf(part(b_h, i)) for i in H]
    kh = [bf(part(k_h, i)) for i in H]
    vv = [bf(part(v_all, i)) for i in H]
    ci2 = lax.broadcasted_iota(jnp.int32, (C, 2 * C), 1)
    ri2 = lax.broadcasted_iota(jnp.int32, (C, 2 * C), 0)
    cm2 = jnp.where(ci2 >= C, ci2 - C, ci2)
    left = ci2 < C
    ar = [jnp.concatenate([at[h], bf(rt[h])], axis=0) for h in H]
    bk = [jnp.concatenate([bt[h], kt[h]], axis=0) for h in H]
    g = [_bdot_nt(ar[h], bk[h]) for h in H]
    top = [jnp.where(ri2 > cm2, g[h][0:C], 0.0) for h in H]
    bot = [bf(jnp.where(ri2 >= cm2, g[h][C:2 * C], 0.0)) for h in H]
    a_ab = [top[h][:, 0:C] for h in H]
    akv = [_bdot(top[h][:, C:2 * C], vv[h]) for h in H]
    z = [jnp.concatenate([a_ab[h], eye], axis=1) for h in H]
    for _ in range(6):
        z = [_bdot(z[h][:, 0:C], z[h]) + jnp.where(left, 0.0, z[h]) for h in H]
    tinv = [bf(z[h][:, C:2 * C]) for h in H]
    wu = [_bdot(tinv[h], jnp.concatenate([at[h], bf(akv[h])], axis=1)) for h in H]
    w_m = [bf(wu[h][:, 0:C]) for h in H]
    uv = [jnp.concatenate([bf(wu[h][:, C:2 * C]), vv[h]], axis=0) for h in H]
    q_m = [rt[h] + _bdot(bot[h][:, 0:C], w_m[h]) for h in H]
    y0 = [_bdot(bot[h], uv[h]) for h in H]
    m_k = [_bdot_tn(w_m[h], bh[h]) for h in H]
    n0 = [_bdot_tn(uv[h], jnp.concatenate([bh[h], kh[h]], axis=0)) for h in H]
    state = [s_ref[h] for h in range(HEADS)]
    blocks = []
    for g in range(G):
        outs = []
        for h in range(HEADS):
            i = g * HEADS + h
            y = _bdot_nt(q_m[i], state[h]) + y0[i]
            state[h] = state[h] * lam_c[g][:, sls[h]] + _bdot(state[h], m_k[i]) + n0[i]
            mean = jnp.mean(y, axis=-1, keepdims=True)
            var = jnp.mean(jnp.square(y - mean), axis=-1, keepdims=True)
            outs.append((y - mean) * lax.rsqrt(var + LN_X_EPS))
        blocks.append(jnp.concatenate(outs, axis=-1))
    for h in range(HEADS):
        s_ref[h] = state[h]
    yn = jnp.concatenate(blocks, axis=0)
    yn = yn * lnw_ref[...] + lnb_ref[...] + bonus
    o_ref[...] = (yn * g_in).astype(BF16)


def _rwkv_core(ur, mu, w0, a0, k_k, k_a, r_k, w_lora, ln_w, ln_b):
    T = ur.shape[0]
    rows = CHUNK * CHUNKS_PER_STEP
    row = lambda w: pl.BlockSpec((1, w), lambda c: (0, 0))
    return pl.pallas_call(
        _rwkv_core_kernel,
        grid=(T // rows,),
        in_specs=[
            pl.BlockSpec((rows, RWKV_COLS), lambda c: (c, 0)),
            pl.BlockSpec((8, RWKV_COLS), lambda c: (jnp.maximum(c * (rows // 8) - 1, 0), 0)),
            row(RWKV_COLS), row(WIDTH), row(WIDTH), row(WIDTH), row(WIDTH), row(WIDTH),
            pl.BlockSpec((LORA_COLS, 3 * WIDTH), lambda c: (0, 0)),
            row(WIDTH), row(WIDTH),
        ],
        out_specs=pl.BlockSpec((rows, WIDTH), lambda c: (c, 0)),
        out_shape=jax.ShapeDtypeStruct((T, WIDTH), BF16),
        scratch_shapes=[pltpu.VMEM((HEADS, HEAD_DIM, HEAD_DIM), F32)],
        compiler_params=_cparams(("arbitrary",)),
        name="rwkv_core",
    )(ur, ur, mu, w0, a0, k_k, k_a, r_k, w_lora, ln_w, ln_b)


def _t5_bucket(dist):
    n = jnp.maximum(dist, 0)
    max_exact = REL_BUCKETS // 2
    nf = jnp.maximum(n, max_exact).astype(jnp.float32)
    large = max_exact + (jnp.log(nf / max_exact) / math.log(REL_MAX_DISTANCE / max_exact)
                         * (REL_BUCKETS - max_exact)).astype(jnp.int32)
    large = jnp.minimum(large, REL_BUCKETS - 1)
    return jnp.where(n < max_exact, n, large)


def _bucket_tiles():
    i = jnp.arange(MOBA_BLOCK)[None, :]
    j = jnp.arange(MOBA_BLOCK)[:, None]
    d = jnp.arange(N_BIAS_TILES + 1)[:, None, None]
    dist = d * MOBA_BLOCK + i - j
    bucket = _t5_bucket(dist)
    bucket = jnp.where(d == N_BIAS_TILES - 1, REL_BUCKETS - 1, bucket)
    return jnp.where((dist < 0) | (d == N_BIAS_TILES), -1, bucket).astype(jnp.int32)


BIAS_ROWS = 16


def _bias_tiles_kernel(idx_ref, rb_ref, o_ref):
    def rows(c, carry):
        r0 = pl.multiple_of(c * BIAS_ROWS, BIAS_ROWS)
        idx = idx_ref[0, pl.ds(r0, BIAS_ROWS), :]
        acc = [jnp.where(idx < 0, NEG_INF, 0.0)] * HEADS
        for bkt in range(REL_BUCKETS):
            hit = idx == bkt
            acc = [jnp.where(hit, rb_ref[bkt, h] * LOG2E, acc[h]) for h in range(HEADS)]
        for h in range(HEADS):
            o_ref[h, 0, pl.ds(r0, BIAS_ROWS), :] = acc[h]
        return carry

    lax.fori_loop(0, MOBA_BLOCK // BIAS_ROWS, rows, 0)


def _bias_tiles(rel_bias):
    idx = _bucket_tiles()
    n = N_BIAS_TILES + 1
    return pl.pallas_call(
        _bias_tiles_kernel,
        grid=(n,),
        in_specs=[
            pl.BlockSpec((1, MOBA_BLOCK, MOBA_BLOCK), lambda d: (d, 0, 0)),
            pl.BlockSpec(memory_space=pltpu.SMEM),
        ],
        out_specs=pl.BlockSpec((HEADS, 1, MOBA_BLOCK, MOBA_BLOCK), lambda d: (0, d, 0, 0)),
        out_shape=jax.ShapeDtypeStruct((HEADS, n, MOBA_BLOCK, MOBA_BLOCK), F32),
        compiler_params=_cparams(("parallel",)),
        name="bias_tiles",
    )(idx, rel_bias)


def _moba_kernel(q_ref, ka_ref, vt_ref, r_ref, bias_ref, o_ref, s_ref, p_ref, acc_ref):
    qb = pl.program_id(1)
    B = MOBA_BLOCK
    q_tr = q_ref[...].astype(F32).T
    q_tr_bf = q_tr.astype(BF16)
    blk = lax.broadcasted_iota(jnp.int32, (HEAD_DIM, B), 0)
    big = jnp.int32(1 << 20)
    q_t, q_own_t = [], []
    for h2 in range(2):
        off = HEAD_DIM * (1 - h2)
        valid = blk < qb
        sc = jnp.dot(r_ref[0, h2, off:off + HEAD_DIM, :], q_tr_bf, preferred_element_type=F32)
        s = jnp.where(valid, sc, NEG_INF)
        sel = jnp.zeros((HEAD_DIM, B), jnp.bool_)
        for _ in range(MOBA_TOP):
            m = jnp.max(s, axis=0, keepdims=True)
            idx = jnp.min(jnp.where(s == m, blk, big), axis=0, keepdims=True)
            pick = blk == idx
            sel = jnp.logical_or(sel, pick)
            s = jnp.where(pick, -jnp.inf, s)
        sel = jnp.logical_and(sel, valid)
        choice = jnp.where(sel, 0.0, NEG_INF)
        q_head = q_tr[h2 * HEAD_DIM:(h2 + 1) * HEAD_DIM]
        parts = [q_head, choice] if h2 == 0 else [choice, q_head]
        own = [q_head, jnp.zeros_like(choice)] if h2 == 0 else [jnp.zeros_like(choice), q_head]
        q_t.append(jnp.concatenate(parts, axis=0).astype(BF16))
        q_own_t.append(jnp.concatenate(own, axis=0).astype(BF16))

    carry = []
    for h2 in range(2):
        k_own = ka_ref[h2, pl.ds(pl.multiple_of(qb * B, B), B), :]
        s0 = jnp.dot(k_own, q_own_t[h2], preferred_element_type=F32) + bias_ref[h2, 0]
        m0 = jnp.max(s0, axis=0, keepdims=True)
        p0 = jnp.exp2(s0 - m0).astype(BF16)
        carry += [m0, jnp.dot(vt_ref[h2, qb], p0, preferred_element_type=F32)]

    n_tiles = (qb + 1) // 2
    last = jnp.maximum(n_tiles - 1, 0)
    for h2 in range(2):
        acc_ref[h2] = carry[2 * h2 + 1]

    @pl.when((pl.program_id(0) == 0) & (qb == 0))
    def _():
        s_ref[...] = jnp.zeros_like(s_ref)
        p_ref[...] = jnp.zeros_like(p_ref)

    def trip(t, w, stats):
        r = 1 - w
        ok = (t >= 1) & (t <= n_tiles)
        d_a = jnp.where(ok, jnp.clip(qb - 2 * (t - 1), 0, N_BIAS_TILES - 1), N_BIAS_TILES)
        d_b = jnp.where(ok, jnp.clip(qb - 2 * (t - 1) - 1, 0, N_BIAS_TILES - 1), N_BIAS_TILES)
        start = pl.multiple_of(jnp.clip(t, 0, last) * (2 * B), 2 * B)
        v_a = 2 * jnp.clip(t - 2, 0, last)
        v_b = v_a + 1
        out = []
        for h2 in range(2):
            m_prev, alpha_p = stats[2 * h2], stats[2 * h2 + 1]
            acc_ref[h2] = (acc_ref[h2] * alpha_p
                           + jnp.dot(vt_ref[h2, v_a], p_ref[r, h2, 0:B], preferred_element_type=F32)
                           + jnp.dot(vt_ref[h2, v_b], p_ref[r, h2, B:2 * B],
                                     preferred_element_type=F32))
            s_a = s_ref[r, h2, 0:B] + bias_ref[h2, d_a]
            s_b = s_ref[r, h2, B:2 * B] + bias_ref[h2, d_b]
            m_new = jnp.maximum(m_prev, jnp.max(jnp.maximum(s_a, s_b), axis=0, keepdims=True))
            out += [m_new, jnp.exp2(m_prev - m_new)]
            p_ref[w, h2, 0:B] = jnp.exp2(s_a - m_new).astype(BF16)
            p_ref[w, h2, B:2 * B] = jnp.exp2(s_b - m_new).astype(BF16)
            s_ref[w, h2] = jnp.dot(ka_ref[h2, pl.ds(start, 2 * B), :], q_t[h2],
                                   preferred_element_type=F32)
        return out

    def body(u, stats):
        stats = trip(2 * u, 0, list(stats))
        return tuple(trip(2 * u + 1, 1, stats))

    stats = []
    for h2 in range(2):
        stats += [carry[2 * h2], jnp.ones_like(carry[2 * h2])]
    lax.fori_loop(0, (n_tiles + 3) // 2, body, tuple(stats))
    outs = [acc_ref[h2, 0:HEAD_DIM] / acc_ref[h2, HEAD_DIM:HEAD_DIM + 1] for h2 in range(2)]
    o_ref[...] = jnp.concatenate(outs, axis=0).T.astype(BF16)


def _moba(q, k_aug, vt_aug, r_mats, bias_tiles):
    T = q.shape[0]
    nb = T // MOBA_BLOCK
    npair = HEADS // 2
    once = pl.Buffered(1)
    return pl.pallas_call(
        _moba_kernel,
        grid=(npair, nb),
        in_specs=[
            pl.BlockSpec((MOBA_BLOCK, LANES), lambda p, qb: (qb, p)),
            pl.BlockSpec((2, T, LANES), lambda p, qb: (p, 0, 0), pipeline_mode=once),
            pl.BlockSpec((2, nb, VT_ROWS, MOBA_BLOCK), lambda p, qb: (p, 0, 0, 0), pipeline_mode=once),
            pl.BlockSpec((1, 2, LANES, LANES), lambda p, qb: (p, 0, 0, 0)),
            pl.BlockSpec((2, N_BIAS_TILES + 1, MOBA_BLOCK, MOBA_BLOCK), lambda p, qb: (p, 0, 0, 0),
                         pipeline_mode=once),
        ],
        out_specs=pl.BlockSpec((MOBA_BLOCK, LANES), lambda p, qb: (qb, p)),
        out_shape=jax.ShapeDtypeStruct((T, WIDTH), BF16),
        scratch_shapes=[pltpu.VMEM((2, 2, 2 * MOBA_BLOCK, MOBA_BLOCK), F32),
                        pltpu.VMEM((2, 2, 2 * MOBA_BLOCK, MOBA_BLOCK), BF16),
                        pltpu.VMEM((2, VT_ROWS, MOBA_BLOCK), F32)],
        compiler_params=_cparams(("arbitrary", "arbitrary")),
        name="moba",
    )(q, k_aug, vt_aug, r_mats, bias_tiles)


def _score_mats(kmean):
    nb = kmean.shape[0]
    km = kmean.reshape(nb, HEADS, HEAD_DIM).transpose(1, 2, 0)
    km = jnp.pad(km, ((0, 0), (0, 0), (0, HEAD_DIM - nb)))
    z = jnp.zeros((HEADS // 2, HEAD_DIM, HEAD_DIM), F32)
    even = jnp.concatenate([jnp.concatenate([z, km[0::2]], axis=2),
                            jnp.concatenate([z, z], axis=2)], axis=1)
    odd = jnp.concatenate([jnp.concatenate([z, z], axis=2),
                           jnp.concatenate([km[1::2], z], axis=2)], axis=1)
    return jnp.swapaxes(jnp.stack([even, odd], axis=1), -1, -2).astype(BF16)


def _memkv_kernel(m_ref, g_ref, w_ref, o_ref):
    h = _rms(m_ref[...], g_ref[...]).astype(BF16)
    o_ref[...] = jnp.dot(h, w_ref[...], preferred_element_type=F32).astype(BF16)


def _memkv(mem2d, g, w_bf):
    return pl.pallas_call(
        _memkv_kernel,
        out_shape=jax.ShapeDtypeStruct((N_MEM, 2 * XATTN_WIDTH), BF16),
        compiler_params=pltpu.CompilerParams(vmem_limit_bytes=VMEM_LIMIT),
        name="memkv",
    )(mem2d, g, w_bf)


def _merge_kernel(x_ref, oa_ref, ob_ref, gt_ref, woa_ref, wob_ref, wout_ref, nx_ref, wq_ref,
                  kv_ref, wo_ref, nf_ref, wr_ref, x2_ref, h3_ref, rt_ref, cnt_ref, run_ref):
    pa = jnp.dot(oa_ref[...], woa_ref[...], preferred_element_type=F32)
    pb = jnp.dot(ob_ref[...], wob_ref[...], preferred_element_type=F32)
    ga = gt_ref[:, 0:D_MODEL].astype(F32)
    gb = gt_ref[:, D_MODEL:2 * D_MODEL].astype(F32)
    merged = _sigmoid(ga) * pa + _sigmoid(gb) * pb
    x1 = x_ref[...] + jnp.dot(merged.astype(BF16), wout_ref[...], preferred_element_type=F32)
    h2 = _rms(x1, nx_ref[...]).astype(BF16)
    q = jnp.dot(h2, wq_ref[...], preferred_element_type=F32).astype(BF16)
    heads = []
    for h in range(XATTN_HEADS):
        sl = slice(h * XATTN_HEAD_DIM, (h + 1) * XATTN_HEAD_DIM)
        km = kv_ref[:, sl]
        vm = kv_ref[:, XATTN_WIDTH + h * XATTN_HEAD_DIM:XATTN_WIDTH + (h + 1) * XATTN_HEAD_DIM]
        lg = lax.dot_general(q[:, sl], km, (((1,), (1,)), ((), ())),
                             preferred_element_type=F32) * (XATTN_HEAD_DIM ** -0.5)
        e = jnp.exp(lg - jnp.max(lg, axis=-1, keepdims=True))
        p = e / jnp.sum(e, axis=-1, keepdims=True)
        heads.append(jnp.dot(p.astype(BF16), vm, preferred_element_type=F32))
    o = jnp.concatenate(heads, axis=-1).astype(BF16)
    x2 = x1 + jnp.dot(o, wo_ref[...], preferred_element_type=F32)
    x2_ref[...] = x2
    h3 = _rms(x2, nf_ref[...])
    _store_token_tiles(h3_ref, h3)
    lg = _bdot(h3, wr_ref[...]).T
    tm = lg.shape[1]
    row = lax.broadcasted_iota(jnp.int32, lg.shape, 0)
    big = jnp.int32(1 << 20)
    is_g = row < N_GROUPS
    gl = jnp.where(is_g, lg, -jnp.inf)
    gmax = jnp.max(gl, axis=0, keepdims=True)
    gsel = jnp.min(jnp.where(gl == gmax, row, big), axis=0, keepdims=True)
    p_top = 1.0 / jnp.sum(jnp.where(is_g, jnp.exp(gl - gmax), 0.0), axis=0, keepdims=True)
    e_id = row - N_GROUPS
    in_grp = (e_id >= gsel * EXPERTS_PER_GROUP) & (e_id < (gsel + 1) * EXPERTS_PER_GROUP)
    el = jnp.where(in_grp, lg, -jnp.inf)
    v1 = jnp.max(el, axis=0, keepdims=True)
    i1 = jnp.min(jnp.where(el == v1, row, big), axis=0, keepdims=True)
    el2 = jnp.where(row == i1, -jnp.inf, el)
    v2 = jnp.max(el2, axis=0, keepdims=True)
    i2 = jnp.min(jnp.where(el2 == v2, row, big), axis=0, keepdims=True)
    e21 = jnp.exp(v2 - v1)
    g1 = p_top / (1.0 + e21)
    g2 = p_top * e21 / (1.0 + e21)
    step = pl.program_id(0)

    @pl.when(step == 0)
    def _():
        run_ref[...] = jnp.zeros_like(run_ref)

    oh1 = jnp.where(row == i1, 1.0, 0.0)
    oh2 = jnp.where(row == i2, 1.0, 0.0)
    both = oh1 + oh2
    ri = lax.broadcasted_iota(jnp.int32, (tm, tm), 0)
    ci = lax.broadcasted_iota(jnp.int32, (tm, tm), 1)
    before = jnp.dot(both, jnp.where(ri < ci, 1.0, 0.0), preferred_element_type=F32) + run_ref[:, 0:1]
    r1 = jnp.sum(before * oh1, axis=0, keepdims=True)
    r2 = jnp.sum(before * oh2, axis=0, keepdims=True)
    run_ref[...] = run_ref[...] + jnp.sum(both, axis=1, keepdims=True)
    cnt_ref[...] = run_ref[...]
    vals = [(i1 - N_GROUPS).astype(F32), (i2 - N_GROUPS).astype(F32), g1, g2, r1, r2]
    rt = jnp.zeros_like(lg)
    for k, v in enumerate(vals):
        rt = jnp.where(row == k, v, rt)
    rt_ref[...] = rt.T


def _merge(x2d, o_a, o_b, gates, w_oa, w_ob, w_out, n_x, w_q, kv, w_o, n_f, w_r):
    T = x2d.shape[0]
    tm = TOKEN_TILE
    full = lambda a: pl.BlockSpec(a.shape, lambda i: (0,) * a.ndim, pipeline_mode=pl.Buffered(1))
    tile = lambda w: pl.BlockSpec((tm, w), lambda i: (i, 0))
    return pl.pallas_call(
        _merge_kernel,
        grid=(T // tm,),
        in_specs=[tile(D_MODEL), tile(WIDTH), tile(WIDTH), tile(GATE_COLS), full(w_oa), full(w_ob),
                  full(w_out), full(n_x), full(w_q), full(kv), full(w_o), full(n_f), full(w_r)],
        out_specs=[tile(D_MODEL), pl.BlockSpec((tm * ROW_TILES, LANES), lambda i: (i, 0)), tile(LANES),
                   pl.BlockSpec((LANES, LANES), lambda i: (0, 0))],
        out_shape=[jax.ShapeDtypeStruct((T, D_MODEL), F32),
                   jax.ShapeDtypeStruct((T * ROW_TILES, LANES), U32),
                   jax.ShapeDtypeStruct((T, LANES), F32),
                   jax.ShapeDtypeStruct((LANES, LANES), F32)],
        scratch_shapes=[pltpu.VMEM((LANES, LANES), F32)],
        compiler_params=_cparams(("arbitrary",)),
        name="merge",
    )(x2d, o_a, o_b, gates, w_oa, w_ob, w_out, n_x, w_q, kv, w_o, n_f, w_r)


def _dispatch_kernel(pos_ref, zs_ref, zv_ref, nu_ref, h_ref, xs_hbm, zero_ref, sem):
    i = pl.program_id(0)
    tm = h_ref.shape[0] // ROW_TILES

    @pl.when(i == 0)
    def _():
        zero_ref[...] = jnp.zeros_like(zero_ref)

        def clear(e):
            dst = xs_hbm.at[pl.ds(pl.multiple_of(zs_ref[e] * ROW_TILES, ROW_TILES),
                                  MOE_BLOCK * ROW_TILES), :]
            return pltpu.make_async_copy(zero_ref, dst, sem)

        for e in range(N_EXPERTS):
            @pl.when(zv_ref[e] > 0)
            def _():
                clear(e).start()

        for e in range(N_EXPERTS):
            @pl.when(zv_ref[e] > 0)
            def _():
                clear(e).wait()

        def tail(b):
            dst = xs_hbm.at[pl.ds(pl.multiple_of(b * (MOE_BLOCK * ROW_TILES), MOE_BLOCK * ROW_TILES),
                                  MOE_BLOCK * ROW_TILES), :]
            return pltpu.make_async_copy(zero_ref, dst, sem)

        n_blocks = xs_hbm.shape[0] // (MOE_BLOCK * ROW_TILES)

        def start_tail(b, c):
            tail(b).start()
            return c

        def wait_tail(b, c):
            tail(b).wait()
            return c

        lax.fori_loop(nu_ref[0], n_blocks, start_tail, 0)
        lax.fori_loop(nu_ref[0], n_blocks, wait_tail, 0)

    def row_copy(r, k):
        p = pos_ref[2 * (i * tm + r) + k]
        dst = xs_hbm.at[pl.ds(pl.multiple_of(p * ROW_TILES, ROW_TILES), ROW_TILES), :]
        return pltpu.make_async_copy(h_ref.at[pl.ds(r * ROW_TILES, ROW_TILES), :], dst, sem)

    for r in range(tm):
        row_copy(r, 0).start(priority=0)
        row_copy(r, 1).start(priority=1)
    for _ in range(tm):
        row_copy(0, 0).wait()
        row_copy(0, 1).wait()


def _dispatch(pos, zstart, zvalid, n_used, h3t, cap):
    T = h3t.shape[0] // ROW_TILES
    tm = ROW_DMA_TILE
    grid_spec = pltpu.PrefetchScalarGridSpec(
        num_scalar_prefetch=4,
        grid=(T // tm,),
        in_specs=[pl.BlockSpec((tm * ROW_TILES, LANES), lambda i, ps, zs, zv, nu: (i, 0))],
        out_specs=pl.BlockSpec(memory_space=pl.ANY),
        scratch_shapes=[pltpu.VMEM((MOE_BLOCK * ROW_TILES, LANES), U32), pltpu.SemaphoreType.DMA],
    )
    return pl.pallas_call(
        _dispatch_kernel,
        grid_spec=grid_spec,
        out_shape=jax.ShapeDtypeStruct((cap * ROW_TILES, LANES), U32),
        compiler_params=_cparams(("arbitrary",)),
        name="dispatch",
    )(pos, zstart, zvalid, n_used, h3t)


def _moe_kernel(be_ref, nu_ref, x_ref, wg_ref, wu_ref, wd_ref, o_ref):
    i = pl.program_id(0)

    @pl.when(i < nu_ref[0])
    def _():
        xb = jnp.concatenate([x.astype(BF16) for x in _load_token_tiles(x_ref, MOE_BLOCK)], axis=-1)
        gate = _bdot(xb, wg_ref[0])
        up = _bdot(xb, wu_ref[0])
        hid = gate * _sigmoid(gate) * up
        _store_token_tiles(o_ref, _bdot(hid, wd_ref[0]))

    @pl.when(i >= nu_ref[0])
    def _():
        o_ref[...] = jnp.zeros_like(o_ref)


def _moe(block_expert, n_used, xs, wg, wu, wd):
    n_blocks = xs.shape[0] // (MOE_BLOCK * ROW_TILES)
    used = lambda i, nu: jnp.minimum(i, jnp.maximum(nu[0] - 1, 0))
    grid_spec = pltpu.PrefetchScalarGridSpec(
        num_scalar_prefetch=2,
        grid=(n_blocks,),
        in_specs=[
            pl.BlockSpec((MOE_BLOCK * ROW_TILES, LANES), lambda i, be, nu: (used(i, nu), 0)),
            pl.BlockSpec((1, D_MODEL, D_EXPERT), lambda i, be, nu: (be[i], 0, 0)),
            pl.BlockSpec((1, D_MODEL, D_EXPERT), lambda i, be, nu: (be[i], 0, 0)),
            pl.BlockSpec((1, D_EXPERT, D_MODEL), lambda i, be, nu: (be[i], 0, 0)),
        ],
        out_specs=pl.BlockSpec((MOE_BLOCK * ROW_TILES, LANES), lambda i, be, nu: (i, 0)),
    )
    return pl.pallas_call(
        _moe_kernel,
        grid_spec=grid_spec,
        out_shape=jax.ShapeDtypeStruct(xs.shape, U32),
        compiler_params=_cparams(("arbitrary",)),
        name="moe",
    )(block_expert, n_used, xs, wg, wu, wd)


def _moe_plan(expert, rank, counts, n_tok):
    n_assign = n_tok * 2
    e_flat = expert.reshape(n_assign)
    rank = rank.reshape(n_assign)
    padded = (counts + MOE_BLOCK - 1) // MOE_BLOCK * MOE_BLOCK
    p_end = jnp.cumsum(padded)
    p_start = p_end - padded
    pos = (p_start[e_flat] + rank).astype(jnp.int32)
    n_blocks = (n_assign + N_EXPERTS * (MOE_BLOCK - 1) + MOE_BLOCK - 1) // MOE_BLOCK
    cap = n_blocks * MOE_BLOCK
    block_start = jnp.arange(n_blocks, dtype=jnp.int32) * MOE_BLOCK
    block_expert = jnp.minimum(jnp.sum(p_end[None, :] <= block_start[:, None], axis=1),
                               N_EXPERTS - 1).astype(jnp.int32)
    n_used = (p_end[-1] // MOE_BLOCK).astype(jnp.int32).reshape(1)
    last_block = jnp.maximum(p_end - MOE_BLOCK, 0).astype(jnp.int32)
    nonempty = (padded > 0).astype(jnp.int32)
    return pos, cap, block_expert, n_used, last_block, nonempty


def _final_kernel(pos_ref, x_ref, rt_ref, y_hbm, g_ref, o_ref, y1_ref, y2_ref, sem):
    i = pl.program_id(0)
    n = pl.num_programs(0)
    tm = x_ref.shape[0]
    slot = i % 2

    def copies(tile, r, s):
        p1 = pos_ref[2 * (tile * tm + r)]
        p2 = pos_ref[2 * (tile * tm + r) + 1]
        src = lambda p: y_hbm.at[pl.ds(pl.multiple_of(p * ROW_TILES, ROW_TILES), ROW_TILES), :]
        dst = lambda ref: ref.at[s, pl.ds(r * ROW_TILES, ROW_TILES), :]
        return (pltpu.make_async_copy(src(p1), dst(y1_ref), sem.at[s]),
                pltpu.make_async_copy(src(p2), dst(y2_ref), sem.at[s]))

    def gather(tile, s):
        for r in range(tm):
            c1, c2 = copies(tile, r, s)
            c1.start(priority=0)
            c2.start(priority=1)

    def wait_rows(s):
        for _ in range(tm):
            c1, c2 = copies(0, 0, s)
            c1.wait()
            c2.wait()

    @pl.when(i == 0)
    def _():
        gather(0, 0)

    wait_rows(slot)
    gather(jnp.minimum(i + 1, n - 1), 1 - slot)
    rt = rt_ref[...]
    g1 = rt[:, 2:3]
    g2 = rt[:, 3:4]
    y1 = jnp.concatenate(_load_token_tiles(y1_ref.at[slot], tm), axis=-1)
    y2 = jnp.concatenate(_load_token_tiles(y2_ref.at[slot], tm), axis=-1)
    x3 = x_ref[...] + y1 * g1 + y2 * g2
    o_ref[...] = _rms(x3, g_ref[...])

    @pl.when(i == n - 1)
    def _():
        wait_rows(1 - slot)


def _final(pos, x2, rt, y_sorted, g):
    T = x2.shape[0]
    tm = ROW_DMA_TILE
    grid_spec = pltpu.PrefetchScalarGridSpec(
        num_scalar_prefetch=1,
        grid=(T // tm,),
        in_specs=[
            pl.BlockSpec((tm, D_MODEL), lambda i, ps: (i, 0)),
            pl.BlockSpec((tm, LANES), lambda i, ps: (i, 0)),
            pl.BlockSpec(memory_space=pl.ANY),
            pl.BlockSpec((1, D_MODEL), lambda i, ps: (0, 0)),
        ],
        out_specs=pl.BlockSpec((tm, D_MODEL), lambda i, ps: (i, 0)),
        scratch_shapes=[pltpu.VMEM((2, tm * ROW_TILES, LANES), U32),
                        pltpu.VMEM((2, tm * ROW_TILES, LANES), U32),
                        pltpu.SemaphoreType.DMA((2,))],
    )
    return pl.pallas_call(
        _final_kernel,
        grid_spec=grid_spec,
        out_shape=jax.ShapeDtypeStruct((T, D_MODEL), F32),
        compiler_params=_cparams(("arbitrary",)),
        name="final",
    )(pos, x2, rt, y_sorted, g)


def _lora_weight(decay_up, iclr_up, gate_up):
    w = jnp.zeros((LORA_COLS, 3 * WIDTH), F32)
    w = w.at[0:DECAY_LORA, 0:WIDTH].set(decay_up)
    w = w.at[DECAY_LORA:DECAY_LORA + ICLR_LORA, WIDTH:2 * WIDTH].set(iclr_up)
    w = w.at[DECAY_LORA + ICLR_LORA:, 2 * WIDTH:].set(gate_up)
    return w


def _router_weight(w_group, w_expert):
    w = jnp.zeros((D_MODEL, LANES), F32)
    w = w.at[:, 0:N_GROUPS].set(w_group)
    return w.at[:, N_GROUPS:N_GROUPS + N_EXPERTS].set(w_expert)


def kernel(x, mem, rel_bias, mem_norm, norm_mix, w_in, tshift_mu, decay_w0, decay_up, iclr_a0,
           iclr_up, gate_up, k_k, k_a, r_k, ln_x_w, ln_x_b, w_o_rwkv, w_o_moba, w_out,
           norm_xattn, w_q_x, w_kv_x, w_o_x, norm_ffn, w_router_group, w_router_expert,
           w_exp_gate, w_exp_up, w_exp_down, norm_final):
    B, T, D = x.shape
    assert B == 1 and D == D_MODEL and T % TOKEN_TILE == 0 and T // MOBA_BLOCK <= HEAD_DIM
    assert norm_mix.shape[0] == 1
    row = lambda a: a.reshape(1, -1)
    x2d = x.reshape(T, D)
    kv = _memkv(mem.reshape(N_MEM, D), row(mem_norm), w_kv_x[0].astype(BF16))

    ur, q, gates, k_aug, vt_aug, kmean = _inproj(x2d, row(norm_mix[0]), w_in[0].astype(BF16))
    kmean = kmean.reshape(T // MOBA_BLOCK, WIDTH)
    o_a = _rwkv_core(ur, row(tshift_mu[0]), row(decay_w0[0]), row(iclr_a0[0]), row(k_k[0]),
                     row(k_a[0]), row(r_k[0]),
                     _lora_weight(decay_up[0], iclr_up[0], gate_up[0]).astype(BF16),
                     row(ln_x_w[0]), row(ln_x_b[0]))

    o_b = _moba(q, k_aug, vt_aug, _score_mats(kmean), _bias_tiles(rel_bias))

    x2, h3, rt, counts = _merge(
        x2d, o_a, o_b, gates, w_o_rwkv[0].astype(BF16), w_o_moba[0].astype(BF16),
        w_out[0].astype(BF16), row(norm_xattn[0]), w_q_x[0].astype(BF16), kv,
        w_o_x[0].astype(BF16), row(norm_ffn[0]),
        _router_weight(w_router_group[0], w_router_expert[0]).astype(BF16))

    expert = rt[:, 0:2].astype(jnp.int32)
    rank = rt[:, 4:6].astype(jnp.int32)
    pos, cap, block_expert, n_used, last_block, nonempty = _moe_plan(
        expert, rank, counts[N_GROUPS:N_GROUPS + N_EXPERTS, 0].astype(jnp.int32), T)
    x_sorted = _dispatch(pos, last_block, nonempty, n_used, h3, cap)
    y_sorted = _moe(block_expert, n_used, x_sorted, w_exp_gate[0], w_exp_up[0], w_exp_down[0])
    out = _final(pos, x2, rt, y_sorted, row(norm_final))
    return out.reshape(B, T, D)
```

```python
import math

import jax
import jax.numpy as jnp
from jax import lax
from jax.experimental import pallas as pl
from jax.experimental.pallas import tpu as pltpu

F32 = jnp.float32
BF16 = jnp.bfloat16

D_MODEL = 1024
N_MEM = 256
NORM_EPS = 1e-6
NEG_INF = -1e30

HEADS = 8
HEAD_DIM = 64
WIDTH = HEADS * HEAD_DIM
DECAY_LORA = 64
ICLR_LORA = 64
GATE_LORA = 128
LORA_COLS = DECAY_LORA + ICLR_LORA + GATE_LORA
RWKV_COLS = 3 * WIDTH + LORA_COLS
QKV_COLS = 3 * WIDTH
GATE_COLS = 2 * D_MODEL
IN_COLS = RWKV_COLS + QKV_COLS + GATE_COLS
LN_X_EPS = 64e-5
KK_EPS = 1e-12

MOBA_BLOCK = 256
MOBA_TOP = 3
REL_BUCKETS = 32
REL_MAX_DISTANCE = 4096
N_BIAS_TILES = 14

XATTN_HEADS = 4
XATTN_HEAD_DIM = 128
XATTN_WIDTH = XATTN_HEADS * XATTN_HEAD_DIM

N_GROUPS = 4
EXPERTS_PER_GROUP = 8
N_EXPERTS = N_GROUPS * EXPERTS_PER_GROUP
D_EXPERT = 512
MOE_BLOCK = 256

TOKEN_TILE = 512
ROW_DMA_TILE = 256
VT_ROWS = HEAD_DIM + 16
CHUNK = 64
CHUNKS_PER_STEP = 4
LANES = 128
VMEM_LIMIT = 48 * 1024 * 1024

LOG2E = math.log2(math.e)


def _cparams(sem):
    return pltpu.CompilerParams(dimension_semantics=sem, vmem_limit_bytes=VMEM_LIMIT)


def _bdot(a, b):
    return jnp.dot(a.astype(BF16), b.astype(BF16), preferred_element_type=F32)


def _bdot_nt(a, b):
    return lax.dot_general(a.astype(BF16), b.astype(BF16), (((1,), (1,)), ((), ())),
                           preferred_element_type=F32)


def _bdot_tn(a, b):
    return lax.dot_general(a.astype(BF16), b.astype(BF16), (((0,), (0,)), ((), ())),
                           preferred_element_type=F32)


def _rms(x, g):
    return x * lax.rsqrt(jnp.mean(x * x, axis=-1, keepdims=True) + NORM_EPS) * g


def _sigmoid(x):
    return 1.0 / (1.0 + jnp.exp(-x))


ROW_TILES = D_MODEL // (2 * LANES)
U32 = jnp.uint32
HI_MASK = 0xFFFF0000


def _bits(x):
    return pltpu.bitcast(x.astype(BF16).astype(F32), U32)


def _store_token_tiles(ref, x):
    n = x.shape[0]
    for s in range(ROW_TILES):
        lo = x[:, 2 * s * LANES:(2 * s + 1) * LANES]
        hi = x[:, (2 * s + 1) * LANES:(2 * s + 2) * LANES]
        word = (_bits(hi) & U32(HI_MASK)) | lax.shift_right_logical(_bits(lo), U32(16))
        ref[pl.ds(s, n, stride=ROW_TILES), :] = word


def _load_token_tiles(ref, n):
    out = []
    for s in range(ROW_TILES):
        word = ref[pl.ds(s, n, stride=ROW_TILES), :]
        out.append(pltpu.bitcast(lax.shift_left(word, U32(16)), F32))
        out.append(pltpu.bitcast(word & U32(HI_MASK), F32))
    return out


def _inproj_kernel(x_ref, g_ref, w_ref, ur_ref, q_ref, gate_ref, ka_ref, vt_ref, km_ref):
    i = pl.program_id(0)
    B = MOBA_BLOCK
    h = _rms(x_ref[...], g_ref[...]).astype(BF16)
    tm = h.shape[0]
    step = 256
    for c0 in range(0, RWKV_COLS, step):
        ur_ref[:, c0:c0 + step] = jnp.dot(h, w_ref[:, c0:c0 + step], preferred_element_type=F32)
    for c0 in range(0, WIDTH, step):
        o = jnp.dot(h, w_ref[:, RWKV_COLS + c0:RWKV_COLS + c0 + step], preferred_element_type=F32)
        q_ref[:, c0:c0 + step] = (o * (HEAD_DIM ** -0.5 * LOG2E)).astype(BF16)
    lane = lax.broadcasted_iota(jnp.int32, (B, LANES), 1)
    ones = jnp.ones((VT_ROWS - HEAD_DIM, B), F32)
    for c0 in range(0, WIDTH, step):
        kcol = RWKV_COLS + WIDTH + c0
        k = jnp.dot(h, w_ref[:, kcol:kcol + step], preferred_element_type=F32).astype(BF16)
        v = jnp.dot(h, w_ref[:, kcol + WIDTH:kcol + WIDTH + step], preferred_element_type=F32)
        v = v.astype(BF16).astype(F32)
        for pp in range(step // LANES):
            pair = c0 // LANES + pp
            for jj in range(tm // B):
                rows = slice(jj * B, (jj + 1) * B)
                kblk = k[rows, pp * LANES:(pp + 1) * LANES]
                km_ref[0, jj:jj + 1, pair * LANES:(pair + 1) * LANES] = jnp.mean(
                    kblk.astype(F32), axis=0, keepdims=True)
                v_t = v[rows, pp * LANES:(pp + 1) * LANES].T
                for h2 in range(2):
                    off = HEAD_DIM * (1 - h2)
                    inr = (lane >= off) & (lane < off + HEAD_DIM)
                    hot = jnp.where(lane == off + i * (tm // B) + jj, 1.0, 0.0).astype(BF16)
                    ka_ref[2 * pair + h2, rows, :] = jnp.where(inr, hot, kblk)
                    tile = jnp.concatenate([v_t[h2 * HEAD_DIM:(h2 + 1) * HEAD_DIM], ones], axis=0)
                    vt_ref[2 * pair + h2, jj] = tile.astype(BF16)
    base = RWKV_COLS + QKV_COLS
    for c0 in range(0, GATE_COLS, step):
        o = jnp.dot(h, w_ref[:, base + c0:base + c0 + step], preferred_element_type=F32)
        gate_ref[:, c0:c0 + step] = o.astype(BF16)


def _inproj(x2d, g, w_bf):
    T = x2d.shape[0]
    tm = TOKEN_TILE
    bpt = tm // MOBA_BLOCK
    return pl.pallas_call(
        _inproj_kernel,
        grid=(T // tm,),
        in_specs=[
            pl.BlockSpec((tm, D_MODEL), lambda i: (i, 0)),
            pl.BlockSpec((1, D_MODEL), lambda i: (0, 0)),
            pl.BlockSpec((D_MODEL, IN_COLS), lambda i: (0, 0), pipeline_mode=pl.Buffered(1)),
        ],
        out_specs=[
            pl.BlockSpec((tm, RWKV_COLS), lambda i: (i, 0)),
            pl.BlockSpec((tm, WIDTH), lambda i: (i, 0)),
            pl.BlockSpec((tm, GATE_COLS), lambda i: (i, 0)),
            pl.BlockSpec((HEADS, tm, LANES), lambda i: (0, i, 0)),
            pl.BlockSpec((HEADS, bpt, VT_ROWS, MOBA_BLOCK), lambda i: (0, i, 0, 0)),
            pl.BlockSpec((1, bpt, WIDTH), lambda i: (i, 0, 0)),
        ],
        out_shape=[
            jax.ShapeDtypeStruct((T, RWKV_COLS), F32),
            jax.ShapeDtypeStruct((T, WIDTH), BF16),
            jax.ShapeDtypeStruct((T, GATE_COLS), BF16),
            jax.ShapeDtypeStruct((HEADS, T, LANES), BF16),
            jax.ShapeDtypeStruct((HEADS, T // MOBA_BLOCK, VT_ROWS, MOBA_BLOCK), BF16),
            jax.ShapeDtypeStruct((T // tm, bpt, WIDTH), F32),
        ],
        compiler_params=_cparams(("parallel",)),
        name="inproj",
    )(x2d, g, w_bf)


def _head_ones():
    r = lax.broadcasted_iota(jnp.int32, (LANES, LANES), 0) // HEAD_DIM
    c = lax.broadcasted_iota(jnp.int32, (LANES, LANES), 1) // HEAD_DIM
    return jnp.where(r == c, 1.0, 0.0).astype(BF16)


def _head_sum(x, ones):
    parts = [_bdot(x[:, c:c + LANES], ones) for c in range(0, WIDTH, LANES)]
    return jnp.concatenate(parts, axis=-1)


def _rwkv_prep_values(u_ref, up_ref, mu_ref, w0_ref, a0_ref, kk_ref, ka_ref, rk_ref, wl_ref):
    i = pl.program_id(0)
    u = u_ref[...]
    tm = u.shape[0]
    prev_last = up_ref[7:8, :] * jnp.where(i > 0, 1.0, 0.0)
    rolled = pltpu.roll(u, 1, 0)
    row = lax.broadcasted_iota(jnp.int32, u.shape, 0)
    u_prev = jnp.where(row == 0, prev_last, rolled)
    u = u + mu_ref[...] * (u_prev - u)
    r = u[:, 0:WIDTH]
    k = u[:, WIDTH:2 * WIDTH]
    v = u[:, 2 * WIDTH:3 * WIDTH]
    lo = u[:, 3 * WIDTH:3 * WIDTH + LORA_COLS]
    lane = lax.broadcasted_iota(jnp.int32, lo.shape, 1)
    act = jnp.where(lane < DECAY_LORA, jnp.tanh(lo),
                    jnp.where(lane < DECAY_LORA + ICLR_LORA, lo, _sigmoid(lo)))
    up = _bdot(act, wl_ref[...])
    z = -(w0_ref[...] + up[:, 0:WIDTH])
    softplus = jnp.maximum(z, 0.0) + jnp.log(1.0 + jnp.exp(-jnp.abs(z)))
    w_log = -softplus - 0.5
    lw = -jnp.exp(w_log)
    iclr = _sigmoid(a0_ref[...] + up[:, WIDTH:2 * WIDTH])
    g = up[:, 2 * WIDTH:3 * WIDTH]
    ones = _head_ones()
    kk = k * kk_ref[...]
    kk = kk * lax.rsqrt(_head_sum(kk * kk, ones) + KK_EPS)
    k2 = k * (1.0 + (iclr - 1.0) * ka_ref[...])
    bonus = _head_sum(r * k2 * rk_ref[...], ones) * v
    return r, lw, k2, v, -kk, kk * iclr, g, bonus


def _rwkv_core_kernel(u_ref, up_ref, mu_ref, w0_ref, a0_ref, kk_ref, ka_ref, rk_ref, wl_ref,
                      lnw_ref, lnb_ref, o_ref, s_ref):
    c = pl.program_id(0)

    @pl.when(c == 0)
    def _():
        s_ref[...] = jnp.zeros_like(s_ref)

    C = CHUNK
    G = CHUNKS_PER_STEP
    ri = lax.broadcasted_iota(jnp.int32, (C, C), 0)
    ci = lax.broadcasted_iota(jnp.int32, (C, C), 1)
    rg = lax.broadcasted_iota(jnp.int32, (G * C, G * C), 0)
    cg = lax.broadcasted_iota(jnp.int32, (G * C, G * C), 1)
    r_in, lw, k_in, v_all, a_in, b_in, g_in, bonus = _rwkv_prep_values(
        u_ref, up_ref, mu_ref, w0_ref, a0_ref, kk_ref, ka_ref, rk_ref, wl_ref)
    tri = jnp.where((rg >= cg) & (rg // C == cg // C), 1.0, 0.0).astype(BF16)
    lw_hi = lw.astype(BF16)
    lw_r1 = lw - lw_hi.astype(F32)
    lw_mid = lw_r1.astype(BF16)
    lw_lo = (lw_r1 - lw_mid.astype(F32)).astype(BF16)
    cum = (jnp.dot(tri, lw_hi, preferred_element_type=F32)
           + jnp.dot(tri, lw_mid, preferred_element_type=F32)
           + jnp.dot(tri, lw_lo, preferred_element_type=F32))
    lam = jnp.exp(cum)
    inv_lam = jnp.exp(-cum)
    r_t = r_in * lam
    a_t = a_in * jnp.exp(cum - lw)
    b_t = b_in * inv_lam
    k_t = k_in * inv_lam
    tots = [cum[g * C + C - 1:g * C + C, :] for g in range(G)]
    rowg = lax.broadcasted_iota(jnp.int32, cum.shape, 0) // C
    tot = tots[G - 1]
    for g in range(G - 2, -1, -1):
        tot = jnp.where(rowg == g, tots[g], tot)
    rest = jnp.exp(tot - cum)
    b_h = b_in * rest
    k_h = k_in * rest
    lam_c = [jnp.exp(t) for t in tots]
    eye = jnp.where(ri == ci, 1.0, 0.0)
    H = range(G * HEADS)
    sls = [slice(h * HEAD_DIM, (h + 1) * HEAD_DIM) for h in range(HEADS)]
    bf = lambda x: x.astype(BF16)
    part = lambda x, i: x[(i // HEADS) * C:(i // HEADS + 1) * C, sls[i % HEADS]]
    at = [bf(part(a_t, i)) for i in H]
    rt = [part(r_t, i) for i in H]
    bt = [bf(part(b_t, i)) for i in H]
    kt = [bf(part(k_t, i)) for i in H]
    bh = [bf(part(b_h, i)) for i in H]
    kh = [bf(part(k_h, i)) for i in H]
    vv = [bf(part(v_all, i)) for i in H]
    ci2 = lax.broadcasted_iota(jnp.int32, (C, 2 * C), 1)
    ri2 = lax.broadcasted_iota(jnp.int32, (C, 2 * C), 0)
    cm2 = jnp.where(ci2 >= C, ci2 - C, ci2)
    left = ci2 < C
    ar = [jnp.concatenate([at[h], bf(rt[h])], axis=0) for h in H]
    bk = [jnp.concatenate([bt[h], kt[h]], axis=0) for h in H]
    g = [_bdot_nt(ar[h], bk[h]) for h in H]
    top = [jnp.where(ri2 > cm2, g[h][0:C], 0.0) for h in H]
    bot = [bf(jnp.where(ri2 >= cm2, g[h][C:2 * C], 0.0)) for h in H]
    a_ab = [top[h][:, 0:C] for h in H]
    akv = [_bdot(top[h][:, C:2 * C], vv[h]) for h in H]
    z = [jnp.concatenate([a_ab[h], eye], axis=1) for h in H]
    for _ in range(6):
        z = [_bdot(z[h][:, 0:C], z[h]) + jnp.where(left, 0.0, z[h]) for h in H]
    tinv = [bf(z[h][:, C:2 * C]) for h in H]
    wu = [_bdot(tinv[h], jnp.concatenate([at[h], bf(akv[h])], axis=1)) for h in H]
    w_m = [bf(wu[h][:, 0:C]) for h in H]
    uv = [jnp.concatenate([bf(wu[h][:, C:2 * C]), vv[h]], axis=0) for h in H]
    q_m = [rt[h] + _bdot(bot[h][:, 0:C], w_m[h]) for h in H]
    y0 = [_bdot(bot[h], uv[h]) for h in H]
    m_k = [_bdot_tn(w_m[h], bh[h]) for h in H]
    n0 = [_bdot_tn(uv[h], jnp.concatenate([bh[h], kh[h]], axis=0)) for h in H]
    state = [s_ref[h] for h in range(HEADS)]
    blocks = []
    for g in range(G):
        outs = []
        for h in range(HEADS):
            i = g * HEADS + h
            y = _bdot_nt(q_m[i], state[h]) + y0[i]
            state[h] = state[h] * lam_c[g][:, sls[h]] + _bdot(state[h], m_k[i]) + n0[i]
            mean = jnp.mean(y, axis=-1, keepdims=True)
            var = jnp.mean(jnp.square(y - mean), axis=-1, keepdims=True)
            outs.append((y - mean) * lax.rsqrt(var + LN_X_EPS))
        blocks.append(jnp.concatenate(outs, axis=-1))
    for h in range(HEADS):
        s_ref[h] = state[h]
    yn = jnp.concatenate(blocks, axis=0)
    yn = yn * lnw_ref[...] + lnb_ref[...] + bonus
    o_ref[...] = (yn * g_in).astype(BF16)


def _rwkv_core(ur, mu, w0, a0, k_k, k_a, r_k, w_lora, ln_w, ln_b):
    T = ur.shape[0]
    rows = CHUNK * CHUNKS_PER_STEP
    row = lambda w: pl.BlockSpec((1, w), lambda c: (0, 0))
    return pl.pallas_call(
        _rwkv_core_kernel,
        grid=(T // rows,),
        in_specs=[
            pl.BlockSpec((rows, RWKV_COLS), lambda c: (c, 0)),
            pl.BlockSpec((8, RWKV_COLS), lambda c: (jnp.maximum(c * (rows // 8) - 1, 0), 0)),
            row(RWKV_COLS), row(WIDTH), row(WIDTH), row(WIDTH), row(WIDTH), row(WIDTH),
            pl.BlockSpec((LORA_COLS, 3 * WIDTH), lambda c: (0, 0)),
            row(WIDTH), row(WIDTH),
        ],
        out_specs=pl.BlockSpec((rows, WIDTH), lambda c: (c, 0)),
        out_shape=jax.ShapeDtypeStruct((T, WIDTH), BF16),
        scratch_shapes=[pltpu.VMEM((HEADS, HEAD_DIM, HEAD_DIM), F32)],
        compiler_params=_cparams(("arbitrary",)),
        name="rwkv_core",
    )(ur, ur, mu, w0, a0, k_k, k_a, r_k, w_lora, ln_w, ln_b)


def _t5_bucket(dist):
    n = jnp.maximum(dist, 0)
    max_exact = REL_BUCKETS // 2
    nf = jnp.maximum(n, max_exact).astype(jnp.float32)
    large = max_exact + (jnp.log(nf / max_exact) / math.log(REL_MAX_DISTANCE / max_exact)
                         * (REL_BUCKETS - max_exact)).astype(jnp.int32)
    large = jnp.minimum(large, REL_BUCKETS - 1)
    return jnp.where(n < max_exact, n, large)


def _bucket_tiles():
    i = jnp.arange(MOBA_BLOCK)[None, :]
    j = jnp.arange(MOBA_BLOCK)[:, None]
    d = jnp.arange(N_BIAS_TILES + 1)[:, None, None]
    dist = d * MOBA_BLOCK + i - j
    bucket = _t5_bucket(dist)
    bucket = jnp.where(d == N_BIAS_TILES - 1, REL_BUCKETS - 1, bucket)
    return jnp.where((dist < 0) | (d == N_BIAS_TILES), -1, bucket).astype(jnp.int32)


BIAS_ROWS = 16


def _bias_tiles_kernel(idx_ref, rb_ref, o_ref):
    def rows(c, carry):
        r0 = pl.multiple_of(c * BIAS_ROWS, BIAS_ROWS)
        idx = idx_ref[0, pl.ds(r0, BIAS_ROWS), :]
        acc = [jnp.where(idx < 0, NEG_INF, 0.0)] * HEADS
        for bkt in range(REL_BUCKETS):
            hit = idx == bkt
            acc = [jnp.where(hit, rb_ref[bkt, h] * LOG2E, acc[h]) for h in range(HEADS)]
        for h in range(HEADS):
            o_ref[h, 0, pl.ds(r0, BIAS_ROWS), :] = acc[h]
        return carry

    lax.fori_loop(0, MOBA_BLOCK // BIAS_ROWS, rows, 0)


def _bias_tiles(rel_bias):
    idx = _bucket_tiles()
    n = N_BIAS_TILES + 1
    return pl.pallas_call(
        _bias_tiles_kernel,
        grid=(n,),
        in_specs=[
            pl.BlockSpec((1, MOBA_BLOCK, MOBA_BLOCK), lambda d: (d, 0, 0)),
            pl.BlockSpec(memory_space=pltpu.SMEM),
        ],
        out_specs=pl.BlockSpec((HEADS, 1, MOBA_BLOCK, MOBA_BLOCK), lambda d: (0, d, 0, 0)),
        out_shape=jax.ShapeDtypeStruct((HEADS, n, MOBA_BLOCK, MOBA_BLOCK), F32),
        compiler_params=_cparams(("parallel",)),
        name="bias_tiles",
    )(idx, rel_bias)


def _moba_kernel(q_ref, ka_ref, vt_ref, r_ref, bias_ref, o_ref, s_ref, p_ref, acc_ref):
    qb = pl.program_id(1)
    B = MOBA_BLOCK
    q_tr = q_ref[...].astype(F32).T
    q_tr_bf = q_tr.astype(BF16)
    blk = lax.broadcasted_iota(jnp.int32, (HEAD_DIM, B), 0)
    big = jnp.int32(1 << 20)
    q_t, q_own_t = [], []
    for h2 in range(2):
        off = HEAD_DIM * (1 - h2)
        valid = blk < qb
        sc = jnp.dot(r_ref[0, h2, off:off + HEAD_DIM, :], q_tr_bf, preferred_element_type=F32)
        s = jnp.where(valid, sc, NEG_INF)
        sel = jnp.zeros((HEAD_DIM, B), jnp.bool_)
        for _ in range(MOBA_TOP):
            m = jnp.max(s, axis=0, keepdims=True)
            idx = jnp.min(jnp.where(s == m, blk, big), axis=0, keepdims=True)
            pick = blk == idx
            sel = jnp.logical_or(sel, pick)
            s = jnp.where(pick, -jnp.inf, s)
        sel = jnp.logical_and(sel, valid)
        choice = jnp.where(sel, 0.0, NEG_INF)
        q_head = q_tr[h2 * HEAD_DIM:(h2 + 1) * HEAD_DIM]
        parts = [q_head, choice] if h2 == 0 else [choice, q_head]
        own = [q_head, jnp.zeros_like(choice)] if h2 == 0 else [jnp.zeros_like(choice), q_head]
        q_t.append(jnp.concatenate(parts, axis=0).astype(BF16))
        q_own_t.append(jnp.concatenate(own, axis=0).astype(BF16))

    carry = []
    for h2 in range(2):
        k_own = ka_ref[h2, pl.ds(pl.multiple_of(qb * B, B), B), :]
        s0 = jnp.dot(k_own, q_own_t[h2], preferred_element_type=F32) + bias_ref[h2, 0]
        m0 = jnp.max(s0, axis=0, keepdims=True)
        p0 = jnp.exp2(s0 - m0).astype(BF16)
        carry += [m0, jnp.dot(vt_ref[h2, qb], p0, preferred_element_type=F32)]

    n_tiles = (qb + 1) // 2
    last = jnp.maximum(n_tiles - 1, 0)
    for h2 in range(2):
        acc_ref[h2] = carry[2 * h2 + 1]

    def trip(t, w, stats, scores=True, softmax=True, accumulate=True):
        r = 1 - w
        ok = (t >= 1) & (t <= n_tiles)
        d_a = jnp.where(ok, jnp.clip(qb - 2 * (t - 1), 0, N_BIAS_TILES - 1), N_BIAS_TILES)
        d_b = jnp.where(ok, jnp.clip(qb - 2 * (t - 1) - 1, 0, N_BIAS_TILES - 1), N_BIAS_TILES)
        start = pl.multiple_of(jnp.clip(t, 0, last) * (2 * B), 2 * B)
        v_a = 2 * jnp.clip(t - 2, 0, last)
        v_b = v_a + 1
        out = []
        for h2 in range(2):
            m_prev, alpha_p = stats[2 * h2], stats[2 * h2 + 1]
            if accumulate:
                acc_ref[h2] = (acc_ref[h2] * alpha_p
                               + jnp.dot(vt_ref[h2, v_a], p_ref[r, h2, 0:B], preferred_element_type=F32)
                               + jnp.dot(vt_ref[h2, v_b], p_ref[r, h2, B:2 * B],
                                         preferred_element_type=F32))
            if softmax:
                s_a = s_ref[r, h2, 0:B] + bias_ref[h2, d_a]
                s_b = s_ref[r, h2, B:2 * B] + bias_ref[h2, d_b]
                m_new = jnp.maximum(m_prev, jnp.max(jnp.maximum(s_a, s_b), axis=0, keepdims=True))
                out += [m_new, jnp.exp2(m_prev - m_new)]
                p_ref[w, h2, 0:B] = jnp.exp2(s_a - m_new).astype(BF16)
                p_ref[w, h2, B:2 * B] = jnp.exp2(s_b - m_new).astype(BF16)
            else:
                out += [m_prev, alpha_p]
            if scores:
                s_ref[w, h2] = jnp.dot(ka_ref[h2, pl.ds(start, 2 * B), :], q_t[h2],
                                       preferred_element_type=F32)
        return out

    def body(u, stats):
        stats = trip(2 * u, 0, list(stats))
        return tuple(trip(2 * u + 1, 1, stats))

    stats = []
    for h2 in range(2):
        stats += [carry[2 * h2], jnp.ones_like(carry[2 * h2])]
    stats = trip(0, 0, stats, softmax=False, accumulate=False)
    stats = trip(1, 1, stats, accumulate=False)
    stats = lax.fori_loop(1, (n_tiles + 2) // 2, body, tuple(stats))

    @pl.when(n_tiles % 2 == 1)
    def _():
        trip(n_tiles + 1, 0, list(stats), scores=False, softmax=False)
    outs = [acc_ref[h2, 0:HEAD_DIM] / acc_ref[h2, HEAD_DIM:HEAD_DIM + 1] for h2 in range(2)]
    o_ref[...] = jnp.concatenate(outs, axis=0).T.astype(BF16)


def _moba(q, k_aug, vt_aug, r_mats, bias_tiles):
    T = q.shape[0]
    nb = T // MOBA_BLOCK
    npair = HEADS // 2
    once = pl.Buffered(1)
    return pl.pallas_call(
        _moba_kernel,
        grid=(npair, nb),
        in_specs=[
            pl.BlockSpec((MOBA_BLOCK, LANES), lambda p, qb: (qb, p)),
            pl.BlockSpec((2, T, LANES), lambda p, qb: (p, 0, 0), pipeline_mode=once),
            pl.BlockSpec((2, nb, VT_ROWS, MOBA_BLOCK), lambda p, qb: (p, 0, 0, 0), pipeline_mode=once),
            pl.BlockSpec((1, 2, LANES, LANES), lambda p, qb: (p, 0, 0, 0)),
            pl.BlockSpec((2, N_BIAS_TILES + 1, MOBA_BLOCK, MOBA_BLOCK), lambda p, qb: (p, 0, 0, 0),
                         pipeline_mode=once),
        ],
        out_specs=pl.BlockSpec((MOBA_BLOCK, LANES), lambda p, qb: (qb, p)),
        out_shape=jax.ShapeDtypeStruct((T, WIDTH), BF16),
        scratch_shapes=[pltpu.VMEM((2, 2, 2 * MOBA_BLOCK, MOBA_BLOCK), F32),
                        pltpu.VMEM((2, 2, 2 * MOBA_BLOCK, MOBA_BLOCK), BF16),
                        pltpu.VMEM((2, VT_ROWS, MOBA_BLOCK), F32)],
        compiler_params=_cparams(("arbitrary", "arbitrary")),
        name="moba",
    )(q, k_aug, vt_aug, r_mats, bias_tiles)


def _score_mats(kmean):
    nb = kmean.shape[0]
    km = kmean.reshape(nb, HEADS, HEAD_DIM).transpose(1, 2, 0)
    km = jnp.pad(km, ((0, 0), (0, 0), (0, HEAD_DIM - nb)))
    z = jnp.zeros((HEADS // 2, HEAD_DIM, HEAD_DIM), F32)
    even = jnp.concatenate([jnp.concatenate([z, km[0::2]], axis=2),
                            jnp.concatenate([z, z], axis=2)], axis=1)
    odd = jnp.concatenate([jnp.concatenate([z, z], axis=2),
                           jnp.concatenate([km[1::2], z], axis=2)], axis=1)
    return jnp.swapaxes(jnp.stack([even, odd], axis=1), -1, -2).astype(BF16)


def _memkv_kernel(m_ref, g_ref, w_ref, o_ref):
    h = _rms(m_ref[...], g_ref[...]).astype(BF16)
    o_ref[...] = jnp.dot(h, w_ref[...], preferred_element_type=F32).astype(BF16)


def _memkv(mem2d, g, w_bf):
    return pl.pallas_call(
        _memkv_kernel,
        out_shape=jax.ShapeDtypeStruct((N_MEM, 2 * XATTN_WIDTH), BF16),
        compiler_params=pltpu.CompilerParams(vmem_limit_bytes=VMEM_LIMIT),
        name="memkv",
    )(mem2d, g, w_bf)


def _merge_kernel(x_ref, oa_ref, ob_ref, gt_ref, woa_ref, wob_ref, wout_ref, nx_ref, wq_ref,
                  kv_ref, wo_ref, nf_ref, wr_ref, x2_ref, h3_ref, rt_ref, cnt_ref, run_ref):
    pa = jnp.dot(oa_ref[...], woa_ref[...], preferred_element_type=F32)
    pb = jnp.dot(ob_ref[...], wob_ref[...], preferred_element_type=F32)
    ga = gt_ref[:, 0:D_MODEL].astype(F32)
    gb = gt_ref[:, D_MODEL:2 * D_MODEL].astype(F32)
    merged = _sigmoid(ga) * pa + _sigmoid(gb) * pb
    x1 = x_ref[...] + jnp.dot(merged.astype(BF16), wout_ref[...], preferred_element_type=F32)
    h2 = _rms(x1, nx_ref[...]).astype(BF16)
    q = jnp.dot(h2, wq_ref[...], preferred_element_type=F32).astype(BF16)
    heads = []
    for h in range(XATTN_HEADS):
        sl = slice(h * XATTN_HEAD_DIM, (h + 1) * XATTN_HEAD_DIM)
        km = kv_ref[:, sl]
        vm = kv_ref[:, XATTN_WIDTH + h * XATTN_HEAD_DIM:XATTN_WIDTH + (h + 1) * XATTN_HEAD_DIM]
        lg = lax.dot_general(q[:, sl], km, (((1,), (1,)), ((), ())),
                             preferred_element_type=F32) * (XATTN_HEAD_DIM ** -0.5)
        e = jnp.exp(lg - jnp.max(lg, axis=-1, keepdims=True))
        p = e / jnp.sum(e, axis=-1, keepdims=True)
        heads.append(jnp.dot(p.astype(BF16), vm, preferred_element_type=F32))
    o = jnp.concatenate(heads, axis=-1).astype(BF16)
    x2 = x1 + jnp.dot(o, wo_ref[...], preferred_element_type=F32)
    x2_ref[...] = x2
    h3 = _rms(x2, nf_ref[...])
    _store_token_tiles(h3_ref, h3)
    lg = _bdot(h3, wr_ref[...]).T
    tm = lg.shape[1]
    row = lax.broadcasted_iota(jnp.int32, lg.shape, 0)
    big = jnp.int32(1 << 20)
    is_g = row < N_GROUPS
    gl = jnp.where(is_g, lg, -jnp.inf)
    gmax = jnp.max(gl, axis=0, keepdims=True)
    gsel = jnp.min(jnp.where(gl == gmax, row, big), axis=0, keepdims=True)
    p_top = 1.0 / jnp.sum(jnp.where(is_g, jnp.exp(gl - gmax), 0.0), axis=0, keepdims=True)
    e_id = row - N_GROUPS
    in_grp = (e_id >= gsel * EXPERTS_PER_GROUP) & (e_id < (gsel + 1) * EXPERTS_PER_GROUP)
    el = jnp.where(in_grp, lg, -jnp.inf)
    v1 = jnp.max(el, axis=0, keepdims=True)
    i1 = jnp.min(jnp.where(el == v1, row, big), axis=0, keepdims=True)
    el2 = jnp.where(row == i1, -jnp.inf, el)
    v2 = jnp.max(el2, axis=0, keepdims=True)
    i2 = jnp.min(jnp.where(el2 == v2, row, big), axis=0, keepdims=True)
    e21 = jnp.exp(v2 - v1)
    g1 = p_top / (1.0 + e21)
    g2 = p_top * e21 / (1.0 + e21)
    step = pl.program_id(0)

    @pl.when(step == 0)
    def _():
        run_ref[...] = jnp.zeros_like(run_ref)

    oh1 = jnp.where(row == i1, 1.0, 0.0)
    oh2 = jnp.where(row == i2, 1.0, 0.0)
    both = oh1 + oh2
    ri = lax.broadcasted_iota(jnp.int32, (tm, tm), 0)
    ci = lax.broadcasted_iota(jnp.int32, (tm, tm), 1)
    before = jnp.dot(both, jnp.where(ri < ci, 1.0, 0.0), preferred_element_type=F32) + run_ref[:, 0:1]
    r1 = jnp.sum(before * oh1, axis=0, keepdims=True)
    r2 = jnp.sum(before * oh2, axis=0, keepdims=True)
    run_ref[...] = run_ref[...] + jnp.sum(both, axis=1, keepdims=True)
    cnt_ref[...] = run_ref[...]
    vals = [(i1 - N_GROUPS).astype(F32), (i2 - N_GROUPS).astype(F32), g1, g2, r1, r2]
    rt = jnp.zeros_like(lg)
    for k, v in enumerate(vals):
        rt = jnp.where(row == k, v, rt)
    rt_ref[...] = rt.T


def _merge(x2d, o_a, o_b, gates, w_oa, w_ob, w_out, n_x, w_q, kv, w_o, n_f, w_r):
    T = x2d.shape[0]
    tm = TOKEN_TILE
    full = lambda a: pl.BlockSpec(a.shape, lambda i: (0,) * a.ndim, pipeline_mode=pl.Buffered(1))
    tile = lambda w: pl.BlockSpec((tm, w), lambda i: (i, 0))
    return pl.pallas_call(
        _merge_kernel,
        grid=(T // tm,),
        in_specs=[tile(D_MODEL), tile(WIDTH), tile(WIDTH), tile(GATE_COLS), full(w_oa), full(w_ob),
                  full(w_out), full(n_x), full(w_q), full(kv), full(w_o), full(n_f), full(w_r)],
        out_specs=[tile(D_MODEL), pl.BlockSpec((tm * ROW_TILES, LANES), lambda i: (i, 0)), tile(LANES),
                   pl.BlockSpec((LANES, LANES), lambda i: (0, 0))],
        out_shape=[jax.ShapeDtypeStruct((T, D_MODEL), F32),
                   jax.ShapeDtypeStruct((T * ROW_TILES, LANES), U32),
                   jax.ShapeDtypeStruct((T, LANES), F32),
                   jax.ShapeDtypeStruct((LANES, LANES), F32)],
        scratch_shapes=[pltpu.VMEM((LANES, LANES), F32)],
        compiler_params=_cparams(("arbitrary",)),
        name="merge",
    )(x2d, o_a, o_b, gates, w_oa, w_ob, w_out, n_x, w_q, kv, w_o, n_f, w_r)


def _dispatch_kernel(pos_ref, zs_ref, zv_ref, nu_ref, h_ref, xs_hbm, zero_ref, sem):
    i = pl.program_id(0)
    tm = h_ref.shape[0] // ROW_TILES

    @pl.when(i == 0)
    def _():
        zero_ref[...] = jnp.zeros_like(zero_ref)

        def clear(e):
            dst = xs_hbm.at[pl.ds(pl.multiple_of(zs_ref[e] * ROW_TILES, ROW_TILES),
                                  MOE_BLOCK * ROW_TILES), :]
            return pltpu.make_async_copy(zero_ref, dst, sem)

        for e in range(N_EXPERTS):
            @pl.when(zv_ref[e] > 0)
            def _():
                clear(e).start()

        for e in range(N_EXPERTS):
            @pl.when(zv_ref[e] > 0)
            def _():
                clear(e).wait()

        def tail(b):
            dst = xs_hbm.at[pl.ds(pl.multiple_of(b * (MOE_BLOCK * ROW_TILES), MOE_BLOCK * ROW_TILES),
                                  MOE_BLOCK * ROW_TILES), :]
            return pltpu.make_async_copy(zero_ref, dst, sem)

        n_blocks = xs_hbm.shape[0] // (MOE_BLOCK * ROW_TILES)

        def start_tail(b, c):
            tail(b).start()
            return c

        def wait_tail(b, c):
            tail(b).wait()
            return c

        lax.fori_loop(nu_ref[0], n_blocks, start_tail, 0)
        lax.fori_loop(nu_ref[0], n_blocks, wait_tail, 0)

    def row_copy(r, k):
        p = pos_ref[2 * (i * tm + r) + k]
        dst = xs_hbm.at[pl.ds(pl.multiple_of(p * ROW_TILES, ROW_TILES), ROW_TILES), :]
        return pltpu.make_async_copy(h_ref.at[pl.ds(r * ROW_TILES, ROW_TILES), :], dst, sem)

    for r in range(tm):
        row_copy(r, 0).start(priority=0)
        row_copy(r, 1).start(priority=1)
    for _ in range(tm):
        row_copy(0, 0).wait()
        row_copy(0, 1).wait()


def _dispatch(pos, zstart, zvalid, n_used, h3t, cap):
    T = h3t.shape[0] // ROW_TILES
    tm = ROW_DMA_TILE
    grid_spec = pltpu.PrefetchScalarGridSpec(
        num_scalar_prefetch=4,
        grid=(T // tm,),
        in_specs=[pl.BlockSpec((tm * ROW_TILES, LANES), lambda i, ps, zs, zv, nu: (i, 0))],
        out_specs=pl.BlockSpec(memory_space=pl.ANY),
        scratch_shapes=[pltpu.VMEM((MOE_BLOCK * ROW_TILES, LANES), U32), pltpu.SemaphoreType.DMA],
    )
    return pl.pallas_call(
        _dispatch_kernel,
        grid_spec=grid_spec,
        out_shape=jax.ShapeDtypeStruct((cap * ROW_TILES, LANES), U32),
        compiler_params=_cparams(("arbitrary",)),
        name="dispatch",
    )(pos, zstart, zvalid, n_used, h3t)


def _moe_kernel(be_ref, nu_ref, x_ref, wg_ref, wu_ref, wd_ref, o_ref):
    i = pl.program_id(0)

    @pl.when(i < nu_ref[0])
    def _():
        xb = jnp.concatenate([x.astype(BF16) for x in _load_token_tiles(x_ref, MOE_BLOCK)], axis=-1)
        gate = _bdot(xb, wg_ref[0])
        up = _bdot(xb, wu_ref[0])
        hid = gate * _sigmoid(gate) * up
        _store_token_tiles(o_ref, _bdot(hid, wd_ref[0]))

    @pl.when(i >= nu_ref[0])
    def _():
        o_ref[...] = jnp.zeros_like(o_ref)


def _moe(block_expert, n_used, xs, wg, wu, wd):
    n_blocks = xs.shape[0] // (MOE_BLOCK * ROW_TILES)
    used = lambda i, nu: jnp.minimum(i, jnp.maximum(nu[0] - 1, 0))
    grid_spec = pltpu.PrefetchScalarGridSpec(
        num_scalar_prefetch=2,
        grid=(n_blocks,),
        in_specs=[
            pl.BlockSpec((MOE_BLOCK * ROW_TILES, LANES), lambda i, be, nu: (used(i, nu), 0)),
            pl.BlockSpec((1, D_MODEL, D_EXPERT), lambda i, be, nu: (be[i], 0, 0)),
            pl.BlockSpec((1, D_MODEL, D_EXPERT), lambda i, be, nu: (be[i], 0, 0)),
            pl.BlockSpec((1, D_EXPERT, D_MODEL), lambda i, be, nu: (be[i], 0, 0)),
        ],
        out_specs=pl.BlockSpec((MOE_BLOCK * ROW_TILES, LANES), lambda i, be, nu: (i, 0)),
    )
    return pl.pallas_call(
        _moe_kernel,
        grid_spec=grid_spec,
        out_shape=jax.ShapeDtypeStruct(xs.shape, U32),
        compiler_params=_cparams(("arbitrary",)),
        name="moe",
    )(block_expert, n_used, xs, wg, wu, wd)


def _moe_plan(expert, rank, counts, n_tok):
    n_assign = n_tok * 2
    e_flat = expert.reshape(n_assign)
    rank = rank.reshape(n_assign)
    padded = (counts + MOE_BLOCK - 1) // MOE_BLOCK * MOE_BLOCK
    p_end = jnp.cumsum(padded)
    p_start = p_end - padded
    pos = (p_start[e_flat] + rank).astype(jnp.int32)
    n_blocks = (n_assign + N_EXPERTS * (MOE_BLOCK - 1) + MOE_BLOCK - 1) // MOE_BLOCK
    cap = n_blocks * MOE_BLOCK
    block_start = jnp.arange(n_blocks, dtype=jnp.int32) * MOE_BLOCK
    block_expert = jnp.minimum(jnp.sum(p_end[None, :] <= block_start[:, None], axis=1),
                               N_EXPERTS - 1).astype(jnp.int32)
    n_used = (p_end[-1] // MOE_BLOCK).astype(jnp.int32).reshape(1)
    last_block = jnp.maximum(p_end - MOE_BLOCK, 0).astype(jnp.int32)
    nonempty = (padded > 0).astype(jnp.int32)
    return pos, cap, block_expert, n_used, last_block, nonempty


def _final_kernel(pos_ref, x_ref, rt_ref, y_hbm, g_ref, o_ref, y1_ref, y2_ref, sem):
    i = pl.program_id(0)
    n = pl.num_programs(0)
    tm = x_ref.shape[0]
    slot = i % 2

    def copies(tile, r, s):
        p1 = pos_ref[2 * (tile * tm + r)]
        p2 = pos_ref[2 * (tile * tm + r) + 1]
        src = lambda p: y_hbm.at[pl.ds(pl.multiple_of(p * ROW_TILES, ROW_TILES), ROW_TILES), :]
        dst = lambda ref: ref.at[s, pl.ds(r * ROW_TILES, ROW_TILES), :]
        return (pltpu.make_async_copy(src(p1), dst(y1_ref), sem.at[s]),
                pltpu.make_async_copy(src(p2), dst(y2_ref), sem.at[s]))

    def gather(tile, s):
        for r in range(tm):
            c1, c2 = copies(tile, r, s)
            c1.start(priority=0)
            c2.start(priority=1)

    def wait_rows(s):
        for _ in range(tm):
            c1, c2 = copies(0, 0, s)
            c1.wait()
            c2.wait()

    @pl.when(i == 0)
    def _():
        gather(0, 0)

    wait_rows(slot)
    gather(jnp.minimum(i + 1, n - 1), 1 - slot)
    rt = rt_ref[...]
    g1 = rt[:, 2:3]
    g2 = rt[:, 3:4]
    y1 = jnp.concatenate(_load_token_tiles(y1_ref.at[slot], tm), axis=-1)
    y2 = jnp.concatenate(_load_token_tiles(y2_ref.at[slot], tm), axis=-1)
    x3 = x_ref[...] + y1 * g1 + y2 * g2
    o_ref[...] = _rms(x3, g_ref[...])

    @pl.when(i == n - 1)
    def _():
        wait_rows(1 - slot)


def _final(pos, x2, rt, y_sorted, g):
    T = x2.shape[0]
    tm = ROW_DMA_TILE
    grid_spec = pltpu.PrefetchScalarGridSpec(
        num_scalar_prefetch=1,
        grid=(T // tm,),
        in_specs=[
            pl.BlockSpec((tm, D_MODEL), lambda i, ps: (i, 0)),
            pl.BlockSpec((tm, LANES), lambda i, ps: (i, 0)),
            pl.BlockSpec(memory_space=pl.ANY),
            pl.BlockSpec((1, D_MODEL), lambda i, ps: (0, 0)),
        ],
        out_specs=pl.BlockSpec((tm, D_MODEL), lambda i, ps: (i, 0)),
        scratch_shapes=[pltpu.VMEM((2, tm * ROW_TILES, LANES), U32),
                        pltpu.VMEM((2, tm * ROW_TILES, LANES), U32),
                        pltpu.SemaphoreType.DMA((2,))],
    )
    return pl.pallas_call(
        _final_kernel,
        grid_spec=grid_spec,
        out_shape=jax.ShapeDtypeStruct((T, D_MODEL), F32),
        compiler_params=_cparams(("arbitrary",)),
        name="final",
    )(pos, x2, rt, y_sorted, g)


def _lora_weight(decay_up, iclr_up, gate_up):
    w = jnp.zeros((LORA_COLS, 3 * WIDTH), F32)
    w = w.at[0:DECAY_LORA, 0:WIDTH].set(decay_up)
    w = w.at[DECAY_LORA:DECAY_LORA + ICLR_LORA, WIDTH:2 * WIDTH].set(iclr_up)
    w = w.at[DECAY_LORA + ICLR_LORA:, 2 * WIDTH:].set(gate_up)
    return w


def _router_weight(w_group, w_expert):
    w = jnp.zeros((D_MODEL, LANES), F32)
    w = w.at[:, 0:N_GROUPS].set(w_group)
    return w.at[:, N_GROUPS:N_GROUPS + N_EXPERTS].set(w_expert)


def kernel(x, mem, rel_bias, mem_norm, norm_mix, w_in, tshift_mu, decay_w0, decay_up, iclr_a0,
           iclr_up, gate_up, k_k, k_a, r_k, ln_x_w, ln_x_b, w_o_rwkv, w_o_moba, w_out,
           norm_xattn, w_q_x, w_kv_x, w_o_x, norm_ffn, w_router_group, w_router_expert,
           w_exp_gate, w_exp_up, w_exp_down, norm_final):
    B, T, D = x.shape
    assert B == 1 and D == D_MODEL and T % TOKEN_TILE == 0 and T // MOBA_BLOCK <= HEAD_DIM
    assert norm_mix.shape[0] == 1
    row = lambda a: a.reshape(1, -1)
    x2d = x.reshape(T, D)
    kv = _memkv(mem.reshape(N_MEM, D), row(mem_norm), w_kv_x[0].astype(BF16))

    ur, q, gates, k_aug, vt_aug, kmean = _inproj(x2d, row(norm_mix[0]), w_in[0].astype(BF16))
    kmean = kmean.reshape(T // MOBA_BLOCK, WIDTH)
    o_a = _rwkv_core(ur, row(tshift_mu[0]), row(decay_w0[0]), row(iclr_a0[0]), row(k_k[0]),
                     row(k_a[0]), row(r_k[0]),
                     _lora_weight(decay_up[0], iclr_up[0], gate_up[0]).astype(BF16),
                     row(ln_x_w[0]), row(ln_x_b[0]))

    o_b = _moba(q, k_aug, vt_aug, _score_mats(kmean), _bias_tiles(rel_bias))

    x2, h3, rt, counts = _merge(
        x2d, o_a, o_b, gates, w_o_rwkv[0].astype(BF16), w_o_moba[0].astype(BF16),
        w_out[0].astype(BF16), row(norm_xattn[0]), w_q_x[0].astype(BF16), kv,
        w_o_x[0].astype(BF16), row(norm_ffn[0]),
        _router_weight(w_router_group[0], w_router_expert[0]).astype(BF16))

    expert = rt[:, 0:2].astype(jnp.int32)
    rank = rt[:, 4:6].astype(jnp.int32)
    pos, cap, block_expert, n_used, last_block, nonempty = _moe_plan(
        expert, rank, counts[N_GROUPS:N_GROUPS + N_EXPERTS, 0].astype(jnp.int32), T)
    x_sorted = _dispatch(pos, last_block, nonempty, n_used, h3, cap)
    y_sorted = _moe(block_expert, n_used, x_sorted, w_exp_gate[0], w_exp_up[0], w_exp_down[0])
    out = _final(pos, x2, rt, y_sorted, row(norm_final))
    return out.reshape(B, T, D)
```

```python
import math

import jax
import jax.numpy as jnp
from jax import lax
from jax.experimental import pallas as pl
from jax.experimental.pallas import tpu as pltpu

F32 = jnp.float32
BF16 = jnp.bfloat16

D_MODEL = 1024
N_MEM = 256
NORM_EPS = 1e-6
NEG_INF = -1e30

HEADS = 8
HEAD_DIM = 64
WIDTH = HEADS * HEAD_DIM
DECAY_LORA = 64
ICLR_LORA = 64
GATE_LORA = 128
LORA_COLS = DECAY_LORA + ICLR_LORA + GATE_LORA
RWKV_COLS = 3 * WIDTH + LORA_COLS
QKV_COLS = 3 * WIDTH
GATE_COLS = 2 * D_MODEL
IN_COLS = RWKV_COLS + QKV_COLS + GATE_COLS
LN_X_EPS = 64e-5
KK_EPS = 1e-12

MOBA_BLOCK = 256
MOBA_TOP = 3
REL_BUCKETS = 32
REL_MAX_DISTANCE = 4096
N_BIAS_TILES = 14

XATTN_HEADS = 4
XATTN_HEAD_DIM = 128
XATTN_WIDTH = XATTN_HEADS * XATTN_HEAD_DIM

N_GROUPS = 4
EXPERTS_PER_GROUP = 8
N_EXPERTS = N_GROUPS * EXPERTS_PER_GROUP
D_EXPERT = 512
MOE_BLOCK = 256

TOKEN_TILE = 512
ROW_DMA_TILE = 256
VT_ROWS = HEAD_DIM + 16
CHUNK = 64
CHUNKS_PER_STEP = 4
LANES = 128
VMEM_LIMIT = 48 * 1024 * 1024

LOG2E = math.log2(math.e)


def _cparams(sem):
    return pltpu.CompilerParams(dimension_semantics=sem, vmem_limit_bytes=VMEM_LIMIT)


def _bdot(a, b):
    return jnp.dot(a.astype(BF16), b.astype(BF16), preferred_element_type=F32)


def _bdot_nt(a, b):
    return lax.dot_general(a.astype(BF16), b.astype(BF16), (((1,), (1,)), ((), ())),
                           preferred_element_type=F32)


def _bdot_tn(a, b):
    return lax.dot_general(a.astype(BF16), b.astype(BF16), (((0,), (0,)), ((), ())),
                           preferred_element_type=F32)


def _rms(x, g):
    return x * lax.rsqrt(jnp.mean(x * x, axis=-1, keepdims=True) + NORM_EPS) * g


def _sigmoid(x):
    return 1.0 / (1.0 + jnp.exp(-x))


ROW_TILES = D_MODEL // (2 * LANES)
U32 = jnp.uint32
HI_MASK = 0xFFFF0000


def _bits(x):
    return pltpu.bitcast(x.astype(BF16).astype(F32), U32)


def _store_token_tiles(ref, x):
    n = x.shape[0]
    for s in range(ROW_TILES):
        lo = x[:, 2 * s * LANES:(2 * s + 1) * LANES]
        hi = x[:, (2 * s + 1) * LANES:(2 * s + 2) * LANES]
        word = (_bits(hi) & U32(HI_MASK)) | lax.shift_right_logical(_bits(lo), U32(16))
        ref[pl.ds(s, n, stride=ROW_TILES), :] = word


def _load_token_tiles(ref, n):
    out = []
    for s in range(ROW_TILES):
        word = ref[pl.ds(s, n, stride=ROW_TILES), :]
        out.append(pltpu.bitcast(lax.shift_left(word, U32(16)), F32))
        out.append(pltpu.bitcast(word & U32(HI_MASK), F32))
    return out


def _inproj_kernel(x_ref, g_ref, w_ref, ur_ref, q_ref, gate_ref, ka_ref, vt_ref, km_ref):
    i = pl.program_id(0)
    B = MOBA_BLOCK
    h = _rms(x_ref[...], g_ref[...]).astype(BF16)
    tm = h.shape[0]
    step = 256
    for c0 in range(0, RWKV_COLS, step):
        ur_ref[:, c0:c0 + step] = jnp.dot(h, w_ref[:, c0:c0 + step], preferred_element_type=F32)
    for c0 in range(0, WIDTH, step):
        o = jnp.dot(h, w_ref[:, RWKV_COLS + c0:RWKV_COLS + c0 + step], preferred_element_type=F32)
        q_ref[:, c0:c0 + step] = (o * (HEAD_DIM ** -0.5 * LOG2E)).astype(BF16)
    lane = lax.broadcasted_iota(jnp.int32, (B, LANES), 1)
    ones = jnp.ones((VT_ROWS - HEAD_DIM, B), F32)
    for c0 in range(0, WIDTH, step):
        kcol = RWKV_COLS + WIDTH + c0
        k = jnp.dot(h, w_ref[:, kcol:kcol + step], preferred_element_type=F32).astype(BF16)
        v = jnp.dot(h, w_ref[:, kcol + WIDTH:kcol + WIDTH + step], preferred_element_type=F32)
        v = v.astype(BF16).astype(F32)
        for pp in range(step // LANES):
            pair = c0 // LANES + pp
            for jj in range(tm // B):
                rows = slice(jj * B, (jj + 1) * B)
                kblk = k[rows, pp * LANES:(pp + 1) * LANES]
                km_ref[0, jj:jj + 1, pair * LANES:(pair + 1) * LANES] = jnp.mean(
                    kblk.astype(F32), axis=0, keepdims=True)
                v_t = v[rows, pp * LANES:(pp + 1) * LANES].T
                for h2 in range(2):
                    off = HEAD_DIM * (1 - h2)
                    inr = (lane >= off) & (lane < off + HEAD_DIM)
                    hot = jnp.where(lane == off + i * (tm // B) + jj, 1.0, 0.0).astype(BF16)
                    ka_ref[2 * pair + h2, rows, :] = jnp.where(inr, hot, kblk)
                    tile = jnp.concatenate([v_t[h2 * HEAD_DIM:(h2 + 1) * HEAD_DIM], ones], axis=0)
                    vt_ref[2 * pair + h2, jj] = tile.astype(BF16)
    base = RWKV_COLS + QKV_COLS
    for c0 in range(0, GATE_COLS, step):
        o = jnp.dot(h, w_ref[:, base + c0:base + c0 + step], preferred_element_type=F32)
        gate_ref[:, c0:c0 + step] = o.astype(BF16)


def _inproj(x2d, g, w_bf):
    T = x2d.shape[0]
    tm = TOKEN_TILE
    bpt = tm // MOBA_BLOCK
    return pl.pallas_call(
        _inproj_kernel,
        grid=(T // tm,),
        in_specs=[
            pl.BlockSpec((tm, D_MODEL), lambda i: (i, 0)),
            pl.BlockSpec((1, D_MODEL), lambda i: (0, 0)),
            pl.BlockSpec((D_MODEL, IN_COLS), lambda i: (0, 0), pipeline_mode=pl.Buffered(1)),
        ],
        out_specs=[
            pl.BlockSpec((tm, RWKV_COLS), lambda i: (i, 0)),
            pl.BlockSpec((tm, WIDTH), lambda i: (i, 0)),
            pl.BlockSpec((tm, GATE_COLS), lambda i: (i, 0)),
            pl.BlockSpec((HEADS, tm, LANES), lambda i: (0, i, 0)),
            pl.BlockSpec((HEADS, bpt, VT_ROWS, MOBA_BLOCK), lambda i: (0, i, 0, 0)),
            pl.BlockSpec((1, bpt, WIDTH), lambda i: (i, 0, 0)),
        ],
        out_shape=[
            jax.ShapeDtypeStruct((T, RWKV_COLS), F32),
            jax.ShapeDtypeStruct((T, WIDTH), BF16),
            jax.ShapeDtypeStruct((T, GATE_COLS), BF16),
            jax.ShapeDtypeStruct((HEADS, T, LANES), BF16),
            jax.ShapeDtypeStruct((HEADS, T // MOBA_BLOCK, VT_ROWS, MOBA_BLOCK), BF16),
            jax.ShapeDtypeStruct((T // tm, bpt, WIDTH), F32),
        ],
        compiler_params=_cparams(("parallel",)),
        name="inproj",
    )(x2d, g, w_bf)


def _head_ones():
    r = lax.broadcasted_iota(jnp.int32, (LANES, LANES), 0) // HEAD_DIM
    c = lax.broadcasted_iota(jnp.int32, (LANES, LANES), 1) // HEAD_DIM
    return jnp.where(r == c, 1.0, 0.0).astype(BF16)


def _head_sum(x, ones):
    parts = [_bdot(x[:, c:c + LANES], ones) for c in range(0, WIDTH, LANES)]
    return jnp.concatenate(parts, axis=-1)


def _rwkv_prep_values(u_ref, up_ref, mu_ref, w0_ref, a0_ref, kk_ref, ka_ref, rk_ref, wl_ref):
    i = pl.program_id(0)
    u = u_ref[...]
    tm = u.shape[0]
    prev_last = up_ref[7:8, :] * jnp.where(i > 0, 1.0, 0.0)
    rolled = pltpu.roll(u, 1, 0)
    row = lax.broadcasted_iota(jnp.int32, u.shape, 0)
    u_prev = jnp.where(row == 0, prev_last, rolled)
    u = u + mu_ref[...] * (u_prev - u)
    r = u[:, 0:WIDTH]
    k = u[:, WIDTH:2 * WIDTH]
    v = u[:, 2 * WIDTH:3 * WIDTH]
    lo = u[:, 3 * WIDTH:3 * WIDTH + LORA_COLS]
    lane = lax.broadcasted_iota(jnp.int32, lo.shape, 1)
    act = jnp.where(lane < DECAY_LORA, jnp.tanh(lo),
                    jnp.where(lane < DECAY_LORA + ICLR_LORA, lo, _sigmoid(lo)))
    up = _bdot(act, wl_ref[...])
    z = -(w0_ref[...] + up[:, 0:WIDTH])
    softplus = jnp.maximum(z, 0.0) + jnp.log(1.0 + jnp.exp(-jnp.abs(z)))
    w_log = -softplus - 0.5
    lw = -jnp.exp(w_log)
    iclr = _sigmoid(a0_ref[...] + up[:, WIDTH:2 * WIDTH])
    g = up[:, 2 * WIDTH:3 * WIDTH]
    ones = _head_ones()
    kk = k * kk_ref[...]
    kk = kk * lax.rsqrt(_head_sum(kk * kk, ones) + KK_EPS)
    k2 = k * (1.0 + (iclr - 1.0) * ka_ref[...])
    bonus = _head_sum(r * k2 * rk_ref[...], ones) * v
    return r, lw, k2, v, -kk, kk * iclr, g, bonus


def _rwkv_core_kernel(u_ref, up_ref, mu_ref, w0_ref, a0_ref, kk_ref, ka_ref, rk_ref, wl_ref,
                      lnw_ref, lnb_ref, o_ref, s_ref):
    c = pl.program_id(0)

    @pl.when(c == 0)
    def _():
        s_ref[...] = jnp.zeros_like(s_ref)

    C = CHUNK
    G = CHUNKS_PER_STEP
    ri = lax.broadcasted_iota(jnp.int32, (C, C), 0)
    ci = lax.broadcasted_iota(jnp.int32, (C, C), 1)
    rg = lax.broadcasted_iota(jnp.int32, (G * C, G * C), 0)
    cg = lax.broadcasted_iota(jnp.int32, (G * C, G * C), 1)
    r_in, lw, k_in, v_all, a_in, b_in, g_in, bonus = _rwkv_prep_values(
        u_ref, up_ref, mu_ref, w0_ref, a0_ref, kk_ref, ka_ref, rk_ref, wl_ref)
    tri = jnp.where((rg >= cg) & (rg // C == cg // C), 1.0, 0.0).astype(BF16)
    lw_hi = lw.astype(BF16)
    lw_r1 = lw - lw_hi.astype(F32)
    lw_mid = lw_r1.astype(BF16)
    lw_lo = (lw_r1 - lw_mid.astype(F32)).astype(BF16)
    cum = (jnp.dot(tri, lw_hi, preferred_element_type=F32)
           + jnp.dot(tri, lw_mid, preferred_element_type=F32)
           + jnp.dot(tri, lw_lo, preferred_element_type=F32))
    lam = jnp.exp(cum)
    inv_lam = jnp.exp(-cum)
    r_t = r_in * lam
    a_t = a_in * jnp.exp(cum - lw)
    b_t = b_in * inv_lam
    k_t = k_in * inv_lam
    tots = [cum[g * C + C - 1:g * C + C, :] for g in range(G)]
    rowg = lax.broadcasted_iota(jnp.int32, cum.shape, 0) // C
    tot = tots[G - 1]
    for g in range(G - 2, -1, -1):
        tot = jnp.where(rowg == g, tots[g], tot)
    rest = jnp.exp(tot - cum)
    b_h = b_in * rest
    k_h = k_in * rest
    lam_c = [jnp.exp(t) for t in tots]
    eye = jnp.where(ri == ci, 1.0, 0.0)
    H = range(G * HEADS)
    sls = [slice(h * HEAD_DIM, (h + 1) * HEAD_DIM) for h in range(HEADS)]
    bf = lambda x: x.astype(BF16)
    part = lambda x, i: x[(i // HEADS) * C:(i // HEADS + 1) * C, sls[i % HEADS]]
    at = [bf(part(a_t, i)) for i in H]
    rt = [part(r_t, i) for i in H]
    bt = [bf(part(b_t, i)) for i in H]
    kt = [bf(part(k_t, i)) for i in H]
    bh = [bf(part(b_h, i)) for i in H]
    kh = [bf(part(k_h, i)) for i in H]
    vv = [bf(part(v_all, i)) for i in H]
    ci2 = lax.broadcasted_iota(jnp.int32, (C, 2 * C), 1)
    ri2 = lax.broadcasted_iota(jnp.int32, (C, 2 * C), 0)
    cm2 = jnp.where(ci2 >= C, ci2 - C, ci2)
    left = ci2 < C
    ar = [jnp.concatenate([at[h], bf(rt[h])], axis=0) for h in H]
    bk = [jnp.concatenate([bt[h], kt[h]], axis=0) for h in H]
    g = [_bdot_nt(ar[h], bk[h]) for h in H]
    top = [jnp.where(ri2 > cm2, g[h][0:C], 0.0) for h in H]
    bot = [bf(jnp.where(ri2 >= cm2, g[h][C:2 * C], 0.0)) for h in H]
    a_ab = [top[h][:, 0:C] for h in H]
    akv = [_bdot(top[h][:, C:2 * C], vv[h]) for h in H]
    z = [jnp.concatenate([a_ab[h], eye], axis=1) for h in H]
    for _ in range(6):
        z = [_bdot(z[h][:, 0:C], z[h]) + jnp.where(left, 0.0, z[h]) for h in H]
    tinv = [bf(z[h][:, C:2 * C]) for h in H]
    wu = [_bdot(tinv[h], jnp.concatenate([at[h], bf(akv[h])], axis=1)) for h in H]
    w_m = [bf(wu[h][:, 0:C]) for h in H]
    uv = [jnp.concatenate([bf(wu[h][:, C:2 * C]), vv[h]], axis=0) for h in H]
    q_m = [rt[h] + _bdot(bot[h][:, 0:C], w_m[h]) for h in H]
    y0 = [_bdot(bot[h], uv[h]) for h in H]
    m_k = [_bdot_tn(w_m[h], bh[h]) for h in H]
    n0 = [_bdot_tn(uv[h], jnp.concatenate([bh[h], kh[h]], axis=0)) for h in H]
    state = [s_ref[h] for h in range(HEADS)]
    blocks = []
    for g in range(G):
        outs = []
        for h in range(HEADS):
            i = g * HEADS + h
            y = _bdot_nt(q_m[i], state[h]) + y0[i]
            state[h] = state[h] * lam_c[g][:, sls[h]] + _bdot(state[h], m_k[i]) + n0[i]
            mean = jnp.mean(y, axis=-1, keepdims=True)
            var = jnp.mean(jnp.square(y - mean), axis=-1, keepdims=True)
            outs.append((y - mean) * lax.rsqrt(var + LN_X_EPS))
        blocks.append(jnp.concatenate(outs, axis=-1))
    for h in range(HEADS):
        s_ref[h] = state[h]
    yn = jnp.concatenate(blocks, axis=0)
    yn = yn * lnw_ref[...] + lnb_ref[...] + bonus
    o_ref[...] = (yn * g_in).astype(BF16)


def _rwkv_core(ur, mu, w0, a0, k_k, k_a, r_k, w_lora, ln_w, ln_b):
    T = ur.shape[0]
    rows = CHUNK * CHUNKS_PER_STEP
    row = lambda w: pl.BlockSpec((1, w), lambda c: (0, 0))
    return pl.pallas_call(
        _rwkv_core_kernel,
        grid=(T // rows,),
        in_specs=[
            pl.BlockSpec((rows, RWKV_COLS), lambda c: (c, 0)),
            pl.BlockSpec((8, RWKV_COLS), lambda c: (jnp.maximum(c * (rows // 8) - 1, 0), 0)),
            row(RWKV_COLS), row(WIDTH), row(WIDTH), row(WIDTH), row(WIDTH), row(WIDTH),
            pl.BlockSpec((LORA_COLS, 3 * WIDTH), lambda c: (0, 0)),
            row(WIDTH), row(WIDTH),
        ],
        out_specs=pl.BlockSpec((rows, WIDTH), lambda c: (c, 0)),
        out_shape=jax.ShapeDtypeStruct((T, WIDTH), BF16),
        scratch_shapes=[pltpu.VMEM((HEADS, HEAD_DIM, HEAD_DIM), F32)],
        compiler_params=_cparams(("arbitrary",)),
        name="rwkv_core",
    )(ur, ur, mu, w0, a0, k_k, k_a, r_k, w_lora, ln_w, ln_b)


def _t5_bucket(dist):
    n = jnp.maximum(dist, 0)
    max_exact = REL_BUCKETS // 2
    nf = jnp.maximum(n, max_exact).astype(jnp.float32)
    large = max_exact + (jnp.log(nf / max_exact) / math.log(REL_MAX_DISTANCE / max_exact)
                         * (REL_BUCKETS - max_exact)).astype(jnp.int32)
    large = jnp.minimum(large, REL_BUCKETS - 1)
    return jnp.where(n < max_exact, n, large)


def _bucket_tiles():
    i = jnp.arange(MOBA_BLOCK)[None, :]
    j = jnp.arange(MOBA_BLOCK)[:, None]
    d = jnp.arange(N_BIAS_TILES + 1)[:, None, None]
    dist = d * MOBA_BLOCK + i - j
    bucket = _t5_bucket(dist)
    bucket = jnp.where(d == N_BIAS_TILES - 1, REL_BUCKETS - 1, bucket)
    return jnp.where((dist < 0) | (d == N_BIAS_TILES), -1, bucket).astype(jnp.int32)


BIAS_ROWS = 16


def _bias_tiles_kernel(idx_ref, rb_ref, o_ref):
    def rows(c, carry):
        r0 = pl.multiple_of(c * BIAS_ROWS, BIAS_ROWS)
        idx = idx_ref[0, pl.ds(r0, BIAS_ROWS), :]
        acc = [jnp.where(idx < 0, NEG_INF, 0.0)] * HEADS
        for bkt in range(REL_BUCKETS):
            hit = idx == bkt
            acc = [jnp.where(hit, rb_ref[bkt, h] * LOG2E, acc[h]) for h in range(HEADS)]
        for h in range(HEADS):
            o_ref[h, 0, pl.ds(r0, BIAS_ROWS), :] = acc[h]
        return carry

    lax.fori_loop(0, MOBA_BLOCK // BIAS_ROWS, rows, 0)


def _bias_tiles(rel_bias):
    idx = _bucket_tiles()
    n = N_BIAS_TILES + 1
    return pl.pallas_call(
        _bias_tiles_kernel,
        grid=(n,),
        in_specs=[
            pl.BlockSpec((1, MOBA_BLOCK, MOBA_BLOCK), lambda d: (d, 0, 0)),
            pl.BlockSpec(memory_space=pltpu.SMEM),
        ],
        out_specs=pl.BlockSpec((HEADS, 1, MOBA_BLOCK, MOBA_BLOCK), lambda d: (0, d, 0, 0)),
        out_shape=jax.ShapeDtypeStruct((HEADS, n, MOBA_BLOCK, MOBA_BLOCK), F32),
        compiler_params=_cparams(("parallel",)),
        name="bias_tiles",
    )(idx, rel_bias)


def _moba_kernel(q_ref, ka_ref, vt_ref, r_ref, bias_ref, o_ref, s_ref, p_ref, acc_ref):
    qb = pl.program_id(1)
    B = MOBA_BLOCK
    q_tr = q_ref[...].astype(F32).T
    q_tr_bf = q_tr.astype(BF16)
    blk = lax.broadcasted_iota(jnp.int32, (HEAD_DIM, B), 0)
    big = jnp.int32(1 << 20)
    q_t, q_own_t = [], []
    for h2 in range(2):
        off = HEAD_DIM * (1 - h2)
        valid = blk < qb
        sc = jnp.dot(r_ref[0, h2, off:off + HEAD_DIM, :], q_tr_bf, preferred_element_type=F32)
        s = jnp.where(valid, sc, NEG_INF)
        sel = jnp.zeros((HEAD_DIM, B), jnp.bool_)
        for _ in range(MOBA_TOP):
            m = jnp.max(s, axis=0, keepdims=True)
            idx = jnp.min(jnp.where(s == m, blk, big), axis=0, keepdims=True)
            pick = blk == idx
            sel = jnp.logical_or(sel, pick)
            s = jnp.where(pick, -jnp.inf, s)
        sel = jnp.logical_and(sel, valid)
        choice = jnp.where(sel, 0.0, NEG_INF)
        q_head = q_tr[h2 * HEAD_DIM:(h2 + 1) * HEAD_DIM]
        parts = [q_head, choice] if h2 == 0 else [choice, q_head]
        own = [q_head, jnp.zeros_like(choice)] if h2 == 0 else [jnp.zeros_like(choice), q_head]
        q_t.append(jnp.concatenate(parts, axis=0).astype(BF16))
        q_own_t.append(jnp.concatenate(own, axis=0).astype(BF16))

    carry = []
    for h2 in range(2):
        k_own = ka_ref[h2, pl.ds(pl.multiple_of(qb * B, B), B), :]
        s0 = jnp.dot(k_own, q_own_t[h2], preferred_element_type=F32) + bias_ref[h2, 0]
        m0 = jnp.max(s0, axis=0, keepdims=True)
        p0 = jnp.exp2(s0 - m0).astype(BF16)
        carry += [m0, jnp.dot(vt_ref[h2, qb], p0, preferred_element_type=F32)]

    n_tiles = (qb + 1) // 2
    last = jnp.maximum(n_tiles - 1, 0)
    for h2 in range(2):
        acc_ref[h2] = carry[2 * h2 + 1]

    def trip(t, w, stats, scores=True, softmax=True, accumulate=True):
        r = 1 - w
        ok = (t >= 1) & (t <= n_tiles)
        d_a = jnp.where(ok, jnp.clip(qb - 2 * (t - 1), 0, N_BIAS_TILES - 1), N_BIAS_TILES)
        d_b = jnp.where(ok, jnp.clip(qb - 2 * (t - 1) - 1, 0, N_BIAS_TILES - 1), N_BIAS_TILES)
        start = pl.multiple_of(jnp.clip(t, 0, last) * (2 * B), 2 * B)
        v_a = 2 * jnp.clip(t - 2, 0, last)
        v_b = v_a + 1
        out = []
        for h2 in range(2):
            m_prev, alpha_p = stats[2 * h2], stats[2 * h2 + 1]
            if accumulate:
                acc_ref[h2] = (acc_ref[h2] * alpha_p
                               + jnp.dot(vt_ref[h2, v_a], p_ref[r, h2, 0:B], preferred_element_type=F32)
                               + jnp.dot(vt_ref[h2, v_b], p_ref[r, h2, B:2 * B],
                                         preferred_element_type=F32))
            if softmax:
                s_a = s_ref[r, h2, 0:B] + bias_ref[h2, d_a]
                s_b = s_ref[r, h2, B:2 * B] + bias_ref[h2, d_b]
                m_new = jnp.maximum(m_prev, jnp.max(jnp.maximum(s_a, s_b), axis=0, keepdims=True))
                out += [m_new, jnp.exp2(m_prev - m_new)]
                p_ref[w, h2, 0:B] = jnp.exp2(s_a - m_new).astype(BF16)
                p_ref[w, h2, B:2 * B] = jnp.exp2(s_b - m_new).astype(BF16)
            else:
                out += [m_prev, alpha_p]
            if scores:
                s_ref[w, h2] = jnp.dot(ka_ref[h2, pl.ds(start, 2 * B), :], q_t[h2],
                                       preferred_element_type=F32)
        return out

    def body(u, stats):
        stats = trip(2 * u, 0, list(stats))
        return tuple(trip(2 * u + 1, 1, stats))

    stats = []
    for h2 in range(2):
        stats += [carry[2 * h2], jnp.ones_like(carry[2 * h2])]
    stats = trip(0, 0, stats, softmax=False, accumulate=False)
    stats = trip(1, 1, stats, accumulate=False)
    stats = list(lax.fori_loop(1, n_tiles // 2, body, tuple(stats)))
    odd = n_tiles % 2 == 1

    @pl.when(jnp.logical_not(odd) & (n_tiles >= 2))
    def _():
        st = trip(n_tiles, 0, stats, scores=False)
        trip(n_tiles + 1, 1, st, scores=False, softmax=False)

    @pl.when(odd & (n_tiles >= 3))
    def _():
        st = trip(n_tiles - 1, 0, stats)
        st = trip(n_tiles, 1, st, scores=False)
        trip(n_tiles + 1, 0, st, scores=False, softmax=False)

    @pl.when(n_tiles == 1)
    def _():
        trip(2, 0, stats, scores=False, softmax=False)
    outs = [acc_ref[h2, 0:HEAD_DIM] / acc_ref[h2, HEAD_DIM:HEAD_DIM + 1] for h2 in range(2)]
    o_ref[...] = jnp.concatenate(outs, axis=0).T.astype(BF16)


def _moba(q, k_aug, vt_aug, r_mats, bias_tiles):
    T = q.shape[0]
    nb = T // MOBA_BLOCK
    npair = HEADS // 2
    once = pl.Buffered(1)
    return pl.pallas_call(
        _moba_kernel,
        grid=(npair, nb),
        in_specs=[
            pl.BlockSpec((MOBA_BLOCK, LANES), lambda p, qb: (qb, p)),
            pl.BlockSpec((2, T, LANES), lambda p, qb: (p, 0, 0), pipeline_mode=once),
            pl.BlockSpec((2, nb, VT_ROWS, MOBA_BLOCK), lambda p, qb: (p, 0, 0, 0), pipeline_mode=once),
            pl.BlockSpec((1, 2, LANES, LANES), lambda p, qb: (p, 0, 0, 0)),
            pl.BlockSpec((2, N_BIAS_TILES + 1, MOBA_BLOCK, MOBA_BLOCK), lambda p, qb: (p, 0, 0, 0),
                         pipeline_mode=once),
        ],
        out_specs=pl.BlockSpec((MOBA_BLOCK, LANES), lambda p, qb: (qb, p)),
        out_shape=jax.ShapeDtypeStruct((T, WIDTH), BF16),
        scratch_shapes=[pltpu.VMEM((2, 2, 2 * MOBA_BLOCK, MOBA_BLOCK), F32),
                        pltpu.VMEM((2, 2, 2 * MOBA_BLOCK, MOBA_BLOCK), BF16),
                        pltpu.VMEM((2, VT_ROWS, MOBA_BLOCK), F32)],
        compiler_params=_cparams(("arbitrary", "arbitrary")),
        name="moba",
    )(q, k_aug, vt_aug, r_mats, bias_tiles)


def _score_mats(kmean):
    nb = kmean.shape[0]
    km = kmean.reshape(nb, HEADS, HEAD_DIM).transpose(1, 2, 0)
    km = jnp.pad(km, ((0, 0), (0, 0), (0, HEAD_DIM - nb)))
    z = jnp.zeros((HEADS // 2, HEAD_DIM, HEAD_DIM), F32)
    even = jnp.concatenate([jnp.concatenate([z, km[0::2]], axis=2),
                            jnp.concatenate([z, z], axis=2)], axis=1)
    odd = jnp.concatenate([jnp.concatenate([z, z], axis=2),
                           jnp.concatenate([km[1::2], z], axis=2)], axis=1)
    return jnp.swapaxes(jnp.stack([even, odd], axis=1), -1, -2).astype(BF16)


def _memkv_kernel(m_ref, g_ref, w_ref, o_ref):
    h = _rms(m_ref[...], g_ref[...]).astype(BF16)
    o_ref[...] = jnp.dot(h, w_ref[...], preferred_element_type=F32).astype(BF16)


def _memkv(mem2d, g, w_bf):
    return pl.pallas_call(
        _memkv_kernel,
        out_shape=jax.ShapeDtypeStruct((N_MEM, 2 * XATTN_WIDTH), BF16),
        compiler_params=pltpu.CompilerParams(vmem_limit_bytes=VMEM_LIMIT),
        name="memkv",
    )(mem2d, g, w_bf)


def _merge_kernel(x_ref, oa_ref, ob_ref, gt_ref, woa_ref, wob_ref, wout_ref, nx_ref, wq_ref,
                  kv_ref, wo_ref, nf_ref, wr_ref, x2_ref, h3_ref, rt_ref, cnt_ref, run_ref):
    pa = jnp.dot(oa_ref[...], woa_ref[...], preferred_element_type=F32)
    pb = jnp.dot(ob_ref[...], wob_ref[...], preferred_element_type=F32)
    ga = gt_ref[:, 0:D_MODEL].astype(F32)
    gb = gt_ref[:, D_MODEL:2 * D_MODEL].astype(F32)
    merged = _sigmoid(ga) * pa + _sigmoid(gb) * pb
    x1 = x_ref[...] + jnp.dot(merged.astype(BF16), wout_ref[...], preferred_element_type=F32)
    h2 = _rms(x1, nx_ref[...]).astype(BF16)
    q = jnp.dot(h2, wq_ref[...], preferred_element_type=F32).astype(BF16)
    heads = []
    for h in range(XATTN_HEADS):
        sl = slice(h * XATTN_HEAD_DIM, (h + 1) * XATTN_HEAD_DIM)
        km = kv_ref[:, sl]
        vm = kv_ref[:, XATTN_WIDTH + h * XATTN_HEAD_DIM:XATTN_WIDTH + (h + 1) * XATTN_HEAD_DIM]
        lg = lax.dot_general(q[:, sl], km, (((1,), (1,)), ((), ())),
                             preferred_element_type=F32) * (XATTN_HEAD_DIM ** -0.5)
        e = jnp.exp(lg - jnp.max(lg, axis=-1, keepdims=True))
        p = e / jnp.sum(e, axis=-1, keepdims=True)
        heads.append(jnp.dot(p.astype(BF16), vm, preferred_element_type=F32))
    o = jnp.concatenate(heads, axis=-1).astype(BF16)
    x2 = x1 + jnp.dot(o, wo_ref[...], preferred_element_type=F32)
    x2_ref[...] = x2
    h3 = _rms(x2, nf_ref[...])
    _store_token_tiles(h3_ref, h3)
    lg = _bdot(h3, wr_ref[...]).T
    tm = lg.shape[1]
    row = lax.broadcasted_iota(jnp.int32, lg.shape, 0)
    big = jnp.int32(1 << 20)
    is_g = row < N_GROUPS
    gl = jnp.where(is_g, lg, -jnp.inf)
    gmax = jnp.max(gl, axis=0, keepdims=True)
    gsel = jnp.min(jnp.where(gl == gmax, row, big), axis=0, keepdims=True)
    p_top = 1.0 / jnp.sum(jnp.where(is_g, jnp.exp(gl - gmax), 0.0), axis=0, keepdims=True)
    e_id = row - N_GROUPS
    in_grp = (e_id >= gsel * EXPERTS_PER_GROUP) & (e_id < (gsel + 1) * EXPERTS_PER_GROUP)
    el = jnp.where(in_grp, lg, -jnp.inf)
    v1 = jnp.max(el, axis=0, keepdims=True)
    i1 = jnp.min(jnp.where(el == v1, row, big), axis=0, keepdims=True)
    el2 = jnp.where(row == i1, -jnp.inf, el)
    v2 = jnp.max(el2, axis=0, keepdims=True)
    i2 = jnp.min(jnp.where(el2 == v2, row, big), axis=0, keepdims=True)
    e21 = jnp.exp(v2 - v1)
    g1 = p_top / (1.0 + e21)
    g2 = p_top * e21 / (1.0 + e21)
    step = pl.program_id(0)

    @pl.when(step == 0)
    def _():
        run_ref[...] = jnp.zeros_like(run_ref)

    oh1 = jnp.where(row == i1, 1.0, 0.0)
    oh2 = jnp.where(row == i2, 1.0, 0.0)
    both = oh1 + oh2
    ri = lax.broadcasted_iota(jnp.int32, (tm, tm), 0)
    ci = lax.broadcasted_iota(jnp.int32, (tm, tm), 1)
    before = jnp.dot(both, jnp.where(ri < ci, 1.0, 0.0), preferred_element_type=F32) + run_ref[:, 0:1]
    r1 = jnp.sum(before * oh1, axis=0, keepdims=True)
    r2 = jnp.sum(before * oh2, axis=0, keepdims=True)
    run_ref[...] = run_ref[...] + jnp.sum(both, axis=1, keepdims=True)
    cnt_ref[...] = run_ref[...]
    vals = [(i1 - N_GROUPS).astype(F32), (i2 - N_GROUPS).astype(F32), g1, g2, r1, r2]
    rt = jnp.zeros_like(lg)
    for k, v in enumerate(vals):
        rt = jnp.where(row == k, v, rt)
    rt_ref[...] = rt.T


def _merge(x2d, o_a, o_b, gates, w_oa, w_ob, w_out, n_x, w_q, kv, w_o, n_f, w_r):
    T = x2d.shape[0]
    tm = TOKEN_TILE
    full = lambda a: pl.BlockSpec(a.shape, lambda i: (0,) * a.ndim, pipeline_mode=pl.Buffered(1))
    tile = lambda w: pl.BlockSpec((tm, w), lambda i: (i, 0))
    return pl.pallas_call(
        _merge_kernel,
        grid=(T // tm,),
        in_specs=[tile(D_MODEL), tile(WIDTH), tile(WIDTH), tile(GATE_COLS), full(w_oa), full(w_ob),
                  full(w_out), full(n_x), full(w_q), full(kv), full(w_o), full(n_f), full(w_r)],
        out_specs=[tile(D_MODEL), pl.BlockSpec((tm * ROW_TILES, LANES), lambda i: (i, 0)), tile(LANES),
                   pl.BlockSpec((LANES, LANES), lambda i: (0, 0))],
        out_shape=[jax.ShapeDtypeStruct((T, D_MODEL), F32),
                   jax.ShapeDtypeStruct((T * ROW_TILES, LANES), U32),
                   jax.ShapeDtypeStruct((T, LANES), F32),
                   jax.ShapeDtypeStruct((LANES, LANES), F32)],
        scratch_shapes=[pltpu.VMEM((LANES, LANES), F32)],
        compiler_params=_cparams(("arbitrary",)),
        name="merge",
    )(x2d, o_a, o_b, gates, w_oa, w_ob, w_out, n_x, w_q, kv, w_o, n_f, w_r)


def _dispatch_kernel(pos_ref, zs_ref, zv_ref, nu_ref, h_ref, xs_hbm, zero_ref, sem):
    i = pl.program_id(0)
    tm = h_ref.shape[0] // ROW_TILES

    @pl.when(i == 0)
    def _():
        zero_ref[...] = jnp.zeros_like(zero_ref)

        def clear(e):
            dst = xs_hbm.at[pl.ds(pl.multiple_of(zs_ref[e] * ROW_TILES, ROW_TILES),
                                  MOE_BLOCK * ROW_TILES), :]
            return pltpu.make_async_copy(zero_ref, dst, sem)

        for e in range(N_EXPERTS):
            @pl.when(zv_ref[e] > 0)
            def _():
                clear(e).start()

        for e in range(N_EXPERTS):
            @pl.when(zv_ref[e] > 0)
            def _():
                clear(e).wait()

        def tail(b):
            dst = xs_hbm.at[pl.ds(pl.multiple_of(b * (MOE_BLOCK * ROW_TILES), MOE_BLOCK * ROW_TILES),
                                  MOE_BLOCK * ROW_TILES), :]
            return pltpu.make_async_copy(zero_ref, dst, sem)

        n_blocks = xs_hbm.shape[0] // (MOE_BLOCK * ROW_TILES)

        def start_tail(b, c):
            tail(b).start()
            return c

        def wait_tail(b, c):
            tail(b).wait()
            return c

        lax.fori_loop(nu_ref[0], n_blocks, start_tail, 0)
        lax.fori_loop(nu_ref[0], n_blocks, wait_tail, 0)

    def row_copy(r, k):
        p = pos_ref[2 * (i * tm + r) + k]
        dst = xs_hbm.at[pl.ds(pl.multiple_of(p * ROW_TILES, ROW_TILES), ROW_TILES), :]
        return pltpu.make_async_copy(h_ref.at[pl.ds(r * ROW_TILES, ROW_TILES), :], dst, sem)

    for r in range(tm):
        row_copy(r, 0).start(priority=0)
        row_copy(r, 1).start(priority=1)
    for _ in range(tm):
        row_copy(0, 0).wait()
        row_copy(0, 1).wait()


def _dispatch(pos, zstart, zvalid, n_used, h3t, cap):
    T = h3t.shape[0] // ROW_TILES
    tm = ROW_DMA_TILE
    grid_spec = pltpu.PrefetchScalarGridSpec(
        num_scalar_prefetch=4,
        grid=(T // tm,),
        in_specs=[pl.BlockSpec((tm * ROW_TILES, LANES), lambda i, ps, zs, zv, nu: (i, 0))],
        out_specs=pl.BlockSpec(memory_space=pl.ANY),
        scratch_shapes=[pltpu.VMEM((MOE_BLOCK * ROW_TILES, LANES), U32), pltpu.SemaphoreType.DMA],
    )
    return pl.pallas_call(
        _dispatch_kernel,
        grid_spec=grid_spec,
        out_shape=jax.ShapeDtypeStruct((cap * ROW_TILES, LANES), U32),
        compiler_params=_cparams(("arbitrary",)),
        name="dispatch",
    )(pos, zstart, zvalid, n_used, h3t)


def _moe_kernel(be_ref, nu_ref, x_ref, wg_ref, wu_ref, wd_ref, o_ref):
    i = pl.program_id(0)

    @pl.when(i < nu_ref[0])
    def _():
        xb = jnp.concatenate([x.astype(BF16) for x in _load_token_tiles(x_ref, MOE_BLOCK)], axis=-1)
        gate = _bdot(xb, wg_ref[0])
        up = _bdot(xb, wu_ref[0])
        hid = gate * _sigmoid(gate) * up
        _store_token_tiles(o_ref, _bdot(hid, wd_ref[0]))

    @pl.when(i >= nu_ref[0])
    def _():
        o_ref[...] = jnp.zeros_like(o_ref)


def _moe(block_expert, n_used, xs, wg, wu, wd):
    n_blocks = xs.shape[0] // (MOE_BLOCK * ROW_TILES)
    used = lambda i, nu: jnp.minimum(i, jnp.maximum(nu[0] - 1, 0))
    grid_spec = pltpu.PrefetchScalarGridSpec(
        num_scalar_prefetch=2,
        grid=(n_blocks,),
        in_specs=[
            pl.BlockSpec((MOE_BLOCK * ROW_TILES, LANES), lambda i, be, nu: (used(i, nu), 0)),
            pl.BlockSpec((1, D_MODEL, D_EXPERT), lambda i, be, nu: (be[i], 0, 0)),
            pl.BlockSpec((1, D_MODEL, D_EXPERT), lambda i, be, nu: (be[i], 0, 0)),
            pl.BlockSpec((1, D_EXPERT, D_MODEL), lambda i, be, nu: (be[i], 0, 0)),
        ],
        out_specs=pl.BlockSpec((MOE_BLOCK * ROW_TILES, LANES), lambda i, be, nu: (i, 0)),
    )
    return pl.pallas_call(
        _moe_kernel,
        grid_spec=grid_spec,
        out_shape=jax.ShapeDtypeStruct(xs.shape, U32),
        compiler_params=_cparams(("arbitrary",)),
        name="moe",
    )(block_expert, n_used, xs, wg, wu, wd)


def _moe_plan(expert, rank, counts, n_tok):
    n_assign = n_tok * 2
    e_flat = expert.reshape(n_assign)
    rank = rank.reshape(n_assign)
    padded = (counts + MOE_BLOCK - 1) // MOE_BLOCK * MOE_BLOCK
    p_end = jnp.cumsum(padded)
    p_start = p_end - padded
    pos = (p_start[e_flat] + rank).astype(jnp.int32)
    n_blocks = (n_assign + N_EXPERTS * (MOE_BLOCK - 1) + MOE_BLOCK - 1) // MOE_BLOCK
    cap = n_blocks * MOE_BLOCK
    block_start = jnp.arange(n_blocks, dtype=jnp.int32) * MOE_BLOCK
    block_expert = jnp.minimum(jnp.sum(p_end[None, :] <= block_start[:, None], axis=1),
                               N_EXPERTS - 1).astype(jnp.int32)
    n_used = (p_end[-1] // MOE_BLOCK).astype(jnp.int32).reshape(1)
    last_block = jnp.maximum(p_end - MOE_BLOCK, 0).astype(jnp.int32)
    nonempty = (padded > 0).astype(jnp.int32)
    return pos, cap, block_expert, n_used, last_block, nonempty


def _final_kernel(pos_ref, x_ref, rt_ref, y_hbm, g_ref, o_ref, y1_ref, y2_ref, sem):
    i = pl.program_id(0)
    n = pl.num_programs(0)
    tm = x_ref.shape[0]
    slot = i % 2

    def copies(tile, r, s):
        p1 = pos_ref[2 * (tile * tm + r)]
        p2 = pos_ref[2 * (tile * tm + r) + 1]
        src = lambda p: y_hbm.at[pl.ds(pl.multiple_of(p * ROW_TILES, ROW_TILES), ROW_TILES), :]
        dst = lambda ref: ref.at[s, pl.ds(r * ROW_TILES, ROW_TILES), :]
        return (pltpu.make_async_copy(src(p1), dst(y1_ref), sem.at[s]),
                pltpu.make_async_copy(src(p2), dst(y2_ref), sem.at[s]))

    def gather(tile, s):
        for r in range(tm):
            c1, c2 = copies(tile, r, s)
            c1.start(priority=0)
            c2.start(priority=1)

    def wait_rows(s):
        for _ in range(tm):
            c1, c2 = copies(0, 0, s)
            c1.wait()
            c2.wait()

    @pl.when(i == 0)
    def _():
        gather(0, 0)

    wait_rows(slot)
    gather(jnp.minimum(i + 1, n - 1), 1 - slot)
    rt = rt_ref[...]
    g1 = rt[:, 2:3]
    g2 = rt[:, 3:4]
    y1 = jnp.concatenate(_load_token_tiles(y1_ref.at[slot], tm), axis=-1)
    y2 = jnp.concatenate(_load_token_tiles(y2_ref.at[slot], tm), axis=-1)
    x3 = x_ref[...] + y1 * g1 + y2 * g2
    o_ref[...] = _rms(x3, g_ref[...])

    @pl.when(i == n - 1)
    def _():
        wait_rows(1 - slot)


def _final(pos, x2, rt, y_sorted, g):
    T = x2.shape[0]
    tm = ROW_DMA_TILE
    grid_spec = pltpu.PrefetchScalarGridSpec(
        num_scalar_prefetch=1,
        grid=(T // tm,),
        in_specs=[
            pl.BlockSpec((tm, D_MODEL), lambda i, ps: (i, 0)),
            pl.BlockSpec((tm, LANES), lambda i, ps: (i, 0)),
            pl.BlockSpec(memory_space=pl.ANY),
            pl.BlockSpec((1, D_MODEL), lambda i, ps: (0, 0)),
        ],
        out_specs=pl.BlockSpec((tm, D_MODEL), lambda i, ps: (i, 0)),
        scratch_shapes=[pltpu.VMEM((2, tm * ROW_TILES, LANES), U32),
                        pltpu.VMEM((2, tm * ROW_TILES, LANES), U32),
                        pltpu.SemaphoreType.DMA((2,))],
    )
    return pl.pallas_call(
        _final_kernel,
        grid_spec=grid_spec,
        out_shape=jax.ShapeDtypeStruct((T, D_MODEL), F32),
        compiler_params=_cparams(("arbitrary",)),
        name="final",
    )(pos, x2, rt, y_sorted, g)


def _lora_weight(decay_up, iclr_up, gate_up):
    w = jnp.zeros((LORA_COLS, 3 * WIDTH), F32)
    w = w.at[0:DECAY_LORA, 0:WIDTH].set(decay_up)
    w = w.at[DECAY_LORA:DECAY_LORA + ICLR_LORA, WIDTH:2 * WIDTH].set(iclr_up)
    w = w.at[DECAY_LORA + ICLR_LORA:, 2 * WIDTH:].set(gate_up)
    return w


def _router_weight(w_group, w_expert):
    w = jnp.zeros((D_MODEL, LANES), F32)
    w = w.at[:, 0:N_GROUPS].set(w_group)
    return w.at[:, N_GROUPS:N_GROUPS + N_EXPERTS].set(w_expert)


def kernel(x, mem, rel_bias, mem_norm, norm_mix, w_in, tshift_mu, decay_w0, decay_up, iclr_a0,
           iclr_up, gate_up, k_k, k_a, r_k, ln_x_w, ln_x_b, w_o_rwkv, w_o_moba, w_out,
           norm_xattn, w_q_x, w_kv_x, w_o_x, norm_ffn, w_router_group, w_router_expert,
           w_exp_gate, w_exp_up, w_exp_down, norm_final):
    B, T, D = x.shape
    assert B == 1 and D == D_MODEL and T % TOKEN_TILE == 0 and T // MOBA_BLOCK <= HEAD_DIM
    assert norm_mix.shape[0] == 1
    row = lambda a: a.reshape(1, -1)
    x2d = x.reshape(T, D)
    kv = _memkv(mem.reshape(N_MEM, D), row(mem_norm), w_kv_x[0].astype(BF16))

    ur, q, gates, k_aug, vt_aug, kmean = _inproj(x2d, row(norm_mix[0]), w_in[0].astype(BF16))
    kmean = kmean.reshape(T // MOBA_BLOCK, WIDTH)
    o_a = _rwkv_core(ur, row(tshift_mu[0]), row(decay_w0[0]), row(iclr_a0[0]), row(k_k[0]),
                     row(k_a[0]), row(r_k[0]),
                     _lora_weight(decay_up[0], iclr_up[0], gate_up[0]).astype(BF16),
                     row(ln_x_w[0]), row(ln_x_b[0]))

    o_b = _moba(q, k_aug, vt_aug, _score_mats(kmean), _bias_tiles(rel_bias))

    x2, h3, rt, counts = _merge(
        x2d, o_a, o_b, gates, w_o_rwkv[0].astype(BF16), w_o_moba[0].astype(BF16),
        w_out[0].astype(BF16), row(norm_xattn[0]), w_q_x[0].astype(BF16), kv,
        w_o_x[0].astype(BF16), row(norm_ffn[0]),
        _router_weight(w_router_group[0], w_router_expert[0]).astype(BF16))

    expert = rt[:, 0:2].astype(jnp.int32)
    rank = rt[:, 4:6].astype(jnp.int32)
    pos, cap, block_expert, n_used, last_block, nonempty = _moe_plan(
        expert, rank, counts[N_GROUPS:N_GROUPS + N_EXPERTS, 0].astype(jnp.int32), T)
    x_sorted = _dispatch(pos, last_block, nonempty, n_used, h3, cap)
    y_sorted = _moe(block_expert, n_used, x_sorted, w_exp_gate[0], w_exp_up[0], w_exp_down[0])
    out = _final(pos, x2, rt, y_sorted, row(norm_final))
    return out.reshape(B, T, D)
```

```python
import math

import jax
import jax.numpy as jnp
from jax import lax
from jax.experimental import pallas as pl
from jax.experimental.pallas import tpu as pltpu

F32 = jnp.float32
BF16 = jnp.bfloat16

D_MODEL = 1024
N_MEM = 256
NORM_EPS = 1e-6
NEG_INF = -1e30

HEADS = 8
HEAD_DIM = 64
WIDTH = HEADS * HEAD_DIM
DECAY_LORA = 64
ICLR_LORA = 64
GATE_LORA = 128
LORA_COLS = DECAY_LORA + ICLR_LORA + GATE_LORA
RWKV_COLS = 3 * WIDTH + LORA_COLS
QKV_COLS = 3 * WIDTH
GATE_COLS = 2 * D_MODEL
IN_COLS = RWKV_COLS + QKV_COLS + GATE_COLS
LN_X_EPS = 64e-5
KK_EPS = 1e-12

MOBA_BLOCK = 256
MOBA_TOP = 3
REL_BUCKETS = 32
REL_MAX_DISTANCE = 4096
N_BIAS_TILES = 14

XATTN_HEADS = 4
XATTN_HEAD_DIM = 128
XATTN_WIDTH = XATTN_HEADS * XATTN_HEAD_DIM

N_GROUPS = 4
EXPERTS_PER_GROUP = 8
N_EXPERTS = N_GROUPS * EXPERTS_PER_GROUP
D_EXPERT = 512
MOE_BLOCK = 256

TOKEN_TILE = 512
ROW_DMA_TILE = 256
Q_BLOCKS = 2
VT_ROWS = HEAD_DIM + 16
CHUNK = 64
CHUNKS_PER_STEP = 4
LANES = 128
VMEM_LIMIT = 48 * 1024 * 1024

LOG2E = math.log2(math.e)


def _cparams(sem):
    return pltpu.CompilerParams(dimension_semantics=sem, vmem_limit_bytes=VMEM_LIMIT)


def _bdot(a, b):
    return jnp.dot(a.astype(BF16), b.astype(BF16), preferred_element_type=F32)


def _bdot_nt(a, b):
    return lax.dot_general(a.astype(BF16), b.astype(BF16), (((1,), (1,)), ((), ())),
                           preferred_element_type=F32)


def _bdot_tn(a, b):
    return lax.dot_general(a.astype(BF16), b.astype(BF16), (((0,), (0,)), ((), ())),
                           preferred_element_type=F32)


def _rms(x, g):
    return x * lax.rsqrt(jnp.mean(x * x, axis=-1, keepdims=True) + NORM_EPS) * g


def _sigmoid(x):
    return 1.0 / (1.0 + jnp.exp(-x))


ROW_TILES = D_MODEL // (2 * LANES)
U32 = jnp.uint32
HI_MASK = 0xFFFF0000


def _bits(x):
    return pltpu.bitcast(x.astype(BF16).astype(F32), U32)


def _store_token_tiles(ref, x):
    n = x.shape[0]
    for s in range(ROW_TILES):
        lo = x[:, 2 * s * LANES:(2 * s + 1) * LANES]
        hi = x[:, (2 * s + 1) * LANES:(2 * s + 2) * LANES]
        word = (_bits(hi) & U32(HI_MASK)) | lax.shift_right_logical(_bits(lo), U32(16))
        ref[pl.ds(s, n, stride=ROW_TILES), :] = word


def _load_token_tiles(ref, n):
    out = []
    for s in range(ROW_TILES):
        word = ref[pl.ds(s, n, stride=ROW_TILES), :]
        out.append(pltpu.bitcast(lax.shift_left(word, U32(16)), F32))
        out.append(pltpu.bitcast(word & U32(HI_MASK), F32))
    return out


def _inproj_kernel(x_ref, g_ref, w_ref, ur_ref, q_ref, gate_ref, ka_ref, vt_ref, km_ref):
    i = pl.program_id(0)
    B = MOBA_BLOCK
    h = _rms(x_ref[...], g_ref[...]).astype(BF16)
    tm = h.shape[0]
    step = 256
    for c0 in range(0, RWKV_COLS, step):
        ur_ref[:, c0:c0 + step] = jnp.dot(h, w_ref[:, c0:c0 + step], preferred_element_type=F32)
    for c0 in range(0, WIDTH, step):
        o = jnp.dot(h, w_ref[:, RWKV_COLS + c0:RWKV_COLS + c0 + step], preferred_element_type=F32)
        q_ref[:, c0:c0 + step] = (o * (HEAD_DIM ** -0.5 * LOG2E)).astype(BF16)
    lane = lax.broadcasted_iota(jnp.int32, (B, LANES), 1)
    ones = jnp.ones((VT_ROWS - HEAD_DIM, B), F32)
    for c0 in range(0, WIDTH, step):
        kcol = RWKV_COLS + WIDTH + c0
        k = jnp.dot(h, w_ref[:, kcol:kcol + step], preferred_element_type=F32).astype(BF16)
        v = jnp.dot(h, w_ref[:, kcol + WIDTH:kcol + WIDTH + step], preferred_element_type=F32)
        v = v.astype(BF16).astype(F32)
        for pp in range(step // LANES):
            pair = c0 // LANES + pp
            for jj in range(tm // B):
                rows = slice(jj * B, (jj + 1) * B)
                kblk = k[rows, pp * LANES:(pp + 1) * LANES]
                km_ref[0, jj:jj + 1, pair * LANES:(pair + 1) * LANES] = jnp.mean(
                    kblk.astype(F32), axis=0, keepdims=True)
                v_t = v[rows, pp * LANES:(pp + 1) * LANES].T
                for h2 in range(2):
                    off = HEAD_DIM * (1 - h2)
                    inr = (lane >= off) & (lane < off + HEAD_DIM)
                    hot = jnp.where(lane == off + i * (tm // B) + jj, 1.0, 0.0).astype(BF16)
                    ka_ref[2 * pair + h2, rows, :] = jnp.where(inr, hot, kblk)
                    tile = jnp.concatenate([v_t[h2 * HEAD_DIM:(h2 + 1) * HEAD_DIM], ones], axis=0)
                    vt_ref[2 * pair + h2, jj] = tile.astype(BF16)
    base = RWKV_COLS + QKV_COLS
    for c0 in range(0, GATE_COLS, step):
        o = jnp.dot(h, w_ref[:, base + c0:base + c0 + step], preferred_element_type=F32)
        gate_ref[:, c0:c0 + step] = o.astype(BF16)


def _inproj(x2d, g, w_bf):
    T = x2d.shape[0]
    tm = TOKEN_TILE
    bpt = tm // MOBA_BLOCK
    return pl.pallas_call(
        _inproj_kernel,
        grid=(T // tm,),
        in_specs=[
            pl.BlockSpec((tm, D_MODEL), lambda i: (i, 0)),
            pl.BlockSpec((1, D_MODEL), lambda i: (0, 0)),
            pl.BlockSpec((D_MODEL, IN_COLS), lambda i: (0, 0), pipeline_mode=pl.Buffered(1)),
        ],
        out_specs=[
            pl.BlockSpec((tm, RWKV_COLS), lambda i: (i, 0)),
            pl.BlockSpec((tm, WIDTH), lambda i: (i, 0)),
            pl.BlockSpec((tm, GATE_COLS), lambda i: (i, 0)),
            pl.BlockSpec((HEADS, tm, LANES), lambda i: (0, i, 0)),
            pl.BlockSpec((HEADS, bpt, VT_ROWS, MOBA_BLOCK), lambda i: (0, i, 0, 0)),
            pl.BlockSpec((1, bpt, WIDTH), lambda i: (i, 0, 0)),
        ],
        out_shape=[
            jax.ShapeDtypeStruct((T, RWKV_COLS), F32),
            jax.ShapeDtypeStruct((T, WIDTH), BF16),
            jax.ShapeDtypeStruct((T, GATE_COLS), BF16),
            jax.ShapeDtypeStruct((HEADS, T, LANES), BF16),
            jax.ShapeDtypeStruct((HEADS, T // MOBA_BLOCK, VT_ROWS, MOBA_BLOCK), BF16),
            jax.ShapeDtypeStruct((T // tm, bpt, WIDTH), F32),
        ],
        compiler_params=_cparams(("parallel",)),
        name="inproj",
    )(x2d, g, w_bf)


def _head_ones():
    r = lax.broadcasted_iota(jnp.int32, (LANES, LANES), 0) // HEAD_DIM
    c = lax.broadcasted_iota(jnp.int32, (LANES, LANES), 1) // HEAD_DIM
    return jnp.where(r == c, 1.0, 0.0).astype(BF16)


def _head_sum(x, ones):
    parts = [_bdot(x[:, c:c + LANES], ones) for c in range(0, WIDTH, LANES)]
    return jnp.concatenate(parts, axis=-1)


def _rwkv_prep_values(u_ref, up_ref, mu_ref, w0_ref, a0_ref, kk_ref, ka_ref, rk_ref, wl_ref):
    i = pl.program_id(0)
    u = u_ref[...]
    tm = u.shape[0]
    prev_last = up_ref[7:8, :] * jnp.where(i > 0, 1.0, 0.0)
    rolled = pltpu.roll(u, 1, 0)
    row = lax.broadcasted_iota(jnp.int32, u.shape, 0)
    u_prev = jnp.where(row == 0, prev_last, rolled)
    u = u + mu_ref[...] * (u_prev - u)
    r = u[:, 0:WIDTH]
    k = u[:, WIDTH:2 * WIDTH]
    v = u[:, 2 * WIDTH:3 * WIDTH]
    lo = u[:, 3 * WIDTH:3 * WIDTH + LORA_COLS]
    lane = lax.broadcasted_iota(jnp.int32, lo.shape, 1)
    act = jnp.where(lane < DECAY_LORA, jnp.tanh(lo),
                    jnp.where(lane < DECAY_LORA + ICLR_LORA, lo, _sigmoid(lo)))
    up = _bdot(act, wl_ref[...])
    z = -(w0_ref[...] + up[:, 0:WIDTH])
    softplus = jnp.maximum(z, 0.0) + jnp.log(1.0 + jnp.exp(-jnp.abs(z)))
    w_log = -softplus - 0.5
    lw = -jnp.exp(w_log)
    iclr = _sigmoid(a0_ref[...] + up[:, WIDTH:2 * WIDTH])
    g = up[:, 2 * WIDTH:3 * WIDTH]
    ones = _head_ones()
    kk = k * kk_ref[...]
    kk = kk * lax.rsqrt(_head_sum(kk * kk, ones) + KK_EPS)
    k2 = k * (1.0 + (iclr - 1.0) * ka_ref[...])
    bonus = _head_sum(r * k2 * rk_ref[...], ones) * v
    return r, lw, k2, v, -kk, kk * iclr, g, bonus


def _rwkv_core_kernel(u_ref, up_ref, mu_ref, w0_ref, a0_ref, kk_ref, ka_ref, rk_ref, wl_ref,
                      lnw_ref, lnb_ref, o_ref, s_ref):
    c = pl.program_id(0)

    @pl.when(c == 0)
    def _():
        s_ref[...] = jnp.zeros_like(s_ref)

    C = CHUNK
    G = CHUNKS_PER_STEP
    ri = lax.broadcasted_iota(jnp.int32, (C, C), 0)
    ci = lax.broadcasted_iota(jnp.int32, (C, C), 1)
    rg = lax.broadcasted_iota(jnp.int32, (G * C, G * C), 0)
    cg = lax.broadcasted_iota(jnp.int32, (G * C, G * C), 1)
    r_in, lw, k_in, v_all, a_in, b_in, g_in, bonus = _rwkv_prep_values(
        u_ref, up_ref, mu_ref, w0_ref, a0_ref, kk_ref, ka_ref, rk_ref, wl_ref)
    tri = jnp.where((rg >= cg) & (rg // C == cg // C), 1.0, 0.0).astype(BF16)
    lw_hi = lw.astype(BF16)
    lw_r1 = lw - lw_hi.astype(F32)
    lw_mid = lw_r1.astype(BF16)
    lw_lo = (lw_r1 - lw_mid.astype(F32)).astype(BF16)
    cum = (jnp.dot(tri, lw_hi, preferred_element_type=F32)
           + jnp.dot(tri, lw_mid, preferred_element_type=F32)
           + jnp.dot(tri, lw_lo, preferred_element_type=F32))
    lam = jnp.exp(cum)
    inv_lam = jnp.exp(-cum)
    r_t = r_in * lam
    a_t = a_in * jnp.exp(cum - lw)
    b_t = b_in * inv_lam
    k_t = k_in * inv_lam
    tots = [cum[g * C + C - 1:g * C + C, :] for g in range(G)]
    rowg = lax.broadcasted_iota(jnp.int32, cum.shape, 0) // C
    tot = tots[G - 1]
    for g in range(G - 2, -1, -1):
        tot = jnp.where(rowg == g, tots[g], tot)
    rest = jnp.exp(tot - cum)
    b_h = b_in * rest
    k_h = k_in * rest
    lam_c = [jnp.exp(t) for t in tots]
    eye = jnp.where(ri == ci, 1.0, 0.0)
    H = range(G * HEADS)
    sls = [slice(h * HEAD_DIM, (h + 1) * HEAD_DIM) for h in range(HEADS)]
    bf = lambda x: x.astype(BF16)
    part = lambda x, i: x[(i // HEADS) * C:(i // HEADS + 1) * C, sls[i % HEADS]]
    at = [bf(part(a_t, i)) for i in H]
    rt = [part(r_t, i) for i in H]
    bt = [bf(part(b_t, i)) for i in H]
    kt = [bf(part(k_t, i)) for i in H]
    bh = [bf(part(b_h, i)) for i in H]
    kh = [bf(part(k_h, i)) for i in H]
    vv = [bf(part(v_all, i)) for i in H]
    ci2 = lax.broadcasted_iota(jnp.int32, (C, 2 * C), 1)
    ri2 = lax.broadcasted_iota(jnp.int32, (C, 2 * C), 0)
    cm2 = jnp.where(ci2 >= C, ci2 - C, ci2)
    left = ci2 < C
    ar = [jnp.concatenate([at[h], bf(rt[h])], axis=0) for h in H]
    bk = [jnp.concatenate([bt[h], kt[h]], axis=0) for h in H]
    g = [_bdot_nt(ar[h], bk[h]) for h in H]
    top = [jnp.where(ri2 > cm2, g[h][0:C], 0.0) for h in H]
    bot = [bf(jnp.where(ri2 >= cm2, g[h][C:2 * C], 0.0)) for h in H]
    a_ab = [top[h][:, 0:C] for h in H]
    akv = [_bdot(top[h][:, C:2 * C], vv[h]) for h in H]
    z = [jnp.concatenate([a_ab[h], eye], axis=1) for h in H]
    for _ in range(6):
        z = [_bdot(z[h][:, 0:C], z[h]) + jnp.where(left, 0.0, z[h]) for h in H]
    tinv = [bf(z[h][:, C:2 * C]) for h in H]
    wu = [_bdot(tinv[h], jnp.concatenate([at[h], bf(akv[h])], axis=1)) for h in H]
    w_m = [bf(wu[h][:, 0:C]) for h in H]
    uv = [jnp.concatenate([bf(wu[h][:, C:2 * C]), vv[h]], axis=0) for h in H]
    q_m = [rt[h] + _bdot(bot[h][:, 0:C], w_m[h]) for h in H]
    y0 = [_bdot(bot[h], uv[h]) for h in H]
    m_k = [_bdot_tn(w_m[h], bh[h]) for h in H]
    n0 = [_bdot_tn(uv[h], jnp.concatenate([bh[h], kh[h]], axis=0)) for h in H]
    state = [s_ref[h] for h in range(HEADS)]
    blocks = []
    for g in range(G):
        outs = []
        for h in range(HEADS):
            i = g * HEADS + h
            y = _bdot_nt(q_m[i], state[h]) + y0[i]
            state[h] = state[h] * lam_c[g][:, sls[h]] + _bdot(state[h], m_k[i]) + n0[i]
            mean = jnp.mean(y, axis=-1, keepdims=True)
            var = jnp.mean(jnp.square(y - mean), axis=-1, keepdims=True)
            outs.append((y - mean) * lax.rsqrt(var + LN_X_EPS))
        blocks.append(jnp.concatenate(outs, axis=-1))
    for h in range(HEADS):
        s_ref[h] = state[h]
    yn = jnp.concatenate(blocks, axis=0)
    yn = yn * lnw_ref[...] + lnb_ref[...] + bonus
    o_ref[...] = (yn * g_in).astype(BF16)


def _rwkv_core(ur, mu, w0, a0, k_k, k_a, r_k, w_lora, ln_w, ln_b):
    T = ur.shape[0]
    rows = CHUNK * CHUNKS_PER_STEP
    row = lambda w: pl.BlockSpec((1, w), lambda c: (0, 0))
    return pl.pallas_call(
        _rwkv_core_kernel,
        grid=(T // rows,),
        in_specs=[
            pl.BlockSpec((rows, RWKV_COLS), lambda c: (c, 0)),
            pl.BlockSpec((8, RWKV_COLS), lambda c: (jnp.maximum(c * (rows // 8) - 1, 0), 0)),
            row(RWKV_COLS), row(WIDTH), row(WIDTH), row(WIDTH), row(WIDTH), row(WIDTH),
            pl.BlockSpec((LORA_COLS, 3 * WIDTH), lambda c: (0, 0)),
            row(WIDTH), row(WIDTH),
        ],
        out_specs=pl.BlockSpec((rows, WIDTH), lambda c: (c, 0)),
        out_shape=jax.ShapeDtypeStruct((T, WIDTH), BF16),
        scratch_shapes=[pltpu.VMEM((HEADS, HEAD_DIM, HEAD_DIM), F32)],
        compiler_params=_cparams(("arbitrary",)),
        name="rwkv_core",
    )(ur, ur, mu, w0, a0, k_k, k_a, r_k, w_lora, ln_w, ln_b)


def _t5_bucket(dist):
    n = jnp.maximum(dist, 0)
    max_exact = REL_BUCKETS // 2
    nf = jnp.maximum(n, max_exact).astype(jnp.float32)
    large = max_exact + (jnp.log(nf / max_exact) / math.log(REL_MAX_DISTANCE / max_exact)
                         * (REL_BUCKETS - max_exact)).astype(jnp.int32)
    large = jnp.minimum(large, REL_BUCKETS - 1)
    return jnp.where(n < max_exact, n, large)


def _bucket_tiles():
    i = jnp.arange(MOBA_BLOCK)[None, :]
    j = jnp.arange(MOBA_BLOCK)[:, None]
    d = jnp.arange(N_BIAS_TILES + 1)[:, None, None]
    dist = d * MOBA_BLOCK + i - j
    bucket = _t5_bucket(dist)
    bucket = jnp.where(d == N_BIAS_TILES - 1, REL_BUCKETS - 1, bucket)
    return jnp.where((dist < 0) | (d == N_BIAS_TILES), -1, bucket).astype(jnp.int32)


BIAS_ROWS = 16


def _bias_tiles_kernel(idx_ref, rb_ref, o_ref):
    def rows(c, carry):
        r0 = pl.multiple_of(c * BIAS_ROWS, BIAS_ROWS)
        idx = idx_ref[0, pl.ds(r0, BIAS_ROWS), :]
        acc = [jnp.where(idx < 0, NEG_INF, 0.0)] * HEADS
        for bkt in range(REL_BUCKETS):
            hit = idx == bkt
            acc = [jnp.where(hit, rb_ref[bkt, h] * LOG2E, acc[h]) for h in range(HEADS)]
        for h in range(HEADS):
            o_ref[h, 0, pl.ds(r0, BIAS_ROWS), :] = acc[h]
        return carry

    lax.fori_loop(0, MOBA_BLOCK // BIAS_ROWS, rows, 0)


def _bias_tiles(rel_bias):
    idx = _bucket_tiles()
    n = N_BIAS_TILES + 1
    return pl.pallas_call(
        _bias_tiles_kernel,
        grid=(n,),
        in_specs=[
            pl.BlockSpec((1, MOBA_BLOCK, MOBA_BLOCK), lambda d: (d, 0, 0)),
            pl.BlockSpec(memory_space=pltpu.SMEM),
        ],
        out_specs=pl.BlockSpec((HEADS, 1, MOBA_BLOCK, MOBA_BLOCK), lambda d: (0, d, 0, 0)),
        out_shape=jax.ShapeDtypeStruct((HEADS, n, MOBA_BLOCK, MOBA_BLOCK), F32),
        compiler_params=_cparams(("parallel",)),
        name="bias_tiles",
    )(idx, rel_bias)


def _moba_kernel(q_ref, ka_ref, vt_ref, r_ref, bias_ref, o_ref, s_ref, p_ref, acc_ref):
    g = pl.program_id(1)
    B = MOBA_BLOCK
    chains = [(qi, h2) for qi in range(Q_BLOCKS) for h2 in range(2)]
    qbs = [Q_BLOCKS * g + qi for qi in range(Q_BLOCKS)]
    q_all = q_ref[...].astype(F32).T
    blk = lax.broadcasted_iota(jnp.int32, (HEAD_DIM, B), 0)
    big = jnp.int32(1 << 20)
    q_t, q_own_t = [], []
    for qi, h2 in chains:
        q_tr = q_all[:, qi * B:(qi + 1) * B]
        off = HEAD_DIM * (1 - h2)
        valid = blk < qbs[qi]
        sc = jnp.dot(r_ref[0, h2, off:off + HEAD_DIM, :], q_tr.astype(BF16),
                     preferred_element_type=F32)
        s = jnp.where(valid, sc, NEG_INF)
        sel = jnp.zeros((HEAD_DIM, B), jnp.bool_)
        for _ in range(MOBA_TOP):
            m = jnp.max(s, axis=0, keepdims=True)
            idx = jnp.min(jnp.where(s == m, blk, big), axis=0, keepdims=True)
            pick = blk == idx
            sel = jnp.logical_or(sel, pick)
            s = jnp.where(pick, -jnp.inf, s)
        sel = jnp.logical_and(sel, valid)
        choice = jnp.where(sel, 0.0, NEG_INF)
        q_head = q_tr[h2 * HEAD_DIM:(h2 + 1) * HEAD_DIM]
        parts = [q_head, choice] if h2 == 0 else [choice, q_head]
        own = [q_head, jnp.zeros_like(choice)] if h2 == 0 else [jnp.zeros_like(choice), q_head]
        q_t.append(jnp.concatenate(parts, axis=0).astype(BF16))
        q_own_t.append(jnp.concatenate(own, axis=0).astype(BF16))

    carry = []
    for c, (qi, h2) in enumerate(chains):
        k_own = ka_ref[h2, pl.ds(pl.multiple_of(qbs[qi] * B, B), B), :]
        s0 = jnp.dot(k_own, q_own_t[c], preferred_element_type=F32) + bias_ref[h2, 0]
        m0 = jnp.max(s0, axis=0, keepdims=True)
        p0 = jnp.exp2(s0 - m0).astype(BF16)
        carry += [m0, jnp.dot(vt_ref[h2, qbs[qi]], p0, preferred_element_type=F32)]

    n_tiles = (qbs[-1] + 1) // 2
    last = n_tiles - 1
    for c in range(len(chains)):
        acc_ref[c] = carry[2 * c + 1]

    def trip(t, w, stats, scores=True, softmax=True, accumulate=True):
        r = 1 - w
        ok = (t >= 1) & (t <= n_tiles)
        start = pl.multiple_of(jnp.clip(t, 0, last) * (2 * B), 2 * B)
        v_a = 2 * jnp.clip(t - 2, 0, last)
        v_b = v_a + 1
        out = []
        for c, (qi, h2) in enumerate(chains):
            d_a = jnp.where(ok, jnp.clip(qbs[qi] - 2 * (t - 1), 0, N_BIAS_TILES - 1), N_BIAS_TILES)
            d_b = jnp.where(ok, jnp.clip(qbs[qi] - 2 * (t - 1) - 1, 0, N_BIAS_TILES - 1), N_BIAS_TILES)
            m_prev, alpha_p = stats[2 * c], stats[2 * c + 1]
            if accumulate:
                acc_ref[c] = (acc_ref[c] * alpha_p
                              + jnp.dot(vt_ref[h2, v_a], p_ref[r, c, 0:B], preferred_element_type=F32)
                              + jnp.dot(vt_ref[h2, v_b], p_ref[r, c, B:2 * B],
                                        preferred_element_type=F32))
            if softmax:
                s_a = s_ref[r, c, 0:B] + bias_ref[h2, d_a]
                s_b = s_ref[r, c, B:2 * B] + bias_ref[h2, d_b]
                m_new = jnp.maximum(m_prev, jnp.max(jnp.maximum(s_a, s_b), axis=0, keepdims=True))
                out += [m_new, jnp.exp2(m_prev - m_new)]
                p_ref[w, c, 0:B] = jnp.exp2(s_a - m_new).astype(BF16)
                p_ref[w, c, B:2 * B] = jnp.exp2(s_b - m_new).astype(BF16)
            else:
                out += [m_prev, alpha_p]
            if scores:
                s_ref[w, c] = jnp.dot(ka_ref[h2, pl.ds(start, 2 * B), :], q_t[c],
                                      preferred_element_type=F32)
        return out

    def body(u, stats):
        stats = trip(2 * u, 0, list(stats))
        return tuple(trip(2 * u + 1, 1, stats))

    stats = []
    for c in range(len(chains)):
        stats += [carry[2 * c], jnp.ones_like(carry[2 * c])]
    stats = trip(0, 0, stats, softmax=False, accumulate=False)
    stats = trip(1, 1, stats, accumulate=False)
    stats = list(lax.fori_loop(1, n_tiles // 2, body, tuple(stats)))
    odd = n_tiles % 2 == 1

    @pl.when(jnp.logical_not(odd) & (n_tiles >= 2))
    def _():
        st = trip(n_tiles, 0, stats, scores=False)
        trip(n_tiles + 1, 1, st, scores=False, softmax=False)

    @pl.when(odd & (n_tiles >= 3))
    def _():
        st = trip(n_tiles - 1, 0, stats)
        st = trip(n_tiles, 1, st, scores=False)
        trip(n_tiles + 1, 0, st, scores=False, softmax=False)

    @pl.when(n_tiles == 1)
    def _():
        trip(2, 0, stats, scores=False, softmax=False)
    outs = [acc_ref[c, 0:HEAD_DIM] / acc_ref[c, HEAD_DIM:HEAD_DIM + 1] for c in range(len(chains))]
    for qi in range(Q_BLOCKS):
        pair = jnp.concatenate(outs[2 * qi:2 * qi + 2], axis=0)
        o_ref[qi * B:(qi + 1) * B, :] = pair.T.astype(BF16)


def _moba(q, k_aug, vt_aug, r_mats, bias_tiles):
    T = q.shape[0]
    nb = T // MOBA_BLOCK
    npair = HEADS // 2
    nc = 2 * Q_BLOCKS
    once = pl.Buffered(1)
    return pl.pallas_call(
        _moba_kernel,
        grid=(npair, nb // Q_BLOCKS),
        in_specs=[
            pl.BlockSpec((Q_BLOCKS * MOBA_BLOCK, LANES), lambda p, qb: (qb, p)),
            pl.BlockSpec((2, T, LANES), lambda p, qb: (p, 0, 0), pipeline_mode=once),
            pl.BlockSpec((2, nb, VT_ROWS, MOBA_BLOCK), lambda p, qb: (p, 0, 0, 0), pipeline_mode=once),
            pl.BlockSpec((1, 2, LANES, LANES), lambda p, qb: (p, 0, 0, 0)),
            pl.BlockSpec((2, N_BIAS_TILES + 1, MOBA_BLOCK, MOBA_BLOCK), lambda p, qb: (p, 0, 0, 0),
                         pipeline_mode=once),
        ],
        out_specs=pl.BlockSpec((Q_BLOCKS * MOBA_BLOCK, LANES), lambda p, qb: (qb, p)),
        out_shape=jax.ShapeDtypeStruct((T, WIDTH), BF16),
        scratch_shapes=[pltpu.VMEM((2, nc, 2 * MOBA_BLOCK, MOBA_BLOCK), F32),
                        pltpu.VMEM((2, nc, 2 * MOBA_BLOCK, MOBA_BLOCK), BF16),
                        pltpu.VMEM((nc, VT_ROWS, MOBA_BLOCK), F32)],
        compiler_params=_cparams(("arbitrary", "arbitrary")),
        name="moba",
    )(q, k_aug, vt_aug, r_mats, bias_tiles)


def _score_mats(kmean):
    nb = kmean.shape[0]
    km = kmean.reshape(nb, HEADS, HEAD_DIM).transpose(1, 2, 0)
    km = jnp.pad(km, ((0, 0), (0, 0), (0, HEAD_DIM - nb)))
    z = jnp.zeros((HEADS // 2, HEAD_DIM, HEAD_DIM), F32)
    even = jnp.concatenate([jnp.concatenate([z, km[0::2]], axis=2),
                            jnp.concatenate([z, z], axis=2)], axis=1)
    odd = jnp.concatenate([jnp.concatenate([z, z], axis=2),
                           jnp.concatenate([km[1::2], z], axis=2)], axis=1)
    return jnp.swapaxes(jnp.stack([even, odd], axis=1), -1, -2).astype(BF16)


def _memkv_kernel(m_ref, g_ref, w_ref, o_ref):
    h = _rms(m_ref[...], g_ref[...]).astype(BF16)
    o_ref[...] = jnp.dot(h, w_ref[...], preferred_element_type=F32).astype(BF16)


def _memkv(mem2d, g, w_bf):
    return pl.pallas_call(
        _memkv_kernel,
        out_shape=jax.ShapeDtypeStruct((N_MEM, 2 * XATTN_WIDTH), BF16),
        compiler_params=pltpu.CompilerParams(vmem_limit_bytes=VMEM_LIMIT),
        name="memkv",
    )(mem2d, g, w_bf)


def _merge_kernel(x_ref, oa_ref, ob_ref, gt_ref, woa_ref, wob_ref, wout_ref, nx_ref, wq_ref,
                  kv_ref, wo_ref, nf_ref, wr_ref, x2_ref, h3_ref, rt_ref, cnt_ref, run_ref):
    pa = jnp.dot(oa_ref[...], woa_ref[...], preferred_element_type=F32)
    pb = jnp.dot(ob_ref[...], wob_ref[...], preferred_element_type=F32)
    ga = gt_ref[:, 0:D_MODEL].astype(F32)
    gb = gt_ref[:, D_MODEL:2 * D_MODEL].astype(F32)
    merged = _sigmoid(ga) * pa + _sigmoid(gb) * pb
    x1 = x_ref[...] + jnp.dot(merged.astype(BF16), wout_ref[...], preferred_element_type=F32)
    h2 = _rms(x1, nx_ref[...]).astype(BF16)
    q = jnp.dot(h2, wq_ref[...], preferred_element_type=F32).astype(BF16)
    heads = []
    for h in range(XATTN_HEADS):
        sl = slice(h * XATTN_HEAD_DIM, (h + 1) * XATTN_HEAD_DIM)
        km = kv_ref[:, sl]
        vm = kv_ref[:, XATTN_WIDTH + h * XATTN_HEAD_DIM:XATTN_WIDTH + (h + 1) * XATTN_HEAD_DIM]
        lg = lax.dot_general(q[:, sl], km, (((1,), (1,)), ((), ())),
                             preferred_element_type=F32) * (XATTN_HEAD_DIM ** -0.5)
        e = jnp.exp(lg - jnp.max(lg, axis=-1, keepdims=True))
        p = e / jnp.sum(e, axis=-1, keepdims=True)
        heads.append(jnp.dot(p.astype(BF16), vm, preferred_element_type=F32))
    o = jnp.concatenate(heads, axis=-1).astype(BF16)
    x2 = x1 + jnp.dot(o, wo_ref[...], preferred_element_type=F32)
    x2_ref[...] = x2
    h3 = _rms(x2, nf_ref[...])
    _store_token_tiles(h3_ref, h3)
    lg = _bdot(h3, wr_ref[...]).T
    tm = lg.shape[1]
    row = lax.broadcasted_iota(jnp.int32, lg.shape, 0)
    big = jnp.int32(1 << 20)
    is_g = row < N_GROUPS
    gl = jnp.where(is_g, lg, -jnp.inf)
    gmax = jnp.max(gl, axis=0, keepdims=True)
    gsel = jnp.min(jnp.where(gl == gmax, row, big), axis=0, keepdims=True)
    p_top = 1.0 / jnp.sum(jnp.where(is_g, jnp.exp(gl - gmax), 0.0), axis=0, keepdims=True)
    e_id = row - N_GROUPS
    in_grp = (e_id >= gsel * EXPERTS_PER_GROUP) & (e_id < (gsel + 1) * EXPERTS_PER_GROUP)
    el = jnp.where(in_grp, lg, -jnp.inf)
    v1 = jnp.max(el, axis=0, keepdims=True)
    i1 = jnp.min(jnp.where(el == v1, row, big), axis=0, keepdims=True)
    el2 = jnp.where(row == i1, -jnp.inf, el)
    v2 = jnp.max(el2, axis=0, keepdims=True)
    i2 = jnp.min(jnp.where(el2 == v2, row, big), axis=0, keepdims=True)
    e21 = jnp.exp(v2 - v1)
    g1 = p_top / (1.0 + e21)
    g2 = p_top * e21 / (1.0 + e21)
    step = pl.program_id(0)

    @pl.when(step == 0)
    def _():
        run_ref[...] = jnp.zeros_like(run_ref)

    oh1 = jnp.where(row == i1, 1.0, 0.0)
    oh2 = jnp.where(row == i2, 1.0, 0.0)
    both = oh1 + oh2
    ri = lax.broadcasted_iota(jnp.int32, (tm, tm), 0)
    ci = lax.broadcasted_iota(jnp.int32, (tm, tm), 1)
    before = jnp.dot(both, jnp.where(ri < ci, 1.0, 0.0), preferred_element_type=F32) + run_ref[:, 0:1]
    r1 = jnp.sum(before * oh1, axis=0, keepdims=True)
    r2 = jnp.sum(before * oh2, axis=0, keepdims=True)
    run_ref[...] = run_ref[...] + jnp.sum(both, axis=1, keepdims=True)
    cnt_ref[...] = run_ref[...]
    vals = [(i1 - N_GROUPS).astype(F32), (i2 - N_GROUPS).astype(F32), g1, g2, r1, r2]
    rt = jnp.zeros_like(lg)
    for k, v in enumerate(vals):
        rt = jnp.where(row == k, v, rt)
    rt_ref[...] = rt.T


def _merge(x2d, o_a, o_b, gates, w_oa, w_ob, w_out, n_x, w_q, kv, w_o, n_f, w_r):
    T = x2d.shape[0]
    tm = TOKEN_TILE
    full = lambda a: pl.BlockSpec(a.shape, lambda i: (0,) * a.ndim, pipeline_mode=pl.Buffered(1))
    tile = lambda w: pl.BlockSpec((tm, w), lambda i: (i, 0))
    return pl.pallas_call(
        _merge_kernel,
        grid=(T // tm,),
        in_specs=[tile(D_MODEL), tile(WIDTH), tile(WIDTH), tile(GATE_COLS), full(w_oa), full(w_ob),
                  full(w_out), full(n_x), full(w_q), full(kv), full(w_o), full(n_f), full(w_r)],
        out_specs=[tile(D_MODEL), pl.BlockSpec((tm * ROW_TILES, LANES), lambda i: (i, 0)), tile(LANES),
                   pl.BlockSpec((LANES, LANES), lambda i: (0, 0))],
        out_shape=[jax.ShapeDtypeStruct((T, D_MODEL), F32),
                   jax.ShapeDtypeStruct((T * ROW_TILES, LANES), U32),
                   jax.ShapeDtypeStruct((T, LANES), F32),
                   jax.ShapeDtypeStruct((LANES, LANES), F32)],
        scratch_shapes=[pltpu.VMEM((LANES, LANES), F32)],
        compiler_params=_cparams(("arbitrary",)),
        name="merge",
    )(x2d, o_a, o_b, gates, w_oa, w_ob, w_out, n_x, w_q, kv, w_o, n_f, w_r)


def _dispatch_kernel(pos_ref, zs_ref, zv_ref, nu_ref, h_ref, xs_hbm, zero_ref, sem):
    i = pl.program_id(0)
    tm = h_ref.shape[0] // ROW_TILES

    @pl.when(i == 0)
    def _():
        zero_ref[...] = jnp.zeros_like(zero_ref)

        def clear(e):
            dst = xs_hbm.at[pl.ds(pl.multiple_of(zs_ref[e] * ROW_TILES, ROW_TILES),
                                  MOE_BLOCK * ROW_TILES), :]
            return pltpu.make_async_copy(zero_ref, dst, sem)

        for e in range(N_EXPERTS):
            @pl.when(zv_ref[e] > 0)
            def _():
                clear(e).start()

        for e in range(N_EXPERTS):
            @pl.when(zv_ref[e] > 0)
            def _():
                clear(e).wait()

        def tail(b):
            dst = xs_hbm.at[pl.ds(pl.multiple_of(b * (MOE_BLOCK * ROW_TILES), MOE_BLOCK * ROW_TILES),
                                  MOE_BLOCK * ROW_TILES), :]
            return pltpu.make_async_copy(zero_ref, dst, sem)

        n_blocks = xs_hbm.shape[0] // (MOE_BLOCK * ROW_TILES)

        def start_tail(b, c):
            tail(b).start()
            return c

        def wait_tail(b, c):
            tail(b).wait()
            return c

        lax.fori_loop(nu_ref[0], n_blocks, start_tail, 0)
        lax.fori_loop(nu_ref[0], n_blocks, wait_tail, 0)

    def row_copy(r, k):
        p = pos_ref[2 * (i * tm + r) + k]
        dst = xs_hbm.at[pl.ds(pl.multiple_of(p * ROW_TILES, ROW_TILES), ROW_TILES), :]
        return pltpu.make_async_copy(h_ref.at[pl.ds(r * ROW_TILES, ROW_TILES), :], dst, sem)

    for r in range(tm):
        row_copy(r, 0).start(priority=0)
        row_copy(r, 1).start(priority=1)
    for _ in range(tm):
        row_copy(0, 0).wait()
        row_copy(0, 1).wait()


def _dispatch(pos, zstart, zvalid, n_used, h3t, cap):
    T = h3t.shape[0] // ROW_TILES
    tm = ROW_DMA_TILE
    grid_spec = pltpu.PrefetchScalarGridSpec(
        num_scalar_prefetch=4,
        grid=(T // tm,),
        in_specs=[pl.BlockSpec((tm * ROW_TILES, LANES), lambda i, ps, zs, zv, nu: (i, 0))],
        out_specs=pl.BlockSpec(memory_space=pl.ANY),
        scratch_shapes=[pltpu.VMEM((MOE_BLOCK * ROW_TILES, LANES), U32), pltpu.SemaphoreType.DMA],
    )
    return pl.pallas_call(
        _dispatch_kernel,
        grid_spec=grid_spec,
        out_shape=jax.ShapeDtypeStruct((cap * ROW_TILES, LANES), U32),
        compiler_params=_cparams(("arbitrary",)),
        name="dispatch",
    )(pos, zstart, zvalid, n_used, h3t)


def _moe_kernel(be_ref, nu_ref, x_ref, wg_ref, wu_ref, wd_ref, o_ref):
    i = pl.program_id(0)

    @pl.when(i < nu_ref[0])
    def _():
        xb = jnp.concatenate([x.astype(BF16) for x in _load_token_tiles(x_ref, MOE_BLOCK)], axis=-1)
        gate = _bdot(xb, wg_ref[0])
        up = _bdot(xb, wu_ref[0])
        hid = gate * _sigmoid(gate) * up
        _store_token_tiles(o_ref, _bdot(hid, wd_ref[0]))

    @pl.when(i >= nu_ref[0])
    def _():
        o_ref[...] = jnp.zeros_like(o_ref)


def _moe(block_expert, n_used, xs, wg, wu, wd):
    n_blocks = xs.shape[0] // (MOE_BLOCK * ROW_TILES)
    used = lambda i, nu: jnp.minimum(i, jnp.maximum(nu[0] - 1, 0))
    grid_spec = pltpu.PrefetchScalarGridSpec(
        num_scalar_prefetch=2,
        grid=(n_blocks,),
        in_specs=[
            pl.BlockSpec((MOE_BLOCK * ROW_TILES, LANES), lambda i, be, nu: (used(i, nu), 0)),
            pl.BlockSpec((1, D_MODEL, D_EXPERT), lambda i, be, nu: (be[i], 0, 0)),
            pl.BlockSpec((1, D_MODEL, D_EXPERT), lambda i, be, nu: (be[i], 0, 0)),
            pl.BlockSpec((1, D_EXPERT, D_MODEL), lambda i, be, nu: (be[i], 0, 0)),
        ],
        out_specs=pl.BlockSpec((MOE_BLOCK * ROW_TILES, LANES), lambda i, be, nu: (i, 0)),
    )
    return pl.pallas_call(
        _moe_kernel,
        grid_spec=grid_spec,
        out_shape=jax.ShapeDtypeStruct(xs.shape, U32),
        compiler_params=_cparams(("arbitrary",)),
        name="moe",
    )(block_expert, n_used, xs, wg, wu, wd)


def _moe_plan(expert, rank, counts, n_tok):
    n_assign = n_tok * 2
    e_flat = expert.reshape(n_assign)
    rank = rank.reshape(n_assign)
    padded = (counts + MOE_BLOCK - 1) // MOE_BLOCK * MOE_BLOCK
    p_end = jnp.cumsum(padded)
    p_start = p_end - padded
    pos = (p_start[e_flat] + rank).astype(jnp.int32)
    n_blocks = (n_assign + N_EXPERTS * (MOE_BLOCK - 1) + MOE_BLOCK - 1) // MOE_BLOCK
    cap = n_blocks * MOE_BLOCK
    block_start = jnp.arange(n_blocks, dtype=jnp.int32) * MOE_BLOCK
    block_expert = jnp.minimum(jnp.sum(p_end[None, :] <= block_start[:, None], axis=1),
                               N_EXPERTS - 1).astype(jnp.int32)
    n_used = (p_end[-1] // MOE_BLOCK).astype(jnp.int32).reshape(1)
    last_block = jnp.maximum(p_end - MOE_BLOCK, 0).astype(jnp.int32)
    nonempty = (padded > 0).astype(jnp.int32)
    return pos, cap, block_expert, n_used, last_block, nonempty


def _final_kernel(pos_ref, x_ref, rt_ref, y_hbm, g_ref, o_ref, y1_ref, y2_ref, sem):
    i = pl.program_id(0)
    n = pl.num_programs(0)
    tm = x_ref.shape[0]
    slot = i % 2

    def copies(tile, r, s):
        p1 = pos_ref[2 * (tile * tm + r)]
        p2 = pos_ref[2 * (tile * tm + r) + 1]
        src = lambda p: y_hbm.at[pl.ds(pl.multiple_of(p * ROW_TILES, ROW_TILES), ROW_TILES), :]
        dst = lambda ref: ref.at[s, pl.ds(r * ROW_TILES, ROW_TILES), :]
        return (pltpu.make_async_copy(src(p1), dst(y1_ref), sem.at[s]),
                pltpu.make_async_copy(src(p2), dst(y2_ref), sem.at[s]))

    def gather(tile, s):
        for r in range(tm):
            c1, c2 = copies(tile, r, s)
            c1.start(priority=0)
            c2.start(priority=1)

    def wait_rows(s):
        for _ in range(tm):
            c1, c2 = copies(0, 0, s)
            c1.wait()
            c2.wait()

    @pl.when(i == 0)
    def _():
        gather(0, 0)

    wait_rows(slot)
    gather(jnp.minimum(i + 1, n - 1), 1 - slot)
    rt = rt_ref[...]
    g1 = rt[:, 2:3]
    g2 = rt[:, 3:4]
    y1 = jnp.concatenate(_load_token_tiles(y1_ref.at[slot], tm), axis=-1)
    y2 = jnp.concatenate(_load_token_tiles(y2_ref.at[slot], tm), axis=-1)
    x3 = x_ref[...] + y1 * g1 + y2 * g2
    o_ref[...] = _rms(x3, g_ref[...])

    @pl.when(i == n - 1)
    def _():
        wait_rows(1 - slot)


def _final(pos, x2, rt, y_sorted, g):
    T = x2.shape[0]
    tm = ROW_DMA_TILE
    grid_spec = pltpu.PrefetchScalarGridSpec(
        num_scalar_prefetch=1,
        grid=(T // tm,),
        in_specs=[
            pl.BlockSpec((tm, D_MODEL), lambda i, ps: (i, 0)),
            pl.BlockSpec((tm, LANES), lambda i, ps: (i, 0)),
            pl.BlockSpec(memory_space=pl.ANY),
            pl.BlockSpec((1, D_MODEL), lambda i, ps: (0, 0)),
        ],
        out_specs=pl.BlockSpec((tm, D_MODEL), lambda i, ps: (i, 0)),
        scratch_shapes=[pltpu.VMEM((2, tm * ROW_TILES, LANES), U32),
                        pltpu.VMEM((2, tm * ROW_TILES, LANES), U32),
                        pltpu.SemaphoreType.DMA((2,))],
    )
    return pl.pallas_call(
        _final_kernel,
        grid_spec=grid_spec,
        out_shape=jax.ShapeDtypeStruct((T, D_MODEL), F32),
        compiler_params=_cparams(("arbitrary",)),
        name="final",
    )(pos, x2, rt, y_sorted, g)


def _lora_weight(decay_up, iclr_up, gate_up):
    w = jnp.zeros((LORA_COLS, 3 * WIDTH), F32)
    w = w.at[0:DECAY_LORA, 0:WIDTH].set(decay_up)
    w = w.at[DECAY_LORA:DECAY_LORA + ICLR_LORA, WIDTH:2 * WIDTH].set(iclr_up)
    w = w.at[DECAY_LORA + ICLR_LORA:, 2 * WIDTH:].set(gate_up)
    return w


def _router_weight(w_group, w_expert):
    w = jnp.zeros((D_MODEL, LANES), F32)
    w = w.at[:, 0:N_GROUPS].set(w_group)
    return w.at[:, N_GROUPS:N_GROUPS + N_EXPERTS].set(w_expert)


def kernel(x, mem, rel_bias, mem_norm, norm_mix, w_in, tshift_mu, decay_w0, decay_up, iclr_a0,
           iclr_up, gate_up, k_k, k_a, r_k, ln_x_w, ln_x_b, w_o_rwkv, w_o_moba, w_out,
           norm_xattn, w_q_x, w_kv_x, w_o_x, norm_ffn, w_router_group, w_router_expert,
           w_exp_gate, w_exp_up, w_exp_down, norm_final):
    B, T, D = x.shape
    assert B == 1 and D == D_MODEL and T % TOKEN_TILE == 0 and T // MOBA_BLOCK <= HEAD_DIM
    assert norm_mix.shape[0] == 1
    row = lambda a: a.reshape(1, -1)
    x2d = x.reshape(T, D)
    kv = _memkv(mem.reshape(N_MEM, D), row(mem_norm), w_kv_x[0].astype(BF16))

    ur, q, gates, k_aug, vt_aug, kmean = _inproj(x2d, row(norm_mix[0]), w_in[0].astype(BF16))
    kmean = kmean.reshape(T // MOBA_BLOCK, WIDTH)
    o_a = _rwkv_core(ur, row(tshift_mu[0]), row(decay_w0[0]), row(iclr_a0[0]), row(k_k[0]),
                     row(k_a[0]), row(r_k[0]),
                     _lora_weight(decay_up[0], iclr_up[0], gate_up[0]).astype(BF16),
                     row(ln_x_w[0]), row(ln_x_b[0]))

    o_b = _moba(q, k_aug, vt_aug, _score_mats(kmean), _bias_tiles(rel_bias))

    x2, h3, rt, counts = _merge(
        x2d, o_a, o_b, gates, w_o_rwkv[0].astype(BF16), w_o_moba[0].astype(BF16),
        w_out[0].astype(BF16), row(norm_xattn[0]), w_q_x[0].astype(BF16), kv,
        w_o_x[0].astype(BF16), row(norm_ffn[0]),
        _router_weight(w_router_group[0], w_router_expert[0]).astype(BF16))

    expert = rt[:, 0:2].astype(jnp.int32)
    rank = rt[:, 4:6].astype(jnp.int32)
    pos, cap, block_expert, n_used, last_block, nonempty = _moe_plan(
        expert, rank, counts[N_GROUPS:N_GROUPS + N_EXPERTS, 0].astype(jnp.int32), T)
    x_sorted = _dispatch(pos, last_block, nonempty, n_used, h3, cap)
    y_sorted = _moe(block_expert, n_used, x_sorted, w_exp_gate[0], w_exp_up[0], w_exp_down[0])
    out = _final(pos, x2, rt, y_sorted, row(norm_final))
    return out.reshape(B, T, D)
```

```python
import math

import jax
import jax.numpy as jnp
from jax import lax
from jax.experimental import pallas as pl
from jax.experimental.pallas import tpu as pltpu

F32 = jnp.float32
BF16 = jnp.bfloat16

D_MODEL = 1024
N_MEM = 256
NORM_EPS = 1e-6
NEG_INF = -1e30

HEADS = 8
HEAD_DIM = 64
WIDTH = HEADS * HEAD_DIM
DECAY_LORA = 64
ICLR_LORA = 64
GATE_LORA = 128
LORA_COLS = DECAY_LORA + ICLR_LORA + GATE_LORA
RWKV_COLS = 3 * WIDTH + LORA_COLS
QKV_COLS = 3 * WIDTH
GATE_COLS = 2 * D_MODEL
IN_COLS = RWKV_COLS + QKV_COLS + GATE_COLS
LN_X_EPS = 64e-5
KK_EPS = 1e-12

MOBA_BLOCK = 256
MOBA_TOP = 3
REL_BUCKETS = 32
REL_MAX_DISTANCE = 4096
N_BIAS_TILES = 14

XATTN_HEADS = 4
XATTN_HEAD_DIM = 128
XATTN_WIDTH = XATTN_HEADS * XATTN_HEAD_DIM

N_GROUPS = 4
EXPERTS_PER_GROUP = 8
N_EXPERTS = N_GROUPS * EXPERTS_PER_GROUP
D_EXPERT = 512
MOE_BLOCK = 256

TOKEN_TILE = 512
ROW_DMA_TILE = 256
VT_ROWS = HEAD_DIM + 16
CHUNK = 64
CHUNKS_PER_STEP = 4
LANES = 128
VMEM_LIMIT = 48 * 1024 * 1024

LOG2E = math.log2(math.e)


def _cparams(sem):
    return pltpu.CompilerParams(dimension_semantics=sem, vmem_limit_bytes=VMEM_LIMIT)


def _bdot(a, b):
    return jnp.dot(a.astype(BF16), b.astype(BF16), preferred_element_type=F32)


def _bdot_nt(a, b):
    return lax.dot_general(a.astype(BF16), b.astype(BF16), (((1,), (1,)), ((), ())),
                           preferred_element_type=F32)


def _bdot_tn(a, b):
    return lax.dot_general(a.astype(BF16), b.astype(BF16), (((0,), (0,)), ((), ())),
                           preferred_element_type=F32)


def _rms(x, g):
    return x * lax.rsqrt(jnp.mean(x * x, axis=-1, keepdims=True) + NORM_EPS) * g


def _sigmoid(x):
    return 1.0 / (1.0 + jnp.exp(-x))


ROW_TILES = D_MODEL // (2 * LANES)
U32 = jnp.uint32
HI_MASK = 0xFFFF0000


def _bits(x):
    return pltpu.bitcast(x.astype(BF16).astype(F32), U32)


def _store_token_tiles(ref, x):
    n = x.shape[0]
    for s in range(ROW_TILES):
        lo = x[:, 2 * s * LANES:(2 * s + 1) * LANES]
        hi = x[:, (2 * s + 1) * LANES:(2 * s + 2) * LANES]
        word = (_bits(hi) & U32(HI_MASK)) | lax.shift_right_logical(_bits(lo), U32(16))
        ref[pl.ds(s, n, stride=ROW_TILES), :] = word


def _load_token_tiles(ref, n):
    out = []
    for s in range(ROW_TILES):
        word = ref[pl.ds(s, n, stride=ROW_TILES), :]
        out.append(pltpu.bitcast(lax.shift_left(word, U32(16)), F32))
        out.append(pltpu.bitcast(word & U32(HI_MASK), F32))
    return out


def _inproj_kernel(x_ref, g_ref, w_ref, ur_ref, q_ref, gate_ref, ka_ref, vt_ref, km_ref):
    i = pl.program_id(0)
    B = MOBA_BLOCK
    h = _rms(x_ref[...], g_ref[...]).astype(BF16)
    tm = h.shape[0]
    step = 256
    for c0 in range(0, RWKV_COLS, step):
        ur_ref[:, c0:c0 + step] = jnp.dot(h, w_ref[:, c0:c0 + step], preferred_element_type=F32)
    for c0 in range(0, WIDTH, step):
        o = jnp.dot(h, w_ref[:, RWKV_COLS + c0:RWKV_COLS + c0 + step], preferred_element_type=F32)
        q_ref[:, c0:c0 + step] = (o * (HEAD_DIM ** -0.5 * LOG2E)).astype(BF16)
    lane = lax.broadcasted_iota(jnp.int32, (B, LANES), 1)
    ones = jnp.ones((VT_ROWS - HEAD_DIM, B), F32)
    for c0 in range(0, WIDTH, step):
        kcol = RWKV_COLS + WIDTH + c0
        k = jnp.dot(h, w_ref[:, kcol:kcol + step], preferred_element_type=F32).astype(BF16)
        v = jnp.dot(h, w_ref[:, kcol + WIDTH:kcol + WIDTH + step], preferred_element_type=F32)
        v = v.astype(BF16).astype(F32)
        for pp in range(step // LANES):
            pair = c0 // LANES + pp
            for jj in range(tm // B):
                rows = slice(jj * B, (jj + 1) * B)
                kblk = k[rows, pp * LANES:(pp + 1) * LANES]
                km_ref[0, jj:jj + 1, pair * LANES:(pair + 1) * LANES] = jnp.mean(
                    kblk.astype(F32), axis=0, keepdims=True)
                v_t = v[rows, pp * LANES:(pp + 1) * LANES].T
                for h2 in range(2):
                    off = HEAD_DIM * (1 - h2)
                    inr = (lane >= off) & (lane < off + HEAD_DIM)
                    hot = jnp.where(lane == off + i * (tm // B) + jj, 1.0, 0.0).astype(BF16)
                    ka_ref[2 * pair + h2, rows, :] = jnp.where(inr, hot, kblk)
                    tile = jnp.concatenate([v_t[h2 * HEAD_DIM:(h2 + 1) * HEAD_DIM], ones], axis=0)
                    vt_ref[2 * pair + h2, jj] = tile.astype(BF16)
    base = RWKV_COLS + QKV_COLS
    for c0 in range(0, GATE_COLS, step):
        o = jnp.dot(h, w_ref[:, base + c0:base + c0 + step], preferred_element_type=F32)
        gate_ref[:, c0:c0 + step] = o.astype(BF16)


def _inproj(x2d, g, w_bf):
    T = x2d.shape[0]
    tm = TOKEN_TILE
    bpt = tm // MOBA_BLOCK
    return pl.pallas_call(
        _inproj_kernel,
        grid=(T // tm,),
        in_specs=[
            pl.BlockSpec((tm, D_MODEL), lambda i: (i, 0)),
            pl.BlockSpec((1, D_MODEL), lambda i: (0, 0)),
            pl.BlockSpec((D_MODEL, IN_COLS), lambda i: (0, 0), pipeline_mode=pl.Buffered(1)),
        ],
        out_specs=[
            pl.BlockSpec((tm, RWKV_COLS), lambda i: (i, 0)),
            pl.BlockSpec((tm, WIDTH), lambda i: (i, 0)),
            pl.BlockSpec((tm, GATE_COLS), lambda i: (i, 0)),
            pl.BlockSpec((HEADS, tm, LANES), lambda i: (0, i, 0)),
            pl.BlockSpec((HEADS, bpt, VT_ROWS, MOBA_BLOCK), lambda i: (0, i, 0, 0)),
            pl.BlockSpec((1, bpt, WIDTH), lambda i: (i, 0, 0)),
        ],
        out_shape=[
            jax.ShapeDtypeStruct((T, RWKV_COLS), F32),
            jax.ShapeDtypeStruct((T, WIDTH), BF16),
            jax.ShapeDtypeStruct((T, GATE_COLS), BF16),
            jax.ShapeDtypeStruct((HEADS, T, LANES), BF16),
            jax.ShapeDtypeStruct((HEADS, T // MOBA_BLOCK, VT_ROWS, MOBA_BLOCK), BF16),
            jax.ShapeDtypeStruct((T // tm, bpt, WIDTH), F32),
        ],
        compiler_params=_cparams(("parallel",)),
        name="inproj",
    )(x2d, g, w_bf)


def _head_ones():
    r = lax.broadcasted_iota(jnp.int32, (LANES, LANES), 0) // HEAD_DIM
    c = lax.broadcasted_iota(jnp.int32, (LANES, LANES), 1) // HEAD_DIM
    return jnp.where(r == c, 1.0, 0.0).astype(BF16)


def _head_sum(x, ones):
    parts = [_bdot(x[:, c:c + LANES], ones) for c in range(0, WIDTH, LANES)]
    return jnp.concatenate(parts, axis=-1)


def _rwkv_prep_values(u_ref, up_ref, mu_ref, w0_ref, a0_ref, kk_ref, ka_ref, rk_ref, wl_ref):
    i = pl.program_id(0)
    u = u_ref[...]
    tm = u.shape[0]
    prev_last = up_ref[7:8, :] * jnp.where(i > 0, 1.0, 0.0)
    rolled = pltpu.roll(u, 1, 0)
    row = lax.broadcasted_iota(jnp.int32, u.shape, 0)
    u_prev = jnp.where(row == 0, prev_last, rolled)
    u = u + mu_ref[...] * (u_prev - u)
    r = u[:, 0:WIDTH]
    k = u[:, WIDTH:2 * WIDTH]
    v = u[:, 2 * WIDTH:3 * WIDTH]
    lo = u[:, 3 * WIDTH:3 * WIDTH + LORA_COLS]
    lane = lax.broadcasted_iota(jnp.int32, lo.shape, 1)
    act = jnp.where(lane < DECAY_LORA, jnp.tanh(lo),
                    jnp.where(lane < DECAY_LORA + ICLR_LORA, lo, _sigmoid(lo)))
    up = _bdot(act, wl_ref[...])
    z = -(w0_ref[...] + up[:, 0:WIDTH])
    softplus = jnp.maximum(z, 0.0) + jnp.log(1.0 + jnp.exp(-jnp.abs(z)))
    w_log = -softplus - 0.5
    lw = -jnp.exp(w_log)
    iclr = _sigmoid(a0_ref[...] + up[:, WIDTH:2 * WIDTH])
    g = up[:, 2 * WIDTH:3 * WIDTH]
    ones = _head_ones()
    kk = k * kk_ref[...]
    kk = kk * lax.rsqrt(_head_sum(kk * kk, ones) + KK_EPS)
    k2 = k * (1.0 + (iclr - 1.0) * ka_ref[...])
    bonus = _head_sum(r * k2 * rk_ref[...], ones) * v
    return r, lw, k2, v, -kk, kk * iclr, g, bonus


def _rwkv_core_kernel(u_ref, up_ref, mu_ref, w0_ref, a0_ref, kk_ref, ka_ref, rk_ref, wl_ref,
                      lnw_ref, lnb_ref, o_ref, s_ref):
    c = pl.program_id(0)

    @pl.when(c == 0)
    def _():
        s_ref[...] = jnp.zeros_like(s_ref)

    C = CHUNK
    G = CHUNKS_PER_STEP
    ri = lax.broadcasted_iota(jnp.int32, (C, C), 0)
    ci = lax.broadcasted_iota(jnp.int32, (C, C), 1)
    rg = lax.broadcasted_iota(jnp.int32, (G * C, G * C), 0)
    cg = lax.broadcasted_iota(jnp.int32, (G * C, G * C), 1)
    r_in, lw, k_in, v_all, a_in, b_in, g_in, bonus = _rwkv_prep_values(
        u_ref, up_ref, mu_ref, w0_ref, a0_ref, kk_ref, ka_ref, rk_ref, wl_ref)
    tri = jnp.where((rg >= cg) & (rg // C == cg // C), 1.0, 0.0).astype(BF16)
    lw_hi = lw.astype(BF16)
    lw_r1 = lw - lw_hi.astype(F32)
    lw_mid = lw_r1.astype(BF16)
    lw_lo = (lw_r1 - lw_mid.astype(F32)).astype(BF16)
    cum = (jnp.dot(tri, lw_hi, preferred_element_type=F32)
           + jnp.dot(tri, lw_mid, preferred_element_type=F32)
           + jnp.dot(tri, lw_lo, preferred_element_type=F32))
    lam = jnp.exp(cum)
    inv_lam = jnp.exp(-cum)
    r_t = r_in * lam
    a_t = a_in * jnp.exp(cum - lw)
    b_t = b_in * inv_lam
    k_t = k_in * inv_lam
    tots = [cum[g * C + C - 1:g * C + C, :] for g in range(G)]
    rowg = lax.broadcasted_iota(jnp.int32, cum.shape, 0) // C
    tot = tots[G - 1]
    for g in range(G - 2, -1, -1):
        tot = jnp.where(rowg == g, tots[g], tot)
    rest = jnp.exp(tot - cum)
    b_h = b_in * rest
    k_h = k_in * rest
    lam_c = [jnp.exp(t) for t in tots]
    eye = jnp.where(ri == ci, 1.0, 0.0)
    H = range(G * HEADS)
    sls = [slice(h * HEAD_DIM, (h + 1) * HEAD_DIM) for h in range(HEADS)]
    bf = lambda x: x.astype(BF16)
    part = lambda x, i: x[(i // HEADS) * C:(i // HEADS + 1) * C, sls[i % HEADS]]
    at = [bf(part(a_t, i)) for i in H]
    rt = [part(r_t, i) for i in H]
    bt = [bf(part(b_t, i)) for i in H]
    kt = [bf(part(k_t, i)) for i in H]
    bh = [bf(part(b_h, i)) for i in H]
    kh = [bf(part(k_h, i)) for i in H]
    vv = [bf(part(v_all, i)) for i in H]
    ci2 = lax.broadcasted_iota(jnp.int32, (C, 2 * C), 1)
    ri2 = lax.broadcasted_iota(jnp.int32, (C, 2 * C), 0)
    cm2 = jnp.where(ci2 >= C, ci2 - C, ci2)
    left = ci2 < C
    ar = [jnp.concatenate([at[h], bf(rt[h])], axis=0) for h in H]
    bk = [jnp.concatenate([bt[h], kt[h]], axis=0) for h in H]
    g = [_bdot_nt(ar[h], bk[h]) for h in H]
    top = [jnp.where(ri2 > cm2, g[h][0:C], 0.0) for h in H]
    bot = [bf(jnp.where(ri2 >= cm2, g[h][C:2 * C], 0.0)) for h in H]
    a_ab = [top[h][:, 0:C] for h in H]
    akv = [_bdot(top[h][:, C:2 * C], vv[h]) for h in H]
    z = [jnp.concatenate([a_ab[h], eye], axis=1) for h in H]
    for _ in range(6):
        z = [_bdot(z[h][:, 0:C], z[h]) + jnp.where(left, 0.0, z[h]) for h in H]
    tinv = [bf(z[h][:, C:2 * C]) for h in H]
    wu = [_bdot(tinv[h], jnp.concatenate([at[h], bf(akv[h])], axis=1)) for h in H]
    w_m = [bf(wu[h][:, 0:C]) for h in H]
    uv = [jnp.concatenate([bf(wu[h][:, C:2 * C]), vv[h]], axis=0) for h in H]
    q_m = [rt[h] + _bdot(bot[h][:, 0:C], w_m[h]) for h in H]
    y0 = [_bdot(bot[h], uv[h]) for h in H]
    m_k = [_bdot_tn(w_m[h], bh[h]) for h in H]
    n0 = [_bdot_tn(uv[h], jnp.concatenate([bh[h], kh[h]], axis=0)) for h in H]
    state = [s_ref[h] for h in range(HEADS)]
    blocks = []
    for g in range(G):
        outs = []
        for h in range(HEADS):
            i = g * HEADS + h
            y = _bdot_nt(q_m[i], state[h]) + y0[i]
            state[h] = state[h] * lam_c[g][:, sls[h]] + _bdot(state[h], m_k[i]) + n0[i]
            mean = jnp.mean(y, axis=-1, keepdims=True)
            var = jnp.mean(jnp.square(y - mean), axis=-1, keepdims=True)
            outs.append((y - mean) * lax.rsqrt(var + LN_X_EPS))
        blocks.append(jnp.concatenate(outs, axis=-1))
    for h in range(HEADS):
        s_ref[h] = state[h]
    yn = jnp.concatenate(blocks, axis=0)
    yn = yn * lnw_ref[...] + lnb_ref[...] + bonus
    o_ref[...] = (yn * g_in).astype(BF16)


def _rwkv_core(ur, mu, w0, a0, k_k, k_a, r_k, w_lora, ln_w, ln_b):
    T = ur.shape[0]
    rows = CHUNK * CHUNKS_PER_STEP
    row = lambda w: pl.BlockSpec((1, w), lambda c: (0, 0))
    return pl.pallas_call(
        _rwkv_core_kernel,
        grid=(T // rows,),
        in_specs=[
            pl.BlockSpec((rows, RWKV_COLS), lambda c: (c, 0)),
            pl.BlockSpec((8, RWKV_COLS), lambda c: (jnp.maximum(c * (rows // 8) - 1, 0), 0)),
            row(RWKV_COLS), row(WIDTH), row(WIDTH), row(WIDTH), row(WIDTH), row(WIDTH),
            pl.BlockSpec((LORA_COLS, 3 * WIDTH), lambda c: (0, 0)),
            row(WIDTH), row(WIDTH),
        ],
        out_specs=pl.BlockSpec((rows, WIDTH), lambda c: (c, 0)),
        out_shape=jax.ShapeDtypeStruct((T, WIDTH), BF16),
        scratch_shapes=[pltpu.VMEM((HEADS, HEAD_DIM, HEAD_DIM), F32)],
        compiler_params=_cparams(("arbitrary",)),
        name="rwkv_core",
    )(ur, ur, mu, w0, a0, k_k, k_a, r_k, w_lora, ln_w, ln_b)


def _t5_bucket(dist):
    n = jnp.maximum(dist, 0)
    max_exact = REL_BUCKETS // 2
    nf = jnp.maximum(n, max_exact).astype(jnp.float32)
    large = max_exact + (jnp.log(nf / max_exact) / math.log(REL_MAX_DISTANCE / max_exact)
                         * (REL_BUCKETS - max_exact)).astype(jnp.int32)
    large = jnp.minimum(large, REL_BUCKETS - 1)
    return jnp.where(n < max_exact, n, large)


def _bucket_tiles():
    i = jnp.arange(MOBA_BLOCK)[None, :]
    j = jnp.arange(MOBA_BLOCK)[:, None]
    d = jnp.arange(N_BIAS_TILES + 1)[:, None, None]
    dist = d * MOBA_BLOCK + i - j
    bucket = _t5_bucket(dist)
    bucket = jnp.where(d == N_BIAS_TILES - 1, REL_BUCKETS - 1, bucket)
    return jnp.where((dist < 0) | (d == N_BIAS_TILES), -1, bucket).astype(jnp.int32)


BIAS_ROWS = 16


def _bias_tiles_kernel(idx_ref, rb_ref, o_ref):
    def rows(c, carry):
        r0 = pl.multiple_of(c * BIAS_ROWS, BIAS_ROWS)
        idx = idx_ref[0, pl.ds(r0, BIAS_ROWS), :]
        acc = [jnp.where(idx < 0, NEG_INF, 0.0)] * HEADS
        for bkt in range(REL_BUCKETS):
            hit = idx == bkt
            acc = [jnp.where(hit, rb_ref[bkt, h] * LOG2E, acc[h]) for h in range(HEADS)]
        for h in range(HEADS):
            o_ref[h, 0, pl.ds(r0, BIAS_ROWS), :] = acc[h]
        return carry

    lax.fori_loop(0, MOBA_BLOCK // BIAS_ROWS, rows, 0)


def _bias_tiles(rel_bias):
    idx = _bucket_tiles()
    n = N_BIAS_TILES + 1
    return pl.pallas_call(
        _bias_tiles_kernel,
        grid=(n,),
        in_specs=[
            pl.BlockSpec((1, MOBA_BLOCK, MOBA_BLOCK), lambda d: (d, 0, 0)),
            pl.BlockSpec(memory_space=pltpu.SMEM),
        ],
        out_specs=pl.BlockSpec((HEADS, 1, MOBA_BLOCK, MOBA_BLOCK), lambda d: (0, d, 0, 0)),
        out_shape=jax.ShapeDtypeStruct((HEADS, n, MOBA_BLOCK, MOBA_BLOCK), F32),
        compiler_params=_cparams(("parallel",)),
        name="bias_tiles",
    )(idx, rel_bias)


def _moba_kernel(q_ref, ka_ref, vt_ref, r_ref, bias_ref, o_ref, s_ref, p_ref, acc_ref):
    qb = pl.program_id(1)
    B = MOBA_BLOCK
    q_tr = q_ref[...].astype(F32).T
    q_tr_bf = q_tr.astype(BF16)
    blk = lax.broadcasted_iota(jnp.int32, (HEAD_DIM, B), 0)
    big = jnp.int32(1 << 20)
    q_t, q_own_t = [], []
    for h2 in range(2):
        off = HEAD_DIM * (1 - h2)
        valid = blk < qb
        sc = jnp.dot(r_ref[0, h2, off:off + HEAD_DIM, :], q_tr_bf, preferred_element_type=F32)
        s = jnp.where(valid, sc, NEG_INF)
        sel = jnp.zeros((HEAD_DIM, B), jnp.bool_)
        for _ in range(MOBA_TOP):
            m = jnp.max(s, axis=0, keepdims=True)
            idx = jnp.min(jnp.where(s == m, blk, big), axis=0, keepdims=True)
            pick = blk == idx
            sel = jnp.logical_or(sel, pick)
            s = jnp.where(pick, -jnp.inf, s)
        sel = jnp.logical_and(sel, valid)
        choice = jnp.where(sel, 0.0, NEG_INF)
        q_head = q_tr[h2 * HEAD_DIM:(h2 + 1) * HEAD_DIM]
        parts = [q_head, choice] if h2 == 0 else [choice, q_head]
        own = [q_head, jnp.zeros_like(choice)] if h2 == 0 else [jnp.zeros_like(choice), q_head]
        q_t.append(jnp.concatenate(parts, axis=0).astype(BF16))
        q_own_t.append(jnp.concatenate(own, axis=0).astype(BF16))

    carry = []
    for h2 in range(2):
        k_own = ka_ref[h2, pl.ds(pl.multiple_of(qb * B, B), B), :]
        s0 = jnp.dot(k_own, q_own_t[h2], preferred_element_type=F32) + bias_ref[h2, 0]
        m0 = jnp.max(s0, axis=0, keepdims=True)
        p0 = jnp.exp2(s0 - m0).astype(BF16)
        carry += [m0, jnp.dot(vt_ref[h2, qb], p0, preferred_element_type=F32)]

    n_tiles = (qb + 1) // 2
    last = jnp.maximum(n_tiles - 1, 0)
    for h2 in range(2):
        acc_ref[h2] = carry[2 * h2 + 1]

    def trip(t, w, stats, scores=True, softmax=True, accumulate=True):
        r = 1 - w
        ok = (t >= 1) & (t <= n_tiles)
        d_a = jnp.where(ok, jnp.clip(qb - 2 * (t - 1), 0, N_BIAS_TILES - 1), N_BIAS_TILES)
        d_b = jnp.where(ok, jnp.clip(qb - 2 * (t - 1) - 1, 0, N_BIAS_TILES - 1), N_BIAS_TILES)
        start = pl.multiple_of(jnp.clip(t, 0, last) * (2 * B), 2 * B)
        v_a = 2 * jnp.clip(t - 2, 0, last)
        v_b = v_a + 1
        out = []
        for h2 in range(2):
            m_prev, alpha_p = stats[2 * h2], stats[2 * h2 + 1]
            if accumulate:
                acc_ref[h2] = (acc_ref[h2] * alpha_p
                               + jnp.dot(vt_ref[h2, v_a], p_ref[r, h2, 0:B], preferred_element_type=F32)
                               + jnp.dot(vt_ref[h2, v_b], p_ref[r, h2, B:2 * B],
                                         preferred_element_type=F32))
            if softmax:
                s_a = s_ref[r, h2, 0:B] + bias_ref[h2, d_a]
                s_b = s_ref[r, h2, B:2 * B] + bias_ref[h2, d_b]
                m_new = jnp.maximum(m_prev, jnp.max(jnp.maximum(s_a, s_b), axis=0, keepdims=True))
                out += [m_new, jnp.exp2(m_prev - m_new)]
                p_ref[w, h2, 0:B] = jnp.exp2(s_a - m_new).astype(BF16)
                p_ref[w, h2, B:2 * B] = jnp.exp2(s_b - m_new).astype(BF16)
            else:
                out += [m_prev, alpha_p]
            if scores:
                s_ref[w, h2] = jnp.dot(ka_ref[h2, pl.ds(start, 2 * B), :], q_t[h2],
                                       preferred_element_type=F32)
        return out

    def body(u, stats):
        stats = trip(2 * u, 0, list(stats))
        return tuple(trip(2 * u + 1, 1, stats))

    stats = []
    for h2 in range(2):
        stats += [carry[2 * h2], jnp.ones_like(carry[2 * h2])]
    stats = trip(0, 0, stats, softmax=False, accumulate=False)
    stats = trip(1, 1, stats, accumulate=False)
    stats = list(lax.fori_loop(1, n_tiles // 2, body, tuple(stats)))
    odd = n_tiles % 2 == 1

    @pl.when(jnp.logical_not(odd) & (n_tiles >= 2))
    def _():
        st = trip(n_tiles, 0, stats, scores=False)
        trip(n_tiles + 1, 1, st, scores=False, softmax=False)

    @pl.when(odd & (n_tiles >= 3))
    def _():
        st = trip(n_tiles - 1, 0, stats)
        st = trip(n_tiles, 1, st, scores=False)
        trip(n_tiles + 1, 0, st, scores=False, softmax=False)

    @pl.when(n_tiles == 1)
    def _():
        trip(2, 0, stats, scores=False, softmax=False)
    outs = [acc_ref[h2, 0:HEAD_DIM] / acc_ref[h2, HEAD_DIM:HEAD_DIM + 1] for h2 in range(2)]
    o_ref[...] = jnp.concatenate(outs, axis=0).T.astype(BF16)


def _moba(q, k_aug, vt_aug, r_mats, bias_tiles):
    T = q.shape[0]
    nb = T // MOBA_BLOCK
    npair = HEADS // 2
    once = pl.Buffered(1)
    return pl.pallas_call(
        _moba_kernel,
        grid=(npair, nb),
        in_specs=[
            pl.BlockSpec((MOBA_BLOCK, LANES), lambda p, qb: (qb, p)),
            pl.BlockSpec((2, T, LANES), lambda p, qb: (p, 0, 0), pipeline_mode=once),
            pl.BlockSpec((2, nb, VT_ROWS, MOBA_BLOCK), lambda p, qb: (p, 0, 0, 0), pipeline_mode=once),
            pl.BlockSpec((1, 2, LANES, LANES), lambda p, qb: (p, 0, 0, 0)),
            pl.BlockSpec((2, N_BIAS_TILES + 1, MOBA_BLOCK, MOBA_BLOCK), lambda p, qb: (p, 0, 0, 0),
                         pipeline_mode=once),
        ],
        out_specs=pl.BlockSpec((MOBA_BLOCK, LANES), lambda p, qb: (qb, p)),
        out_shape=jax.ShapeDtypeStruct((T, WIDTH), BF16),
        scratch_shapes=[pltpu.VMEM((2, 2, 2 * MOBA_BLOCK, MOBA_BLOCK), F32),
                        pltpu.VMEM((2, 2, 2 * MOBA_BLOCK, MOBA_BLOCK), BF16),
                        pltpu.VMEM((2, VT_ROWS, MOBA_BLOCK), F32)],
        compiler_params=_cparams(("arbitrary", "arbitrary")),
        name="moba",
    )(q, k_aug, vt_aug, r_mats, bias_tiles)


def _score_mats(kmean):
    nb = kmean.shape[0]
    km = kmean.reshape(nb, HEADS, HEAD_DIM).transpose(1, 2, 0)
    km = jnp.pad(km, ((0, 0), (0, 0), (0, HEAD_DIM - nb)))
    z = jnp.zeros((HEADS // 2, HEAD_DIM, HEAD_DIM), F32)
    even = jnp.concatenate([jnp.concatenate([z, km[0::2]], axis=2),
                            jnp.concatenate([z, z], axis=2)], axis=1)
    odd = jnp.concatenate([jnp.concatenate([z, z], axis=2),
                           jnp.concatenate([km[1::2], z], axis=2)], axis=1)
    return jnp.swapaxes(jnp.stack([even, odd], axis=1), -1, -2).astype(BF16)


def _memkv_kernel(m_ref, g_ref, w_ref, o_ref):
    h = _rms(m_ref[...], g_ref[...]).astype(BF16)
    o_ref[...] = jnp.dot(h, w_ref[...], preferred_element_type=F32).astype(BF16)


def _memkv(mem2d, g, w_bf):
    return pl.pallas_call(
        _memkv_kernel,
        out_shape=jax.ShapeDtypeStruct((N_MEM, 2 * XATTN_WIDTH), BF16),
        compiler_params=pltpu.CompilerParams(vmem_limit_bytes=VMEM_LIMIT),
        name="memkv",
    )(mem2d, g, w_bf)


def _merge_kernel(x_ref, oa_ref, ob_ref, gt_ref, woa_ref, wob_ref, wout_ref, nx_ref, wq_ref,
                  kv_ref, wo_ref, nf_ref, wr_ref, x2_ref, h3_ref, rt_ref, cnt_ref, run_ref):
    pa = jnp.dot(oa_ref[...], woa_ref[...], preferred_element_type=F32)
    pb = jnp.dot(ob_ref[...], wob_ref[...], preferred_element_type=F32)
    ga = gt_ref[:, 0:D_MODEL].astype(F32)
    gb = gt_ref[:, D_MODEL:2 * D_MODEL].astype(F32)
    merged = _sigmoid(ga) * pa + _sigmoid(gb) * pb
    x1 = x_ref[...] + jnp.dot(merged.astype(BF16), wout_ref[...], preferred_element_type=F32)
    h2 = _rms(x1, nx_ref[...]).astype(BF16)
    q = jnp.dot(h2, wq_ref[...], preferred_element_type=F32).astype(BF16)
    heads = []
    for h in range(XATTN_HEADS):
        sl = slice(h * XATTN_HEAD_DIM, (h + 1) * XATTN_HEAD_DIM)
        km = kv_ref[:, sl]
        vm = kv_ref[:, XATTN_WIDTH + h * XATTN_HEAD_DIM:XATTN_WIDTH + (h + 1) * XATTN_HEAD_DIM]
        lg = lax.dot_general(q[:, sl], km, (((1,), (1,)), ((), ())),
                             preferred_element_type=F32) * (XATTN_HEAD_DIM ** -0.5)
        e = jnp.exp(lg - jnp.max(lg, axis=-1, keepdims=True))
        p = e / jnp.sum(e, axis=-1, keepdims=True)
        heads.append(jnp.dot(p.astype(BF16), vm, preferred_element_type=F32))
    o = jnp.concatenate(heads, axis=-1).astype(BF16)
    x2 = x1 + jnp.dot(o, wo_ref[...], preferred_element_type=F32)
    x2_ref[...] = x2
    h3 = _rms(x2, nf_ref[...])
    _store_token_tiles(h3_ref, h3)
    lg = _bdot(h3, wr_ref[...]).T
    tm = lg.shape[1]
    row = lax.broadcasted_iota(jnp.int32, lg.shape, 0)
    big = jnp.int32(1 << 20)
    is_g = row < N_GROUPS
    gl = jnp.where(is_g, lg, -jnp.inf)
    gmax = jnp.max(gl, axis=0, keepdims=True)
    gsel = jnp.min(jnp.where(gl == gmax, row, big), axis=0, keepdims=True)
    p_top = 1.0 / jnp.sum(jnp.where(is_g, jnp.exp(gl - gmax), 0.0), axis=0, keepdims=True)
    e_id = row - N_GROUPS
    in_grp = (e_id >= gsel * EXPERTS_PER_GROUP) & (e_id < (gsel + 1) * EXPERTS_PER_GROUP)
    el = jnp.where(in_grp, lg, -jnp.inf)
    v1 = jnp.max(el, axis=0, keepdims=True)
    i1 = jnp.min(jnp.where(el == v1, row, big), axis=0, keepdims=True)
    el2 = jnp.where(row == i1, -jnp.inf, el)
    v2 = jnp.max(el2, axis=0, keepdims=True)
    i2 = jnp.min(jnp.where(el2 == v2, row, big), axis=0, keepdims=True)
    e21 = jnp.exp(v2 - v1)
    g1 = p_top / (1.0 + e21)
    g2 = p_top * e21 / (1.0 + e21)
    step = pl.program_id(0)

    @pl.when(step == 0)
    def _():
        run_ref[...] = jnp.zeros_like(run_ref)

    oh1 = jnp.where(row == i1, 1.0, 0.0)
    oh2 = jnp.where(row == i2, 1.0, 0.0)
    both = oh1 + oh2
    ri = lax.broadcasted_iota(jnp.int32, (tm, tm), 0)
    ci = lax.broadcasted_iota(jnp.int32, (tm, tm), 1)
    before = jnp.dot(both, jnp.where(ri < ci, 1.0, 0.0), preferred_element_type=F32) + run_ref[:, 0:1]
    r1 = jnp.sum(before * oh1, axis=0, keepdims=True)
    r2 = jnp.sum(before * oh2, axis=0, keepdims=True)
    run_ref[...] = run_ref[...] + jnp.sum(both, axis=1, keepdims=True)
    cnt_ref[...] = run_ref[...]
    vals = [(i1 - N_GROUPS).astype(F32), (i2 - N_GROUPS).astype(F32), g1, g2, r1, r2]
    rt = jnp.zeros_like(lg)
    for k, v in enumerate(vals):
        rt = jnp.where(row == k, v, rt)
    rt_ref[...] = rt.T


def _merge(x2d, o_a, o_b, gates, w_oa, w_ob, w_out, n_x, w_q, kv, w_o, n_f, w_r):
    T = x2d.shape[0]
    tm = TOKEN_TILE
    full = lambda a: pl.BlockSpec(a.shape, lambda i: (0,) * a.ndim, pipeline_mode=pl.Buffered(1))
    tile = lambda w: pl.BlockSpec((tm, w), lambda i: (i, 0))
    return pl.pallas_call(
        _merge_kernel,
        grid=(T // tm,),
        in_specs=[tile(D_MODEL), tile(WIDTH), tile(WIDTH), tile(GATE_COLS), full(w_oa), full(w_ob),
                  full(w_out), full(n_x), full(w_q), full(kv), full(w_o), full(n_f), full(w_r)],
        out_specs=[tile(D_MODEL), pl.BlockSpec((tm * ROW_TILES, LANES), lambda i: (i, 0)), tile(LANES),
                   pl.BlockSpec((LANES, LANES), lambda i: (0, 0))],
        out_shape=[jax.ShapeDtypeStruct((T, D_MODEL), F32),
                   jax.ShapeDtypeStruct((T * ROW_TILES, LANES), U32),
                   jax.ShapeDtypeStruct((T, LANES), F32),
                   jax.ShapeDtypeStruct((LANES, LANES), F32)],
        scratch_shapes=[pltpu.VMEM((LANES, LANES), F32)],
        compiler_params=_cparams(("arbitrary",)),
        name="merge",
    )(x2d, o_a, o_b, gates, w_oa, w_ob, w_out, n_x, w_q, kv, w_o, n_f, w_r)


def _dispatch_kernel(pos_ref, zs_ref, zv_ref, nu_ref, h_ref, xs_hbm, zero_ref, sem):
    i = pl.program_id(0)
    tm = h_ref.shape[0] // ROW_TILES

    @pl.when(i == 0)
    def _():
        zero_ref[...] = jnp.zeros_like(zero_ref)

        def clear(e):
            dst = xs_hbm.at[pl.ds(pl.multiple_of(zs_ref[e] * ROW_TILES, ROW_TILES),
                                  MOE_BLOCK * ROW_TILES), :]
            return pltpu.make_async_copy(zero_ref, dst, sem)

        for e in range(N_EXPERTS):
            @pl.when(zv_ref[e] > 0)
            def _():
                clear(e).start()

        for e in range(N_EXPERTS):
            @pl.when(zv_ref[e] > 0)
            def _():
                clear(e).wait()

        def tail(b):
            dst = xs_hbm.at[pl.ds(pl.multiple_of(b * (MOE_BLOCK * ROW_TILES), MOE_BLOCK * ROW_TILES),
                                  MOE_BLOCK * ROW_TILES), :]
            return pltpu.make_async_copy(zero_ref, dst, sem)

        n_blocks = xs_hbm.shape[0] // (MOE_BLOCK * ROW_TILES)

        def start_tail(b, c):
            tail(b).start()
            return c

        def wait_tail(b, c):
            tail(b).wait()
            return c

        lax.fori_loop(nu_ref[0], n_blocks, start_tail, 0)
        lax.fori_loop(nu_ref[0], n_blocks, wait_tail, 0)

    def row_copy(r, k):
        p = pos_ref[2 * (i * tm + r) + k]
        dst = xs_hbm.at[pl.ds(pl.multiple_of(p * ROW_TILES, ROW_TILES), ROW_TILES), :]
        return pltpu.make_async_copy(h_ref.at[pl.ds(r * ROW_TILES, ROW_TILES), :], dst, sem)

    for r in range(tm):
        row_copy(r, 0).start(priority=0)
        row_copy(r, 1).start(priority=1)
    for _ in range(tm):
        row_copy(0, 0).wait()
        row_copy(0, 1).wait()


def _dispatch(pos, zstart, zvalid, n_used, h3t, cap):
    T = h3t.shape[0] // ROW_TILES
    tm = ROW_DMA_TILE
    grid_spec = pltpu.PrefetchScalarGridSpec(
        num_scalar_prefetch=4,
        grid=(T // tm,),
        in_specs=[pl.BlockSpec((tm * ROW_TILES, LANES), lambda i, ps, zs, zv, nu: (i, 0))],
        out_specs=pl.BlockSpec(memory_space=pl.ANY),
        scratch_shapes=[pltpu.VMEM((MOE_BLOCK * ROW_TILES, LANES), U32), pltpu.SemaphoreType.DMA],
    )
    return pl.pallas_call(
        _dispatch_kernel,
        grid_spec=grid_spec,
        out_shape=jax.ShapeDtypeStruct((cap * ROW_TILES, LANES), U32),
        compiler_params=_cparams(("arbitrary",)),
        name="dispatch",
    )(pos, zstart, zvalid, n_used, h3t)


def _moe_kernel(be_ref, nu_ref, x_ref, wg_ref, wu_ref, wd_ref, o_ref):
    i = pl.program_id(0)

    @pl.when(i < nu_ref[0])
    def _():
        xb = jnp.concatenate([x.astype(BF16) for x in _load_token_tiles(x_ref, MOE_BLOCK)], axis=-1)
        gate = _bdot(xb, wg_ref[0])
        up = _bdot(xb, wu_ref[0])
        hid = gate * _sigmoid(gate) * up
        _store_token_tiles(o_ref, _bdot(hid, wd_ref[0]))

    @pl.when(i >= nu_ref[0])
    def _():
        o_ref[...] = jnp.zeros_like(o_ref)


def _moe(block_expert, n_used, xs, wg, wu, wd):
    n_blocks = xs.shape[0] // (MOE_BLOCK * ROW_TILES)
    used = lambda i, nu: jnp.minimum(i, jnp.maximum(nu[0] - 1, 0))
    grid_spec = pltpu.PrefetchScalarGridSpec(
        num_scalar_prefetch=2,
        grid=(n_blocks,),
        in_specs=[
            pl.BlockSpec((MOE_BLOCK * ROW_TILES, LANES), lambda i, be, nu: (used(i, nu), 0)),
            pl.BlockSpec((1, D_MODEL, D_EXPERT), lambda i, be, nu: (be[i], 0, 0)),
            pl.BlockSpec((1, D_MODEL, D_EXPERT), lambda i, be, nu: (be[i], 0, 0)),
            pl.BlockSpec((1, D_EXPERT, D_MODEL), lambda i, be, nu: (be[i], 0, 0)),
        ],
        out_specs=pl.BlockSpec((MOE_BLOCK * ROW_TILES, LANES), lambda i, be, nu: (i, 0)),
    )
    return pl.pallas_call(
        _moe_kernel,
        grid_spec=grid_spec,
        out_shape=jax.ShapeDtypeStruct(xs.shape, U32),
        compiler_params=_cparams(("arbitrary",)),
        name="moe",
    )(block_expert, n_used, xs, wg, wu, wd)


def _moe_plan(expert, rank, counts, n_tok):
    n_assign = n_tok * 2
    e_flat = expert.reshape(n_assign)
    rank = rank.reshape(n_assign)
    padded = (counts + MOE_BLOCK - 1) // MOE_BLOCK * MOE_BLOCK
    p_end = jnp.cumsum(padded)
    p_start = p_end - padded
    pos = (p_start[e_flat] + rank).astype(jnp.int32)
    n_blocks = (n_assign + N_EXPERTS * (MOE_BLOCK - 1) + MOE_BLOCK - 1) // MOE_BLOCK
    cap = n_blocks * MOE_BLOCK
    block_start = jnp.arange(n_blocks, dtype=jnp.int32) * MOE_BLOCK
    block_expert = jnp.minimum(jnp.sum(p_end[None, :] <= block_start[:, None], axis=1),
                               N_EXPERTS - 1).astype(jnp.int32)
    n_used = (p_end[-1] // MOE_BLOCK).astype(jnp.int32).reshape(1)
    last_block = jnp.maximum(p_end - MOE_BLOCK, 0).astype(jnp.int32)
    nonempty = (padded > 0).astype(jnp.int32)
    return pos, cap, block_expert, n_used, last_block, nonempty


def _final_kernel(pos_ref, x_ref, rt_ref, y_hbm, g_ref, o_ref, y1_ref, y2_ref, sem):
    i = pl.program_id(0)
    n = pl.num_programs(0)
    tm = x_ref.shape[0]
    slot = i % 2

    def copies(tile, r, s):
        p1 = pos_ref[2 * (tile * tm + r)]
        p2 = pos_ref[2 * (tile * tm + r) + 1]
        src = lambda p: y_hbm.at[pl.ds(pl.multiple_of(p * ROW_TILES, ROW_TILES), ROW_TILES), :]
        dst = lambda ref: ref.at[s, pl.ds(r * ROW_TILES, ROW_TILES), :]
        return (pltpu.make_async_copy(src(p1), dst(y1_ref), sem.at[s]),
                pltpu.make_async_copy(src(p2), dst(y2_ref), sem.at[s]))

    def gather(tile, s):
        for r in range(tm):
            c1, c2 = copies(tile, r, s)
            c1.start(priority=0)
            c2.start(priority=1)

    def wait_rows(s):
        for _ in range(tm):
            c1, c2 = copies(0, 0, s)
            c1.wait()
            c2.wait()

    @pl.when(i == 0)
    def _():
        gather(0, 0)

    wait_rows(slot)
    gather(jnp.minimum(i + 1, n - 1), 1 - slot)
    rt = rt_ref[...]
    g1 = rt[:, 2:3]
    g2 = rt[:, 3:4]
    y1 = jnp.concatenate(_load_token_tiles(y1_ref.at[slot], tm), axis=-1)
    y2 = jnp.concatenate(_load_token_tiles(y2_ref.at[slot], tm), axis=-1)
    x3 = x_ref[...] + y1 * g1 + y2 * g2
    o_ref[...] = _rms(x3, g_ref[...])

    @pl.when(i == n - 1)
    def _():
        wait_rows(1 - slot)


def _final(pos, x2, rt, y_sorted, g):
    T = x2.shape[0]
    tm = ROW_DMA_TILE
    grid_spec = pltpu.PrefetchScalarGridSpec(
        num_scalar_prefetch=1,
        grid=(T // tm,),
        in_specs=[
            pl.BlockSpec((tm, D_MODEL), lambda i, ps: (i, 0)),
            pl.BlockSpec((tm, LANES), lambda i, ps: (i, 0)),
            pl.BlockSpec(memory_space=pl.ANY),
            pl.BlockSpec((1, D_MODEL), lambda i, ps: (0, 0)),
        ],
        out_specs=pl.BlockSpec((tm, D_MODEL), lambda i, ps: (i, 0)),
        scratch_shapes=[pltpu.VMEM((2, tm * ROW_TILES, LANES), U32),
                        pltpu.VMEM((2, tm * ROW_TILES, LANES), U32),
                        pltpu.SemaphoreType.DMA((2,))],
    )
    return pl.pallas_call(
        _final_kernel,
        grid_spec=grid_spec,
        out_shape=jax.ShapeDtypeStruct((T, D_MODEL), F32),
        compiler_params=_cparams(("arbitrary",)),
        name="final",
    )(pos, x2, rt, y_sorted, g)


def _lora_weight(decay_up, iclr_up, gate_up):
    w = jnp.zeros((LORA_COLS, 3 * WIDTH), F32)
    w = w.at[0:DECAY_LORA, 0:WIDTH].set(decay_up)
    w = w.at[DECAY_LORA:DECAY_LORA + ICLR_LORA, WIDTH:2 * WIDTH].set(iclr_up)
    w = w.at[DECAY_LORA + ICLR_LORA:, 2 * WIDTH:].set(gate_up)
    return w


def _router_weight(w_group, w_expert):
    w = jnp.zeros((D_MODEL, LANES), F32)
    w = w.at[:, 0:N_GROUPS].set(w_group)
    return w.at[:, N_GROUPS:N_GROUPS + N_EXPERTS].set(w_expert)


def kernel(x, mem, rel_bias, mem_norm, norm_mix, w_in, tshift_mu, decay_w0, decay_up, iclr_a0,
           iclr_up, gate_up, k_k, k_a, r_k, ln_x_w, ln_x_b, w_o_rwkv, w_o_moba, w_out,
           norm_xattn, w_q_x, w_kv_x, w_o_x, norm_ffn, w_router_group, w_router_expert,
           w_exp_gate, w_exp_up, w_exp_down, norm_final):
    B, T, D = x.shape
    assert B == 1 and D == D_MODEL and T % TOKEN_TILE == 0 and T // MOBA_BLOCK <= HEAD_DIM
    assert norm_mix.shape[0] == 1
    row = lambda a: a.reshape(1, -1)
    x2d = x.reshape(T, D)
    kv = _memkv(mem.reshape(N_MEM, D), row(mem_norm), w_kv_x[0].astype(BF16))

    ur, q, gates, k_aug, vt_aug, kmean = _inproj(x2d, row(norm_mix[0]), w_in[0].astype(BF16))
    kmean = kmean.reshape(T // MOBA_BLOCK, WIDTH)
    o_a = _rwkv_core(ur, row(tshift_mu[0]), row(decay_w0[0]), row(iclr_a0[0]), row(k_k[0]),
                     row(k_a[0]), row(r_k[0]),
                     _lora_weight(decay_up[0], iclr_up[0], gate_up[0]).astype(BF16),
                     row(ln_x_w[0]), row(ln_x_b[0]))

    o_b = _moba(q, k_aug, vt_aug, _score_mats(kmean), _bias_tiles(rel_bias))

    x2, h3, rt, counts = _merge(
        x2d, o_a, o_b, gates, w_o_rwkv[0].astype(BF16), w_o_moba[0].astype(BF16),
        w_out[0].astype(BF16), row(norm_xattn[0]), w_q_x[0].astype(BF16), kv,
        w_o_x[0].astype(BF16), row(norm_ffn[0]),
        _router_weight(w_router_group[0], w_router_expert[0]).astype(BF16))

    expert = rt[:, 0:2].astype(jnp.int32)
    rank = rt[:, 4:6].astype(jnp.int32)
    pos, cap, block_expert, n_used, last_block, nonempty = _moe_plan(
        expert, rank, counts[N_GROUPS:N_GROUPS + N_EXPERTS, 0].astype(jnp.int32), T)
    x_sorted = _dispatch(pos, last_block, nonempty, n_used, h3, cap)
    y_sorted = _moe(block_expert, n_used, x_sorted, w_exp_gate[0], w_exp_up[0], w_exp_down[0])
    out = _final(pos, x2, rt, y_sorted, row(norm_final))
    return out.reshape(B, T, D)
```

```python
import math

import jax
import jax.numpy as jnp
from jax import lax
from jax.experimental import pallas as pl
from jax.experimental.pallas import tpu as pltpu

F32 = jnp.float32
BF16 = jnp.bfloat16

D_MODEL = 1024
N_MEM = 256
NORM_EPS = 1e-6
NEG_INF = -1e30

HEADS = 8
HEAD_DIM = 64
WIDTH = HEADS * HEAD_DIM
DECAY_LORA = 64
ICLR_LORA = 64
GATE_LORA = 128
LORA_COLS = DECAY_LORA + ICLR_LORA + GATE_LORA
RWKV_COLS = 3 * WIDTH + LORA_COLS
QKV_COLS = 3 * WIDTH
GATE_COLS = 2 * D_MODEL
IN_COLS = RWKV_COLS + QKV_COLS + GATE_COLS
LN_X_EPS = 64e-5
KK_EPS = 1e-12

MOBA_BLOCK = 256
MOBA_TOP = 3
REL_BUCKETS = 32
REL_MAX_DISTANCE = 4096
N_BIAS_TILES = 14

XATTN_HEADS = 4
XATTN_HEAD_DIM = 128
XATTN_WIDTH = XATTN_HEADS * XATTN_HEAD_DIM

N_GROUPS = 4
EXPERTS_PER_GROUP = 8
N_EXPERTS = N_GROUPS * EXPERTS_PER_GROUP
D_EXPERT = 512
MOE_BLOCK = 256

TOKEN_TILE = 512
ROW_DMA_TILE = 256
VT_ROWS = HEAD_DIM + 16
CHUNK = 64
CHUNKS_PER_STEP = 4
LANES = 128
VMEM_LIMIT = 48 * 1024 * 1024

LOG2E = math.log2(math.e)


def _cparams(sem):
    return pltpu.CompilerParams(dimension_semantics=sem, vmem_limit_bytes=VMEM_LIMIT)


def _bdot(a, b):
    return jnp.dot(a.astype(BF16), b.astype(BF16), preferred_element_type=F32)


def _bdot_nt(a, b):
    return lax.dot_general(a.astype(BF16), b.astype(BF16), (((1,), (1,)), ((), ())),
                           preferred_element_type=F32)


def _bdot_tn(a, b):
    return lax.dot_general(a.astype(BF16), b.astype(BF16), (((0,), (0,)), ((), ())),
                           preferred_element_type=F32)


def _rms(x, g):
    return x * lax.rsqrt(jnp.mean(x * x, axis=-1, keepdims=True) + NORM_EPS) * g


def _sigmoid(x):
    return 1.0 / (1.0 + jnp.exp(-x))


ROW_TILES = D_MODEL // (2 * LANES)
U32 = jnp.uint32
HI_MASK = 0xFFFF0000


def _bits(x):
    return pltpu.bitcast(x.astype(BF16).astype(F32), U32)


def _store_token_tiles(ref, x):
    n = x.shape[0]
    for s in range(ROW_TILES):
        lo = x[:, 2 * s * LANES:(2 * s + 1) * LANES]
        hi = x[:, (2 * s + 1) * LANES:(2 * s + 2) * LANES]
        word = (_bits(hi) & U32(HI_MASK)) | lax.shift_right_logical(_bits(lo), U32(16))
        ref[pl.ds(s, n, stride=ROW_TILES), :] = word


def _load_token_tiles(ref, n):
    out = []
    for s in range(ROW_TILES):
        word = ref[pl.ds(s, n, stride=ROW_TILES), :]
        out.append(pltpu.bitcast(lax.shift_left(word, U32(16)), F32))
        out.append(pltpu.bitcast(word & U32(HI_MASK), F32))
    return out


def _inproj_kernel(x_ref, g_ref, w_ref, ur_ref, q_ref, gate_ref, ka_ref, vt_ref, km_ref):
    i = pl.program_id(0)
    B = MOBA_BLOCK
    h = _rms(x_ref[...], g_ref[...]).astype(BF16)
    tm = h.shape[0]
    step = 256
    for c0 in range(0, RWKV_COLS, step):
        ur_ref[:, c0:c0 + step] = jnp.dot(h, w_ref[:, c0:c0 + step], preferred_element_type=F32)
    for c0 in range(0, WIDTH, step):
        o = jnp.dot(h, w_ref[:, RWKV_COLS + c0:RWKV_COLS + c0 + step], preferred_element_type=F32)
        q_ref[:, c0:c0 + step] = (o * (HEAD_DIM ** -0.5 * LOG2E)).astype(BF16)
    lane = lax.broadcasted_iota(jnp.int32, (B, LANES), 1)
    ones = jnp.ones((VT_ROWS - HEAD_DIM, B), F32)
    for c0 in range(0, WIDTH, step):
        kcol = RWKV_COLS + WIDTH + c0
        k = jnp.dot(h, w_ref[:, kcol:kcol + step], preferred_element_type=F32).astype(BF16)
        v = jnp.dot(h, w_ref[:, kcol + WIDTH:kcol + WIDTH + step], preferred_element_type=F32)
        v = v.astype(BF16).astype(F32)
        for pp in range(step // LANES):
            pair = c0 // LANES + pp
            for jj in range(tm // B):
                rows = slice(jj * B, (jj + 1) * B)
                kblk = k[rows, pp * LANES:(pp + 1) * LANES]
                km_ref[0, jj:jj + 1, pair * LANES:(pair + 1) * LANES] = jnp.mean(
                    kblk.astype(F32), axis=0, keepdims=True)
                v_t = v[rows, pp * LANES:(pp + 1) * LANES].T
                for h2 in range(2):
                    off = HEAD_DIM * (1 - h2)
                    inr = (lane >= off) & (lane < off + HEAD_DIM)
                    hot = jnp.where(lane == off + i * (tm // B) + jj, 1.0, 0.0).astype(BF16)
                    ka_ref[2 * pair + h2, rows, :] = jnp.where(inr, hot, kblk)
                    tile = jnp.concatenate([v_t[h2 * HEAD_DIM:(h2 + 1) * HEAD_DIM], ones], axis=0)
                    vt_ref[2 * pair + h2, jj] = tile.astype(BF16)
    base = RWKV_COLS + QKV_COLS
    for c0 in range(0, GATE_COLS, step):
        o = jnp.dot(h, w_ref[:, base + c0:base + c0 + step], preferred_element_type=F32)
        gate_ref[:, c0:c0 + step] = o.astype(BF16)


def _inproj(x2d, g, w_bf):
    T = x2d.shape[0]
    tm = TOKEN_TILE
    bpt = tm // MOBA_BLOCK
    return pl.pallas_call(
        _inproj_kernel,
        grid=(T // tm,),
        in_specs=[
            pl.BlockSpec((tm, D_MODEL), lambda i: (i, 0)),
            pl.BlockSpec((1, D_MODEL), lambda i: (0, 0)),
            pl.BlockSpec((D_MODEL, IN_COLS), lambda i: (0, 0), pipeline_mode=pl.Buffered(1)),
        ],
        out_specs=[
            pl.BlockSpec((tm, RWKV_COLS), lambda i: (i, 0)),
            pl.BlockSpec((tm, WIDTH), lambda i: (i, 0)),
            pl.BlockSpec((tm, GATE_COLS), lambda i: (i, 0)),
            pl.BlockSpec((HEADS, tm, LANES), lambda i: (0, i, 0)),
            pl.BlockSpec((HEADS, bpt, VT_ROWS, MOBA_BLOCK), lambda i: (0, i, 0, 0)),
            pl.BlockSpec((1, bpt, WIDTH), lambda i: (i, 0, 0)),
        ],
        out_shape=[
            jax.ShapeDtypeStruct((T, RWKV_COLS), F32),
            jax.ShapeDtypeStruct((T, WIDTH), BF16),
            jax.ShapeDtypeStruct((T, GATE_COLS), BF16),
            jax.ShapeDtypeStruct((HEADS, T, LANES), BF16),
            jax.ShapeDtypeStruct((HEADS, T // MOBA_BLOCK, VT_ROWS, MOBA_BLOCK), BF16),
            jax.ShapeDtypeStruct((T // tm, bpt, WIDTH), F32),
        ],
        compiler_params=_cparams(("parallel",)),
        name="inproj",
    )(x2d, g, w_bf)


def _head_ones():
    r = lax.broadcasted_iota(jnp.int32, (LANES, LANES), 0) // HEAD_DIM
    c = lax.broadcasted_iota(jnp.int32, (LANES, LANES), 1) // HEAD_DIM
    return jnp.where(r == c, 1.0, 0.0).astype(BF16)


def _head_sum(x, ones):
    parts = [_bdot(x[:, c:c + LANES], ones) for c in range(0, WIDTH, LANES)]
    return jnp.concatenate(parts, axis=-1)


def _rwkv_prep_values(u_ref, up_ref, mu_ref, w0_ref, a0_ref, kk_ref, ka_ref, rk_ref, wl_ref):
    i = pl.program_id(0)
    u = u_ref[...]
    tm = u.shape[0]
    prev_last = up_ref[7:8, :] * jnp.where(i > 0, 1.0, 0.0)
    rolled = pltpu.roll(u, 1, 0)
    row = lax.broadcasted_iota(jnp.int32, u.shape, 0)
    u_prev = jnp.where(row == 0, prev_last, rolled)
    u = u + mu_ref[...] * (u_prev - u)
    r = u[:, 0:WIDTH]
    k = u[:, WIDTH:2 * WIDTH]
    v = u[:, 2 * WIDTH:3 * WIDTH]
    lo = u[:, 3 * WIDTH:3 * WIDTH + LORA_COLS]
    lane = lax.broadcasted_iota(jnp.int32, lo.shape, 1)
    act = jnp.where(lane < DECAY_LORA, jnp.tanh(lo),
                    jnp.where(lane < DECAY_LORA + ICLR_LORA, lo, _sigmoid(lo)))
    up = _bdot(act, wl_ref[...])
    z = -(w0_ref[...] + up[:, 0:WIDTH])
    softplus = jnp.maximum(z, 0.0) + jnp.log(1.0 + jnp.exp(-jnp.abs(z)))
    w_log = -softplus - 0.5
    lw = -jnp.exp(w_log)
    iclr = _sigmoid(a0_ref[...] + up[:, WIDTH:2 * WIDTH])
    g = up[:, 2 * WIDTH:3 * WIDTH]
    ones = _head_ones()
    kk = k * kk_ref[...]
    kk = kk * lax.rsqrt(_head_sum(kk * kk, ones) + KK_EPS)
    k2 = k * (1.0 + (iclr - 1.0) * ka_ref[...])
    bonus = _head_sum(r * k2 * rk_ref[...], ones) * v
    return r, lw, k2, v, -kk, kk * iclr, g, bonus


def _rwkv_core_kernel(u_ref, up_ref, mu_ref, w0_ref, a0_ref, kk_ref, ka_ref, rk_ref, wl_ref,
                      lnw_ref, lnb_ref, o_ref, s_ref):
    c = pl.program_id(0)

    @pl.when(c == 0)
    def _():
        s_ref[...] = jnp.zeros_like(s_ref)

    C = CHUNK
    G = CHUNKS_PER_STEP
    ri = lax.broadcasted_iota(jnp.int32, (C, C), 0)
    ci = lax.broadcasted_iota(jnp.int32, (C, C), 1)
    rg = lax.broadcasted_iota(jnp.int32, (G * C, G * C), 0)
    cg = lax.broadcasted_iota(jnp.int32, (G * C, G * C), 1)
    r_in, lw, k_in, v_all, a_in, b_in, g_in, bonus = _rwkv_prep_values(
        u_ref, up_ref, mu_ref, w0_ref, a0_ref, kk_ref, ka_ref, rk_ref, wl_ref)
    tri = jnp.where((rg >= cg) & (rg // C == cg // C), 1.0, 0.0).astype(BF16)
    lw_hi = lw.astype(BF16)
    lw_r1 = lw - lw_hi.astype(F32)
    lw_mid = lw_r1.astype(BF16)
    lw_lo = (lw_r1 - lw_mid.astype(F32)).astype(BF16)
    cum = (jnp.dot(tri, lw_hi, preferred_element_type=F32)
           + jnp.dot(tri, lw_mid, preferred_element_type=F32)
           + jnp.dot(tri, lw_lo, preferred_element_type=F32))
    lam = jnp.exp(cum)
    inv_lam = jnp.exp(-cum)
    r_t = r_in * lam
    a_t = a_in * jnp.exp(cum - lw)
    b_t = b_in * inv_lam
    k_t = k_in * inv_lam
    tots = [cum[g * C + C - 1:g * C + C, :] for g in range(G)]
    rowg = lax.broadcasted_iota(jnp.int32, cum.shape, 0) // C
    tot = tots[G - 1]
    for g in range(G - 2, -1, -1):
        tot = jnp.where(rowg == g, tots[g], tot)
    rest = jnp.exp(tot - cum)
    b_h = b_in * rest
    k_h = k_in * rest
    lam_c = [jnp.exp(t) for t in tots]
    eye = jnp.where(ri == ci, 1.0, 0.0)
    H = range(G * HEADS)
    sls = [slice(h * HEAD_DIM, (h + 1) * HEAD_DIM) for h in range(HEADS)]
    bf = lambda x: x.astype(BF16)
    part = lambda x, i: x[(i // HEADS) * C:(i // HEADS + 1) * C, sls[i % HEADS]]
    at = [bf(part(a_t, i)) for i in H]
    rt = [part(r_t, i) for i in H]
    bt = [bf(part(b_t, i)) for i in H]
    kt = [bf(part(k_t, i)) for i in H]
    bh = [bf(part(b_h, i)) for i in H]
    kh = [bf(part(k_h, i)) for i in H]
    vv = [bf(part(v_all, i)) for i in H]
    ci2 = lax.broadcasted_iota(jnp.int32, (C, 2 * C), 1)
    ri2 = lax.broadcasted_iota(jnp.int32, (C, 2 * C), 0)
    cm2 = jnp.where(ci2 >= C, ci2 - C, ci2)
    left = ci2 < C
    ar = [jnp.concatenate([at[h], bf(rt[h])], axis=0) for h in H]
    bk = [jnp.concatenate([bt[h], kt[h]], axis=0) for h in H]
    g = [_bdot_nt(ar[h], bk[h]) for h in H]
    top = [jnp.where(ri2 > cm2, g[h][0:C], 0.0) for h in H]
    bot = [bf(jnp.where(ri2 >= cm2, g[h][C:2 * C], 0.0)) for h in H]
    a_ab = [top[h][:, 0:C] for h in H]
    akv = [_bdot(top[h][:, C:2 * C], vv[h]) for h in H]
    z = [jnp.concatenate([a_ab[h], eye], axis=1) for h in H]
    for _ in range(6):
        z = [_bdot(z[h][:, 0:C], z[h]) + jnp.where(left, 0.0, z[h]) for h in H]
    tinv = [bf(z[h][:, C:2 * C]) for h in H]
    wu = [_bdot(tinv[h], jnp.concatenate([at[h], bf(akv[h])], axis=1)) for h in H]
    w_m = [bf(wu[h][:, 0:C]) for h in H]
    uv = [jnp.concatenate([bf(wu[h][:, C:2 * C]), vv[h]], axis=0) for h in H]
    q_m = [rt[h] + _bdot(bot[h][:, 0:C], w_m[h]) for h in H]
    y0 = [_bdot(bot[h], uv[h]) for h in H]
    m_k = [_bdot_tn(w_m[h], bh[h]) for h in H]
    n0 = [_bdot_tn(uv[h], jnp.concatenate([bh[h], kh[h]], axis=0)) for h in H]
    state = [s_ref[h] for h in range(HEADS)]
    blocks = []
    for g in range(G):
        outs = []
        for h in range(HEADS):
            i = g * HEADS + h
            y = _bdot_nt(q_m[i], state[h]) + y0[i]
            state[h] = state[h] * lam_c[g][:, sls[h]] + _bdot(state[h], m_k[i]) + n0[i]
            mean = jnp.mean(y, axis=-1, keepdims=True)
            var = jnp.mean(jnp.square(y - mean), axis=-1, keepdims=True)
            outs.append((y - mean) * lax.rsqrt(var + LN_X_EPS))
        blocks.append(jnp.concatenate(outs, axis=-1))
    for h in range(HEADS):
        s_ref[h] = state[h]
    yn = jnp.concatenate(blocks, axis=0)
    yn = yn * lnw_ref[...] + lnb_ref[...] + bonus
    o_ref[...] = (yn * g_in).astype(BF16)


def _rwkv_core(ur, mu, w0, a0, k_k, k_a, r_k, w_lora, ln_w, ln_b):
    T = ur.shape[0]
    rows = CHUNK * CHUNKS_PER_STEP
    row = lambda w: pl.BlockSpec((1, w), lambda c: (0, 0))
    return pl.pallas_call(
        _rwkv_core_kernel,
        grid=(T // rows,),
        in_specs=[
            pl.BlockSpec((rows, RWKV_COLS), lambda c: (c, 0)),
            pl.BlockSpec((8, RWKV_COLS), lambda c: (jnp.maximum(c * (rows // 8) - 1, 0), 0)),
            row(RWKV_COLS), row(WIDTH), row(WIDTH), row(WIDTH), row(WIDTH), row(WIDTH),
            pl.BlockSpec((LORA_COLS, 3 * WIDTH), lambda c: (0, 0)),
            row(WIDTH), row(WIDTH),
        ],
        out_specs=pl.BlockSpec((rows, WIDTH), lambda c: (c, 0)),
        out_shape=jax.ShapeDtypeStruct((T, WIDTH), BF16),
        scratch_shapes=[pltpu.VMEM((HEADS, HEAD_DIM, HEAD_DIM), F32)],
        compiler_params=_cparams(("arbitrary",)),
        name="rwkv_core",
    )(ur, ur, mu, w0, a0, k_k, k_a, r_k, w_lora, ln_w, ln_b)


def _t5_bucket(dist):
    n = jnp.maximum(dist, 0)
    max_exact = REL_BUCKETS // 2
    nf = jnp.maximum(n, max_exact).astype(jnp.float32)
    large = max_exact + (jnp.log(nf / max_exact) / math.log(REL_MAX_DISTANCE / max_exact)
                         * (REL_BUCKETS - max_exact)).astype(jnp.int32)
    large = jnp.minimum(large, REL_BUCKETS - 1)
    return jnp.where(n < max_exact, n, large)


def _bucket_tiles():
    i = jnp.arange(MOBA_BLOCK)[None, :]
    j = jnp.arange(MOBA_BLOCK)[:, None]
    d = jnp.arange(N_BIAS_TILES + 1)[:, None, None]
    dist = d * MOBA_BLOCK + i - j
    bucket = _t5_bucket(dist)
    bucket = jnp.where(d == N_BIAS_TILES - 1, REL_BUCKETS - 1, bucket)
    return jnp.where((dist < 0) | (d == N_BIAS_TILES), -1, bucket).astype(jnp.int32)


BIAS_ROWS = 16


def _bias_tiles_kernel(idx_ref, rb_ref, o_ref):
    def rows(c, carry):
        r0 = pl.multiple_of(c * BIAS_ROWS, BIAS_ROWS)
        idx = idx_ref[0, pl.ds(r0, BIAS_ROWS), :]
        acc = [jnp.where(idx < 0, NEG_INF, 0.0)] * HEADS
        for bkt in range(REL_BUCKETS):
            hit = idx == bkt
            acc = [jnp.where(hit, rb_ref[bkt, h] * LOG2E, acc[h]) for h in range(HEADS)]
        for h in range(HEADS):
            o_ref[h, 0, pl.ds(r0, BIAS_ROWS), :] = acc[h]
        return carry

    lax.fori_loop(0, MOBA_BLOCK // BIAS_ROWS, rows, 0)


def _bias_tiles(rel_bias):
    idx = _bucket_tiles()
    n = N_BIAS_TILES + 1
    return pl.pallas_call(
        _bias_tiles_kernel,
        grid=(n,),
        in_specs=[
            pl.BlockSpec((1, MOBA_BLOCK, MOBA_BLOCK), lambda d: (d, 0, 0)),
            pl.BlockSpec(memory_space=pltpu.SMEM),
        ],
        out_specs=pl.BlockSpec((HEADS, 1, MOBA_BLOCK, MOBA_BLOCK), lambda d: (0, d, 0, 0)),
        out_shape=jax.ShapeDtypeStruct((HEADS, n, MOBA_BLOCK, MOBA_BLOCK), F32),
        compiler_params=_cparams(("parallel",)),
        name="bias_tiles",
    )(idx, rel_bias)


def _moba_kernel(q_ref, ka_ref, vt_ref, r_ref, bias_ref, o_ref, s_ref, p_ref, acc_ref):
    qb = pl.program_id(1)
    B = MOBA_BLOCK
    q_tr = q_ref[...].astype(F32).T
    q_tr_bf = q_tr.astype(BF16)
    blk = lax.broadcasted_iota(jnp.int32, (HEAD_DIM, B), 0)
    big = jnp.int32(1 << 20)
    q_t, q_own_t = [], []
    for h2 in range(2):
        off = HEAD_DIM * (1 - h2)
        valid = blk < qb
        sc = jnp.dot(r_ref[0, h2, off:off + HEAD_DIM, :], q_tr_bf, preferred_element_type=F32)
        s = jnp.where(valid, sc, NEG_INF)
        sel = jnp.zeros((HEAD_DIM, B), jnp.bool_)
        for _ in range(MOBA_TOP):
            m = jnp.max(s, axis=0, keepdims=True)
            idx = jnp.min(jnp.where(s == m, blk, big), axis=0, keepdims=True)
            pick = blk == idx
            sel = jnp.logical_or(sel, pick)
            s = jnp.where(pick, -jnp.inf, s)
        sel = jnp.logical_and(sel, valid)
        choice = jnp.where(sel, 0.0, NEG_INF)
        q_head = q_tr[h2 * HEAD_DIM:(h2 + 1) * HEAD_DIM]
        parts = [q_head, choice] if h2 == 0 else [choice, q_head]
        own = [q_head, jnp.zeros_like(choice)] if h2 == 0 else [jnp.zeros_like(choice), q_head]
        q_t.append(jnp.concatenate(parts, axis=0).astype(BF16))
        q_own_t.append(jnp.concatenate(own, axis=0).astype(BF16))

    carry = []
    for h2 in range(2):
        k_own = ka_ref[h2, pl.ds(pl.multiple_of(qb * B, B), B), :]
        s0 = jnp.dot(k_own, q_own_t[h2], preferred_element_type=F32) + bias_ref[h2, 0]
        m0 = jnp.max(s0, axis=0, keepdims=True)
        p0 = jnp.exp2(s0 - m0).astype(BF16)
        carry += [m0, jnp.dot(vt_ref[h2, qb], p0, preferred_element_type=F32)]

    n_tiles = (qb + 1) // 2
    last = jnp.maximum(n_tiles - 1, 0)
    for h2 in range(2):
        acc_ref[h2] = carry[2 * h2 + 1]

    def trip(t, w, stats, scores=True, softmax=True, accumulate=True):
        r = 1 - w
        ok = (t >= 1) & (t <= n_tiles)
        d_a = jnp.where(ok, jnp.clip(qb - 2 * (t - 1), 0, N_BIAS_TILES - 1), N_BIAS_TILES)
        d_b = jnp.where(ok, jnp.clip(qb - 2 * (t - 1) - 1, 0, N_BIAS_TILES - 1), N_BIAS_TILES)
        start = pl.multiple_of(jnp.clip(t, 0, last) * (2 * B), 2 * B)
        v_a = 2 * jnp.clip(t - 2, 0, last)
        v_b = v_a + 1
        out = []
        for h2 in range(2):
            m_prev, alpha_p = stats[2 * h2], stats[2 * h2 + 1]
            if accumulate:
                acc_ref[h2] = (acc_ref[h2] * alpha_p
                               + jnp.dot(vt_ref[h2, v_a], p_ref[r, h2, 0:B], preferred_element_type=F32)
                               + jnp.dot(vt_ref[h2, v_b], p_ref[r, h2, B:2 * B],
                                         preferred_element_type=F32))
            if softmax:
                s_a = s_ref[r, h2, 0:B] + bias_ref[h2, d_a]
                s_b = s_ref[r, h2, B:2 * B] + bias_ref[h2, d_b]
                m_new = jnp.maximum(m_prev, jnp.max(jnp.maximum(s_a, s_b), axis=0, keepdims=True))
                out += [m_new, jnp.exp2(m_prev - m_new)]
                p_ref[w, h2, 0:B] = jnp.exp2(s_a - m_new).astype(BF16)
                p_ref[w, h2, B:2 * B] = jnp.exp2(s_b - m_new).astype(BF16)
            else:
                out += [m_prev, alpha_p]
            if scores:
                s_ref[w, h2] = jnp.dot(ka_ref[h2, pl.ds(start, 2 * B), :], q_t[h2],
                                       preferred_element_type=F32)
        return out

    def body(u, stats):
        stats = trip(2 * u, 0, list(stats))
        return tuple(trip(2 * u + 1, 1, stats))

    stats = []
    for h2 in range(2):
        stats += [carry[2 * h2], jnp.ones_like(carry[2 * h2])]
    stats = trip(0, 0, stats, softmax=False, accumulate=False)
    stats = trip(1, 1, stats, accumulate=False)
    stats = list(lax.fori_loop(1, n_tiles // 2, body, tuple(stats)))
    odd = n_tiles % 2 == 1

    @pl.when(jnp.logical_not(odd) & (n_tiles >= 2))
    def _():
        st = trip(n_tiles, 0, stats, scores=False)
        trip(n_tiles + 1, 1, st, scores=False, softmax=False)

    @pl.when(odd & (n_tiles >= 3))
    def _():
        st = trip(n_tiles - 1, 0, stats)
        st = trip(n_tiles, 1, st, scores=False)
        trip(n_tiles + 1, 0, st, scores=False, softmax=False)

    @pl.when(n_tiles == 1)
    def _():
        trip(2, 0, stats, scores=False, softmax=False)
    outs = [acc_ref[h2, 0:HEAD_DIM] / acc_ref[h2, HEAD_DIM:HEAD_DIM + 1] for h2 in range(2)]
    o_ref[...] = jnp.concatenate(outs, axis=0).T.astype(BF16)


def _moba(q, k_aug, vt_aug, r_mats, bias_tiles):
    T = q.shape[0]
    nb = T // MOBA_BLOCK
    npair = HEADS // 2
    once = pl.Buffered(1)
    return pl.pallas_call(
        _moba_kernel,
        grid=(npair, nb),
        in_specs=[
            pl.BlockSpec((MOBA_BLOCK, LANES), lambda p, qb: (qb, p)),
            pl.BlockSpec((2, T, LANES), lambda p, qb: (p, 0, 0)),
            pl.BlockSpec((2, nb, VT_ROWS, MOBA_BLOCK), lambda p, qb: (p, 0, 0, 0)),
            pl.BlockSpec((1, 2, LANES, LANES), lambda p, qb: (p, 0, 0, 0)),
            pl.BlockSpec((2, N_BIAS_TILES + 1, MOBA_BLOCK, MOBA_BLOCK), lambda p, qb: (p, 0, 0, 0),
                         pipeline_mode=once),
        ],
        out_specs=pl.BlockSpec((MOBA_BLOCK, LANES), lambda p, qb: (qb, p)),
        out_shape=jax.ShapeDtypeStruct((T, WIDTH), BF16),
        scratch_shapes=[pltpu.VMEM((2, 2, 2 * MOBA_BLOCK, MOBA_BLOCK), F32),
                        pltpu.VMEM((2, 2, 2 * MOBA_BLOCK, MOBA_BLOCK), BF16),
                        pltpu.VMEM((2, VT_ROWS, MOBA_BLOCK), F32)],
        compiler_params=_cparams(("arbitrary", "arbitrary")),
        name="moba",
    )(q, k_aug, vt_aug, r_mats, bias_tiles)


def _score_mats(kmean):
    nb = kmean.shape[0]
    km = kmean.reshape(nb, HEADS, HEAD_DIM).transpose(1, 2, 0)
    km = jnp.pad(km, ((0, 0), (0, 0), (0, HEAD_DIM - nb)))
    z = jnp.zeros((HEADS // 2, HEAD_DIM, HEAD_DIM), F32)
    even = jnp.concatenate([jnp.concatenate([z, km[0::2]], axis=2),
                            jnp.concatenate([z, z], axis=2)], axis=1)
    odd = jnp.concatenate([jnp.concatenate([z, z], axis=2),
                           jnp.concatenate([km[1::2], z], axis=2)], axis=1)
    return jnp.swapaxes(jnp.stack([even, odd], axis=1), -1, -2).astype(BF16)


def _memkv_kernel(m_ref, g_ref, w_ref, o_ref):
    h = _rms(m_ref[...], g_ref[...]).astype(BF16)
    o_ref[...] = jnp.dot(h, w_ref[...], preferred_element_type=F32).astype(BF16)


def _memkv(mem2d, g, w_bf):
    return pl.pallas_call(
        _memkv_kernel,
        out_shape=jax.ShapeDtypeStruct((N_MEM, 2 * XATTN_WIDTH), BF16),
        compiler_params=pltpu.CompilerParams(vmem_limit_bytes=VMEM_LIMIT),
        name="memkv",
    )(mem2d, g, w_bf)


def _merge_kernel(x_ref, oa_ref, ob_ref, gt_ref, woa_ref, wob_ref, wout_ref, nx_ref, wq_ref,
                  kv_ref, wo_ref, nf_ref, wr_ref, x2_ref, h3_ref, rt_ref, cnt_ref, run_ref):
    pa = jnp.dot(oa_ref[...], woa_ref[...], preferred_element_type=F32)
    pb = jnp.dot(ob_ref[...], wob_ref[...], preferred_element_type=F32)
    ga = gt_ref[:, 0:D_MODEL].astype(F32)
    gb = gt_ref[:, D_MODEL:2 * D_MODEL].astype(F32)
    merged = _sigmoid(ga) * pa + _sigmoid(gb) * pb
    x1 = x_ref[...] + jnp.dot(merged.astype(BF16), wout_ref[...], preferred_element_type=F32)
    h2 = _rms(x1, nx_ref[...]).astype(BF16)
    q = jnp.dot(h2, wq_ref[...], preferred_element_type=F32).astype(BF16)
    heads = []
    for h in range(XATTN_HEADS):
        sl = slice(h * XATTN_HEAD_DIM, (h + 1) * XATTN_HEAD_DIM)
        km = kv_ref[:, sl]
        vm = kv_ref[:, XATTN_WIDTH + h * XATTN_HEAD_DIM:XATTN_WIDTH + (h + 1) * XATTN_HEAD_DIM]
        lg = lax.dot_general(q[:, sl], km, (((1,), (1,)), ((), ())),
                             preferred_element_type=F32) * (XATTN_HEAD_DIM ** -0.5)
        e = jnp.exp(lg - jnp.max(lg, axis=-1, keepdims=True))
        p = e / jnp.sum(e, axis=-1, keepdims=True)
        heads.append(jnp.dot(p.astype(BF16), vm, preferred_element_type=F32))
    o = jnp.concatenate(heads, axis=-1).astype(BF16)
    x2 = x1 + jnp.dot(o, wo_ref[...], preferred_element_type=F32)
    x2_ref[...] = x2
    h3 = _rms(x2, nf_ref[...])
    _store_token_tiles(h3_ref, h3)
    lg = _bdot(h3, wr_ref[...]).T
    tm = lg.shape[1]
    row = lax.broadcasted_iota(jnp.int32, lg.shape, 0)
    big = jnp.int32(1 << 20)
    is_g = row < N_GROUPS
    gl = jnp.where(is_g, lg, -jnp.inf)
    gmax = jnp.max(gl, axis=0, keepdims=True)
    gsel = jnp.min(jnp.where(gl == gmax, row, big), axis=0, keepdims=True)
    p_top = 1.0 / jnp.sum(jnp.where(is_g, jnp.exp(gl - gmax), 0.0), axis=0, keepdims=True)
    e_id = row - N_GROUPS
    in_grp = (e_id >= gsel * EXPERTS_PER_GROUP) & (e_id < (gsel + 1) * EXPERTS_PER_GROUP)
    el = jnp.where(in_grp, lg, -jnp.inf)
    v1 = jnp.max(el, axis=0, keepdims=True)
    i1 = jnp.min(jnp.where(el == v1, row, big), axis=0, keepdims=True)
    el2 = jnp.where(row == i1, -jnp.inf, el)
    v2 = jnp.max(el2, axis=0, keepdims=True)
    i2 = jnp.min(jnp.where(el2 == v2, row, big), axis=0, keepdims=True)
    e21 = jnp.exp(v2 - v1)
    g1 = p_top / (1.0 + e21)
    g2 = p_top * e21 / (1.0 + e21)
    step = pl.program_id(0)

    @pl.when(step == 0)
    def _():
        run_ref[...] = jnp.zeros_like(run_ref)

    oh1 = jnp.where(row == i1, 1.0, 0.0)
    oh2 = jnp.where(row == i2, 1.0, 0.0)
    both = oh1 + oh2
    ri = lax.broadcasted_iota(jnp.int32, (tm, tm), 0)
    ci = lax.broadcasted_iota(jnp.int32, (tm, tm), 1)
    before = jnp.dot(both, jnp.where(ri < ci, 1.0, 0.0), preferred_element_type=F32) + run_ref[:, 0:1]
    r1 = jnp.sum(before * oh1, axis=0, keepdims=True)
    r2 = jnp.sum(before * oh2, axis=0, keepdims=True)
    run_ref[...] = run_ref[...] + jnp.sum(both, axis=1, keepdims=True)
    cnt_ref[...] = run_ref[...]
    vals = [(i1 - N_GROUPS).astype(F32), (i2 - N_GROUPS).astype(F32), g1, g2, r1, r2]
    rt = jnp.zeros_like(lg)
    for k, v in enumerate(vals):
        rt = jnp.where(row == k, v, rt)
    rt_ref[...] = rt.T


def _merge(x2d, o_a, o_b, gates, w_oa, w_ob, w_out, n_x, w_q, kv, w_o, n_f, w_r):
    T = x2d.shape[0]
    tm = TOKEN_TILE
    full = lambda a: pl.BlockSpec(a.shape, lambda i: (0,) * a.ndim, pipeline_mode=pl.Buffered(1))
    tile = lambda w: pl.BlockSpec((tm, w), lambda i: (i, 0))
    return pl.pallas_call(
        _merge_kernel,
        grid=(T // tm,),
        in_specs=[tile(D_MODEL), tile(WIDTH), tile(WIDTH), tile(GATE_COLS), full(w_oa), full(w_ob),
                  full(w_out), full(n_x), full(w_q), full(kv), full(w_o), full(n_f), full(w_r)],
        out_specs=[tile(D_MODEL), pl.BlockSpec((tm * ROW_TILES, LANES), lambda i: (i, 0)), tile(LANES),
                   pl.BlockSpec((LANES, LANES), lambda i: (0, 0))],
        out_shape=[jax.ShapeDtypeStruct((T, D_MODEL), F32),
                   jax.ShapeDtypeStruct((T * ROW_TILES, LANES), U32),
                   jax.ShapeDtypeStruct((T, LANES), F32),
                   jax.ShapeDtypeStruct((LANES, LANES), F32)],
        scratch_shapes=[pltpu.VMEM((LANES, LANES), F32)],
        compiler_params=_cparams(("arbitrary",)),
        name="merge",
    )(x2d, o_a, o_b, gates, w_oa, w_ob, w_out, n_x, w_q, kv, w_o, n_f, w_r)


def _dispatch_kernel(pos_ref, zs_ref, zv_ref, nu_ref, h_ref, xs_hbm, zero_ref, sem):
    i = pl.program_id(0)
    tm = h_ref.shape[0] // ROW_TILES

    @pl.when(i == 0)
    def _():
        zero_ref[...] = jnp.zeros_like(zero_ref)

        def clear(e):
            dst = xs_hbm.at[pl.ds(pl.multiple_of(zs_ref[e] * ROW_TILES, ROW_TILES),
                                  MOE_BLOCK * ROW_TILES), :]
            return pltpu.make_async_copy(zero_ref, dst, sem)

        for e in range(N_EXPERTS):
            @pl.when(zv_ref[e] > 0)
            def _():
                clear(e).start()

        for e in range(N_EXPERTS):
            @pl.when(zv_ref[e] > 0)
            def _():
                clear(e).wait()

        def tail(b):
            dst = xs_hbm.at[pl.ds(pl.multiple_of(b * (MOE_BLOCK * ROW_TILES), MOE_BLOCK * ROW_TILES),
                                  MOE_BLOCK * ROW_TILES), :]
            return pltpu.make_async_copy(zero_ref, dst, sem)

        n_blocks = xs_hbm.shape[0] // (MOE_BLOCK * ROW_TILES)

        def start_tail(b, c):
            tail(b).start()
            return c

        def wait_tail(b, c):
            tail(b).wait()
            return c

        lax.fori_loop(nu_ref[0], n_blocks, start_tail, 0)
        lax.fori_loop(nu_ref[0], n_blocks, wait_tail, 0)

    def row_copy(r, k):
        p = pos_ref[2 * (i * tm + r) + k]
        dst = xs_hbm.at[pl.ds(pl.multiple_of(p * ROW_TILES, ROW_TILES), ROW_TILES), :]
        return pltpu.make_async_copy(h_ref.at[pl.ds(r * ROW_TILES, ROW_TILES), :], dst, sem)

    for r in range(tm):
        row_copy(r, 0).start(priority=0)
        row_copy(r, 1).start(priority=1)
    for _ in range(tm):
        row_copy(0, 0).wait()
        row_copy(0, 1).wait()


def _dispatch(pos, zstart, zvalid, n_used, h3t, cap):
    T = h3t.shape[0] // ROW_TILES
    tm = ROW_DMA_TILE
    grid_spec = pltpu.PrefetchScalarGridSpec(
        num_scalar_prefetch=4,
        grid=(T // tm,),
        in_specs=[pl.BlockSpec((tm * ROW_TILES, LANES), lambda i, ps, zs, zv, nu: (i, 0))],
        out_specs=pl.BlockSpec(memory_space=pl.ANY),
        scratch_shapes=[pltpu.VMEM((MOE_BLOCK * ROW_TILES, LANES), U32), pltpu.SemaphoreType.DMA],
    )
    return pl.pallas_call(
        _dispatch_kernel,
        grid_spec=grid_spec,
        out_shape=jax.ShapeDtypeStruct((cap * ROW_TILES, LANES), U32),
        compiler_params=_cparams(("arbitrary",)),
        name="dispatch",
    )(pos, zstart, zvalid, n_used, h3t)


def _moe_kernel(be_ref, nu_ref, x_ref, wg_ref, wu_ref, wd_ref, o_ref):
    i = pl.program_id(0)

    @pl.when(i < nu_ref[0])
    def _():
        xb = jnp.concatenate([x.astype(BF16) for x in _load_token_tiles(x_ref, MOE_BLOCK)], axis=-1)
        gate = _bdot(xb, wg_ref[0])
        up = _bdot(xb, wu_ref[0])
        hid = gate * _sigmoid(gate) * up
        _store_token_tiles(o_ref, _bdot(hid, wd_ref[0]))

    @pl.when(i >= nu_ref[0])
    def _():
        o_ref[...] = jnp.zeros_like(o_ref)


def _moe(block_expert, n_used, xs, wg, wu, wd):
    n_blocks = xs.shape[0] // (MOE_BLOCK * ROW_TILES)
    used = lambda i, nu: jnp.minimum(i, jnp.maximum(nu[0] - 1, 0))
    grid_spec = pltpu.PrefetchScalarGridSpec(
        num_scalar_prefetch=2,
        grid=(n_blocks,),
        in_specs=[
            pl.BlockSpec((MOE_BLOCK * ROW_TILES, LANES), lambda i, be, nu: (used(i, nu), 0)),
            pl.BlockSpec((1, D_MODEL, D_EXPERT), lambda i, be, nu: (be[i], 0, 0)),
            pl.BlockSpec((1, D_MODEL, D_EXPERT), lambda i, be, nu: (be[i], 0, 0)),
            pl.BlockSpec((1, D_EXPERT, D_MODEL), lambda i, be, nu: (be[i], 0, 0)),
        ],
        out_specs=pl.BlockSpec((MOE_BLOCK * ROW_TILES, LANES), lambda i, be, nu: (i, 0)),
    )
    return pl.pallas_call(
        _moe_kernel,
        grid_spec=grid_spec,
        out_shape=jax.ShapeDtypeStruct(xs.shape, U32),
        compiler_params=_cparams(("arbitrary",)),
        name="moe",
    )(block_expert, n_used, xs, wg, wu, wd)


def _moe_plan(expert, rank, counts, n_tok):
    n_assign = n_tok * 2
    e_flat = expert.reshape(n_assign)
    rank = rank.reshape(n_assign)
    padded = (counts + MOE_BLOCK - 1) // MOE_BLOCK * MOE_BLOCK
    p_end = jnp.cumsum(padded)
    p_start = p_end - padded
    pos = (p_start[e_flat] + rank).astype(jnp.int32)
    n_blocks = (n_assign + N_EXPERTS * (MOE_BLOCK - 1) + MOE_BLOCK - 1) // MOE_BLOCK
    cap = n_blocks * MOE_BLOCK
    block_start = jnp.arange(n_blocks, dtype=jnp.int32) * MOE_BLOCK
    block_expert = jnp.minimum(jnp.sum(p_end[None, :] <= block_start[:, None], axis=1),
                               N_EXPERTS - 1).astype(jnp.int32)
    n_used = (p_end[-1] // MOE_BLOCK).astype(jnp.int32).reshape(1)
    last_block = jnp.maximum(p_end - MOE_BLOCK, 0).astype(jnp.int32)
    nonempty = (padded > 0).astype(jnp.int32)
    return pos, cap, block_expert, n_used, last_block, nonempty


def _final_kernel(pos_ref, x_ref, rt_ref, y_hbm, g_ref, o_ref, y1_ref, y2_ref, sem):
    i = pl.program_id(0)
    n = pl.num_programs(0)
    tm = x_ref.shape[0]
    slot = i % 2

    def copies(tile, r, s):
        p1 = pos_ref[2 * (tile * tm + r)]
        p2 = pos_ref[2 * (tile * tm + r) + 1]
        src = lambda p: y_hbm.at[pl.ds(pl.multiple_of(p * ROW_TILES, ROW_TILES), ROW_TILES), :]
        dst = lambda ref: ref.at[s, pl.ds(r * ROW_TILES, ROW_TILES), :]
        return (pltpu.make_async_copy(src(p1), dst(y1_ref), sem.at[s]),
                pltpu.make_async_copy(src(p2), dst(y2_ref), sem.at[s]))

    def gather(tile, s):
        for r in range(tm):
            c1, c2 = copies(tile, r, s)
            c1.start(priority=0)
            c2.start(priority=1)

    def wait_rows(s):
        for _ in range(tm):
            c1, c2 = copies(0, 0, s)
            c1.wait()
            c2.wait()

    @pl.when(i == 0)
    def _():
        gather(0, 0)

    wait_rows(slot)
    gather(jnp.minimum(i + 1, n - 1), 1 - slot)
    rt = rt_ref[...]
    g1 = rt[:, 2:3]
    g2 = rt[:, 3:4]
    y1 = jnp.concatenate(_load_token_tiles(y1_ref.at[slot], tm), axis=-1)
    y2 = jnp.concatenate(_load_token_tiles(y2_ref.at[slot], tm), axis=-1)
    x3 = x_ref[...] + y1 * g1 + y2 * g2
    o_ref[...] = _rms(x3, g_ref[...])

    @pl.when(i == n - 1)
    def _():
        wait_rows(1 - slot)


def _final(pos, x2, rt, y_sorted, g):
    T = x2.shape[0]
    tm = ROW_DMA_TILE
    grid_spec = pltpu.PrefetchScalarGridSpec(
        num_scalar_prefetch=1,
        grid=(T // tm,),
        in_specs=[
            pl.BlockSpec((tm, D_MODEL), lambda i, ps: (i, 0)),
            pl.BlockSpec((tm, LANES), lambda i, ps: (i, 0)),
            pl.BlockSpec(memory_space=pl.ANY),
            pl.BlockSpec((1, D_MODEL), lambda i, ps: (0, 0)),
        ],
        out_specs=pl.BlockSpec((tm, D_MODEL), lambda i, ps: (i, 0)),
        scratch_shapes=[pltpu.VMEM((2, tm * ROW_TILES, LANES), U32),
                        pltpu.VMEM((2, tm * ROW_TILES, LANES), U32),
                        pltpu.SemaphoreType.DMA((2,))],
    )
    return pl.pallas_call(
        _final_kernel,
        grid_spec=grid_spec,
        out_shape=jax.ShapeDtypeStruct((T, D_MODEL), F32),
        compiler_params=_cparams(("arbitrary",)),
        name="final",
    )(pos, x2, rt, y_sorted, g)


def _lora_weight(decay_up, iclr_up, gate_up):
    w = jnp.zeros((LORA_COLS, 3 * WIDTH), F32)
    w = w.at[0:DECAY_LORA, 0:WIDTH].set(decay_up)
    w = w.at[DECAY_LORA:DECAY_LORA + ICLR_LORA, WIDTH:2 * WIDTH].set(iclr_up)
    w = w.at[DECAY_LORA + ICLR_LORA:, 2 * WIDTH:].set(gate_up)
    return w


def _router_weight(w_group, w_expert):
    w = jnp.zeros((D_MODEL, LANES), F32)
    w = w.at[:, 0:N_GROUPS].set(w_group)
    return w.at[:, N_GROUPS:N_GROUPS + N_EXPERTS].set(w_expert)


def kernel(x, mem, rel_bias, mem_norm, norm_mix, w_in, tshift_mu, decay_w0, decay_up, iclr_a0,
           iclr_up, gate_up, k_k, k_a, r_k, ln_x_w, ln_x_b, w_o_rwkv, w_o_moba, w_out,
           norm_xattn, w_q_x, w_kv_x, w_o_x, norm_ffn, w_router_group, w_router_expert,
           w_exp_gate, w_exp_up, w_exp_down, norm_final):
    B, T, D = x.shape
    assert B == 1 and D == D_MODEL and T % TOKEN_TILE == 0 and T // MOBA_BLOCK <= HEAD_DIM
    assert norm_mix.shape[0] == 1
    row = lambda a: a.reshape(1, -1)
    x2d = x.reshape(T, D)
    kv = _memkv(mem.reshape(N_MEM, D), row(mem_norm), w_kv_x[0].astype(BF16))

    ur, q, gates, k_aug, vt_aug, kmean = _inproj(x2d, row(norm_mix[0]), w_in[0].astype(BF16))
    kmean = kmean.reshape(T // MOBA_BLOCK, WIDTH)
    o_a = _rwkv_core(ur, row(tshift_mu[0]), row(decay_w0[0]), row(iclr_a0[0]), row(k_k[0]),
                     row(k_a[0]), row(r_k[0]),
                     _lora_weight(decay_up[0], iclr_up[0], gate_up[0]).astype(BF16),
                     row(ln_x_w[0]), row(ln_x_b[0]))

    o_b = _moba(q, k_aug, vt_aug, _score_mats(kmean), _bias_tiles(rel_bias))

    x2, h3, rt, counts = _merge(
        x2d, o_a, o_b, gates, w_o_rwkv[0].astype(BF16), w_o_moba[0].astype(BF16),
        w_out[0].astype(BF16), row(norm_xattn[0]), w_q_x[0].astype(BF16), kv,
        w_o_x[0].astype(BF16), row(norm_ffn[0]),
        _router_weight(w_router_group[0], w_router_expert[0]).astype(BF16))

    expert = rt[:, 0:2].astype(jnp.int32)
    rank = rt[:, 4:6].astype(jnp.int32)
    pos, cap, block_expert, n_used, last_block, nonempty = _moe_plan(
        expert, rank, counts[N_GROUPS:N_GROUPS + N_EXPERTS, 0].astype(jnp.int32), T)
    x_sorted = _dispatch(pos, last_block, nonempty, n_used, h3, cap)
    y_sorted = _moe(block_expert, n_used, x_sorted, w_exp_gate[0], w_exp_up[0], w_exp_down[0])
    out = _final(pos, x2, rt, y_sorted, row(norm_final))
    return out.reshape(B, T, D)
```
